```python
import jax, jax.numpy as jnp
from jax import lax
import numpy as np

D_MODEL = 2048
BATCH = 4
SEQ = 2048
DEPTH = 2

GRID_W = 64
CTX_LEN = 256
EPS = 1e-6
NEG_INF = -1e30
ROPE_THETA = 10000.0
Q_BLOCK = 128

V_DIM = 128
MLA_WIDTH = D_MODEL // 2
MLA_HEADS = MLA_WIDTH // V_DIM
QK_NOPE = 128
QK_ROPE = 64
Q_LORA = D_MODEL // 4
KV_LORA = D_MODEL // 8
FNET_WIDTH = D_MODEL // 4
FNET_GROUP_DIM = 128
FNET_GROUPS = FNET_WIDTH // FNET_GROUP_DIM
NA_WIDTH = D_MODEL // 4
NA_HEAD_DIM = 128
NA_HEADS = NA_WIDTH // NA_HEAD_DIM
NA_KH_MAX = 8
NA_KW = 16
MIX_WIDTH = MLA_WIDTH + FNET_WIDTH + NA_WIDTH
IN_SIZES = (Q_LORA, KV_LORA, QK_ROPE, FNET_WIDTH, NA_WIDTH, NA_WIDTH, NA_WIDTH)
IN_COLS = Q_LORA + KV_LORA + QK_ROPE + FNET_WIDTH + 3 * NA_WIDTH
N_GROUPS = 4
EXPERTS_PER_GROUP = 8
N_EXPERTS = N_GROUPS * EXPERTS_PER_GROUP
TOP_K = 2
D_EXPERT = D_MODEL // 4

kernel_name = 'hybrid_mla_fnet_natten_hmoe_dit'


def rmsnorm(x, g):
    xf = x.astype(jnp.float32)
    y = xf * lax.rsqrt(jnp.mean(xf * xf, axis=-1, keepdims=True) + EPS)
    return (y * g.astype(jnp.float32)).astype(x.dtype)


def modulate(h, shift, scale):
    return h * (1.0 + scale) + shift


def modulation(cond, w_ada, b_ada):
    m = jnp.dot(jax.nn.silu(cond), w_ada) + b_ada
    return jnp.split(m, 6, axis=-1)


def split_columns(z, sizes):
    out, start = [], 0
    for s in sizes:
        out.append(z[..., start:start + s])
        start += s
    return out


def axial_rope(n_tokens):
    half = QK_ROPE // 2
    inv_freq = ROPE_THETA ** (-jnp.arange(0, half, 2, dtype=jnp.float32) / half)
    t = jnp.arange(n_tokens, dtype=jnp.int32)
    row = (t // GRID_W).astype(jnp.float32)
    col = (t % GRID_W).astype(jnp.float32)
    ang = jnp.concatenate([row[:, None] * inv_freq, col[:, None] * inv_freq], axis=-1)
    return jnp.cos(ang), jnp.sin(ang)


def apply_rope(x, cos, sin):
    xf = x.astype(jnp.float32)
    x1, x2 = xf[..., 0::2], xf[..., 1::2]
    out = jnp.stack([x1 * cos - x2 * sin, x1 * sin + x2 * cos], axis=-1).reshape(x.shape)
    return out.astype(x.dtype)


def attend(q, k, v, scale):
    s = jnp.einsum('bqhd,bkhd->bhqk', q, k).astype(jnp.float32) * scale
    p = jax.nn.softmax(s, axis=-1).astype(v.dtype)
    return jnp.einsum('bhqk,bkhd->bqhd', p, v)


def block_attention(q, k, v, scale):
    B, L, H, dk = q.shape
    nb = L // Q_BLOCK
    qb = jnp.moveaxis(q.reshape(B, nb, Q_BLOCK, H, dk), 1, 0)
    o = lax.map(lambda qq: attend(qq, k, v, scale), qb)
    return jnp.moveaxis(o, 0, 1).reshape(B, L, H, v.shape[-1])


def mla_queries(zq, g_q, w_uq, rope):
    B, L, _ = zq.shape
    q = (rmsnorm(zq, g_q) @ w_uq).reshape(B, L, MLA_HEADS, QK_NOPE + QK_ROPE)
    if rope is None:
        return q
    cos, sin = rope
    q_rope = apply_rope(q[..., QK_NOPE:], cos[:, None, :], sin[:, None, :])
    return jnp.concatenate([q[..., :QK_NOPE], q_rope], axis=-1)


def mla_keys_values(zkv, zkr, g_kv, w_ukv, rope):
    B, L, _ = zkv.shape
    kv = (rmsnorm(zkv, g_kv) @ w_ukv).reshape(B, L, MLA_HEADS, QK_NOPE + V_DIM)
    k_rope = zkr if rope is None else apply_rope(zkr, rope[0], rope[1])
    k_rope = jnp.broadcast_to(k_rope[:, :, None, :], (B, L, MLA_HEADS, QK_ROPE))
    k = jnp.concatenate([kv[..., :QK_NOPE], k_rope], axis=-1)
    return k, kv[..., QK_NOPE:]


def fourier_mix(z, w_fnet, b_fnet):
    B, L, _ = z.shape
    zg = z.astype(jnp.float32).reshape(B, L, FNET_GROUPS, FNET_GROUP_DIM)
    f = jnp.fft.fftn(zg, axes=(1, 3), norm='ortho').real
    return f.reshape(B, L, FNET_WIDTH).astype(z.dtype) @ w_fnet + b_fnet


def neighbourhood_attention(q, k, v, k_ctx, v_ctx, rpb):
    B, S, H, dh = q.shape
    rows = S // GRID_W
    kh = min(NA_KH_MAX, rows)
    r = np.arange(rows)
    row_start = np.clip(r - kh // 2, 0, rows - kh)
    row_idx = row_start[:, None] + np.arange(kh)[None, :]
    cq = np.arange(GRID_W)
    ck = np.arange(GRID_W)
    col_start = np.clip(cq - NA_KW // 2, 0, GRID_W - NA_KW)
    in_win = (ck[None, :] >= col_start[:, None]) & (ck[None, :] < col_start[:, None] + NA_KW)
    drow = row_idx - r[:, None] + (NA_KH_MAX - 1)
    dcol = np.clip(ck[None, :] - cq[:, None] + (NA_KW - 1), 0, 2 * NA_KW - 2)
    bias = rpb[:, drow[:, None, :, None], dcol[None, :, None, :]].astype(jnp.float32)
    bias = jnp.where(in_win[None, None, :, None, :], bias, NEG_INF).reshape(H, rows, GRID_W, kh * GRID_W)
    n_loc = kh * GRID_W
    qg = q.reshape(B, rows, GRID_W, H, dh)
    kb = k.reshape(B, rows, GRID_W, H, dh)[:, row_idx].reshape(B, rows, n_loc, H, dh)
    vb = v.reshape(B, rows, GRID_W, H, dh)[:, row_idx].reshape(B, rows, n_loc, H, dh)
    scale = NA_HEAD_DIM ** -0.5
    s_loc = jnp.einsum('brqhd,brkhd->bhrqk', qg, kb).astype(jnp.float32) * scale + bias[None]
    s_ctx = jnp.einsum('brqhd,bkhd->bhrqk', qg, k_ctx).astype(jnp.float32) * scale
    p = jax.nn.softmax(jnp.concatenate([s_loc, s_ctx], axis=-1), axis=-1).astype(v.dtype)
    o = (jnp.einsum('bhrqk,brkhd->brqhd', p[..., :n_loc], vb)
         + jnp.einsum('bhrqk,bkhd->brqhd', p[..., n_loc:], v_ctx))
    return o.reshape(B, S, H * dh)


def merge_groups(o_mla, o_f, o_na, g_out, w_out):
    g_mla = g_out[:MLA_WIDTH]
    g_f = g_out[MLA_WIDTH:MLA_WIDTH + FNET_WIDTH]
    g_na = g_out[MLA_WIDTH + FNET_WIDTH:]
    y = jnp.concatenate([rmsnorm(o_mla, g_mla), rmsnorm(o_f, g_f), rmsnorm(o_na, g_na)], axis=-1)
    return y @ w_out


def hier_moe(h, w_rg, b_rg, w_re, b_re, w_gate, w_up, w_down):
    B, L, D = h.shape
    t = h.reshape(B * L, D)
    g_prob = jax.nn.softmax((t @ w_rg + b_rg).astype(jnp.float32), axis=-1)
    g_sel = jnp.argmax(g_prob, axis=-1)
    g_w = jnp.take_along_axis(g_prob, g_sel[:, None], axis=1)
    e_logits = (t @ w_re + b_re).astype(jnp.float32).reshape(-1, N_GROUPS, EXPERTS_PER_GROUP)
    e_logits = jnp.take_along_axis(e_logits, g_sel[:, None, None], axis=1)[:, 0]
    e_prob = jax.nn.softmax(e_logits, axis=-1)
    top_p, top_i = lax.top_k(e_prob, TOP_K)
    top_p = top_p / jnp.sum(top_p, axis=-1, keepdims=True)
    e_gate = jnp.einsum('tk,tke->te', top_p, jax.nn.one_hot(top_i, EXPERTS_PER_GROUP, dtype=jnp.float32)) * g_w
    gate = (jax.nn.one_hot(g_sel, N_GROUPS, dtype=jnp.float32)[:, :, None] * e_gate[:, None, :]).astype(h.dtype)
    out = jnp.zeros_like(t)
    for g in range(N_GROUPS):
        a = jnp.einsum('td,edf->tef', t, w_gate[g])
        u = jnp.einsum('td,edf->tef', t, w_up[g])
        out = out + jnp.einsum('tef,efd->td', jax.nn.silu(a) * u * gate[:, g, :, None], w_down[g])
    return out.reshape(B, L, D)


def layer(x, xc, c, c_ctx, w_ada, b_ada, g_attn, g_ffn, w_in, g_q, w_uq, g_kv, w_ukv,
          w_fnet, b_fnet, na_rpb, g_out, w_out, w_rg, b_rg, w_re, b_re, w_gate, w_up, w_down,
          rope, ctx_out):
    B, S, _ = x.shape
    Lc = xc.shape[1]
    shift_a, scale_a, gate_a, shift_f, scale_f, gate_f = [m[:, None, :] for m in modulation(c, w_ada, b_ada)]
    shift_ac, scale_ac, gate_ac, shift_fc, scale_fc, gate_fc = modulation(c_ctx, w_ada, b_ada)

    h = modulate(rmsnorm(x, g_attn), shift_a, scale_a)
    hc = modulate(rmsnorm(xc, g_attn), shift_ac, scale_ac)
    zq, zkv, zkr, zf, zna_q, zna_k, zna_v = split_columns(h @ w_in, IN_SIZES)
    cq, ckv, ckr, cf, cna_q, cna_k, cna_v = split_columns(hc @ w_in, IN_SIZES)

    mla_scale = (QK_NOPE + QK_ROPE) ** -0.5
    na_scale = NA_HEAD_DIM ** -0.5
    k_lat, v_lat = mla_keys_values(zkv, zkr, g_kv, w_ukv, rope)
    k_ctx, v_ctx = mla_keys_values(ckv, ckr, g_kv, w_ukv, None)
    q_lat = mla_queries(zq, g_q, w_uq, rope)
    o_mla = block_attention(q_lat, jnp.concatenate([k_ctx, k_lat], axis=1),
                            jnp.concatenate([v_ctx, v_lat], axis=1), mla_scale).reshape(B, S, MLA_WIDTH)
    o_f = fourier_mix(zf, w_fnet, b_fnet)
    na_k_ctx = cna_k.reshape(B, Lc, NA_HEADS, NA_HEAD_DIM)
    na_v_ctx = cna_v.reshape(B, Lc, NA_HEADS, NA_HEAD_DIM)
    o_na = neighbourhood_attention(zna_q.reshape(B, S, NA_HEADS, NA_HEAD_DIM),
                                   zna_k.reshape(B, S, NA_HEADS, NA_HEAD_DIM),
                                   zna_v.reshape(B, S, NA_HEADS, NA_HEAD_DIM),
                                   na_k_ctx, na_v_ctx, na_rpb)
    x = x + gate_a * merge_groups(o_mla, o_f, o_na, g_out, w_out)
    hf = modulate(rmsnorm(x, g_ffn), shift_f, scale_f)

    if not ctx_out:
        return x + gate_f * hier_moe(hf, w_rg, b_rg, w_re, b_re, w_gate, w_up, w_down), xc

    oc_mla = attend(mla_queries(cq, g_q, w_uq, None), k_ctx, v_ctx, mla_scale).reshape(B, Lc, MLA_WIDTH)
    oc_f = fourier_mix(cf, w_fnet, b_fnet)
    oc_na = attend(cna_q.reshape(B, Lc, NA_HEADS, NA_HEAD_DIM), na_k_ctx, na_v_ctx, na_scale).reshape(B, Lc, NA_WIDTH)
    xc = xc + gate_ac * merge_groups(oc_mla, oc_f, oc_na, g_out, w_out)
    hfc = modulate(rmsnorm(xc, g_ffn), shift_fc, scale_fc)
    f = hier_moe(jnp.concatenate([hfc, hf], axis=1), w_rg, b_rg, w_re, b_re, w_gate, w_up, w_down)
    return x + gate_f * f[:, Lc:], xc + gate_fc * f[:, :Lc]


def setup_inputs(seed: int = 0) -> dict:
    key = jax.random.key(seed)
    ks = jax.random.split(key, 26)

    def nrm(k, shape, fan_in, gain=1.0):
        return jax.random.normal(k, shape, jnp.float32) * (gain * fan_in ** -0.5)

    def gains(k, shape):
        return 1.0 + 0.05 * jax.random.normal(k, shape, jnp.float32)

    def small(k, shape, s=0.01):
        return s * jax.random.normal(k, shape, jnp.float32)

    L = DEPTH
    return {
        'x': jax.random.normal(ks[0], (BATCH, SEQ, D_MODEL), jnp.float32),
        'c': jax.random.normal(ks[1], (BATCH, D_MODEL), jnp.float32),
        'ctx': jax.random.normal(ks[2], (BATCH, CTX_LEN, D_MODEL), jnp.float32),
        'c_ctx': jax.random.normal(ks[3], (D_MODEL,), jnp.float32),
        'w_ada': nrm(ks[4], (L, D_MODEL, 6 * D_MODEL), D_MODEL, 0.5),
        'b_ada': small(ks[5], (L, 6 * D_MODEL), 0.02),
        'g_attn': gains(ks[6], (L, D_MODEL)),
        'g_ffn': gains(ks[7], (L, D_MODEL)),
        'w_in': nrm(ks[8], (L, D_MODEL, IN_COLS), D_MODEL),
        'g_q': gains(ks[9], (L, Q_LORA)),
        'w_uq': nrm(ks[10], (L, Q_LORA, MLA_HEADS * (QK_NOPE + QK_ROPE)), Q_LORA),
        'g_kv': gains(ks[11], (L, KV_LORA)),
        'w_ukv': nrm(ks[12], (L, KV_LORA, MLA_HEADS * (QK_NOPE + V_DIM)), KV_LORA),
        'w_fnet': nrm(ks[13], (L, FNET_WIDTH, FNET_WIDTH), FNET_WIDTH),
        'b_fnet': small(ks[14], (L, FNET_WIDTH)),
        'na_rpb': small(ks[15], (L, NA_HEADS, 2 * NA_KH_MAX - 1, 2 * NA_KW - 1), 0.2),
        'g_out': gains(ks[16], (L, MIX_WIDTH)),
        'w_out': nrm(ks[17], (L, MIX_WIDTH, D_MODEL), MIX_WIDTH),
        'w_rg': nrm(ks[18], (L, D_MODEL, N_GROUPS), D_MODEL),
        'b_rg': small(ks[19], (L, N_GROUPS)),
        'w_re': nrm(ks[20], (L, D_MODEL, N_EXPERTS), D_MODEL),
        'b_re': small(ks[21], (L, N_EXPERTS)),
        'w_gate': nrm(ks[22], (L, N_GROUPS, EXPERTS_PER_GROUP, D_MODEL, D_EXPERT), D_MODEL),
        'w_up': nrm(ks[23], (L, N_GROUPS, EXPERTS_PER_GROUP, D_MODEL, D_EXPERT), D_MODEL),
        'w_down': nrm(ks[24], (L, N_GROUPS, EXPERTS_PER_GROUP, D_EXPERT, D_MODEL), D_EXPERT),
        'g_final': gains(ks[25], (D_MODEL,)),
    }


def reference(x, c, ctx, c_ctx, w_ada, b_ada, g_attn, g_ffn, w_in, g_q, w_uq, g_kv, w_ukv,
              w_fnet, b_fnet, na_rpb, g_out, w_out, w_rg, b_rg, w_re, b_re, w_gate, w_up, w_down,
              g_final):
    rope = axial_rope(x.shape[1])
    xc = ctx
    for l in range(DEPTH):
        x, xc = layer(x, xc, c, c_ctx, w_ada[l], b_ada[l], g_attn[l], g_ffn[l], w_in[l],
                      g_q[l], w_uq[l], g_kv[l], w_ukv[l], w_fnet[l], b_fnet[l], na_rpb[l],
                      g_out[l], w_out[l], w_rg[l], b_rg[l], w_re[l], b_re[l],
                      w_gate[l], w_up[l], w_down[l], rope, l < DEPTH - 1)
    return rmsnorm(x, g_final)
```

```python
import functools

import numpy as np
import jax
import jax.numpy as jnp
from jax import lax
from jax.experimental import pallas as pl
from jax.experimental.pallas import tpu as pltpu

F32 = jnp.float32
BF16 = jnp.bfloat16

D = 2048
GRID_W = 64
CTX_LEN = 256
EPS = 1e-6
NEG_INF = -1e30
ROPE_THETA = 10000.0

V_DIM = 128
MLA_WIDTH = D // 2
MLA_HEADS = MLA_WIDTH // V_DIM
QK_NOPE = 128
QK_ROPE = 64
Q_LORA = D // 4
KV_LORA = D // 8
FNET_WIDTH = D // 4
FNET_GROUP_DIM = 128
FNET_GROUPS = FNET_WIDTH // FNET_GROUP_DIM
NA_WIDTH = D // 4
NA_HEAD_DIM = 128
NA_HEADS = NA_WIDTH // NA_HEAD_DIM
NA_KH_MAX = 8
NA_KW = 16
N_GROUPS = 4
EXPERTS_PER_GROUP = 8
N_EXPERTS = N_GROUPS * EXPERTS_PER_GROUP
D_EXPERT = D // 4

TM = 256
IN_EXT = 5 * 512 + 256 + 128
ROUTER_COLS = 128
VMEM_LIMIT = 56 * 1024 * 1024


def _cparams(sem):
    return pltpu.CompilerParams(dimension_semantics=sem, vmem_limit_bytes=VMEM_LIMIT)


def _rms(v, g):
    return v * lax.rsqrt(jnp.mean(v * v, axis=-1, keepdims=True) + EPS) * g


def _dot(a, b):
    return jnp.dot(a, b, preferred_element_type=F32)


def _dot_nt(a, b):
    return lax.dot_general(a, b, (((1,), (1,)), ((), ())), preferred_element_type=F32)


def _mod_kernel(c_ref, w_ref, b_ref, o_ref):
    c = c_ref[...]
    s = c * jax.nn.sigmoid(c)
    o_ref[0] = _dot(s.astype(BF16), w_ref[0].astype(BF16)) + b_ref[0]


def _modulation(cond8, w_ada, b_ada):
    L = w_ada.shape[0]
    tn = 1024
    return pl.pallas_call(
        _mod_kernel,
        grid=(L, 6 * D // tn),
        in_specs=[
            pl.BlockSpec((8, D), lambda l, n: (0, 0)),
            pl.BlockSpec((1, D, tn), lambda l, n: (l, 0, n)),
            pl.BlockSpec((1, 1, tn), lambda l, n: (l, 0, n)),
        ],
        out_specs=pl.BlockSpec((1, 8, tn), lambda l, n: (l, 0, n)),
        out_shape=jax.ShapeDtypeStruct((L, 8, 6 * D), F32),
        compiler_params=_cparams(("arbitrary", "arbitrary")),
        name="modulation",
    )(cond8, w_ada, b_ada.reshape(L, 1, 6 * D))


def _tile_of(b, j, j0):
    return b * 9 + j0 + j


def _otile(b, j, j0):
    return b * (9 - j0) + j


def _mod_row(j, b, j0):
    return jnp.where(j0 + j == 0, 4, b)


def _mod_spec(k, j0):
    return pl.BlockSpec((1, 1, D), lambda b, j: (_mod_row(j, b, j0) * 6 + k, 0, 0))


def _inproj_kernel(x_ref, sh_ref, sc_ref, g_ref, w_ref, zq_ref, zkv_ref, zf_ref, zna_ref):
    h = _rms(x_ref[...], g_ref[...])
    h = h * (1.0 + sc_ref[0]) + sh_ref[0]
    z = _dot(h.astype(BF16), w_ref[...])
    zq_ref[...] = z[:, 0:512]
    zf_ref[...] = z[:, 512:1024].astype(BF16)
    zna_ref[...] = z[:, 1024:2560].astype(BF16)
    zkv_ref[...] = z[:, 2560:IN_EXT]


def _in_projection(xs, mods, g_attn, w_in_ext, B):
    T = xs.shape[0]
    row = lambda b, j: (_tile_of(b, j, 0), 0)
    return pl.pallas_call(
        _inproj_kernel,
        grid=(B, 9),
        in_specs=[
            pl.BlockSpec((TM, D), row),
            _mod_spec(0, 0),
            _mod_spec(1, 0),
            pl.BlockSpec((1, D), lambda b, j: (0, 0)),
            pl.BlockSpec((D, IN_EXT), lambda b, j: (0, 0)),
        ],
        out_specs=[
            pl.BlockSpec((TM, 512), row),
            pl.BlockSpec((TM, 384), row),
            pl.BlockSpec((TM, 512), row),
            pl.BlockSpec((TM, 1536), row),
        ],
        out_shape=[
            jax.ShapeDtypeStruct((T, 512), F32),
            jax.ShapeDtypeStruct((T, 384), F32),
            jax.ShapeDtypeStruct((T, 512), BF16),
            jax.ShapeDtypeStruct((T, 1536), BF16),
        ],
        compiler_params=_cparams(("arbitrary", "arbitrary")),
        name="in_projection",
    )(xs, mods, mods, g_attn.reshape(1, D), w_in_ext)


def _rope(r, cos_t, sin_t):
    return r * cos_t + pltpu.roll(r, 32, 1) * sin_t


def _mlaproj_kernel(zq_ref, zkv_ref, gq_ref, gkv_ref, wq_ref, wkv_ref, cos_ref, sin_ref,
                    q_ref, k_ref, v_ref):
    cos_t = cos_ref[...]
    sin_t = sin_ref[...]
    q = _dot(_rms(zq_ref[...], gq_ref[...]).astype(BF16), wq_ref[...])
    kv = _dot(_rms(zkv_ref[:, 0:KV_LORA], gkv_ref[...]).astype(BF16), wkv_ref[...])
    k_rope = _rope(zkv_ref[:, KV_LORA:KV_LORA + 128], cos_t, sin_t).astype(BF16)
    for h in range(MLA_HEADS):
        c = h * 256
        q_ref[:, c:c + 128] = q[:, c:c + 128].astype(BF16)
        q_ref[:, c + 128:c + 256] = _rope(q[:, c + 128:c + 256], cos_t, sin_t).astype(BF16)
        k_ref[:, c:c + 128] = kv[:, c:c + 128].astype(BF16)
        k_ref[:, c + 128:c + 256] = k_rope
        v_ref[:, h * 128:(h + 1) * 128] = kv[:, c + 128:c + 256].astype(BF16)


def _mla_projection(zq, zkv, g_q, g_kv, w_uq_ext, w_ukv, cos_t, sin_t, B):
    T = zq.shape[0]
    row = lambda b, j: (_tile_of(b, j, 0), 0)
    const = lambda b, j: (0, 0)
    return pl.pallas_call(
        _mlaproj_kernel,
        grid=(B, 9),
        in_specs=[
            pl.BlockSpec((TM, 512), row),
            pl.BlockSpec((TM, 384), row),
            pl.BlockSpec((1, Q_LORA), const),
            pl.BlockSpec((1, KV_LORA), const),
            pl.BlockSpec((Q_LORA, 2048), const),
            pl.BlockSpec((KV_LORA, 2048), const),
            pl.BlockSpec((TM, 128), lambda b, j: (j, 0)),
            pl.BlockSpec((TM, 128), lambda b, j: (j, 0)),
        ],
        out_specs=[
            pl.BlockSpec((TM, 2048), row),
            pl.BlockSpec((TM, 2048), row),
            pl.BlockSpec((TM, 1024), row),
        ],
        out_shape=[
            jax.ShapeDtypeStruct((T, 2048), BF16),
            jax.ShapeDtypeStruct((T, 2048), BF16),
            jax.ShapeDtypeStruct((T, 1024), BF16),
        ],
        compiler_params=_cparams(("arbitrary", "arbitrary")),
        name="mla_projection",
    )(zq, zkv, g_q.reshape(1, -1), g_kv.reshape(1, -1), w_uq_ext, w_ukv, cos_t, sin_t)


def _softmax_pv(s, v):
    m = jnp.max(s, axis=-1, keepdims=True)
    p = jnp.exp(s - m)
    l = jnp.sum(p, axis=-1, keepdims=True)
    return _dot(p.astype(BF16), v) / l


def _mla_attn_kernel(q_ref, k_ref, v_ref, o_ref, *, j0, scale):
    j = pl.program_id(2) + j0

    def attend(nk):
        s = _dot_nt(q_ref[...], k_ref[0:nk, :]) * scale
        o_ref[...] = _softmax_pv(s, v_ref[0:nk, :])

    if j0 == 0:
        @pl.when(j == 0)
        def _():
            attend(CTX_LEN)

        @pl.when(j > 0)
        def _():
            attend(9 * TM)
    else:
        attend(9 * TM)


def _mla_attention(q, k, v, B, j0):
    T = B * (9 - j0) * TM
    scale = (QK_NOPE + QK_ROPE) ** -0.5
    return pl.pallas_call(
        functools.partial(_mla_attn_kernel, j0=j0, scale=scale),
        grid=(B, MLA_HEADS, 9 - j0),
        in_specs=[
            pl.BlockSpec((TM, 256), lambda b, h, j: (_tile_of(b, j, j0), h)),
            pl.BlockSpec((9 * TM, 256), lambda b, h, j: (b, h)),
            pl.BlockSpec((9 * TM, 128), lambda b, h, j: (b, h)),
        ],
        out_specs=pl.BlockSpec((TM, 128), lambda b, h, j: (_otile(b, j, j0), h)),
        out_shape=jax.ShapeDtypeStruct((T, MLA_WIDTH), F32),
        compiler_params=_cparams(("arbitrary", "arbitrary", "arbitrary")),
        name="mla_attention",
    )(q, k, v)


def _na_kernel(q_ref, k_ref, v_ref, bias_ref, o_ref, *, j0, scale):
    j = pl.program_id(2) + j0

    def ctx_queries():
        s = _dot_nt(q_ref[...], k_ref[0:CTX_LEN, :]) * scale
        o_ref[...] = _softmax_pv(s, v_ref[0:CTX_LEN, :])

    def grid_queries():
        k_ctx = k_ref[0:CTX_LEN, :]
        v_ctx = v_ref[0:CTX_LEN, :]
        rows = 9 * TM // GRID_W - CTX_LEN // GRID_W
        for i in range(TM // GRID_W):
            r = (j - 1) * (TM // GRID_W) + i
            r_start = jnp.clip(r - NA_KH_MAX // 2, 0, rows - NA_KH_MAX)
            start = pl.multiple_of(CTX_LEN + r_start * GRID_W, GRID_W)
            k_loc = k_ref[pl.ds(start, NA_KH_MAX * GRID_W), :]
            v_loc = v_ref[pl.ds(start, NA_KH_MAX * GRID_W), :]
            q = q_ref[i * GRID_W:(i + 1) * GRID_W, :]
            s_loc = _dot_nt(q, k_loc) * scale + bias_ref[0, i]
            s_ctx = _dot_nt(q, k_ctx) * scale
            m = jnp.maximum(jnp.max(s_loc, axis=-1, keepdims=True),
                            jnp.max(s_ctx, axis=-1, keepdims=True))
            p_loc = jnp.exp(s_loc - m)
            p_ctx = jnp.exp(s_ctx - m)
            l = jnp.sum(p_loc, axis=-1, keepdims=True) + jnp.sum(p_ctx, axis=-1, keepdims=True)
            o = _dot(p_loc.astype(BF16), v_loc) + _dot(p_ctx.astype(BF16), v_ctx)
            o_ref[i * GRID_W:(i + 1) * GRID_W, :] = o / l

    if j0 == 0:
        pl.when(j == 0)(ctx_queries)
        pl.when(j > 0)(grid_queries)
    else:
        grid_queries()


def _na_attention(zna, bias, B, j0):
    T = B * (9 - j0) * TM
    scale = NA_HEAD_DIM ** -0.5
    return pl.pallas_call(
        functools.partial(_na_kernel, j0=j0, scale=scale),
        grid=(B, NA_HEADS, 9 - j0),
        in_specs=[
            pl.BlockSpec((TM, 128), lambda b, h, j: (_tile_of(b, j, j0), h)),
            pl.BlockSpec((9 * TM, 128), lambda b, h, j: (b, NA_HEADS + h)),
            pl.BlockSpec((9 * TM, 128), lambda b, h, j: (b, 2 * NA_HEADS + h)),
            pl.BlockSpec((1, TM // GRID_W, GRID_W, NA_KH_MAX * GRID_W),
                         lambda b, h, j: (h, jnp.maximum(j0 + j - 1, 0), 0, 0)),
        ],
        out_specs=pl.BlockSpec((TM, 128), lambda b, h, j: (_otile(b, j, j0), h)),
        out_shape=jax.ShapeDtypeStruct((T, NA_WIDTH), F32),
        compiler_params=_cparams(("arbitrary", "arbitrary", "arbitrary")),
        name="na_attention",
    )(zna, zna, zna, bias)


def _na_bias(rpb):
    rows = 8 * TM // GRID_W
    kh = NA_KH_MAX
    r = np.arange(rows)
    row_start = np.clip(r - kh // 2, 0, rows - kh)
    row_idx = row_start[:, None] + np.arange(kh)[None, :]
    cq = np.arange(GRID_W)
    ck = np.arange(GRID_W)
    col_start = np.clip(cq - NA_KW // 2, 0, GRID_W - NA_KW)
    in_win = (ck[None, :] >= col_start[:, None]) & (ck[None, :] < col_start[:, None] + NA_KW)
    drow = row_idx - r[:, None] + (NA_KH_MAX - 1)
    dcol = np.clip(ck[None, :] - cq[:, None] + (NA_KW - 1), 0, 2 * NA_KW - 2)
    t = jnp.take(rpb.astype(F32), jnp.asarray(drow.reshape(-1)), axis=1)
    t = jnp.take(t, jnp.asarray(dcol.reshape(-1)), axis=2)
    t = t.reshape(NA_HEADS, rows, kh, GRID_W, GRID_W).transpose(0, 1, 3, 2, 4)
    t = jnp.where(jnp.asarray(in_win)[None, None, :, None, :], t, NEG_INF)
    return t.reshape(NA_HEADS, rows, GRID_W, kh * GRID_W)


def _fnet_kernel(z_ref, csl_ref, csc_ref, cd_ref, w_ref, b_ref, o_ref, ab_ref, *, j0, seq):
    j = pl.program_id(1) + j0

    def small_side(row0, length):
        for g in range(FNET_GROUPS):
            c = g * FNET_GROUP_DIM
            ab = _dot(z_ref[row0:row0 + length, c:c + FNET_GROUP_DIM], cd_ref[...])
            ab_ref[0:length, c:c + FNET_GROUP_DIM] = ab[:, 0:FNET_GROUP_DIM].astype(BF16)
            ab_ref[length:2 * length, c:c + FNET_GROUP_DIM] = ab[:, FNET_GROUP_DIM:].astype(BF16)

    def long_side(cs, length):
        f = _dot(cs, ab_ref[0:2 * length, :]) * (length * FNET_GROUP_DIM) ** -0.5
        o_ref[...] = _dot(f.astype(BF16), w_ref[...]) + b_ref[...]

    if j0 == 0:
        @pl.when(j == 0)
        def _():
            small_side(0, CTX_LEN)
            long_side(csc_ref[...], CTX_LEN)

    @pl.when(j == 1)
    def _():
        small_side(CTX_LEN, seq)

    @pl.when(j >= 1)
    def _():
        long_side(csl_ref[...], seq)


def _fnet(zf, cs_lat, cs_ctx, cd, w_fnet, b_fnet, B, j0):
    seq = 8 * TM
    return pl.pallas_call(
        functools.partial(_fnet_kernel, j0=j0, seq=seq),
        grid=(B, 9 - j0),
        in_specs=[
            pl.BlockSpec((9 * TM, FNET_WIDTH), lambda b, j: (b, 0)),
            pl.BlockSpec((TM, 2 * seq), lambda b, j: (jnp.maximum(j0 + j - 1, 0), 0)),
            pl.BlockSpec((CTX_LEN, 2 * CTX_LEN), lambda b, j: (0, 0)),
            pl.BlockSpec((FNET_GROUP_DIM, 2 * FNET_GROUP_DIM), lambda b, j: (0, 0)),
            pl.BlockSpec((FNET_WIDTH, FNET_WIDTH), lambda b, j: (0, 0)),
            pl.BlockSpec((1, FNET_WIDTH), lambda b, j: (0, 0)),
        ],
        out_specs=pl.BlockSpec((TM, FNET_WIDTH), lambda b, j: (_otile(b, j, j0), 0)),
        out_shape=jax.ShapeDtypeStruct((B * (9 - j0) * TM, FNET_WIDTH), F32),
        scratch_shapes=[pltpu.VMEM((2 * seq, FNET_WIDTH), BF16)],
        compiler_params=_cparams(("arbitrary", "arbitrary")),
        name="fnet",
    )(zf, cs_lat, cs_ctx, cd, w_fnet, b_fnet.reshape(1, -1))


def _dft_cos_sin(n):
    j = jnp.arange(n, dtype=jnp.int32)[:, None]
    if n <= 64:
        ang = ((j * j.T) % n).astype(F32) * (2.0 * np.pi / n)
        return jnp.cos(ang), jnp.sin(ang)
    k1 = jnp.arange(n // 64, dtype=jnp.int32)[None, :]
    k0 = jnp.arange(64, dtype=jnp.int32)[None, :]
    a = ((j * k1 * 64) % n).astype(F32) * (2.0 * np.pi / n)
    b = ((j * k0) % n).astype(F32) * (2.0 * np.pi / n)
    ca, sa, cb, sb = jnp.cos(a), jnp.sin(a), jnp.cos(b), jnp.sin(b)
    c = ca[:, :, None] * cb[:, None, :] - sa[:, :, None] * sb[:, None, :]
    s = sa[:, :, None] * cb[:, None, :] + ca[:, :, None] * sb[:, None, :]
    return c.reshape(n, n), s.reshape(n, n)


def _dft_tables(n):
    c, s = _dft_cos_sin(n)
    return jnp.concatenate([c, s], axis=1).astype(BF16)


def _merge_kernel(om_ref, of_ref, on_ref, x_ref, ga_ref, shf_ref, scf_ref, gout_ref, gffn_ref,
                  wout_ref, wr_ref, br_ref, xn_ref, hf_ref, lg_ref):
    ym = _rms(om_ref[...], gout_ref[:, 0:MLA_WIDTH]).astype(BF16)
    yf = _rms(of_ref[...], gout_ref[:, MLA_WIDTH:MLA_WIDTH + FNET_WIDTH]).astype(BF16)
    yn = _rms(on_ref[...], gout_ref[:, MLA_WIDTH + FNET_WIDTH:]).astype(BF16)
    acc = _dot(ym, wout_ref[0:MLA_WIDTH, :])
    acc = acc + _dot(yf, wout_ref[MLA_WIDTH:MLA_WIDTH + FNET_WIDTH, :])
    acc = acc + _dot(yn, wout_ref[MLA_WIDTH + FNET_WIDTH:, :])
    xn = x_ref[...] + ga_ref[0] * acc
    xn_ref[...] = xn
    hf = _rms(xn, gffn_ref[...]) * (1.0 + scf_ref[0]) + shf_ref[0]
    hf_ref[...] = hf
    lg_ref[...] = jnp.dot(hf, wr_ref[...], preferred_element_type=F32,
                          precision=lax.Precision.HIGHEST) + br_ref[...]


def _merge(o_mla, o_f, o_na, xs, mods, g_out, g_ffn, w_out, w_router, b_router, B, j0):
    T = B * (9 - j0) * TM
    row = lambda b, j: (_otile(b, j, j0), 0)
    const = lambda b, j: (0, 0)
    return pl.pallas_call(
        _merge_kernel,
        grid=(B, 9 - j0),
        in_specs=[
            pl.BlockSpec((TM, MLA_WIDTH), row),
            pl.BlockSpec((TM, FNET_WIDTH), row),
            pl.BlockSpec((TM, NA_WIDTH), row),
            pl.BlockSpec((TM, D), lambda b, j: (_tile_of(b, j, j0), 0)),
            _mod_spec(2, j0),
            _mod_spec(3, j0),
            _mod_spec(4, j0),
            pl.BlockSpec((1, D), const),
            pl.BlockSpec((1, D), const),
            pl.BlockSpec((D, D), const),
            pl.BlockSpec((D, ROUTER_COLS), const),
            pl.BlockSpec((1, ROUTER_COLS), const),
        ],
        out_specs=[
            pl.BlockSpec((TM, D), row),
            pl.BlockSpec((TM, D), row),
            pl.BlockSpec((TM, ROUTER_COLS), row),
        ],
        out_shape=[
            jax.ShapeDtypeStruct((T, D), F32),
            jax.ShapeDtypeStruct((T, D), F32),
            jax.ShapeDtypeStruct((T, ROUTER_COLS), F32),
        ],
        compiler_params=_cparams(("arbitrary", "arbitrary")),
        name="merge",
    )(o_mla, o_f, o_na, xs, mods, mods, mods, g_out.reshape(1, D), g_ffn.reshape(1, D),
      w_out, w_router, b_router)


def _route(logits, rows):
    n = logits.shape[0]
    g_prob = jax.nn.softmax(logits[:, :N_GROUPS], axis=-1)
    g_sel = jnp.argmax(g_prob, axis=-1)
    g_w = jnp.take_along_axis(g_prob, g_sel[:, None], axis=1)
    e_logits = logits[:, N_GROUPS:N_GROUPS + N_EXPERTS].reshape(n, N_GROUPS, EXPERTS_PER_GROUP)
    e_logits = jnp.take_along_axis(e_logits, g_sel[:, None, None], axis=1)[:, 0]
    e_prob = jax.nn.softmax(e_logits, axis=-1)
    top_p, top_i = lax.top_k(e_prob, 2)
    top_p = top_p / jnp.sum(top_p, axis=-1, keepdims=True)
    wts = (top_p * g_w).reshape(-1)
    eid = (g_sel[:, None] * EXPERTS_PER_GROUP + top_i).astype(jnp.int32).reshape(-1)

    onehot = (eid[:, None] == jnp.arange(N_EXPERTS, dtype=jnp.int32)[None, :]).astype(jnp.int32)
    csum = jnp.cumsum(onehot, axis=0)
    counts = csum[-1]
    rank = jnp.sum((csum - 1) * onehot, axis=1)
    tiles = (counts + TM - 1) // TM
    tile_end = jnp.cumsum(tiles)
    tile_start = tile_end - tiles
    slot = jnp.take(tile_start, eid) * TM + rank
    n_used = tile_end[-1]

    n_tiles = (2 * n) // TM + N_EXPERTS
    n_slots = n_tiles * TM
    src = jnp.zeros((n_slots,), jnp.int32).at[slot].set(jnp.repeat(rows, 2))
    w_sorted = jnp.zeros((n_slots,), F32).at[slot].set(wts)
    t_idx = jnp.minimum(jnp.arange(n_tiles, dtype=jnp.int32), n_used - 1)
    tile_expert = jnp.searchsorted(tile_end, t_idx, side="right").astype(jnp.int32)
    return tile_expert, n_used.reshape(1).astype(jnp.int32), src, w_sorted.reshape(-1, 1), slot


def _moe_kernel(te_ref, nu_ref, src_ref, hf_hbm, ws_ref, wg_ref, wu_ref, wd_ref, y_ref, xbuf, sem):
    i = pl.program_id(0)

    @pl.when(i < nu_ref[0])
    def _():
        base = i * TM

        def issue(r, carry):
            tok = src_ref[base + r]
            pltpu.make_async_copy(hf_hbm.at[pl.ds(tok, 1)], xbuf.at[pl.ds(r, 1)], sem).start()
            return carry

        lax.fori_loop(0, TM, issue, 0)
        pltpu.make_async_copy(hf_hbm.at[pl.ds(0, TM)], xbuf, sem).wait()
        x = xbuf[...].astype(BF16)
        a = _dot(x, wg_ref[0].astype(BF16))
        u = _dot(x, wu_ref[0].astype(BF16))
        h = (a * jax.nn.sigmoid(a)) * u * ws_ref[...]
        y_ref[...] = _dot(h.astype(BF16), wd_ref[0].astype(BF16))

    @pl.when(i >= nu_ref[0])
    def _():
        y_ref[...] = jnp.zeros_like(y_ref)


def _moe(hf, plan, w_gate, w_up, w_down):
    tile_expert, n_used, src, w_sorted, _ = plan
    n_tiles = tile_expert.shape[0]
    wmap = lambda i, te, nu, s: (te[i], 0, 0)
    rmap = lambda i, te, nu, s: (i, 0)
    grid_spec = pltpu.PrefetchScalarGridSpec(
        num_scalar_prefetch=3,
        grid=(n_tiles,),
        in_specs=[
            pl.BlockSpec(memory_space=pl.ANY),
            pl.BlockSpec((TM, 1), rmap),
            pl.BlockSpec((1, D, D_EXPERT), wmap),
            pl.BlockSpec((1, D, D_EXPERT), wmap),
            pl.BlockSpec((1, D_EXPERT, D), wmap),
        ],
        out_specs=pl.BlockSpec((TM, D), rmap),
        scratch_shapes=[pltpu.VMEM((TM, D), F32), pltpu.SemaphoreType.DMA],
    )
    return pl.pallas_call(
        _moe_kernel,
        grid_spec=grid_spec,
        out_shape=jax.ShapeDtypeStruct((n_tiles * TM, D), F32),
        compiler_params=_cparams(("arbitrary",)),
        name="moe_experts",
    )(tile_expert, n_used, src, hf, w_sorted,
      w_gate.reshape(N_EXPERTS, D, D_EXPERT), w_up.reshape(N_EXPERTS, D, D_EXPERT),
      w_down.reshape(N_EXPERTS, D_EXPERT, D))


def _combine_kernel(pos_ref, x_ref, gf_ref, g_ref, y_hbm, o_ref, ybuf, sem, *, j0, final):
    b = pl.program_id(0)
    j = pl.program_id(1)
    base = _otile(b, j, j0) * TM

    def issue(r, carry):
        p0 = pos_ref[2 * (base + r)]
        p1 = pos_ref[2 * (base + r) + 1]
        pltpu.make_async_copy(y_hbm.at[pl.ds(p0, 1)], ybuf.at[0, pl.ds(r, 1)], sem).start()
        pltpu.make_async_copy(y_hbm.at[pl.ds(p1, 1)], ybuf.at[1, pl.ds(r, 1)], sem).start()
        return carry

    lax.fori_loop(0, TM, issue, 0)
    pltpu.make_async_copy(y_hbm.at[pl.ds(0, TM)], ybuf.at[0], sem).wait()
    pltpu.make_async_copy(y_hbm.at[pl.ds(0, TM)], ybuf.at[1], sem).wait()
    out = x_ref[...] + gf_ref[0] * (ybuf[0] + ybuf[1])
    if final:
        o_ref[0] = _rms(out, g_ref[...])
    else:
        o_ref[...] = out


def _combine(pos, xn, mods, y, g_final, B, j0, final):
    assert final == (j0 == 1)
    if final:
        out_spec = pl.BlockSpec((1, TM, D), lambda b, j, p: (b, j, 0))
        out_shape = jax.ShapeDtypeStruct((B, 8 * TM, D), F32)
    else:
        out_spec = pl.BlockSpec((TM, D), lambda b, j, p: (_otile(b, j, j0), 0))
        out_shape = jax.ShapeDtypeStruct(xn.shape, F32)
    grid_spec = pltpu.PrefetchScalarGridSpec(
        num_scalar_prefetch=1,
        grid=(B, 9 - j0),
        in_specs=[
            pl.BlockSpec((TM, D), lambda b, j, p: (_otile(b, j, j0), 0)),
            pl.BlockSpec((1, 1, D), lambda b, j, p: (_mod_row(j, b, j0) * 6 + 5, 0, 0)),
            pl.BlockSpec((1, D), lambda b, j, p: (0, 0)),
            pl.BlockSpec(memory_space=pl.ANY),
        ],
        out_specs=out_spec,
        scratch_shapes=[pltpu.VMEM((2, TM, D), F32), pltpu.SemaphoreType.DMA],
    )
    return pl.pallas_call(
        functools.partial(_combine_kernel, j0=j0, final=final),
        grid_spec=grid_spec,
        out_shape=out_shape,
        compiler_params=_cparams(("arbitrary", "arbitrary")),
        name="combine_final" if final else "combine",
    )(pos, xn, mods, g_final.reshape(1, D), y)


def _deinterleave(n):
    half = np.concatenate([np.arange(0, n, 2), np.arange(1, n, 2)])
    return np.concatenate([half, half])


def _prep_w_in(w_in):
    sizes = (Q_LORA, KV_LORA, QK_ROPE, FNET_WIDTH, NA_WIDTH, NA_WIDTH, NA_WIDTH)
    starts = np.concatenate([[0], np.cumsum(sizes)])
    seg = lambda i: np.arange(starts[i], starts[i + 1])
    cols = np.concatenate([seg(0), seg(3), seg(4), seg(5), seg(6), seg(1),
                           starts[2] + _deinterleave(QK_ROPE)])
    return jnp.take(w_in, jnp.asarray(cols), axis=1).astype(BF16)


def _prep_w_uq(w_uq):
    per = QK_NOPE + QK_ROPE
    cols = np.concatenate([
        np.concatenate([h * per + np.arange(QK_NOPE), h * per + QK_NOPE + _deinterleave(QK_ROPE)])
        for h in range(MLA_HEADS)])
    return jnp.take(w_uq, jnp.asarray(cols), axis=1).astype(BF16)


def _rope_tables(seq):
    half = QK_ROPE // 2
    inv_freq = ROPE_THETA ** (-jnp.arange(0, half, 2, dtype=F32) / half)
    t = jnp.arange(seq, dtype=jnp.int32)
    row = (t // GRID_W).astype(F32)
    col = (t % GRID_W).astype(F32)
    ang = jnp.concatenate([row[:, None] * inv_freq, col[:, None] * inv_freq], axis=-1)
    cos, sin = jnp.cos(ang), jnp.sin(ang)
    zeros = jnp.zeros((seq, 64), F32)
    cos_l = jnp.concatenate([cos, cos, zeros], axis=1)
    sin_l = jnp.concatenate([-sin, sin, zeros], axis=1)
    cos_c = jnp.concatenate([jnp.ones((CTX_LEN, 64), F32), jnp.zeros((CTX_LEN, 64), F32)], axis=1)
    sin_c = jnp.zeros((CTX_LEN, 128), F32)
    return jnp.concatenate([cos_c, cos_l], axis=0), jnp.concatenate([sin_c, sin_l], axis=0)


def kernel(x, c, ctx, c_ctx, w_ada, b_ada, g_attn, g_ffn, w_in, g_q, w_uq, g_kv, w_ukv, w_fnet, b_fnet,
           na_rpb, g_out, w_out, w_rg, b_rg, w_re, b_re, w_gate, w_up, w_down, g_final):
    B, S, _ = x.shape
    L = w_ada.shape[0]
    assert ctx.shape[1] == CTX_LEN == TM and S == 8 * TM and B <= 4
    T = B * 9 * TM

    cond8 = jnp.zeros((8, D), F32).at[:B].set(c).at[4].set(c_ctx)
    mods_all = _modulation(cond8, w_ada, b_ada)
    cos_t, sin_t = _rope_tables(S)
    cs_lat = _dft_tables(S)
    cs_ctx = _dft_tables(CTX_LEN)
    cd_c, cd_s = _dft_cos_sin(FNET_GROUP_DIM)
    cd = jnp.concatenate([cd_c, -cd_s], axis=1).astype(BF16)

    xs = jnp.concatenate([ctx, x], axis=1).reshape(T, D)
    out = None
    for l in range(L):
        last = l == L - 1
        j0 = 1 if last else 0
        mods = mods_all[l].reshape(48, 1, D)
        w_in_ext = _prep_w_in(w_in[l])
        w_uq_ext = _prep_w_uq(w_uq[l])
        w_router = jnp.zeros((D, ROUTER_COLS), F32).at[:, :N_GROUPS].set(w_rg[l]) \
            .at[:, N_GROUPS:N_GROUPS + N_EXPERTS].set(w_re[l])
        b_router = jnp.zeros((1, ROUTER_COLS), F32).at[0, :N_GROUPS].set(b_rg[l]) \
            .at[0, N_GROUPS:N_GROUPS + N_EXPERTS].set(b_re[l])

        zq, zkv, zf, zna = _in_projection(xs, mods, g_attn[l], w_in_ext, B)
        q, k, v = _mla_projection(zq, zkv, g_q[l], g_kv[l], w_uq_ext, w_ukv[l].astype(BF16),
                                  cos_t, sin_t, B)
        o_mla = _mla_attention(q, k, v, B, j0)
        o_na = _na_attention(zna, _na_bias(na_rpb[l]), B, j0)
        w_f = w_fnet[l].astype(BF16)
        o_f = _fnet(zf, cs_lat, cs_ctx, cd, w_f, b_fnet[l], B, j0)
        xn, hf, logits = _merge(o_mla, o_f, o_na, xs, mods, g_out[l], g_ffn[l],
                                w_out[l].astype(BF16), w_router, b_router, B, j0)
        plan = _route(logits, jnp.arange(logits.shape[0], dtype=jnp.int32))
        y = _moe(hf, plan, w_gate[l], w_up[l], w_down[l])
        res = _combine(plan[4], xn, mods, y, g_final, B, j0, last)
        if last:
            out = res
        else:
            xs = res
    return out
```

```python
import functools

import numpy as np
import jax
import jax.numpy as jnp
from jax import lax
from jax.experimental import pallas as pl
from jax.experimental.pallas import tpu as pltpu

F32 = jnp.float32
BF16 = jnp.bfloat16

D = 2048
GRID_W = 64
CTX_LEN = 256
EPS = 1e-6
NEG_INF = -1e30
ROPE_THETA = 10000.0

V_DIM = 128
MLA_WIDTH = D // 2
MLA_HEADS = MLA_WIDTH // V_DIM
QK_NOPE = 128
QK_ROPE = 64
Q_LORA = D // 4
KV_LORA = D // 8
FNET_WIDTH = D // 4
FNET_GROUP_DIM = 128
FNET_GROUPS = FNET_WIDTH // FNET_GROUP_DIM
NA_WIDTH = D // 4
NA_HEAD_DIM = 128
NA_HEADS = NA_WIDTH // NA_HEAD_DIM
NA_KH_MAX = 8
NA_KW = 16
N_GROUPS = 4
EXPERTS_PER_GROUP = 8
N_EXPERTS = N_GROUPS * EXPERTS_PER_GROUP
D_EXPERT = D // 4

TM = 256
IN_EXT = 5 * 512 + 256 + 128
ROUTER_COLS = 128
VMEM_LIMIT = 56 * 1024 * 1024


def _cparams(sem):
    return pltpu.CompilerParams(dimension_semantics=sem, vmem_limit_bytes=VMEM_LIMIT)


def _rms(v, g):
    return v * lax.rsqrt(jnp.mean(v * v, axis=-1, keepdims=True) + EPS) * g


def _dot(a, b):
    return jnp.dot(a, b, preferred_element_type=F32)


def _dot_nt(a, b):
    return lax.dot_general(a, b, (((1,), (1,)), ((), ())), preferred_element_type=F32)


def _mod_kernel(c_ref, w_ref, b_ref, o_ref):
    c = c_ref[...]
    s = c * jax.nn.sigmoid(c)
    o_ref[0] = _dot(s.astype(BF16), w_ref[0].astype(BF16)) + b_ref[0]


def _modulation(cond8, w_ada, b_ada):
    L = w_ada.shape[0]
    tn = 1024
    return pl.pallas_call(
        _mod_kernel,
        grid=(L, 6 * D // tn),
        in_specs=[
            pl.BlockSpec((8, D), lambda l, n: (0, 0)),
            pl.BlockSpec((1, D, tn), lambda l, n: (l, 0, n)),
            pl.BlockSpec((1, 1, tn), lambda l, n: (l, 0, n)),
        ],
        out_specs=pl.BlockSpec((1, 8, tn), lambda l, n: (l, 0, n)),
        out_shape=jax.ShapeDtypeStruct((L, 8, 6 * D), F32),
        compiler_params=_cparams(("arbitrary", "arbitrary")),
        name="modulation",
    )(cond8, w_ada, b_ada.reshape(L, 1, 6 * D))


def _tile_of(b, j, j0):
    return b * 9 + j0 + j


def _otile(b, j, j0):
    return b * (9 - j0) + j


def _mod_row(j, b, j0):
    return jnp.where(j0 + j == 0, 4, b)


def _mod_spec(k, j0):
    return pl.BlockSpec((1, 1, D), lambda b, j: (_mod_row(j, b, j0) * 6 + k, 0, 0))


def _inproj_kernel(x_ref, sh_ref, sc_ref, g_ref, w_ref, zq_ref, zkv_ref, zf_ref, zna_ref):
    h = _rms(x_ref[...], g_ref[...])
    h = h * (1.0 + sc_ref[0]) + sh_ref[0]
    z = _dot(h.astype(BF16), w_ref[...])
    zq_ref[...] = z[:, 0:512]
    zf_ref[...] = z[:, 512:1024].astype(BF16)
    zna_ref[...] = z[:, 1024:2560].astype(BF16)
    zkv_ref[...] = z[:, 2560:IN_EXT]


def _in_projection(xs, mods, g_attn, w_in_ext, B):
    T = xs.shape[0]
    row = lambda b, j: (_tile_of(b, j, 0), 0)
    return pl.pallas_call(
        _inproj_kernel,
        grid=(B, 9),
        in_specs=[
            pl.BlockSpec((TM, D), row),
            _mod_spec(0, 0),
            _mod_spec(1, 0),
            pl.BlockSpec((1, D), lambda b, j: (0, 0)),
            pl.BlockSpec((D, IN_EXT), lambda b, j: (0, 0)),
        ],
        out_specs=[
            pl.BlockSpec((TM, 512), row),
            pl.BlockSpec((TM, 384), row),
            pl.BlockSpec((TM, 512), row),
            pl.BlockSpec((TM, 1536), row),
        ],
        out_shape=[
            jax.ShapeDtypeStruct((T, 512), F32),
            jax.ShapeDtypeStruct((T, 384), F32),
            jax.ShapeDtypeStruct((T, 512), BF16),
            jax.ShapeDtypeStruct((T, 1536), BF16),
        ],
        compiler_params=_cparams(("arbitrary", "arbitrary")),
        name="in_projection",
    )(xs, mods, mods, g_attn.reshape(1, D), w_in_ext)


def _rope(r, cos_t, sin_t):
    return r * cos_t + pltpu.roll(r, 32, 1) * sin_t


def _mlaproj_kernel(zq_ref, zkv_ref, gq_ref, gkv_ref, wq_ref, wkv_ref, cos_ref, sin_ref,
                    q_ref, k_ref, v_ref):
    cos_t = cos_ref[...]
    sin_t = sin_ref[...]
    q = _dot(_rms(zq_ref[...], gq_ref[...]).astype(BF16), wq_ref[...])
    kv = _dot(_rms(zkv_ref[:, 0:KV_LORA], gkv_ref[...]).astype(BF16), wkv_ref[...])
    k_rope = _rope(zkv_ref[:, KV_LORA:KV_LORA + 128], cos_t, sin_t).astype(BF16)
    for h in range(MLA_HEADS):
        c = h * 256
        q_ref[:, c:c + 128] = q[:, c:c + 128].astype(BF16)
        q_ref[:, c + 128:c + 256] = _rope(q[:, c + 128:c + 256], cos_t, sin_t).astype(BF16)
        k_ref[:, c:c + 128] = kv[:, c:c + 128].astype(BF16)
        k_ref[:, c + 128:c + 256] = k_rope
        v_ref[:, h * 128:(h + 1) * 128] = kv[:, c + 128:c + 256].astype(BF16)


def _mla_projection(zq, zkv, g_q, g_kv, w_uq_ext, w_ukv, cos_t, sin_t, B):
    T = zq.shape[0]
    row = lambda b, j: (_tile_of(b, j, 0), 0)
    const = lambda b, j: (0, 0)
    return pl.pallas_call(
        _mlaproj_kernel,
        grid=(B, 9),
        in_specs=[
            pl.BlockSpec((TM, 512), row),
            pl.BlockSpec((TM, 384), row),
            pl.BlockSpec((1, Q_LORA), const),
            pl.BlockSpec((1, KV_LORA), const),
            pl.BlockSpec((Q_LORA, 2048), const),
            pl.BlockSpec((KV_LORA, 2048), const),
            pl.BlockSpec((TM, 128), lambda b, j: (j, 0)),
            pl.BlockSpec((TM, 128), lambda b, j: (j, 0)),
        ],
        out_specs=[
            pl.BlockSpec((TM, 2048), row),
            pl.BlockSpec((TM, 2048), row),
            pl.BlockSpec((TM, 1024), row),
        ],
        out_shape=[
            jax.ShapeDtypeStruct((T, 2048), BF16),
            jax.ShapeDtypeStruct((T, 2048), BF16),
            jax.ShapeDtypeStruct((T, 1024), BF16),
        ],
        compiler_params=_cparams(("arbitrary", "arbitrary")),
        name="mla_projection",
    )(zq, zkv, g_q.reshape(1, -1), g_kv.reshape(1, -1), w_uq_ext, w_ukv, cos_t, sin_t)


def _softmax_pv(s, v):
    m = jnp.max(s, axis=-1, keepdims=True)
    p = jnp.exp(s - m)
    l = jnp.sum(p, axis=-1, keepdims=True)
    return _dot(p.astype(BF16), v) / l


def _mla_attn_kernel(q_ref, k_ref, v_ref, o_ref, *, j0, scale):
    j = pl.program_id(2) + j0

    def attend(nk):
        s = _dot_nt(q_ref[...], k_ref[0:nk, :]) * scale
        o_ref[...] = _softmax_pv(s, v_ref[0:nk, :])

    if j0 == 0:
        @pl.when(j == 0)
        def _():
            attend(CTX_LEN)

        @pl.when(j > 0)
        def _():
            attend(9 * TM)
    else:
        attend(9 * TM)


def _mla_attention(q, k, v, B, j0):
    T = B * (9 - j0) * TM
    scale = (QK_NOPE + QK_ROPE) ** -0.5
    return pl.pallas_call(
        functools.partial(_mla_attn_kernel, j0=j0, scale=scale),
        grid=(B, MLA_HEADS, 9 - j0),
        in_specs=[
            pl.BlockSpec((TM, 256), lambda b, h, j: (_tile_of(b, j, j0), h)),
            pl.BlockSpec((9 * TM, 256), lambda b, h, j: (b, h)),
            pl.BlockSpec((9 * TM, 128), lambda b, h, j: (b, h)),
        ],
        out_specs=pl.BlockSpec((TM, 128), lambda b, h, j: (_otile(b, j, j0), h)),
        out_shape=jax.ShapeDtypeStruct((T, MLA_WIDTH), F32),
        compiler_params=_cparams(("arbitrary", "arbitrary", "arbitrary")),
        name="mla_attention",
    )(q, k, v)


def _na_kernel(q_ref, k_ref, v_ref, bias_ref, o_ref, *, j0, scale):
    j = pl.program_id(2) + j0

    def ctx_queries():
        s = _dot_nt(q_ref[...], k_ref[0:CTX_LEN, :]) * scale
        o_ref[...] = _softmax_pv(s, v_ref[0:CTX_LEN, :])

    def grid_queries():
        k_ctx = k_ref[0:CTX_LEN, :]
        v_ctx = v_ref[0:CTX_LEN, :]
        rows = 9 * TM // GRID_W - CTX_LEN // GRID_W
        for i in range(TM // GRID_W):
            r = (j - 1) * (TM // GRID_W) + i
            r_start = jnp.clip(r - NA_KH_MAX // 2, 0, rows - NA_KH_MAX)
            start = pl.multiple_of(CTX_LEN + r_start * GRID_W, GRID_W)
            k_loc = k_ref[pl.ds(start, NA_KH_MAX * GRID_W), :]
            v_loc = v_ref[pl.ds(start, NA_KH_MAX * GRID_W), :]
            q = q_ref[i * GRID_W:(i + 1) * GRID_W, :]
            s_loc = _dot_nt(q, k_loc) * scale + bias_ref[0, r - r_start]
            s_ctx = _dot_nt(q, k_ctx) * scale
            m = jnp.maximum(jnp.max(s_loc, axis=-1, keepdims=True),
                            jnp.max(s_ctx, axis=-1, keepdims=True))
            p_loc = jnp.exp(s_loc - m)
            p_ctx = jnp.exp(s_ctx - m)
            l = jnp.sum(p_loc, axis=-1, keepdims=True) + jnp.sum(p_ctx, axis=-1, keepdims=True)
            o = _dot(p_loc.astype(BF16), v_loc) + _dot(p_ctx.astype(BF16), v_ctx)
            o_ref[i * GRID_W:(i + 1) * GRID_W, :] = o / l

    if j0 == 0:
        pl.when(j == 0)(ctx_queries)
        pl.when(j > 0)(grid_queries)
    else:
        grid_queries()


def _na_attention(zna, bias, B, j0):
    T = B * (9 - j0) * TM
    scale = NA_HEAD_DIM ** -0.5
    return pl.pallas_call(
        functools.partial(_na_kernel, j0=j0, scale=scale),
        grid=(B, NA_HEADS, 9 - j0),
        in_specs=[
            pl.BlockSpec((TM, 128), lambda b, h, j: (_tile_of(b, j, j0), h)),
            pl.BlockSpec((9 * TM, 128), lambda b, h, j: (b, NA_HEADS + h)),
            pl.BlockSpec((9 * TM, 128), lambda b, h, j: (b, 2 * NA_HEADS + h)),
            pl.BlockSpec((1, NA_KH_MAX, GRID_W, NA_KH_MAX * GRID_W), lambda b, h, j: (h, 0, 0, 0)),
        ],
        out_specs=pl.BlockSpec((TM, 128), lambda b, h, j: (_otile(b, j, j0), h)),
        out_shape=jax.ShapeDtypeStruct((T, NA_WIDTH), F32),
        compiler_params=_cparams(("arbitrary", "arbitrary", "arbitrary")),
        name="na_attention",
    )(zna, zna, zna, bias)


def _na_bias(rpb):
    kh = NA_KH_MAX
    p = np.arange(kh)
    drow = np.arange(kh)[None, :] - p[:, None] + (kh - 1)
    cq = np.arange(GRID_W)
    ck = np.arange(GRID_W)
    col_start = np.clip(cq - NA_KW // 2, 0, GRID_W - NA_KW)
    in_win = (ck[None, :] >= col_start[:, None]) & (ck[None, :] < col_start[:, None] + NA_KW)
    dcol = np.clip(ck[None, :] - cq[:, None] + (NA_KW - 1), 0, 2 * NA_KW - 2)
    t = jnp.take(rpb.astype(F32), jnp.asarray(drow.reshape(-1)), axis=2)
    t = jnp.take(t, jnp.asarray(dcol.reshape(-1)), axis=3)
    t = t.reshape(-1, NA_HEADS, kh, kh, GRID_W, GRID_W).transpose(0, 1, 2, 4, 3, 5)
    t = jnp.where(jnp.asarray(in_win)[None, None, None, :, None, :], t, NEG_INF)
    return t.reshape(-1, NA_HEADS, kh, GRID_W, kh * GRID_W)


def _fnet_kernel(z_ref, csl_ref, csc_ref, cd_ref, w_ref, b_ref, o_ref, ab_ref, *, j0, seq):
    j = pl.program_id(1) + j0

    def small_side(row0, length):
        for g in range(FNET_GROUPS):
            c = g * FNET_GROUP_DIM
            ab = _dot(z_ref[row0:row0 + length, c:c + FNET_GROUP_DIM], cd_ref[...])
            ab_ref[0:length, c:c + FNET_GROUP_DIM] = ab[:, 0:FNET_GROUP_DIM].astype(BF16)
            ab_ref[length:2 * length, c:c + FNET_GROUP_DIM] = ab[:, FNET_GROUP_DIM:].astype(BF16)

    def long_side(cs, length):
        f = _dot(cs, ab_ref[0:2 * length, :]) * (length * FNET_GROUP_DIM) ** -0.5
        o_ref[...] = _dot(f.astype(BF16), w_ref[...]) + b_ref[...]

    if j0 == 0:
        @pl.when(j == 0)
        def _():
            small_side(0, CTX_LEN)
            long_side(csc_ref[...], CTX_LEN)

    @pl.when(j == 1)
    def _():
        small_side(CTX_LEN, seq)

    @pl.when(j >= 1)
    def _():
        long_side(csl_ref[...], seq)


def _fnet(zf, cs_lat, cs_ctx, cd, w_fnet, b_fnet, B, j0):
    seq = 8 * TM
    return pl.pallas_call(
        functools.partial(_fnet_kernel, j0=j0, seq=seq),
        grid=(B, 9 - j0),
        in_specs=[
            pl.BlockSpec((9 * TM, FNET_WIDTH), lambda b, j: (b, 0)),
            pl.BlockSpec((TM, 2 * seq), lambda b, j: (jnp.maximum(j0 + j - 1, 0), 0)),
            pl.BlockSpec((CTX_LEN, 2 * CTX_LEN), lambda b, j: (0, 0)),
            pl.BlockSpec((FNET_GROUP_DIM, 2 * FNET_GROUP_DIM), lambda b, j: (0, 0)),
            pl.BlockSpec((FNET_WIDTH, FNET_WIDTH), lambda b, j: (0, 0)),
            pl.BlockSpec((1, FNET_WIDTH), lambda b, j: (0, 0)),
        ],
        out_specs=pl.BlockSpec((TM, FNET_WIDTH), lambda b, j: (_otile(b, j, j0), 0)),
        out_shape=jax.ShapeDtypeStruct((B * (9 - j0) * TM, FNET_WIDTH), F32),
        scratch_shapes=[pltpu.VMEM((2 * seq, FNET_WIDTH), BF16)],
        compiler_params=_cparams(("arbitrary", "arbitrary")),
        name="fnet",
    )(zf, cs_lat, cs_ctx, cd, w_fnet, b_fnet.reshape(1, -1))


def _dft_cos_sin(n):
    j = jnp.arange(n, dtype=jnp.int32)[:, None]
    if n <= 64:
        ang = ((j * j.T) % n).astype(F32) * (2.0 * np.pi / n)
        return jnp.cos(ang), jnp.sin(ang)
    k1 = jnp.arange(n // 64, dtype=jnp.int32)[None, :]
    k0 = jnp.arange(64, dtype=jnp.int32)[None, :]
    a = ((j * k1 * 64) % n).astype(F32) * (2.0 * np.pi / n)
    b = ((j * k0) % n).astype(F32) * (2.0 * np.pi / n)
    ca, sa, cb, sb = jnp.cos(a), jnp.sin(a), jnp.cos(b), jnp.sin(b)
    c = ca[:, :, None] * cb[:, None, :] - sa[:, :, None] * sb[:, None, :]
    s = sa[:, :, None] * cb[:, None, :] + ca[:, :, None] * sb[:, None, :]
    return c.reshape(n, n), s.reshape(n, n)


def _dft_tables(n):
    c, s = _dft_cos_sin(n)
    return jnp.concatenate([c, s], axis=1).astype(BF16)


def _merge_kernel(om_ref, of_ref, on_ref, x_ref, ga_ref, shf_ref, scf_ref, gout_ref, gffn_ref,
                  wout_ref, wr_ref, br_ref, xn_ref, hf_ref, lg_ref):
    ym = _rms(om_ref[...], gout_ref[:, 0:MLA_WIDTH]).astype(BF16)
    yf = _rms(of_ref[...], gout_ref[:, MLA_WIDTH:MLA_WIDTH + FNET_WIDTH]).astype(BF16)
    yn = _rms(on_ref[...], gout_ref[:, MLA_WIDTH + FNET_WIDTH:]).astype(BF16)
    acc = _dot(ym, wout_ref[0:MLA_WIDTH, :])
    acc = acc + _dot(yf, wout_ref[MLA_WIDTH:MLA_WIDTH + FNET_WIDTH, :])
    acc = acc + _dot(yn, wout_ref[MLA_WIDTH + FNET_WIDTH:, :])
    xn = x_ref[...] + ga_ref[0] * acc
    xn_ref[...] = xn
    hf = _rms(xn, gffn_ref[...]) * (1.0 + scf_ref[0]) + shf_ref[0]
    hf_ref[...] = hf
    lg_ref[...] = jnp.dot(hf, wr_ref[...], preferred_element_type=F32,
                          precision=lax.Precision.HIGHEST) + br_ref[...]


def _merge(o_mla, o_f, o_na, xs, mods, g_out, g_ffn, w_out, w_router, b_router, B, j0):
    T = B * (9 - j0) * TM
    row = lambda b, j: (_otile(b, j, j0), 0)
    const = lambda b, j: (0, 0)
    return pl.pallas_call(
        _merge_kernel,
        grid=(B, 9 - j0),
        in_specs=[
            pl.BlockSpec((TM, MLA_WIDTH), row),
            pl.BlockSpec((TM, FNET_WIDTH), row),
            pl.BlockSpec((TM, NA_WIDTH), row),
            pl.BlockSpec((TM, D), lambda b, j: (_tile_of(b, j, j0), 0)),
            _mod_spec(2, j0),
            _mod_spec(3, j0),
            _mod_spec(4, j0),
            pl.BlockSpec((1, D), const),
            pl.BlockSpec((1, D), const),
            pl.BlockSpec((D, D), const),
            pl.BlockSpec((D, ROUTER_COLS), const),
            pl.BlockSpec((1, ROUTER_COLS), const),
        ],
        out_specs=[
            pl.BlockSpec((TM, D), row),
            pl.BlockSpec((TM, D), row),
            pl.BlockSpec((TM, ROUTER_COLS), row),
        ],
        out_shape=[
            jax.ShapeDtypeStruct((T, D), F32),
            jax.ShapeDtypeStruct((T, D), F32),
            jax.ShapeDtypeStruct((T, ROUTER_COLS), F32),
        ],
        compiler_params=_cparams(("arbitrary", "arbitrary")),
        name="merge",
    )(o_mla, o_f, o_na, xs, mods, mods, mods, g_out.reshape(1, D), g_ffn.reshape(1, D),
      w_out, w_router, b_router)


def _route(logits):
    n = logits.shape[0]
    g_prob = jax.nn.softmax(logits[:, :N_GROUPS], axis=-1)
    g_sel = jnp.argmax(g_prob, axis=-1)
    g_w = jnp.take_along_axis(g_prob, g_sel[:, None], axis=1)
    e_logits = logits[:, N_GROUPS:N_GROUPS + N_EXPERTS].reshape(n, N_GROUPS, EXPERTS_PER_GROUP)
    e_logits = jnp.take_along_axis(e_logits, g_sel[:, None, None], axis=1)[:, 0]
    e_prob = jax.nn.softmax(e_logits, axis=-1)
    top_p, top_i = lax.top_k(e_prob, 2)
    top_p = top_p / jnp.sum(top_p, axis=-1, keepdims=True)
    wts = top_p * g_w
    eid = (g_sel[:, None] * EXPERTS_PER_GROUP + top_i).astype(jnp.int32).reshape(-1)

    onehot = (eid[:, None] == jnp.arange(N_EXPERTS, dtype=jnp.int32)[None, :]).astype(jnp.int32)
    csum = jnp.cumsum(onehot, axis=0)
    counts = csum[-1]
    rank = jnp.sum((csum - 1) * onehot, axis=1)
    g_end = jnp.cumsum(counts)
    g_start = g_end - counts
    slot = jnp.take(g_start, eid) + rank
    src = jnp.zeros((2 * n,), jnp.int32).at[slot].set(jnp.arange(2 * n, dtype=jnp.int32) // 2)

    n_tiles = (2 * n) // TM
    max_items = n_tiles + N_EXPERTS - 1
    tile_lo = jnp.arange(n_tiles, dtype=jnp.int32) * TM
    first_e = jnp.searchsorted(g_end, tile_lo, side="right").astype(jnp.int32)
    last_e = jnp.searchsorted(g_end, tile_lo + (TM - 1), side="right").astype(jnp.int32)
    items = last_e - first_e + 1
    item_end = jnp.cumsum(items)
    item_start = item_end - items
    n_items = item_end[-1]
    ii = jnp.minimum(jnp.arange(max_items, dtype=jnp.int32), n_items - 1)
    it_tile = jnp.searchsorted(item_end, ii, side="right").astype(jnp.int32)
    it_exp = jnp.take(first_e, it_tile) + ii - jnp.take(item_start, it_tile)
    it_lo = jnp.clip(jnp.take(g_start, it_exp) - it_tile * TM, 0, TM)
    it_hi = jnp.clip(jnp.take(g_end, it_exp) - it_tile * TM, 0, TM)
    first_visit = ii == jnp.take(item_start, it_tile)
    new_expert = jnp.concatenate([jnp.ones((1,), bool), it_exp[1:] != it_exp[:-1]])
    flags = first_visit.astype(jnp.int32) + 2 * new_expert.astype(jnp.int32)
    meta = (it_tile, it_exp.astype(jnp.int32), it_lo.astype(jnp.int32), it_hi.astype(jnp.int32), flags,
            n_items.reshape(1).astype(jnp.int32), src)
    return meta, slot, wts


def _moe_kernel(tile_ref, exp_ref, lo_ref, hi_ref, flag_ref, ni_ref, src_ref,
                hf_hbm, wg_ref, wu_ref, wd_ref, y_ref, xbuf, wgb, wub, wdb, sem, *, n_tiles):
    i = pl.program_id(0)
    t = tile_ref[i]
    slot = lax.rem(t, 2)

    def gather_start(tile, buf, unrolled):
        base = tile * TM

        def issue(r):
            tok = src_ref[base + r]
            pltpu.make_async_copy(hf_hbm.at[pl.ds(tok, 1)], xbuf.at[buf, pl.ds(r, 1)], sem.at[buf]).start()

        if unrolled:
            for r in range(TM):
                issue(r)
        else:
            lax.fori_loop(0, TM, lambda r, c: (issue(r), c)[1], 0)

    def gather_wait(buf):
        pltpu.make_async_copy(hf_hbm.at[pl.ds(0, TM)], xbuf.at[buf], sem.at[buf]).wait()

    @pl.when(i < ni_ref[0])
    def _():
        first_visit = (flag_ref[i] & 1) != 0
        new_expert = (flag_ref[i] & 2) != 0

        @pl.when(i == 0)
        def _():
            gather_start(0, 0, False)

        @pl.when(first_visit)
        def _():
            gather_wait(slot)

        @pl.when(jnp.logical_and(first_visit, t + 1 < n_tiles))
        def _():
            gather_start(t + 1, 1 - slot, True)

        @pl.when(new_expert)
        def _():
            wgb[...] = wg_ref[0].astype(BF16)
            wub[...] = wu_ref[0].astype(BF16)
            wdb[...] = wd_ref[0].astype(BF16)

        x = xbuf[slot].astype(BF16)
        a = _dot(x, wgb[...])
        u = _dot(x, wub[...])
        row = lax.broadcasted_iota(jnp.int32, (TM, 1), 0)
        mine = jnp.logical_and(row >= lo_ref[i], row < hi_ref[i])
        h = jnp.where(mine, (a * jax.nn.sigmoid(a)) * u, 0.0)
        yv = _dot(h.astype(BF16), wdb[...])

        @pl.when(first_visit)
        def _():
            y_ref[...] = yv

        @pl.when(jnp.logical_not(first_visit))
        def _():
            y_ref[...] += yv


def _moe(hf, meta, w_gate, w_up, w_down, layer):
    n_rows = meta[6].shape[0]
    n_tiles = n_rows // TM
    max_items = meta[0].shape[0]
    e0 = layer * N_EXPERTS
    wmap = lambda i, tile, exp, lo, hi, fl, ni, src: (e0 + exp[i], 0, 0)
    grid_spec = pltpu.PrefetchScalarGridSpec(
        num_scalar_prefetch=7,
        grid=(max_items,),
        in_specs=[
            pl.BlockSpec(memory_space=pl.ANY),
            pl.BlockSpec((1, D, D_EXPERT), wmap),
            pl.BlockSpec((1, D, D_EXPERT), wmap),
            pl.BlockSpec((1, D_EXPERT, D), wmap),
        ],
        out_specs=pl.BlockSpec((TM, D), lambda i, tile, *_: (tile[i], 0)),
        scratch_shapes=[
            pltpu.VMEM((2, TM, D), F32),
            pltpu.VMEM((D, D_EXPERT), BF16),
            pltpu.VMEM((D, D_EXPERT), BF16),
            pltpu.VMEM((D_EXPERT, D), BF16),
            pltpu.SemaphoreType.DMA((2,)),
        ],
    )
    return pl.pallas_call(
        functools.partial(_moe_kernel, n_tiles=n_tiles),
        grid_spec=grid_spec,
        out_shape=jax.ShapeDtypeStruct((n_rows, D), F32),
        compiler_params=_cparams(("arbitrary",)),
        name="moe_experts",
    )(*meta, hf, w_gate.reshape(-1, D, D_EXPERT), w_up.reshape(-1, D, D_EXPERT),
      w_down.reshape(-1, D_EXPERT, D))


def _combine_kernel(pos_ref, x_ref, gf_ref, w_ref, g_ref, y_hbm, o_ref, ybuf, sem, *, j0, final):
    b = pl.program_id(0)
    j = pl.program_id(1)
    base = _otile(b, j, j0) * TM

    def issue(r, carry):
        p0 = pos_ref[2 * (base + r)]
        p1 = pos_ref[2 * (base + r) + 1]
        pltpu.make_async_copy(y_hbm.at[pl.ds(p0, 1)], ybuf.at[0, pl.ds(r, 1)], sem).start()
        pltpu.make_async_copy(y_hbm.at[pl.ds(p1, 1)], ybuf.at[1, pl.ds(r, 1)], sem).start()
        return carry

    lax.fori_loop(0, TM, issue, 0)
    pltpu.make_async_copy(y_hbm.at[pl.ds(0, TM)], ybuf.at[0], sem).wait()
    pltpu.make_async_copy(y_hbm.at[pl.ds(0, TM)], ybuf.at[1], sem).wait()
    w = w_ref[...]
    out = x_ref[...] + gf_ref[0] * (w[:, 0:1] * ybuf[0] + w[:, 1:2] * ybuf[1])
    if final:
        o_ref[0] = _rms(out, g_ref[...])
    else:
        o_ref[...] = out


def _combine(pos, wts, xn, mods, y, g_final, B, j0, final):
    assert final == (j0 == 1)
    if final:
        out_spec = pl.BlockSpec((1, TM, D), lambda b, j, p: (b, j, 0))
        out_shape = jax.ShapeDtypeStruct((B, 8 * TM, D), F32)
    else:
        out_spec = pl.BlockSpec((TM, D), lambda b, j, p: (_otile(b, j, j0), 0))
        out_shape = jax.ShapeDtypeStruct(xn.shape, F32)
    grid_spec = pltpu.PrefetchScalarGridSpec(
        num_scalar_prefetch=1,
        grid=(B, 9 - j0),
        in_specs=[
            pl.BlockSpec((TM, D), lambda b, j, p: (_otile(b, j, j0), 0)),
            pl.BlockSpec((1, 1, D), lambda b, j, p: (_mod_row(j, b, j0) * 6 + 5, 0, 0)),
            pl.BlockSpec((TM, 2), lambda b, j, p: (_otile(b, j, j0), 0)),
            pl.BlockSpec((1, D), lambda b, j, p: (0, 0)),
            pl.BlockSpec(memory_space=pl.ANY),
        ],
        out_specs=out_spec,
        scratch_shapes=[pltpu.VMEM((2, TM, D), F32), pltpu.SemaphoreType.DMA],
    )
    return pl.pallas_call(
        functools.partial(_combine_kernel, j0=j0, final=final),
        grid_spec=grid_spec,
        out_shape=out_shape,
        compiler_params=_cparams(("arbitrary", "arbitrary")),
        name="combine_final" if final else "combine",
    )(pos, xn, mods, wts, g_final.reshape(1, D), y)


def _deinterleave(n):
    half = np.concatenate([np.arange(0, n, 2), np.arange(1, n, 2)])
    return np.concatenate([half, half])


def _prep_w_in(w_in):
    sizes = (Q_LORA, KV_LORA, QK_ROPE, FNET_WIDTH, NA_WIDTH, NA_WIDTH, NA_WIDTH)
    starts = np.concatenate([[0], np.cumsum(sizes)])
    seg = lambda i: np.arange(starts[i], starts[i + 1])
    cols = np.concatenate([seg(0), seg(3), seg(4), seg(5), seg(6), seg(1),
                           starts[2] + _deinterleave(QK_ROPE)])
    return jnp.take(w_in, jnp.asarray(cols), axis=1).astype(BF16)


def _prep_w_uq(w_uq):
    per = QK_NOPE + QK_ROPE
    cols = np.concatenate([
        np.concatenate([h * per + np.arange(QK_NOPE), h * per + QK_NOPE + _deinterleave(QK_ROPE)])
        for h in range(MLA_HEADS)])
    return jnp.take(w_uq, jnp.asarray(cols), axis=1).astype(BF16)


def _rope_tables(seq):
    half = QK_ROPE // 2
    inv_freq = ROPE_THETA ** (-jnp.arange(0, half, 2, dtype=F32) / half)
    t = jnp.arange(seq, dtype=jnp.int32)
    row = (t // GRID_W).astype(F32)
    col = (t % GRID_W).astype(F32)
    ang = jnp.concatenate([row[:, None] * inv_freq, col[:, None] * inv_freq], axis=-1)
    cos, sin = jnp.cos(ang), jnp.sin(ang)
    zeros = jnp.zeros((seq, 64), F32)
    cos_l = jnp.concatenate([cos, cos, zeros], axis=1)
    sin_l = jnp.concatenate([-sin, sin, zeros], axis=1)
    cos_c = jnp.concatenate([jnp.ones((CTX_LEN, 64), F32), jnp.zeros((CTX_LEN, 64), F32)], axis=1)
    sin_c = jnp.zeros((CTX_LEN, 128), F32)
    return jnp.concatenate([cos_c, cos_l], axis=0), jnp.concatenate([sin_c, sin_l], axis=0)


def kernel(x, c, ctx, c_ctx, w_ada, b_ada, g_attn, g_ffn, w_in, g_q, w_uq, g_kv, w_ukv, w_fnet, b_fnet,
           na_rpb, g_out, w_out, w_rg, b_rg, w_re, b_re, w_gate, w_up, w_down, g_final):
    B, S, _ = x.shape
    L = w_ada.shape[0]
    assert ctx.shape[1] == CTX_LEN == TM and S == 8 * TM and B <= 4
    T = B * 9 * TM

    cond8 = jnp.zeros((8, D), F32).at[:B].set(c).at[4].set(c_ctx)
    mods_all = _modulation(cond8, w_ada, b_ada)
    na_bias = _na_bias(na_rpb)
    cos_t, sin_t = _rope_tables(S)
    cs_lat = _dft_tables(S)
    cs_ctx = _dft_tables(CTX_LEN)
    cd_c, cd_s = _dft_cos_sin(FNET_GROUP_DIM)
    cd = jnp.concatenate([cd_c, -cd_s], axis=1).astype(BF16)

    xs = jnp.concatenate([ctx, x], axis=1).reshape(T, D)
    out = None
    for l in range(L):
        last = l == L - 1
        j0 = 1 if last else 0
        mods = mods_all[l].reshape(48, 1, D)
        w_in_ext = _prep_w_in(w_in[l])
        w_uq_ext = _prep_w_uq(w_uq[l])
        w_router = jnp.zeros((D, ROUTER_COLS), F32).at[:, :N_GROUPS].set(w_rg[l]) \
            .at[:, N_GROUPS:N_GROUPS + N_EXPERTS].set(w_re[l])
        b_router = jnp.zeros((1, ROUTER_COLS), F32).at[0, :N_GROUPS].set(b_rg[l]) \
            .at[0, N_GROUPS:N_GROUPS + N_EXPERTS].set(b_re[l])

        zq, zkv, zf, zna = _in_projection(xs, mods, g_attn[l], w_in_ext, B)
        q, k, v = _mla_projection(zq, zkv, g_q[l], g_kv[l], w_uq_ext, w_ukv[l].astype(BF16),
                                  cos_t, sin_t, B)
        o_mla = _mla_attention(q, k, v, B, j0)
        o_na = _na_attention(zna, na_bias[l], B, j0)
        w_f = w_fnet[l].astype(BF16)
        o_f = _fnet(zf, cs_lat, cs_ctx, cd, w_f, b_fnet[l], B, j0)
        xn, hf, logits = _merge(o_mla, o_f, o_na, xs, mods, g_out[l], g_ffn[l],
                                w_out[l].astype(BF16), w_router, b_router, B, j0)
        meta, slot, wts = _route(logits)
        y = _moe(hf, meta, w_gate, w_up, w_down, l)
        res = _combine(slot, wts, xn, mods, y, g_final, B, j0, last)
        if last:
            out = res
        else:
            xs = res
    return out
```

```python
import functools

import numpy as np
import jax
import jax.numpy as jnp
from jax import lax
from jax.experimental import pallas as pl
from jax.experimental.pallas import tpu as pltpu

F32 = jnp.float32
BF16 = jnp.bfloat16

D = 2048
GRID_W = 64
CTX_LEN = 256
EPS = 1e-6
NEG_INF = -1e30
ROPE_THETA = 10000.0

V_DIM = 128
MLA_WIDTH = D // 2
MLA_HEADS = MLA_WIDTH // V_DIM
QK_NOPE = 128
QK_ROPE = 64
Q_LORA = D // 4
KV_LORA = D // 8
FNET_WIDTH = D // 4
FNET_GROUP_DIM = 128
FNET_GROUPS = FNET_WIDTH // FNET_GROUP_DIM
NA_WIDTH = D // 4
NA_HEAD_DIM = 128
NA_HEADS = NA_WIDTH // NA_HEAD_DIM
NA_KH_MAX = 8
NA_KW = 16
N_GROUPS = 4
EXPERTS_PER_GROUP = 8
N_EXPERTS = N_GROUPS * EXPERTS_PER_GROUP
D_EXPERT = D // 4

MLA_QSCALE = (QK_NOPE + QK_ROPE) ** -0.5 * float(np.log2(np.e))
TM = 256
IN_EXT = 5 * 512 + 256 + 128
ROUTER_COLS = 128
VMEM_LIMIT = 56 * 1024 * 1024


def _cparams(sem):
    return pltpu.CompilerParams(dimension_semantics=sem, vmem_limit_bytes=VMEM_LIMIT)


def _rms(v, g):
    return v * lax.rsqrt(jnp.mean(v * v, axis=-1, keepdims=True) + EPS) * g


def _dot(a, b):
    return jnp.dot(a, b, preferred_element_type=F32)


def _dot_nt(a, b):
    return lax.dot_general(a, b, (((1,), (1,)), ((), ())), preferred_element_type=F32)


def _mod_kernel(c_ref, w_ref, b_ref, o_ref):
    c = c_ref[...]
    s = c * jax.nn.sigmoid(c)
    o_ref[0] = _dot(s.astype(BF16), w_ref[0].astype(BF16)) + b_ref[0]


def _modulation(cond8, w_ada, b_ada):
    L = w_ada.shape[0]
    tn = 1024
    return pl.pallas_call(
        _mod_kernel,
        grid=(L, 6 * D // tn),
        in_specs=[
            pl.BlockSpec((8, D), lambda l, n: (0, 0)),
            pl.BlockSpec((1, D, tn), lambda l, n: (l, 0, n)),
            pl.BlockSpec((1, 1, tn), lambda l, n: (l, 0, n)),
        ],
        out_specs=pl.BlockSpec((1, 8, tn), lambda l, n: (l, 0, n)),
        out_shape=jax.ShapeDtypeStruct((L, 8, 6 * D), F32),
        compiler_params=_cparams(("arbitrary", "arbitrary")),
        name="modulation",
    )(cond8, w_ada, b_ada.reshape(L, 1, 6 * D))


def _tile_of(b, j, j0):
    return b * 9 + j0 + j


def _otile(b, j, j0):
    return b * (9 - j0) + j


def _mod_row(j, b, j0):
    return jnp.where(j0 + j == 0, 4, b)


def _mod_spec(k, j0):
    return pl.BlockSpec((1, 1, D), lambda b, j: (_mod_row(j, b, j0) * 6 + k, 0, 0))


def _inproj_kernel(x_ref, sh_ref, sc_ref, g_ref, w_ref, zq_ref, zkv_ref, zf_ref, zna_ref):
    h = _rms(x_ref[...], g_ref[...])
    h = h * (1.0 + sc_ref[0]) + sh_ref[0]
    z = _dot(h.astype(BF16), w_ref[...])
    zq_ref[...] = z[:, 0:512]
    zf_ref[...] = z[:, 512:1024].astype(BF16)
    zna_ref[...] = z[:, 1024:2560].astype(BF16)
    zkv_ref[...] = z[:, 2560:IN_EXT]


def _in_projection(xs, mods, g_attn, w_in_ext, B):
    T = xs.shape[0]
    row = lambda b, j: (_tile_of(b, j, 0), 0)
    return pl.pallas_call(
        _inproj_kernel,
        grid=(B, 9),
        in_specs=[
            pl.BlockSpec((TM, D), row),
            _mod_spec(0, 0),
            _mod_spec(1, 0),
            pl.BlockSpec((1, D), lambda b, j: (0, 0)),
            pl.BlockSpec((D, IN_EXT), lambda b, j: (0, 0)),
        ],
        out_specs=[
            pl.BlockSpec((TM, 512), row),
            pl.BlockSpec((TM, 384), row),
            pl.BlockSpec((TM, 512), row),
            pl.BlockSpec((TM, 1536), row),
        ],
        out_shape=[
            jax.ShapeDtypeStruct((T, 512), F32),
            jax.ShapeDtypeStruct((T, 384), F32),
            jax.ShapeDtypeStruct((T, 512), BF16),
            jax.ShapeDtypeStruct((T, 1536), BF16),
        ],
        compiler_params=_cparams(("arbitrary", "arbitrary")),
        name="in_projection",
    )(xs, mods, mods, g_attn.reshape(1, D), w_in_ext)


def _rope(r, cos_t, sin_t):
    return r * cos_t + pltpu.roll(r, 32, 1) * sin_t


def _mlaproj_kernel(zq_ref, zkv_ref, gq_ref, gkv_ref, wq_ref, wkv_ref, cos_ref, sin_ref,
                    q_ref, k_ref, vt_ref):
    cos_t = cos_ref[...]
    sin_t = sin_ref[...]
    q = _dot(_rms(zq_ref[...], gq_ref[...]).astype(BF16), wq_ref[...]) * MLA_QSCALE
    kv = _dot(_rms(zkv_ref[:, 0:KV_LORA], gkv_ref[...]).astype(BF16), wkv_ref[...])
    k_rope = _rope(zkv_ref[:, KV_LORA:KV_LORA + 128], cos_t, sin_t).astype(BF16)
    for h in range(MLA_HEADS):
        c = h * 256
        q_ref[:, c:c + 128] = q[:, c:c + 128].astype(BF16)
        q_ref[:, c + 128:c + 256] = _rope(q[:, c + 128:c + 256], cos_t, sin_t).astype(BF16)
        k_ref[:, c:c + 128] = kv[:, c:c + 128].astype(BF16)
        k_ref[:, c + 128:c + 256] = k_rope
        vt_ref[h * 128:(h + 1) * 128, :] = kv[:, c + 128:c + 256].T.astype(BF16)


def _mla_projection(zq, zkv, g_q, g_kv, w_uq_ext, w_ukv, cos_t, sin_t, B):
    T = zq.shape[0]
    row = lambda b, j: (_tile_of(b, j, 0), 0)
    const = lambda b, j: (0, 0)
    return pl.pallas_call(
        _mlaproj_kernel,
        grid=(B, 9),
        in_specs=[
            pl.BlockSpec((TM, 512), row),
            pl.BlockSpec((TM, 384), row),
            pl.BlockSpec((1, Q_LORA), const),
            pl.BlockSpec((1, KV_LORA), const),
            pl.BlockSpec((Q_LORA, 2048), const),
            pl.BlockSpec((KV_LORA, 2048), const),
            pl.BlockSpec((TM, 128), lambda b, j: (j, 0)),
            pl.BlockSpec((TM, 128), lambda b, j: (j, 0)),
        ],
        out_specs=[
            pl.BlockSpec((TM, 2048), row),
            pl.BlockSpec((TM, 2048), row),
            pl.BlockSpec((MLA_WIDTH, TM), lambda b, j: (0, _tile_of(b, j, 0))),
        ],
        out_shape=[
            jax.ShapeDtypeStruct((T, 2048), BF16),
            jax.ShapeDtypeStruct((T, 2048), BF16),
            jax.ShapeDtypeStruct((MLA_WIDTH, T), BF16),
        ],
        compiler_params=_cparams(("arbitrary", "arbitrary")),
        name="mla_projection",
    )(zq, zkv, g_q.reshape(1, -1), g_kv.reshape(1, -1), w_uq_ext, w_ukv, cos_t, sin_t)


def _softmax_pv(s, v):
    m = jnp.max(s, axis=-1, keepdims=True)
    p = jnp.exp(s - m)
    l = jnp.sum(p, axis=-1, keepdims=True)
    return _dot(p.astype(BF16), v) / l


def _mla_attn_kernel(q_ref, k_ref, vt_ref, o_ref, *, j0):
    def attend(q0, nk, o0):
        st = _dot_nt(k_ref[0:nk, :], q_ref[q0:q0 + TM, :])
        m = jnp.max(st, axis=0, keepdims=True)
        p = jnp.exp2(st - m)
        l = jnp.sum(p, axis=0, keepdims=True)
        ot = _dot(vt_ref[:, 0:nk], p.astype(BF16)) / l
        o_ref[o0:o0 + TM, :] = ot.T

    if j0 == 0:
        attend(0, CTX_LEN, 0)
    for c in range(8):
        attend(CTX_LEN + c * TM, 9 * TM, (1 - j0 + c) * TM)


def _mla_attention(q, k, vt, B, j0):
    rows = (9 - j0) * TM
    return pl.pallas_call(
        functools.partial(_mla_attn_kernel, j0=j0),
        grid=(B, MLA_HEADS),
        in_specs=[
            pl.BlockSpec((9 * TM, 256), lambda b, h: (b, h)),
            pl.BlockSpec((9 * TM, 256), lambda b, h: (b, h)),
            pl.BlockSpec((V_DIM, 9 * TM), lambda b, h: (h, b)),
        ],
        out_specs=pl.BlockSpec((rows, V_DIM), lambda b, h: (b, h)),
        out_shape=jax.ShapeDtypeStruct((B * rows, MLA_WIDTH), F32),
        compiler_params=_cparams(("arbitrary", "arbitrary")),
        name="mla_attention",
    )(q, k, vt)


def _na_kernel(q_ref, k_ref, v_ref, bias_ref, o_ref, *, j0, scale):
    j = pl.program_id(2) + j0

    def ctx_queries():
        s = _dot_nt(q_ref[...], k_ref[0:CTX_LEN, :]) * scale
        o_ref[...] = _softmax_pv(s, v_ref[0:CTX_LEN, :])

    def grid_queries():
        k_ctx = k_ref[0:CTX_LEN, :]
        v_ctx = v_ref[0:CTX_LEN, :]
        rows = 9 * TM // GRID_W - CTX_LEN // GRID_W
        for i in range(TM // GRID_W):
            r = (j - 1) * (TM // GRID_W) + i
            r_start = jnp.clip(r - NA_KH_MAX // 2, 0, rows - NA_KH_MAX)
            start = pl.multiple_of(CTX_LEN + r_start * GRID_W, GRID_W)
            k_loc = k_ref[pl.ds(start, NA_KH_MAX * GRID_W), :]
            v_loc = v_ref[pl.ds(start, NA_KH_MAX * GRID_W), :]
            q = q_ref[i * GRID_W:(i + 1) * GRID_W, :]
            s_loc = _dot_nt(q, k_loc) * scale + bias_ref[0, r - r_start]
            s_ctx = _dot_nt(q, k_ctx) * scale
            m = jnp.maximum(jnp.max(s_loc, axis=-1, keepdims=True),
                            jnp.max(s_ctx, axis=-1, keepdims=True))
            p_loc = jnp.exp(s_loc - m)
            p_ctx = jnp.exp(s_ctx - m)
            l = jnp.sum(p_loc, axis=-1, keepdims=True) + jnp.sum(p_ctx, axis=-1, keepdims=True)
            o = _dot(p_loc.astype(BF16), v_loc) + _dot(p_ctx.astype(BF16), v_ctx)
            o_ref[i * GRID_W:(i + 1) * GRID_W, :] = o / l

    if j0 == 0:
        pl.when(j == 0)(ctx_queries)
        pl.when(j > 0)(grid_queries)
    else:
        grid_queries()


def _na_attention(zna, bias, B, j0):
    T = B * (9 - j0) * TM
    scale = NA_HEAD_DIM ** -0.5
    return pl.pallas_call(
        functools.partial(_na_kernel, j0=j0, scale=scale),
        grid=(B, NA_HEADS, 9 - j0),
        in_specs=[
            pl.BlockSpec((TM, 128), lambda b, h, j: (_tile_of(b, j, j0), h)),
            pl.BlockSpec((9 * TM, 128), lambda b, h, j: (b, NA_HEADS + h)),
            pl.BlockSpec((9 * TM, 128), lambda b, h, j: (b, 2 * NA_HEADS + h)),
            pl.BlockSpec((1, NA_KH_MAX, GRID_W, NA_KH_MAX * GRID_W), lambda b, h, j: (h, 0, 0, 0)),
        ],
        out_specs=pl.BlockSpec((TM, 128), lambda b, h, j: (_otile(b, j, j0), h)),
        out_shape=jax.ShapeDtypeStruct((T, NA_WIDTH), F32),
        compiler_params=_cparams(("arbitrary", "arbitrary", "arbitrary")),
        name="na_attention",
    )(zna, zna, zna, bias)


def _na_bias(rpb):
    kh = NA_KH_MAX
    p = np.arange(kh)
    drow = np.arange(kh)[None, :] - p[:, None] + (kh - 1)
    cq = np.arange(GRID_W)
    ck = np.arange(GRID_W)
    col_start = np.clip(cq - NA_KW // 2, 0, GRID_W - NA_KW)
    in_win = (ck[None, :] >= col_start[:, None]) & (ck[None, :] < col_start[:, None] + NA_KW)
    dcol = np.clip(ck[None, :] - cq[:, None] + (NA_KW - 1), 0, 2 * NA_KW - 2)
    t = jnp.take(rpb.astype(F32), jnp.asarray(drow.reshape(-1)), axis=2)
    t = jnp.take(t, jnp.asarray(dcol.reshape(-1)), axis=3)
    t = t.reshape(-1, NA_HEADS, kh, kh, GRID_W, GRID_W).transpose(0, 1, 2, 4, 3, 5)
    t = jnp.where(jnp.asarray(in_win)[None, None, None, :, None, :], t, NEG_INF)
    return t.reshape(-1, NA_HEADS, kh, GRID_W, kh * GRID_W)


def _fnet_kernel(z_ref, csl_ref, csc_ref, cd_ref, w_ref, b_ref, o_ref, ab_ref, *, j0, seq):
    j = pl.program_id(1) + j0

    def small_side(row0, length):
        for g in range(FNET_GROUPS):
            c = g * FNET_GROUP_DIM
            ab = _dot(z_ref[row0:row0 + length, c:c + FNET_GROUP_DIM], cd_ref[...])
            ab_ref[0:length, c:c + FNET_GROUP_DIM] = ab[:, 0:FNET_GROUP_DIM].astype(BF16)
            ab_ref[length:2 * length, c:c + FNET_GROUP_DIM] = ab[:, FNET_GROUP_DIM:].astype(BF16)

    def long_side(cs, length):
        f = _dot(cs, ab_ref[0:2 * length, :]) * (length * FNET_GROUP_DIM) ** -0.5
        o_ref[...] = _dot(f.astype(BF16), w_ref[...]) + b_ref[...]

    if j0 == 0:
        @pl.when(j == 0)
        def _():
            small_side(0, CTX_LEN)
            long_side(csc_ref[...], CTX_LEN)

    @pl.when(j == 1)
    def _():
        small_side(CTX_LEN, seq)

    @pl.when(j >= 1)
    def _():
        long_side(csl_ref[...], seq)


def _fnet(zf, cs_lat, cs_ctx, cd, w_fnet, b_fnet, B, j0):
    seq = 8 * TM
    return pl.pallas_call(
        functools.partial(_fnet_kernel, j0=j0, seq=seq),
        grid=(B, 9 - j0),
        in_specs=[
            pl.BlockSpec((9 * TM, FNET_WIDTH), lambda b, j: (b, 0)),
            pl.BlockSpec((TM, 2 * seq), lambda b, j: (jnp.maximum(j0 + j - 1, 0), 0)),
            pl.BlockSpec((CTX_LEN, 2 * CTX_LEN), lambda b, j: (0, 0)),
            pl.BlockSpec((FNET_GROUP_DIM, 2 * FNET_GROUP_DIM), lambda b, j: (0, 0)),
            pl.BlockSpec((FNET_WIDTH, FNET_WIDTH), lambda b, j: (0, 0)),
            pl.BlockSpec((1, FNET_WIDTH), lambda b, j: (0, 0)),
        ],
        out_specs=pl.BlockSpec((TM, FNET_WIDTH), lambda b, j: (_otile(b, j, j0), 0)),
        out_shape=jax.ShapeDtypeStruct((B * (9 - j0) * TM, FNET_WIDTH), F32),
        scratch_shapes=[pltpu.VMEM((2 * seq, FNET_WIDTH), BF16)],
        compiler_params=_cparams(("arbitrary", "arbitrary")),
        name="fnet",
    )(zf, cs_lat, cs_ctx, cd, w_fnet, b_fnet.reshape(1, -1))


def _dft_cos_sin(n):
    j = jnp.arange(n, dtype=jnp.int32)[:, None]
    if n <= 64:
        ang = ((j * j.T) % n).astype(F32) * (2.0 * np.pi / n)
        return jnp.cos(ang), jnp.sin(ang)
    k1 = jnp.arange(n // 64, dtype=jnp.int32)[None, :]
    k0 = jnp.arange(64, dtype=jnp.int32)[None, :]
    a = ((j * k1 * 64) % n).astype(F32) * (2.0 * np.pi / n)
    b = ((j * k0) % n).astype(F32) * (2.0 * np.pi / n)
    ca, sa, cb, sb = jnp.cos(a), jnp.sin(a), jnp.cos(b), jnp.sin(b)
    c = ca[:, :, None] * cb[:, None, :] - sa[:, :, None] * sb[:, None, :]
    s = sa[:, :, None] * cb[:, None, :] + ca[:, :, None] * sb[:, None, :]
    return c.reshape(n, n), s.reshape(n, n)


def _dft_tables(n):
    c, s = _dft_cos_sin(n)
    return jnp.concatenate([c, s], axis=1).astype(BF16)


def _merge_kernel(om_ref, of_ref, on_ref, x_ref, ga_ref, shf_ref, scf_ref, gout_ref, gffn_ref,
                  wout_ref, wr_ref, br_ref, xn_ref, hf_ref, lg_ref):
    ym = _rms(om_ref[...], gout_ref[:, 0:MLA_WIDTH]).astype(BF16)
    yf = _rms(of_ref[...], gout_ref[:, MLA_WIDTH:MLA_WIDTH + FNET_WIDTH]).astype(BF16)
    yn = _rms(on_ref[...], gout_ref[:, MLA_WIDTH + FNET_WIDTH:]).astype(BF16)
    acc = _dot(ym, wout_ref[0:MLA_WIDTH, :])
    acc = acc + _dot(yf, wout_ref[MLA_WIDTH:MLA_WIDTH + FNET_WIDTH, :])
    acc = acc + _dot(yn, wout_ref[MLA_WIDTH + FNET_WIDTH:, :])
    xn = x_ref[...] + ga_ref[0] * acc
    xn_ref[...] = xn
    hf = _rms(xn, gffn_ref[...]) * (1.0 + scf_ref[0]) + shf_ref[0]
    hf_ref[...] = hf
    hi = hf.astype(BF16)
    lo = (hf - hi.astype(F32)).astype(BF16)
    a = _dot(hi, wr_ref[...])
    b = _dot(lo, wr_ref[...])
    small = a[:, ROUTER_COLS:] + (b[:, :ROUTER_COLS] + b[:, ROUTER_COLS:])
    lg_ref[...] = a[:, :ROUTER_COLS] + small + br_ref[...]


def _merge(o_mla, o_f, o_na, xs, mods, g_out, g_ffn, w_out, w_router, b_router, B, j0):
    T = B * (9 - j0) * TM
    row = lambda b, j: (_otile(b, j, j0), 0)
    const = lambda b, j: (0, 0)
    return pl.pallas_call(
        _merge_kernel,
        grid=(B, 9 - j0),
        in_specs=[
            pl.BlockSpec((TM, MLA_WIDTH), row),
            pl.BlockSpec((TM, FNET_WIDTH), row),
            pl.BlockSpec((TM, NA_WIDTH), row),
            pl.BlockSpec((TM, D), lambda b, j: (_tile_of(b, j, j0), 0)),
            _mod_spec(2, j0),
            _mod_spec(3, j0),
            _mod_spec(4, j0),
            pl.BlockSpec((1, D), const),
            pl.BlockSpec((1, D), const),
            pl.BlockSpec((D, D), const),
            pl.BlockSpec((D, 2 * ROUTER_COLS), const),
            pl.BlockSpec((1, ROUTER_COLS), const),
        ],
        out_specs=[
            pl.BlockSpec((TM, D), row),
            pl.BlockSpec((TM, D), row),
            pl.BlockSpec((TM, ROUTER_COLS), row),
        ],
        out_shape=[
            jax.ShapeDtypeStruct((T, D), F32),
            jax.ShapeDtypeStruct((T, D), F32),
            jax.ShapeDtypeStruct((T, ROUTER_COLS), F32),
        ],
        compiler_params=_cparams(("arbitrary", "arbitrary")),
        name="merge",
    )(o_mla, o_f, o_na, xs, mods, mods, mods, g_out.reshape(1, D), g_ffn.reshape(1, D),
      w_out, w_router, b_router)


def _route(logits):
    n = logits.shape[0]
    g_prob = jax.nn.softmax(logits[:, :N_GROUPS], axis=-1)
    g_sel = jnp.argmax(g_prob, axis=-1)
    g_w = jnp.take_along_axis(g_prob, g_sel[:, None], axis=1)
    e_logits = logits[:, N_GROUPS:N_GROUPS + N_EXPERTS].reshape(n, N_GROUPS, EXPERTS_PER_GROUP)
    e_logits = jnp.take_along_axis(e_logits, g_sel[:, None, None], axis=1)[:, 0]
    e_prob = jax.nn.softmax(e_logits, axis=-1)
    top_p, top_i = lax.top_k(e_prob, 2)
    top_p = top_p / jnp.sum(top_p, axis=-1, keepdims=True)
    wts = top_p * g_w
    eid = (g_sel[:, None] * EXPERTS_PER_GROUP + top_i).astype(jnp.int32).reshape(-1)

    onehot = (eid[:, None] == jnp.arange(N_EXPERTS, dtype=jnp.int32)[None, :]).astype(jnp.int32)
    csum = jnp.cumsum(onehot, axis=0)
    counts = csum[-1]
    rank = jnp.sum((csum - 1) * onehot, axis=1)
    g_end = jnp.cumsum(counts)
    g_start = g_end - counts
    slot = jnp.take(g_start, eid) + rank
    src = jnp.zeros((2 * n,), jnp.int32).at[slot].set(jnp.arange(2 * n, dtype=jnp.int32) // 2)

    n_tiles = (2 * n) // TM
    max_items = n_tiles + N_EXPERTS - 1
    tile_lo = jnp.arange(n_tiles, dtype=jnp.int32) * TM
    first_e = jnp.searchsorted(g_end, tile_lo, side="right").astype(jnp.int32)
    last_e = jnp.searchsorted(g_end, tile_lo + (TM - 1), side="right").astype(jnp.int32)
    items = last_e - first_e + 1
    item_end = jnp.cumsum(items)
    item_start = item_end - items
    n_items = item_end[-1]
    ii = jnp.minimum(jnp.arange(max_items, dtype=jnp.int32), n_items - 1)
    it_tile = jnp.searchsorted(item_end, ii, side="right").astype(jnp.int32)
    it_exp = jnp.take(first_e, it_tile) + ii - jnp.take(item_start, it_tile)
    it_lo = jnp.clip(jnp.take(g_start, it_exp) - it_tile * TM, 0, TM)
    it_hi = jnp.clip(jnp.take(g_end, it_exp) - it_tile * TM, 0, TM)
    first_visit = ii == jnp.take(item_start, it_tile)
    new_expert = jnp.concatenate([jnp.ones((1,), bool), it_exp[1:] != it_exp[:-1]])
    flags = first_visit.astype(jnp.int32) + 2 * new_expert.astype(jnp.int32)
    meta = (it_tile, it_exp.astype(jnp.int32), it_lo.astype(jnp.int32), it_hi.astype(jnp.int32), flags,
            n_items.reshape(1).astype(jnp.int32), src)
    return meta, slot, wts


def _moe_kernel(tile_ref, exp_ref, lo_ref, hi_ref, flag_ref, ni_ref, src_ref,
                hf_hbm, wg_ref, wu_ref, wd_ref, y_ref, xbuf, wgb, wub, wdb, sem, *, n_tiles):
    i = pl.program_id(0)
    t = tile_ref[i]
    slot = lax.rem(t, 2)

    def gather_start(tile, buf, unrolled):
        base = tile * TM

        def issue(r):
            tok = src_ref[base + r]
            pltpu.make_async_copy(hf_hbm.at[pl.ds(tok, 1)], xbuf.at[buf, pl.ds(r, 1)], sem.at[buf]).start()

        if unrolled:
            for r in range(TM):
                issue(r)
        else:
            lax.fori_loop(0, TM, lambda r, c: (issue(r), c)[1], 0)

    def gather_wait(buf):
        pltpu.make_async_copy(hf_hbm.at[pl.ds(0, TM)], xbuf.at[buf], sem.at[buf]).wait()

    @pl.when(i < ni_ref[0])
    def _():
        first_visit = (flag_ref[i] & 1) != 0
        new_expert = (flag_ref[i] & 2) != 0

        @pl.when(i == 0)
        def _():
            gather_start(0, 0, False)

        @pl.when(first_visit)
        def _():
            gather_wait(slot)

        @pl.when(jnp.logical_and(first_visit, t + 1 < n_tiles))
        def _():
            gather_start(t + 1, 1 - slot, True)

        @pl.when(new_expert)
        def _():
            wgb[...] = wg_ref[0].astype(BF16)
            wub[...] = wu_ref[0].astype(BF16)
            wdb[...] = wd_ref[0].astype(BF16)

        x = xbuf[slot].astype(BF16)
        a = _dot(x, wgb[...])
        u = _dot(x, wub[...])
        row = lax.broadcasted_iota(jnp.int32, (TM, 1), 0)
        mine = jnp.logical_and(row >= lo_ref[i], row < hi_ref[i])
        h = jnp.where(mine, (a * jax.nn.sigmoid(a)) * u, 0.0)
        yv = _dot(h.astype(BF16), wdb[...])

        @pl.when(first_visit)
        def _():
            y_ref[...] = yv

        @pl.when(jnp.logical_not(first_visit))
        def _():
            y_ref[...] += yv


def _moe(hf, meta, w_gate, w_up, w_down, layer):
    n_rows = meta[6].shape[0]
    n_tiles = n_rows // TM
    max_items = meta[0].shape[0]
    e0 = layer * N_EXPERTS
    wmap = lambda i, tile, exp, lo, hi, fl, ni, src: (e0 + exp[i], 0, 0)
    grid_spec = pltpu.PrefetchScalarGridSpec(
        num_scalar_prefetch=7,
        grid=(max_items,),
        in_specs=[
            pl.BlockSpec(memory_space=pl.ANY),
            pl.BlockSpec((1, D, D_EXPERT), wmap),
            pl.BlockSpec((1, D, D_EXPERT), wmap),
            pl.BlockSpec((1, D_EXPERT, D), wmap),
        ],
        out_specs=pl.BlockSpec((TM, D), lambda i, tile, *_: (tile[i], 0)),
        scratch_shapes=[
            pltpu.VMEM((2, TM, D), F32),
            pltpu.VMEM((D, D_EXPERT), BF16),
            pltpu.VMEM((D, D_EXPERT), BF16),
            pltpu.VMEM((D_EXPERT, D), BF16),
            pltpu.SemaphoreType.DMA((2,)),
        ],
    )
    return pl.pallas_call(
        functools.partial(_moe_kernel, n_tiles=n_tiles),
        grid_spec=grid_spec,
        out_shape=jax.ShapeDtypeStruct((n_rows, D), F32),
        compiler_params=_cparams(("arbitrary",)),
        name="moe_experts",
    )(*meta, hf, w_gate.reshape(-1, D, D_EXPERT), w_up.reshape(-1, D, D_EXPERT),
      w_down.reshape(-1, D_EXPERT, D))


def _combine_kernel(pos_ref, x_ref, gf_ref, w_ref, g_ref, y_hbm, o_ref, ybuf, sem, *, j0, final):
    b = pl.program_id(0)
    j = pl.program_id(1)
    base = _otile(b, j, j0) * TM

    def issue(r, carry):
        p0 = pos_ref[2 * (base + r)]
        p1 = pos_ref[2 * (base + r) + 1]
        pltpu.make_async_copy(y_hbm.at[pl.ds(p0, 1)], ybuf.at[0, pl.ds(r, 1)], sem).start()
        pltpu.make_async_copy(y_hbm.at[pl.ds(p1, 1)], ybuf.at[1, pl.ds(r, 1)], sem).start()
        return carry

    lax.fori_loop(0, TM, issue, 0)
    pltpu.make_async_copy(y_hbm.at[pl.ds(0, TM)], ybuf.at[0], sem).wait()
    pltpu.make_async_copy(y_hbm.at[pl.ds(0, TM)], ybuf.at[1], sem).wait()
    w = w_ref[...]
    out = x_ref[...] + gf_ref[0] * (w[:, 0:1] * ybuf[0] + w[:, 1:2] * ybuf[1])
    if final:
        o_ref[0] = _rms(out, g_ref[...])
    else:
        o_ref[...] = out


def _combine(pos, wts, xn, mods, y, g_final, B, j0, final):
    assert final == (j0 == 1)
    if final:
        out_spec = pl.BlockSpec((1, TM, D), lambda b, j, p: (b, j, 0))
        out_shape = jax.ShapeDtypeStruct((B, 8 * TM, D), F32)
    else:
        out_spec = pl.BlockSpec((TM, D), lambda b, j, p: (_otile(b, j, j0), 0))
        out_shape = jax.ShapeDtypeStruct(xn.shape, F32)
    grid_spec = pltpu.PrefetchScalarGridSpec(
        num_scalar_prefetch=1,
        grid=(B, 9 - j0),
        in_specs=[
            pl.BlockSpec((TM, D), lambda b, j, p: (_otile(b, j, j0), 0)),
            pl.BlockSpec((1, 1, D), lambda b, j, p: (_mod_row(j, b, j0) * 6 + 5, 0, 0)),
            pl.BlockSpec((TM, 2), lambda b, j, p: (_otile(b, j, j0), 0)),
            pl.BlockSpec((1, D), lambda b, j, p: (0, 0)),
            pl.BlockSpec(memory_space=pl.ANY),
        ],
        out_specs=out_spec,
        scratch_shapes=[pltpu.VMEM((2, TM, D), F32), pltpu.SemaphoreType.DMA],
    )
    return pl.pallas_call(
        functools.partial(_combine_kernel, j0=j0, final=final),
        grid_spec=grid_spec,
        out_shape=out_shape,
        compiler_params=_cparams(("arbitrary", "arbitrary")),
        name="combine_final" if final else "combine",
    )(pos, xn, mods, wts, g_final.reshape(1, D), y)


def _deinterleave(n):
    half = np.concatenate([np.arange(0, n, 2), np.arange(1, n, 2)])
    return np.concatenate([half, half])


def _prep_w_in(w_in):
    sizes = (Q_LORA, KV_LORA, QK_ROPE, FNET_WIDTH, NA_WIDTH, NA_WIDTH, NA_WIDTH)
    starts = np.concatenate([[0], np.cumsum(sizes)])
    seg = lambda i: np.arange(starts[i], starts[i + 1])
    cols = np.concatenate([seg(0), seg(3), seg(4), seg(5), seg(6), seg(1),
                           starts[2] + _deinterleave(QK_ROPE)])
    return jnp.take(w_in, jnp.asarray(cols), axis=1).astype(BF16)


def _prep_w_uq(w_uq):
    per = QK_NOPE + QK_ROPE
    cols = np.concatenate([
        np.concatenate([h * per + np.arange(QK_NOPE), h * per + QK_NOPE + _deinterleave(QK_ROPE)])
        for h in range(MLA_HEADS)])
    return jnp.take(w_uq, jnp.asarray(cols), axis=1).astype(BF16)


def _rope_tables(seq):
    half = QK_ROPE // 2
    inv_freq = ROPE_THETA ** (-jnp.arange(0, half, 2, dtype=F32) / half)
    t = jnp.arange(seq, dtype=jnp.int32)
    row = (t // GRID_W).astype(F32)
    col = (t % GRID_W).astype(F32)
    ang = jnp.concatenate([row[:, None] * inv_freq, col[:, None] * inv_freq], axis=-1)
    cos, sin = jnp.cos(ang), jnp.sin(ang)
    zeros = jnp.zeros((seq, 64), F32)
    cos_l = jnp.concatenate([cos, cos, zeros], axis=1)
    sin_l = jnp.concatenate([-sin, sin, zeros], axis=1)
    cos_c = jnp.concatenate([jnp.ones((CTX_LEN, 64), F32), jnp.zeros((CTX_LEN, 64), F32)], axis=1)
    sin_c = jnp.zeros((CTX_LEN, 128), F32)
    return jnp.concatenate([cos_c, cos_l], axis=0), jnp.concatenate([sin_c, sin_l], axis=0)


def kernel(x, c, ctx, c_ctx, w_ada, b_ada, g_attn, g_ffn, w_in, g_q, w_uq, g_kv, w_ukv, w_fnet, b_fnet,
           na_rpb, g_out, w_out, w_rg, b_rg, w_re, b_re, w_gate, w_up, w_down, g_final):
    B, S, _ = x.shape
    L = w_ada.shape[0]
    assert ctx.shape[1] == CTX_LEN == TM and S == 8 * TM and B <= 4
    T = B * 9 * TM

    cond8 = jnp.zeros((8, D), F32).at[:B].set(c).at[4].set(c_ctx)
    mods_all = _modulation(cond8, w_ada, b_ada)
    na_bias = _na_bias(na_rpb)
    cos_t, sin_t = _rope_tables(S)
    cs_lat = _dft_tables(S)
    cs_ctx = _dft_tables(CTX_LEN)
    cd_c, cd_s = _dft_cos_sin(FNET_GROUP_DIM)
    cd = jnp.concatenate([cd_c, -cd_s], axis=1).astype(BF16)

    xs = jnp.concatenate([ctx, x], axis=1).reshape(T, D)
    out = None
    for l in range(L):
        last = l == L - 1
        j0 = 1 if last else 0
        mods = mods_all[l].reshape(48, 1, D)
        w_in_ext = _prep_w_in(w_in[l])
        w_uq_ext = _prep_w_uq(w_uq[l])
        w_router = jnp.zeros((D, ROUTER_COLS), F32).at[:, :N_GROUPS].set(w_rg[l]) \
            .at[:, N_GROUPS:N_GROUPS + N_EXPERTS].set(w_re[l])
        w_scaled = w_router * 65537.0
        w_router_hi = w_scaled - (w_scaled - w_router)
        w_router = jnp.concatenate([w_router_hi, w_router - w_router_hi], axis=1).astype(BF16)
        b_router = jnp.zeros((1, ROUTER_COLS), F32).at[0, :N_GROUPS].set(b_rg[l]) \
            .at[0, N_GROUPS:N_GROUPS + N_EXPERTS].set(b_re[l])

        zq, zkv, zf, zna = _in_projection(xs, mods, g_attn[l], w_in_ext, B)
        q, k, vt = _mla_projection(zq, zkv, g_q[l], g_kv[l], w_uq_ext, w_ukv[l].astype(BF16),
                                   cos_t, sin_t, B)
        o_mla = _mla_attention(q, k, vt, B, j0)
        o_na = _na_attention(zna, na_bias[l], B, j0)
        w_f = w_fnet[l].astype(BF16)
        o_f = _fnet(zf, cs_lat, cs_ctx, cd, w_f, b_fnet[l], B, j0)
        xn, hf, logits = _merge(o_mla, o_f, o_na, xs, mods, g_out[l], g_ffn[l],
                                w_out[l].astype(BF16), w_router, b_router, B, j0)
        meta, slot, wts = _route(logits)
        y = _moe(hf, meta, w_gate, w_up, w_down, l)
        res = _combine(slot, wts, xn, mods, y, g_final, B, j0, last)
        if last:
            out = res
        else:
            xs = res
    return out
```

```python
import functools

import numpy as np
import jax
import jax.numpy as jnp
from jax import lax
from jax.experimental import pallas as pl
from jax.experimental.pallas import tpu as pltpu

F32 = jnp.float32
BF16 = jnp.bfloat16

D = 2048
GRID_W = 64
CTX_LEN = 256
EPS = 1e-6
NEG_INF = -1e30
ROPE_THETA = 10000.0

V_DIM = 128
MLA_WIDTH = D // 2
MLA_HEADS = MLA_WIDTH // V_DIM
QK_NOPE = 128
QK_ROPE = 64
Q_LORA = D // 4
KV_LORA = D // 8
FNET_WIDTH = D // 4
FNET_GROUP_DIM = 128
FNET_GROUPS = FNET_WIDTH // FNET_GROUP_DIM
NA_WIDTH = D // 4
NA_HEAD_DIM = 128
NA_HEADS = NA_WIDTH // NA_HEAD_DIM
NA_KH_MAX = 8
NA_KW = 16
N_GROUPS = 4
EXPERTS_PER_GROUP = 8
N_EXPERTS = N_GROUPS * EXPERTS_PER_GROUP
D_EXPERT = D // 4

MLA_QSCALE = (QK_NOPE + QK_ROPE) ** -0.5 * float(np.log2(np.e))
NA_QSCALE = NA_HEAD_DIM ** -0.5 * float(np.log2(np.e))
NA_WIN_ROWS = 12
TM = 256
IN_EXT = 5 * 512 + 256 + 128
ROUTER_COLS = 128
VMEM_LIMIT = 56 * 1024 * 1024


def _cparams(sem):
    return pltpu.CompilerParams(dimension_semantics=sem, vmem_limit_bytes=VMEM_LIMIT)


def _rms(v, g):
    return v * lax.rsqrt(jnp.mean(v * v, axis=-1, keepdims=True) + EPS) * g


def _dot(a, b):
    return jnp.dot(a, b, preferred_element_type=F32)


def _dot_nt(a, b):
    return lax.dot_general(a, b, (((1,), (1,)), ((), ())), preferred_element_type=F32)


def _mod_kernel(c_ref, w_ref, b_ref, o_ref):
    c = c_ref[...]
    s = c * jax.nn.sigmoid(c)
    o_ref[0] = _dot(s.astype(BF16), w_ref[0].astype(BF16)) + b_ref[0]


def _modulation(cond8, w_ada, b_ada):
    L = w_ada.shape[0]
    tn = 1024
    return pl.pallas_call(
        _mod_kernel,
        grid=(L, 6 * D // tn),
        in_specs=[
            pl.BlockSpec((8, D), lambda l, n: (0, 0)),
            pl.BlockSpec((1, D, tn), lambda l, n: (l, 0, n)),
            pl.BlockSpec((1, 1, tn), lambda l, n: (l, 0, n)),
        ],
        out_specs=pl.BlockSpec((1, 8, tn), lambda l, n: (l, 0, n)),
        out_shape=jax.ShapeDtypeStruct((L, 8, 6 * D), F32),
        compiler_params=_cparams(("arbitrary", "arbitrary")),
        name="modulation",
    )(cond8, w_ada, b_ada.reshape(L, 1, 6 * D))


def _tile_of(b, j, j0):
    return b * 9 + j0 + j


def _otile(b, j, j0):
    return b * (9 - j0) + j


def _mod_row(j, b, j0):
    return jnp.where(j0 + j == 0, 4, b)


def _mod_spec(k, j0):
    return pl.BlockSpec((1, 1, D), lambda b, j: (_mod_row(j, b, j0) * 6 + k, 0, 0))


def _moe_residual(pos_ref, y_hbm, ybuf, sem, x, gate, w, step, n_steps):
    buf = lax.rem(step, 2)

    def start(tile, b, unrolled):
        base = tile * TM

        def issue(r):
            for k in range(2):
                pltpu.make_async_copy(y_hbm.at[pl.ds(pos_ref[2 * (base + r) + k], 1)],
                                      ybuf.at[b, k, pl.ds(r, 1)], sem.at[b]).start()

        if unrolled:
            for r in range(TM):
                issue(r)
        else:
            lax.fori_loop(0, TM, lambda r, c: (issue(r), c)[1], 0)

    def wait(b):
        for k in range(2):
            pltpu.make_async_copy(y_hbm.at[pl.ds(0, TM)], ybuf.at[b, k], sem.at[b]).wait()

    @pl.when(step == 0)
    def _():
        start(0, 0, False)

    start(jnp.minimum(step + 1, n_steps - 1), 1 - buf, True)
    wait(buf)
    out = x + gate * (w[:, 0:1] * ybuf[buf, 0] + w[:, 1:2] * ybuf[buf, 1])

    @pl.when(step == n_steps - 1)
    def _():
        wait(1 - buf)

    return out


_MOE_GATHER_SCRATCH = [pltpu.VMEM((2, 2, TM, D), F32), pltpu.SemaphoreType.DMA((2,))]


def _inproj_body(x, sh_ref, sc_ref, g_ref, w_ref, zq_ref, zkv_ref, zf_ref, naq_ref, nak_ref, navt_ref):
    h = _rms(x, g_ref[...])
    h = h * (1.0 + sc_ref[0]) + sh_ref[0]
    z = _dot(h.astype(BF16), w_ref[...])
    zq_ref[...] = z[:, 0:512]
    zf_ref[...] = z[:, 512:1024].astype(BF16)
    naq_ref[...] = (z[:, 1024:1536] * NA_QSCALE).astype(BF16)
    nak_ref[...] = z[:, 1536:2048].astype(BF16)
    for hd in range(NA_HEADS):
        c = 2048 + hd * NA_HEAD_DIM
        navt_ref[hd * NA_HEAD_DIM:(hd + 1) * NA_HEAD_DIM, :] = z[:, c:c + NA_HEAD_DIM].T.astype(BF16)
    zkv_ref[...] = z[:, 2560:IN_EXT]


def _inproj_kernel(x_ref, *refs):
    _inproj_body(x_ref[...], *refs)


def _inproj_moe_kernel(pos_ref, x_ref, gf_ref, wts_ref, y_hbm, *refs, n_steps):
    *refs, xs_ref, ybuf, sem = refs
    step = pl.program_id(0) * 9 + pl.program_id(1)
    x = _moe_residual(pos_ref, y_hbm, ybuf, sem, x_ref[...], gf_ref[0], wts_ref[...], step, n_steps)
    xs_ref[...] = x
    _inproj_body(x, *refs)


def _in_projection(xs, mods, g_attn, w_in_ext, B, moe=None):
    T = xs.shape[0]
    row = lambda b, j, *_: (_tile_of(b, j, 0), 0)
    const = lambda b, j, *_: (0, 0)
    mod = lambda k: pl.BlockSpec((1, 1, D), lambda b, j, *_: (_mod_row(j, b, 0) * 6 + k, 0, 0))
    in_specs = [
        pl.BlockSpec((TM, D), row),
        mod(0),
        mod(1),
        pl.BlockSpec((1, D), const),
        pl.BlockSpec((D, IN_EXT), const),
    ]
    out_specs = [
        pl.BlockSpec((TM, 512), row),
        pl.BlockSpec((TM, 384), row),
        pl.BlockSpec((TM, 512), row),
        pl.BlockSpec((TM, NA_WIDTH), row),
        pl.BlockSpec((TM, NA_WIDTH), row),
        pl.BlockSpec((NA_WIDTH, TM), lambda b, j, *_: (0, _tile_of(b, j, 0))),
    ]
    out_shape = [
        jax.ShapeDtypeStruct((T, 512), F32),
        jax.ShapeDtypeStruct((T, 384), F32),
        jax.ShapeDtypeStruct((T, 512), BF16),
        jax.ShapeDtypeStruct((T, NA_WIDTH), BF16),
        jax.ShapeDtypeStruct((T, NA_WIDTH), BF16),
        jax.ShapeDtypeStruct((NA_WIDTH, T), BF16),
    ]
    args = [xs, mods, mods, g_attn.reshape(1, D), w_in_ext]
    if moe is None:
        return pl.pallas_call(
            _inproj_kernel,
            grid=(B, 9),
            in_specs=in_specs,
            out_specs=out_specs,
            out_shape=out_shape,
            compiler_params=_cparams(("arbitrary", "arbitrary")),
            name="in_projection",
        )(*args)
    pos, wts, y, mods_prev = moe
    in_specs = [in_specs[0], mod(5), pl.BlockSpec((TM, 2), row), pl.BlockSpec(memory_space=pl.ANY)] \
        + in_specs[1:]
    grid_spec = pltpu.PrefetchScalarGridSpec(
        num_scalar_prefetch=1,
        grid=(B, 9),
        in_specs=in_specs,
        out_specs=out_specs + [pl.BlockSpec((TM, D), row)],
        scratch_shapes=_MOE_GATHER_SCRATCH,
    )
    return pl.pallas_call(
        functools.partial(_inproj_moe_kernel, n_steps=B * 9),
        grid_spec=grid_spec,
        out_shape=out_shape + [jax.ShapeDtypeStruct((T, D), F32)],
        compiler_params=_cparams(("arbitrary", "arbitrary")),
        name="in_projection_moe",
    )(pos, xs, mods_prev, wts, y, *args[1:])


def _rope(r, cos_t, sin_t):
    return r * cos_t + pltpu.roll(r, 32, 1) * sin_t


def _mlaproj_kernel(zq_ref, zkv_ref, gq_ref, gkv_ref, wq_ref, wkv_ref, cos_ref, sin_ref,
                    q_ref, k_ref, vt_ref):
    cos_t = cos_ref[...]
    sin_t = sin_ref[...]
    q = _dot(_rms(zq_ref[...], gq_ref[...]).astype(BF16), wq_ref[...]) * MLA_QSCALE
    kv = _dot(_rms(zkv_ref[:, 0:KV_LORA], gkv_ref[...]).astype(BF16), wkv_ref[...])
    k_rope = _rope(zkv_ref[:, KV_LORA:KV_LORA + 128], cos_t, sin_t).astype(BF16)
    for h in range(MLA_HEADS):
        c = h * 256
        q_ref[:, c:c + 128] = q[:, c:c + 128].astype(BF16)
        q_ref[:, c + 128:c + 256] = _rope(q[:, c + 128:c + 256], cos_t, sin_t).astype(BF16)
        k_ref[:, c:c + 128] = kv[:, c:c + 128].astype(BF16)
        k_ref[:, c + 128:c + 256] = k_rope
        vt_ref[h * 128:(h + 1) * 128, :] = kv[:, c + 128:c + 256].T.astype(BF16)


def _mla_projection(zq, zkv, g_q, g_kv, w_uq_ext, w_ukv, cos_t, sin_t, B):
    T = zq.shape[0]
    row = lambda b, j: (_tile_of(b, j, 0), 0)
    const = lambda b, j: (0, 0)
    return pl.pallas_call(
        _mlaproj_kernel,
        grid=(B, 9),
        in_specs=[
            pl.BlockSpec((TM, 512), row),
            pl.BlockSpec((TM, 384), row),
            pl.BlockSpec((1, Q_LORA), const),
            pl.BlockSpec((1, KV_LORA), const),
            pl.BlockSpec((Q_LORA, 2048), const),
            pl.BlockSpec((KV_LORA, 2048), const),
            pl.BlockSpec((TM, 128), lambda b, j: (j, 0)),
            pl.BlockSpec((TM, 128), lambda b, j: (j, 0)),
        ],
        out_specs=[
            pl.BlockSpec((TM, 2048), row),
            pl.BlockSpec((TM, 2048), row),
            pl.BlockSpec((MLA_WIDTH, TM), lambda b, j: (0, _tile_of(b, j, 0))),
        ],
        out_shape=[
            jax.ShapeDtypeStruct((T, 2048), BF16),
            jax.ShapeDtypeStruct((T, 2048), BF16),
            jax.ShapeDtypeStruct((MLA_WIDTH, T), BF16),
        ],
        compiler_params=_cparams(("arbitrary", "arbitrary")),
        name="mla_projection",
    )(zq, zkv, g_q.reshape(1, -1), g_kv.reshape(1, -1), w_uq_ext, w_ukv, cos_t, sin_t)


def _mla_attn_kernel(q_ref, k_ref, vt_ref, o_ref, *, j0):
    def attend(q0, nq, nk, o0):
        st = _dot_nt(k_ref[0:nk, :], q_ref[q0:q0 + nq, :])
        m = jnp.max(st, axis=0, keepdims=True)
        p = jnp.exp2(st - m)
        l = jnp.sum(p, axis=0, keepdims=True)
        ot = _dot(vt_ref[:, 0:nk], p.astype(BF16)) / l
        o_ref[o0:o0 + nq, :] = ot.T

    if j0 == 0:
        attend(0, CTX_LEN, CTX_LEN, 0)
    for c in range(4):
        attend(CTX_LEN + 2 * c * TM, 2 * TM, 9 * TM, (1 - j0 + 2 * c) * TM)


def _mla_attention(q, k, vt, B, j0):
    rows = (9 - j0) * TM
    return pl.pallas_call(
        functools.partial(_mla_attn_kernel, j0=j0),
        grid=(B, MLA_HEADS),
        in_specs=[
            pl.BlockSpec((9 * TM, 256), lambda b, h: (b, h)),
            pl.BlockSpec((9 * TM, 256), lambda b, h: (b, h)),
            pl.BlockSpec((V_DIM, 9 * TM), lambda b, h: (h, b)),
        ],
        out_specs=pl.BlockSpec((rows, V_DIM), lambda b, h: (b, h)),
        out_shape=jax.ShapeDtypeStruct((B * rows, MLA_WIDTH), F32),
        compiler_params=_cparams(("arbitrary", "arbitrary")),
        name="mla_attention",
    )(q, k, vt)


def _na_chunk(g):
    start_row = min(max(4 * g - 4, 0), 8 * TM // GRID_W - NA_WIN_ROWS)
    pattern = 0 if g == 0 else (2 if g == 7 else 1)
    return start_row, pattern


def _na_kernel(q_ref, k_ref, vt_ref, bias_ref, o_ref, *, j0):
    def finish(parts, o0):
        m = None
        for st, _ in parts:
            pm = jnp.max(st, axis=0, keepdims=True)
            m = pm if m is None else jnp.maximum(m, pm)
        l = None
        ot = None
        for st, vt in parts:
            p = jnp.exp2(st - m)
            pl_sum = jnp.sum(p, axis=0, keepdims=True)
            po = _dot(vt, p.astype(BF16))
            l = pl_sum if l is None else l + pl_sum
            ot = po if ot is None else ot + po
        o_ref[o0:o0 + TM, :] = (ot / l).T

    if j0 == 0:
        st = _dot_nt(k_ref[0:CTX_LEN, :], q_ref[0:CTX_LEN, :])
        finish([(st, vt_ref[:, 0:CTX_LEN])], 0)
    for g in range(8):
        start_row, pattern = _na_chunk(g)
        k0 = CTX_LEN + start_row * GRID_W
        nk = NA_WIN_ROWS * GRID_W
        q = q_ref[CTX_LEN + g * TM:CTX_LEN + (g + 1) * TM, :]
        st_loc = _dot_nt(k_ref[k0:k0 + nk, :], q) + bias_ref[0, pattern]
        st_ctx = _dot_nt(k_ref[0:CTX_LEN, :], q)
        finish([(st_loc, vt_ref[:, k0:k0 + nk]), (st_ctx, vt_ref[:, 0:CTX_LEN])], (1 - j0 + g) * TM)


def _na_attention(naq, nak, navt, bias, B, j0):
    rows = (9 - j0) * TM
    return pl.pallas_call(
        functools.partial(_na_kernel, j0=j0),
        grid=(NA_HEADS, B),
        in_specs=[
            pl.BlockSpec((9 * TM, NA_HEAD_DIM), lambda h, b: (b, h)),
            pl.BlockSpec((9 * TM, NA_HEAD_DIM), lambda h, b: (b, h)),
            pl.BlockSpec((NA_HEAD_DIM, 9 * TM), lambda h, b: (h, b)),
            pl.BlockSpec((1, 3, NA_WIN_ROWS * GRID_W, TM), lambda h, b: (h, 0, 0, 0)),
        ],
        out_specs=pl.BlockSpec((rows, NA_HEAD_DIM), lambda h, b: (b, h)),
        out_shape=jax.ShapeDtypeStruct((B * rows, NA_WIDTH), F32),
        compiler_params=_cparams(("arbitrary", "arbitrary")),
        name="na_attention",
    )(naq, nak, navt, bias)


def _na_bias(rpb):
    kh, rows, nq = NA_KH_MAX, 8 * TM // GRID_W, TM // GRID_W
    drow = np.zeros((3, NA_WIN_ROWS, nq), np.int64)
    row_ok = np.zeros((3, NA_WIN_ROWS, nq), bool)
    for g in (0, 1, 7):
        start_row, pattern = _na_chunk(g)
        for kr in range(NA_WIN_ROWS):
            for qr in range(nq):
                r = 4 * g + qr
                r_start = min(max(r - kh // 2, 0), rows - kh)
                key_row = start_row + kr
                row_ok[pattern, kr, qr] = r_start <= key_row < r_start + kh
                drow[pattern, kr, qr] = min(max(key_row - r + (kh - 1), 0), 2 * kh - 2)
    cq = np.arange(GRID_W)
    ck = np.arange(GRID_W)
    col_start = np.clip(cq - NA_KW // 2, 0, GRID_W - NA_KW)
    col_ok = (ck[:, None] >= col_start[None, :]) & (ck[:, None] < col_start[None, :] + NA_KW)
    dcol = np.clip(ck[:, None] - cq[None, :] + (NA_KW - 1), 0, 2 * NA_KW - 2)
    t = jnp.take(rpb.astype(F32), jnp.asarray(drow.reshape(-1)), axis=2)
    t = jnp.take(t, jnp.asarray(dcol.reshape(-1)), axis=3)
    t = t.reshape(-1, NA_HEADS, 3, NA_WIN_ROWS, nq, GRID_W, GRID_W).transpose(0, 1, 2, 3, 5, 4, 6)
    ok = row_ok[:, :, None, :, None] & col_ok[None, None, :, None, :]
    t = jnp.where(jnp.asarray(ok)[None, None], t * float(np.log2(np.e)), NEG_INF)
    return t.reshape(-1, NA_HEADS, 3, NA_WIN_ROWS * GRID_W, nq * GRID_W)


def _fnet_kernel(z_ref, csl_ref, csc_ref, cd_ref, w_ref, b_ref, o_ref, ab_ref, *, j0, seq):
    j = pl.program_id(1) + j0

    def small_side(row0, length):
        for g in range(FNET_GROUPS):
            c = g * FNET_GROUP_DIM
            ab = _dot(z_ref[row0:row0 + length, c:c + FNET_GROUP_DIM], cd_ref[...])
            ab_ref[0:length, c:c + FNET_GROUP_DIM] = ab[:, 0:FNET_GROUP_DIM].astype(BF16)
            ab_ref[length:2 * length, c:c + FNET_GROUP_DIM] = ab[:, FNET_GROUP_DIM:].astype(BF16)

    def long_side(cs, length):
        f = _dot(cs, ab_ref[0:2 * length, :]) * (length * FNET_GROUP_DIM) ** -0.5
        o_ref[...] = _dot(f.astype(BF16), w_ref[...]) + b_ref[...]

    if j0 == 0:
        @pl.when(j == 0)
        def _():
            small_side(0, CTX_LEN)
            long_side(csc_ref[...], CTX_LEN)

    @pl.when(j == 1)
    def _():
        small_side(CTX_LEN, seq)

    @pl.when(j >= 1)
    def _():
        long_side(csl_ref[...], seq)


def _fnet(zf, cs_lat, cs_ctx, cd, w_fnet, b_fnet, B, j0):
    seq = 8 * TM
    return pl.pallas_call(
        functools.partial(_fnet_kernel, j0=j0, seq=seq),
        grid=(B, 9 - j0),
        in_specs=[
            pl.BlockSpec((9 * TM, FNET_WIDTH), lambda b, j: (b, 0)),
            pl.BlockSpec((TM, 2 * seq), lambda b, j: (jnp.maximum(j0 + j - 1, 0), 0)),
            pl.BlockSpec((CTX_LEN, 2 * CTX_LEN), lambda b, j: (0, 0)),
            pl.BlockSpec((FNET_GROUP_DIM, 2 * FNET_GROUP_DIM), lambda b, j: (0, 0)),
            pl.BlockSpec((FNET_WIDTH, FNET_WIDTH), lambda b, j: (0, 0)),
            pl.BlockSpec((1, FNET_WIDTH), lambda b, j: (0, 0)),
        ],
        out_specs=pl.BlockSpec((TM, FNET_WIDTH), lambda b, j: (_otile(b, j, j0), 0)),
        out_shape=jax.ShapeDtypeStruct((B * (9 - j0) * TM, FNET_WIDTH), F32),
        scratch_shapes=[pltpu.VMEM((2 * seq, FNET_WIDTH), BF16)],
        compiler_params=_cparams(("arbitrary", "arbitrary")),
        name="fnet",
    )(zf, cs_lat, cs_ctx, cd, w_fnet, b_fnet.reshape(1, -1))


def _dft_cos_sin(n):
    j = jnp.arange(n, dtype=jnp.int32)[:, None]
    if n <= 64:
        ang = ((j * j.T) % n).astype(F32) * (2.0 * np.pi / n)
        return jnp.cos(ang), jnp.sin(ang)
    k1 = jnp.arange(n // 64, dtype=jnp.int32)[None, :]
    k0 = jnp.arange(64, dtype=jnp.int32)[None, :]
    a = ((j * k1 * 64) % n).astype(F32) * (2.0 * np.pi / n)
    b = ((j * k0) % n).astype(F32) * (2.0 * np.pi / n)
    ca, sa, cb, sb = jnp.cos(a), jnp.sin(a), jnp.cos(b), jnp.sin(b)
    c = ca[:, :, None] * cb[:, None, :] - sa[:, :, None] * sb[:, None, :]
    s = sa[:, :, None] * cb[:, None, :] + ca[:, :, None] * sb[:, None, :]
    return c.reshape(n, n), s.reshape(n, n)


def _dft_tables(n):
    c, s = _dft_cos_sin(n)
    return jnp.concatenate([c, s], axis=1).astype(BF16)


def _merge_kernel(om_ref, of_ref, on_ref, x_ref, ga_ref, shf_ref, scf_ref, gout_ref, gffn_ref,
                  wout_ref, wr_ref, br_ref, xn_ref, hf_ref, lg_ref):
    ym = _rms(om_ref[...], gout_ref[:, 0:MLA_WIDTH]).astype(BF16)
    yf = _rms(of_ref[...], gout_ref[:, MLA_WIDTH:MLA_WIDTH + FNET_WIDTH]).astype(BF16)
    yn = _rms(on_ref[...], gout_ref[:, MLA_WIDTH + FNET_WIDTH:]).astype(BF16)
    acc = _dot(ym, wout_ref[0:MLA_WIDTH, :])
    acc = acc + _dot(yf, wout_ref[MLA_WIDTH:MLA_WIDTH + FNET_WIDTH, :])
    acc = acc + _dot(yn, wout_ref[MLA_WIDTH + FNET_WIDTH:, :])
    xn = x_ref[...] + ga_ref[0] * acc
    xn_ref[...] = xn
    hf = _rms(xn, gffn_ref[...]) * (1.0 + scf_ref[0]) + shf_ref[0]
    hf_ref[...] = hf
    hi = hf.astype(BF16)
    lo = (hf - hi.astype(F32)).astype(BF16)
    a = _dot(hi, wr_ref[...])
    b = _dot(lo, wr_ref[...])
    small = a[:, ROUTER_COLS:] + (b[:, :ROUTER_COLS] + b[:, ROUTER_COLS:])
    lg_ref[...] = a[:, :ROUTER_COLS] + small + br_ref[...]


def _merge(o_mla, o_f, o_na, xs, mods, g_out, g_ffn, w_out, w_router, b_router, B, j0):
    T = B * (9 - j0) * TM
    row = lambda b, j: (_otile(b, j, j0), 0)
    const = lambda b, j: (0, 0)
    return pl.pallas_call(
        _merge_kernel,
        grid=(B, 9 - j0),
        in_specs=[
            pl.BlockSpec((TM, MLA_WIDTH), row),
            pl.BlockSpec((TM, FNET_WIDTH), row),
            pl.BlockSpec((TM, NA_WIDTH), row),
            pl.BlockSpec((TM, D), lambda b, j: (_tile_of(b, j, j0), 0)),
            _mod_spec(2, j0),
            _mod_spec(3, j0),
            _mod_spec(4, j0),
            pl.BlockSpec((1, D), const),
            pl.BlockSpec((1, D), const),
            pl.BlockSpec((D, D), const),
            pl.BlockSpec((D, 2 * ROUTER_COLS), const),
            pl.BlockSpec((1, ROUTER_COLS), const),
        ],
        out_specs=[
            pl.BlockSpec((TM, D), row),
            pl.BlockSpec((TM, D), row),
            pl.BlockSpec((TM, ROUTER_COLS), row),
        ],
        out_shape=[
            jax.ShapeDtypeStruct((T, D), F32),
            jax.ShapeDtypeStruct((T, D), F32),
            jax.ShapeDtypeStruct((T, ROUTER_COLS), F32),
        ],
        compiler_params=_cparams(("arbitrary", "arbitrary")),
        name="merge",
    )(o_mla, o_f, o_na, xs, mods, mods, mods, g_out.reshape(1, D), g_ffn.reshape(1, D),
      w_out, w_router, b_router)


def _route(logits):
    n = logits.shape[0]
    g_prob = jax.nn.softmax(logits[:, :N_GROUPS], axis=-1)
    g_sel = jnp.argmax(g_prob, axis=-1)
    g_w = jnp.take_along_axis(g_prob, g_sel[:, None], axis=1)
    e_logits = logits[:, N_GROUPS:N_GROUPS + N_EXPERTS].reshape(n, N_GROUPS, EXPERTS_PER_GROUP)
    e_logits = jnp.take_along_axis(e_logits, g_sel[:, None, None], axis=1)[:, 0]
    e_prob = jax.nn.softmax(e_logits, axis=-1)
    top_p, top_i = lax.top_k(e_prob, 2)
    top_p = top_p / jnp.sum(top_p, axis=-1, keepdims=True)
    wts = top_p * g_w
    eid = (g_sel[:, None] * EXPERTS_PER_GROUP + top_i).astype(jnp.int32).reshape(-1)

    onehot = (eid[:, None] == jnp.arange(N_EXPERTS, dtype=jnp.int32)[None, :]).astype(jnp.int32)
    csum = jnp.cumsum(onehot, axis=0)
    counts = csum[-1]
    rank = jnp.sum((csum - 1) * onehot, axis=1)
    g_end = jnp.cumsum(counts)
    g_start = g_end - counts
    slot = jnp.take(g_start, eid) + rank
    src = jnp.zeros((2 * n,), jnp.int32).at[slot].set(jnp.arange(2 * n, dtype=jnp.int32) // 2)

    n_tiles = (2 * n) // TM
    max_items = n_tiles + N_EXPERTS - 1
    tile_lo = jnp.arange(n_tiles, dtype=jnp.int32) * TM
    first_e = jnp.searchsorted(g_end, tile_lo, side="right").astype(jnp.int32)
    last_e = jnp.searchsorted(g_end, tile_lo + (TM - 1), side="right").astype(jnp.int32)
    items = last_e - first_e + 1
    item_end = jnp.cumsum(items)
    item_start = item_end - items
    n_items = item_end[-1]
    ii = jnp.minimum(jnp.arange(max_items, dtype=jnp.int32), n_items - 1)
    it_tile = jnp.searchsorted(item_end, ii, side="right").astype(jnp.int32)
    it_exp = jnp.take(first_e, it_tile) + ii - jnp.take(item_start, it_tile)
    it_lo = jnp.clip(jnp.take(g_start, it_exp) - it_tile * TM, 0, TM)
    it_hi = jnp.clip(jnp.take(g_end, it_exp) - it_tile * TM, 0, TM)
    first_visit = ii == jnp.take(item_start, it_tile)
    new_expert = jnp.concatenate([jnp.ones((1,), bool), it_exp[1:] != it_exp[:-1]])
    flags = first_visit.astype(jnp.int32) + 2 * new_expert.astype(jnp.int32)
    meta = (it_tile, it_exp.astype(jnp.int32), it_lo.astype(jnp.int32), it_hi.astype(jnp.int32), flags,
            n_items.reshape(1).astype(jnp.int32), src)
    return meta, slot, wts


def _moe_kernel(tile_ref, exp_ref, lo_ref, hi_ref, flag_ref, ni_ref, src_ref,
                hf_hbm, wg_ref, wu_ref, wd_ref, y_ref, xbuf, wgb, wub, wdb, sem, *, n_tiles):
    i = pl.program_id(0)
    t = tile_ref[i]
    slot = lax.rem(t, 2)

    def gather_start(tile, buf, unrolled):
        base = tile * TM

        def issue(r):
            tok = src_ref[base + r]
            pltpu.make_async_copy(hf_hbm.at[pl.ds(tok, 1)], xbuf.at[buf, pl.ds(r, 1)], sem.at[buf]).start()

        if unrolled:
            for r in range(TM):
                issue(r)
        else:
            lax.fori_loop(0, TM, lambda r, c: (issue(r), c)[1], 0)

    def gather_wait(buf):
        pltpu.make_async_copy(hf_hbm.at[pl.ds(0, TM)], xbuf.at[buf], sem.at[buf]).wait()

    @pl.when(i < ni_ref[0])
    def _():
        first_visit = (flag_ref[i] & 1) != 0
        new_expert = (flag_ref[i] & 2) != 0

        @pl.when(i == 0)
        def _():
            gather_start(0, 0, False)

        @pl.when(first_visit)
        def _():
            gather_wait(slot)

        @pl.when(jnp.logical_and(first_visit, t + 1 < n_tiles))
        def _():
            gather_start(t + 1, 1 - slot, True)

        @pl.when(new_expert)
        def _():
            wgb[...] = wg_ref[0].astype(BF16)
            wub[...] = wu_ref[0].astype(BF16)
            wdb[...] = wd_ref[0].astype(BF16)

        x = xbuf[slot].astype(BF16)
        a = _dot(x, wgb[...])
        u = _dot(x, wub[...])
        row = lax.broadcasted_iota(jnp.int32, (TM, 1), 0)
        mine = jnp.logical_and(row >= lo_ref[i], row < hi_ref[i])
        h = jnp.where(mine, (a * jax.nn.sigmoid(a)) * u, 0.0)
        yv = _dot(h.astype(BF16), wdb[...])

        @pl.when(first_visit)
        def _():
            y_ref[...] = yv

        @pl.when(jnp.logical_not(first_visit))
        def _():
            y_ref[...] += yv


def _moe(hf, meta, w_gate, w_up, w_down, layer):
    n_rows = meta[6].shape[0]
    n_tiles = n_rows // TM
    max_items = meta[0].shape[0]
    e0 = layer * N_EXPERTS
    wmap = lambda i, tile, exp, lo, hi, fl, ni, src: (e0 + exp[i], 0, 0)
    grid_spec = pltpu.PrefetchScalarGridSpec(
        num_scalar_prefetch=7,
        grid=(max_items,),
        in_specs=[
            pl.BlockSpec(memory_space=pl.ANY),
            pl.BlockSpec((1, D, D_EXPERT), wmap),
            pl.BlockSpec((1, D, D_EXPERT), wmap),
            pl.BlockSpec((1, D_EXPERT, D), wmap),
        ],
        out_specs=pl.BlockSpec((TM, D), lambda i, tile, *_: (tile[i], 0)),
        scratch_shapes=[
            pltpu.VMEM((2, TM, D), F32),
            pltpu.VMEM((D, D_EXPERT), BF16),
            pltpu.VMEM((D, D_EXPERT), BF16),
            pltpu.VMEM((D_EXPERT, D), BF16),
            pltpu.SemaphoreType.DMA((2,)),
        ],
    )
    return pl.pallas_call(
        functools.partial(_moe_kernel, n_tiles=n_tiles),
        grid_spec=grid_spec,
        out_shape=jax.ShapeDtypeStruct((n_rows, D), F32),
        compiler_params=_cparams(("arbitrary",)),
        name="moe_experts",
    )(*meta, hf, w_gate.reshape(-1, D, D_EXPERT), w_up.reshape(-1, D, D_EXPERT),
      w_down.reshape(-1, D_EXPERT, D))


def _final_kernel(pos_ref, x_ref, gf_ref, w_ref, g_ref, y_hbm, o_ref, ybuf, sem, *, n_steps):
    step = pl.program_id(0) * 8 + pl.program_id(1)
    out = _moe_residual(pos_ref, y_hbm, ybuf, sem, x_ref[...], gf_ref[0], w_ref[...], step, n_steps)
    o_ref[0] = _rms(out, g_ref[...])


def _final(pos, wts, xn, mods, y, g_final, B):
    tile = lambda b, j, p: (b * 8 + j, 0)
    grid_spec = pltpu.PrefetchScalarGridSpec(
        num_scalar_prefetch=1,
        grid=(B, 8),
        in_specs=[
            pl.BlockSpec((TM, D), tile),
            pl.BlockSpec((1, 1, D), lambda b, j, p: (b * 6 + 5, 0, 0)),
            pl.BlockSpec((TM, 2), tile),
            pl.BlockSpec((1, D), lambda b, j, p: (0, 0)),
            pl.BlockSpec(memory_space=pl.ANY),
        ],
        out_specs=pl.BlockSpec((1, TM, D), lambda b, j, p: (b, j, 0)),
        scratch_shapes=_MOE_GATHER_SCRATCH,
    )
    return pl.pallas_call(
        functools.partial(_final_kernel, n_steps=B * 8),
        grid_spec=grid_spec,
        out_shape=jax.ShapeDtypeStruct((B, 8 * TM, D), F32),
        compiler_params=_cparams(("arbitrary", "arbitrary")),
        name="final",
    )(pos, xn, mods, wts, g_final.reshape(1, D), y)


def _deinterleave(n):
    half = np.concatenate([np.arange(0, n, 2), np.arange(1, n, 2)])
    return np.concatenate([half, half])


def _prep_w_in(w_in):
    sizes = (Q_LORA, KV_LORA, QK_ROPE, FNET_WIDTH, NA_WIDTH, NA_WIDTH, NA_WIDTH)
    starts = np.concatenate([[0], np.cumsum(sizes)])
    seg = lambda i: np.arange(starts[i], starts[i + 1])
    cols = np.concatenate([seg(0), seg(3), seg(4), seg(5), seg(6), seg(1),
                           starts[2] + _deinterleave(QK_ROPE)])
    return jnp.take(w_in, jnp.asarray(cols), axis=1).astype(BF16)


def _prep_w_uq(w_uq):
    per = QK_NOPE + QK_ROPE
    cols = np.concatenate([
        np.concatenate([h * per + np.arange(QK_NOPE), h * per + QK_NOPE + _deinterleave(QK_ROPE)])
        for h in range(MLA_HEADS)])
    return jnp.take(w_uq, jnp.asarray(cols), axis=1).astype(BF16)


def _rope_tables(seq):
    half = QK_ROPE // 2
    inv_freq = ROPE_THETA ** (-jnp.arange(0, half, 2, dtype=F32) / half)
    t = jnp.arange(seq, dtype=jnp.int32)
    row = (t // GRID_W).astype(F32)
    col = (t % GRID_W).astype(F32)
    ang = jnp.concatenate([row[:, None] * inv_freq, col[:, None] * inv_freq], axis=-1)
    cos, sin = jnp.cos(ang), jnp.sin(ang)
    zeros = jnp.zeros((seq, 64), F32)
    cos_l = jnp.concatenate([cos, cos, zeros], axis=1)
    sin_l = jnp.concatenate([-sin, sin, zeros], axis=1)
    cos_c = jnp.concatenate([jnp.ones((CTX_LEN, 64), F32), jnp.zeros((CTX_LEN, 64), F32)], axis=1)
    sin_c = jnp.zeros((CTX_LEN, 128), F32)
    return jnp.concatenate([cos_c, cos_l], axis=0), jnp.concatenate([sin_c, sin_l], axis=0)


def kernel(x, c, ctx, c_ctx, w_ada, b_ada, g_attn, g_ffn, w_in, g_q, w_uq, g_kv, w_ukv, w_fnet, b_fnet,
           na_rpb, g_out, w_out, w_rg, b_rg, w_re, b_re, w_gate, w_up, w_down, g_final):
    B, S, _ = x.shape
    L = w_ada.shape[0]
    assert ctx.shape[1] == CTX_LEN == TM and S == 8 * TM and B <= 4
    T = B * 9 * TM

    cond8 = jnp.zeros((8, D), F32).at[:B].set(c).at[4].set(c_ctx)
    mods_all = _modulation(cond8, w_ada, b_ada)
    na_bias = _na_bias(na_rpb)
    cos_t, sin_t = _rope_tables(S)
    cs_lat = _dft_tables(S)
    cs_ctx = _dft_tables(CTX_LEN)
    cd_c, cd_s = _dft_cos_sin(FNET_GROUP_DIM)
    cd = jnp.concatenate([cd_c, -cd_s], axis=1).astype(BF16)

    xs = jnp.concatenate([ctx, x], axis=1).reshape(T, D)
    pending = None
    for l in range(L):
        last = l == L - 1
        j0 = 1 if last else 0
        mods = mods_all[l].reshape(48, 1, D)
        w_in_ext = _prep_w_in(w_in[l])
        w_uq_ext = _prep_w_uq(w_uq[l])
        w_router = jnp.zeros((D, ROUTER_COLS), F32).at[:, :N_GROUPS].set(w_rg[l]) \
            .at[:, N_GROUPS:N_GROUPS + N_EXPERTS].set(w_re[l])
        w_scaled = w_router * 65537.0
        w_router_hi = w_scaled - (w_scaled - w_router)
        w_router = jnp.concatenate([w_router_hi, w_router - w_router_hi], axis=1).astype(BF16)
        b_router = jnp.zeros((1, ROUTER_COLS), F32).at[0, :N_GROUPS].set(b_rg[l]) \
            .at[0, N_GROUPS:N_GROUPS + N_EXPERTS].set(b_re[l])

        if pending is None:
            zq, zkv, zf, naq, nak, navt = _in_projection(xs, mods, g_attn[l], w_in_ext, B)
        else:
            zq, zkv, zf, naq, nak, navt, xs = _in_projection(xs, mods, g_attn[l], w_in_ext, B, moe=pending)
        q, k, vt = _mla_projection(zq, zkv, g_q[l], g_kv[l], w_uq_ext, w_ukv[l].astype(BF16),
                                   cos_t, sin_t, B)
        o_mla = _mla_attention(q, k, vt, B, j0)
        o_na = _na_attention(naq, nak, navt, na_bias[l], B, j0)
        w_f = w_fnet[l].astype(BF16)
        o_f = _fnet(zf, cs_lat, cs_ctx, cd, w_f, b_fnet[l], B, j0)
        xn, hf, logits = _merge(o_mla, o_f, o_na, xs, mods, g_out[l], g_ffn[l],
                                w_out[l].astype(BF16), w_router, b_router, B, j0)
        meta, slot, wts = _route(logits)
        y = _moe(hf, meta, w_gate, w_up, w_down, l)
        if last:
            return _final(slot, wts, xn, mods, y, g_final, B)
        xs, pending = xn, (slot, wts, y, mods)
```

```python
import functools

import numpy as np
import jax
import jax.numpy as jnp
from jax import lax
from jax.experimental import pallas as pl
from jax.experimental.pallas import tpu as pltpu

F32 = jnp.float32
BF16 = jnp.bfloat16

D = 2048
GRID_W = 64
CTX_LEN = 256
EPS = 1e-6
NEG_INF = -1e30
ROPE_THETA = 10000.0

V_DIM = 128
MLA_WIDTH = D // 2
MLA_HEADS = MLA_WIDTH // V_DIM
QK_NOPE = 128
QK_ROPE = 64
Q_LORA = D // 4
KV_LORA = D // 8
FNET_WIDTH = D // 4
FNET_GROUP_DIM = 128
FNET_GROUPS = FNET_WIDTH // FNET_GROUP_DIM
NA_WIDTH = D // 4
NA_HEAD_DIM = 128
NA_HEADS = NA_WIDTH // NA_HEAD_DIM
NA_KH_MAX = 8
NA_KW = 16
N_GROUPS = 4
EXPERTS_PER_GROUP = 8
N_EXPERTS = N_GROUPS * EXPERTS_PER_GROUP
D_EXPERT = D // 4

MLA_QSCALE = (QK_NOPE + QK_ROPE) ** -0.5 * float(np.log2(np.e))
NA_QSCALE = NA_HEAD_DIM ** -0.5 * float(np.log2(np.e))
NA_WIN_ROWS = 12
TM = 256
IN_EXT = 5 * 512 + 256 + 128
ROUTER_COLS = 128
VMEM_LIMIT = 56 * 1024 * 1024


def _cparams(sem):
    return pltpu.CompilerParams(dimension_semantics=sem, vmem_limit_bytes=VMEM_LIMIT)


def _rms(v, g):
    return v * lax.rsqrt(jnp.mean(v * v, axis=-1, keepdims=True) + EPS) * g


def _dot(a, b):
    return jnp.dot(a, b, preferred_element_type=F32)


def _dot_nt(a, b):
    return lax.dot_general(a, b, (((1,), (1,)), ((), ())), preferred_element_type=F32)


def _mod_kernel(c_ref, w_ref, b_ref, o_ref):
    c = c_ref[...]
    s = c * jax.nn.sigmoid(c)
    o_ref[0] = _dot(s.astype(BF16), w_ref[0].astype(BF16)) + b_ref[0]


def _modulation(cond8, w_ada, b_ada):
    L = w_ada.shape[0]
    tn = 1024
    return pl.pallas_call(
        _mod_kernel,
        grid=(L, 6 * D // tn),
        in_specs=[
            pl.BlockSpec((8, D), lambda l, n: (0, 0)),
            pl.BlockSpec((1, D, tn), lambda l, n: (l, 0, n)),
            pl.BlockSpec((1, 1, tn), lambda l, n: (l, 0, n)),
        ],
        out_specs=pl.BlockSpec((1, 8, tn), lambda l, n: (l, 0, n)),
        out_shape=jax.ShapeDtypeStruct((L, 8, 6 * D), F32),
        compiler_params=_cparams(("arbitrary", "arbitrary")),
        name="modulation",
    )(cond8, w_ada, b_ada.reshape(L, 1, 6 * D))


def _tile_of(b, j, j0):
    return b * 9 + j0 + j


def _otile(b, j, j0):
    return b * (9 - j0) + j


def _mod_row(j, b, j0):
    return jnp.where(j0 + j == 0, 4, b)


def _mod_spec(k, j0):
    return pl.BlockSpec((1, 1, D), lambda b, j: (_mod_row(j, b, j0) * 6 + k, 0, 0))


def _stream_specs(xs, j0):
    if isinstance(xs, tuple):
        return [pl.BlockSpec((TM, D), lambda b, j, *_: (b, 0)),
                pl.BlockSpec((TM, D), lambda b, j, *_: (b * 8 + jnp.maximum(j0 + j - 1, 0), 0))], list(xs)
    return [pl.BlockSpec((TM, D), lambda b, j, *_: (_tile_of(b, j, j0), 0))], [xs]


def _stream_tile(refs, j0):
    if len(refs) == 1:
        return refs[0][...]
    return jnp.where(pl.program_id(1) + j0 == 0, refs[0][...], refs[1][...])


def _moe_residual(pos_ref, y_hbm, ybuf, sem, x, gate, w, step, n_steps):
    buf = lax.rem(step, 2)

    def start(tile, b, unrolled):
        base = tile * TM

        def issue(r):
            for k in range(2):
                pltpu.make_async_copy(y_hbm.at[pl.ds(pos_ref[2 * (base + r) + k], 1)],
                                      ybuf.at[b, k, pl.ds(r, 1)], sem.at[b]).start()

        if unrolled:
            for r in range(TM):
                issue(r)
        else:
            lax.fori_loop(0, TM, lambda r, c: (issue(r), c)[1], 0)

    def wait(b):
        for k in range(2):
            pltpu.make_async_copy(y_hbm.at[pl.ds(0, TM)], ybuf.at[b, k], sem.at[b]).wait()

    @pl.when(step == 0)
    def _():
        start(0, 0, False)

    start(jnp.minimum(step + 1, n_steps - 1), 1 - buf, True)
    wait(buf)
    out = x + gate * (w[:, 0:1] * ybuf[buf, 0] + w[:, 1:2] * ybuf[buf, 1])

    @pl.when(step == n_steps - 1)
    def _():
        wait(1 - buf)

    return out


_MOE_GATHER_SCRATCH = [pltpu.VMEM((2, 2, TM, D), F32), pltpu.SemaphoreType.DMA((2,))]


def _inproj_body(x, sh_ref, sc_ref, g_ref, w_ref, zq_ref, zkv_ref, zf_ref, naq_ref, nak_ref, navt_ref):
    h = _rms(x, g_ref[...])
    h = h * (1.0 + sc_ref[0]) + sh_ref[0]
    z = _dot(h.astype(BF16), w_ref[...])
    zq_ref[...] = z[:, 0:512]
    zf_ref[...] = z[:, 512:1024].astype(BF16)
    naq_ref[...] = (z[:, 1024:1536] * NA_QSCALE).astype(BF16)
    nak_ref[...] = z[:, 1536:2048].astype(BF16)
    for hd in range(NA_HEADS):
        c = 2048 + hd * NA_HEAD_DIM
        navt_ref[hd * NA_HEAD_DIM:(hd + 1) * NA_HEAD_DIM, :] = z[:, c:c + NA_HEAD_DIM].T.astype(BF16)
    zkv_ref[...] = z[:, 2560:IN_EXT]


def _inproj_kernel(*refs, n_stream):
    _inproj_body(_stream_tile(refs[:n_stream], 0), *refs[n_stream:])


def _inproj_moe_kernel(pos_ref, x_ref, gf_ref, wts_ref, y_hbm, *refs, n_steps):
    *refs, xs_ref, ybuf, sem = refs
    step = pl.program_id(0) * 9 + pl.program_id(1)
    x = _moe_residual(pos_ref, y_hbm, ybuf, sem, x_ref[...], gf_ref[0], wts_ref[...], step, n_steps)
    xs_ref[...] = x
    _inproj_body(x, *refs)


def _in_projection(xs, mods, g_attn, w_in_ext, B, moe=None):
    T = B * 9 * TM
    row = lambda b, j, *_: (_tile_of(b, j, 0), 0)
    const = lambda b, j, *_: (0, 0)
    mod = lambda k: pl.BlockSpec((1, 1, D), lambda b, j, *_: (_mod_row(j, b, 0) * 6 + k, 0, 0))
    in_specs = [
        pl.BlockSpec((TM, D), row),
        mod(0),
        mod(1),
        pl.BlockSpec((1, D), const),
        pl.BlockSpec((D, IN_EXT), const),
    ]
    out_specs = [
        pl.BlockSpec((TM, 512), row),
        pl.BlockSpec((TM, 384), row),
        pl.BlockSpec((TM, 512), row),
        pl.BlockSpec((TM, NA_WIDTH), row),
        pl.BlockSpec((TM, NA_WIDTH), row),
        pl.BlockSpec((NA_WIDTH, TM), lambda b, j, *_: (0, _tile_of(b, j, 0))),
    ]
    out_shape = [
        jax.ShapeDtypeStruct((T, 512), F32),
        jax.ShapeDtypeStruct((T, 384), F32),
        jax.ShapeDtypeStruct((T, 512), BF16),
        jax.ShapeDtypeStruct((T, NA_WIDTH), BF16),
        jax.ShapeDtypeStruct((T, NA_WIDTH), BF16),
        jax.ShapeDtypeStruct((NA_WIDTH, T), BF16),
    ]
    args = [xs, mods, mods, g_attn.reshape(1, D), w_in_ext]
    if moe is None:
        stream_specs, stream_args = _stream_specs(xs, 0)
        return pl.pallas_call(
            functools.partial(_inproj_kernel, n_stream=len(stream_args)),
            grid=(B, 9),
            in_specs=stream_specs + in_specs[1:],
            out_specs=out_specs,
            out_shape=out_shape,
            compiler_params=_cparams(("arbitrary", "arbitrary")),
            name="in_projection",
        )(*stream_args, *args[1:])
    pos, wts, y, mods_prev = moe
    in_specs = [in_specs[0], mod(5), pl.BlockSpec((TM, 2), row), pl.BlockSpec(memory_space=pl.ANY)] \
        + in_specs[1:]
    grid_spec = pltpu.PrefetchScalarGridSpec(
        num_scalar_prefetch=1,
        grid=(B, 9),
        in_specs=in_specs,
        out_specs=out_specs + [pl.BlockSpec((TM, D), row)],
        scratch_shapes=_MOE_GATHER_SCRATCH,
    )
    return pl.pallas_call(
        functools.partial(_inproj_moe_kernel, n_steps=B * 9),
        grid_spec=grid_spec,
        out_shape=out_shape + [jax.ShapeDtypeStruct((T, D), F32)],
        compiler_params=_cparams(("arbitrary", "arbitrary")),
        name="in_projection_moe",
    )(pos, xs, mods_prev, wts, y, *args[1:])


def _rope(r, cos_t, sin_t):
    return r * cos_t + pltpu.roll(r, 32, 1) * sin_t


def _mlaproj_kernel(zq_ref, zkv_ref, gq_ref, gkv_ref, wq_ref, wkv_ref, cos_ref, sin_ref,
                    q_ref, k_ref, vt_ref):
    cos_t = cos_ref[...]
    sin_t = sin_ref[...]
    q = _dot(_rms(zq_ref[...], gq_ref[...]).astype(BF16), wq_ref[...]) * MLA_QSCALE
    kv = _dot(_rms(zkv_ref[:, 0:KV_LORA], gkv_ref[...]).astype(BF16), wkv_ref[...])
    k_rope = _rope(zkv_ref[:, KV_LORA:KV_LORA + 128], cos_t, sin_t).astype(BF16)
    for h in range(MLA_HEADS):
        c = h * 256
        q_ref[:, c:c + 128] = q[:, c:c + 128].astype(BF16)
        q_ref[:, c + 128:c + 256] = _rope(q[:, c + 128:c + 256], cos_t, sin_t).astype(BF16)
        k_ref[:, c:c + 128] = kv[:, c:c + 128].astype(BF16)
        k_ref[:, c + 128:c + 256] = k_rope
        vt_ref[h * 128:(h + 1) * 128, :] = kv[:, c + 128:c + 256].T.astype(BF16)


def _mla_projection(zq, zkv, g_q, g_kv, w_uq_ext, w_ukv, cos_t, sin_t, B):
    T = zq.shape[0]
    row = lambda b, j: (_tile_of(b, j, 0), 0)
    const = lambda b, j: (0, 0)
    return pl.pallas_call(
        _mlaproj_kernel,
        grid=(B, 9),
        in_specs=[
            pl.BlockSpec((TM, 512), row),
            pl.BlockSpec((TM, 384), row),
            pl.BlockSpec((1, Q_LORA), const),
            pl.BlockSpec((1, KV_LORA), const),
            pl.BlockSpec((Q_LORA, 2048), const),
            pl.BlockSpec((KV_LORA, 2048), const),
            pl.BlockSpec((TM, 128), lambda b, j: (j, 0)),
            pl.BlockSpec((TM, 128), lambda b, j: (j, 0)),
        ],
        out_specs=[
            pl.BlockSpec((TM, 2048), row),
            pl.BlockSpec((TM, 2048), row),
            pl.BlockSpec((MLA_WIDTH, TM), lambda b, j: (0, _tile_of(b, j, 0))),
        ],
        out_shape=[
            jax.ShapeDtypeStruct((T, 2048), BF16),
            jax.ShapeDtypeStruct((T, 2048), BF16),
            jax.ShapeDtypeStruct((MLA_WIDTH, T), BF16),
        ],
        compiler_params=_cparams(("arbitrary", "arbitrary")),
        name="mla_projection",
    )(zq, zkv, g_q.reshape(1, -1), g_kv.reshape(1, -1), w_uq_ext, w_ukv, cos_t, sin_t)


def _mla_attn_kernel(q_ref, k_ref, vt_ref, o_ref, *, j0):
    def attend(q0, nq, nk, o0):
        st = _dot_nt(k_ref[0:nk, :], q_ref[q0:q0 + nq, :])
        m = jnp.max(st, axis=0, keepdims=True)
        p = jnp.exp2(st - m)
        l = jnp.sum(p, axis=0, keepdims=True)
        ot = _dot(vt_ref[:, 0:nk], p.astype(BF16)) / l
        o_ref[o0:o0 + nq, :] = ot.T

    if j0 == 0:
        attend(0, CTX_LEN, CTX_LEN, 0)
    for c in range(4):
        attend(CTX_LEN + 2 * c * TM, 2 * TM, 9 * TM, (1 - j0 + 2 * c) * TM)


def _mla_attention(q, k, vt, B, j0):
    rows = (9 - j0) * TM
    return pl.pallas_call(
        functools.partial(_mla_attn_kernel, j0=j0),
        grid=(B, MLA_HEADS),
        in_specs=[
            pl.BlockSpec((9 * TM, 256), lambda b, h: (b, h)),
            pl.BlockSpec((9 * TM, 256), lambda b, h: (b, h)),
            pl.BlockSpec((V_DIM, 9 * TM), lambda b, h: (h, b)),
        ],
        out_specs=pl.BlockSpec((rows, V_DIM), lambda b, h: (b, h)),
        out_shape=jax.ShapeDtypeStruct((B * rows, MLA_WIDTH), F32),
        compiler_params=_cparams(("arbitrary", "arbitrary")),
        name="mla_attention",
    )(q, k, vt)


def _na_chunk(g):
    start_row = min(max(4 * g - 4, 0), 8 * TM // GRID_W - NA_WIN_ROWS)
    pattern = 0 if g == 0 else (2 if g == 7 else 1)
    return start_row, pattern


def _na_kernel(q_ref, k_ref, vt_ref, bias_ref, o_ref, *, j0):
    def finish(parts, o0):
        m = None
        for st, _ in parts:
            pm = jnp.max(st, axis=0, keepdims=True)
            m = pm if m is None else jnp.maximum(m, pm)
        l = None
        ot = None
        for st, vt in parts:
            p = jnp.exp2(st - m)
            pl_sum = jnp.sum(p, axis=0, keepdims=True)
            po = _dot(vt, p.astype(BF16))
            l = pl_sum if l is None else l + pl_sum
            ot = po if ot is None else ot + po
        o_ref[o0:o0 + TM, :] = (ot / l).T

    if j0 == 0:
        st = _dot_nt(k_ref[0:CTX_LEN, :], q_ref[0:CTX_LEN, :])
        finish([(st, vt_ref[:, 0:CTX_LEN])], 0)
    for g in range(8):
        start_row, pattern = _na_chunk(g)
        k0 = CTX_LEN + start_row * GRID_W
        nk = NA_WIN_ROWS * GRID_W
        q = q_ref[CTX_LEN + g * TM:CTX_LEN + (g + 1) * TM, :]
        st_loc = _dot_nt(k_ref[k0:k0 + nk, :], q) + bias_ref[0, pattern]
        st_ctx = _dot_nt(k_ref[0:CTX_LEN, :], q)
        finish([(st_loc, vt_ref[:, k0:k0 + nk]), (st_ctx, vt_ref[:, 0:CTX_LEN])], (1 - j0 + g) * TM)


def _na_attention(naq, nak, navt, bias, B, j0):
    rows = (9 - j0) * TM
    return pl.pallas_call(
        functools.partial(_na_kernel, j0=j0),
        grid=(NA_HEADS, B),
        in_specs=[
            pl.BlockSpec((9 * TM, NA_HEAD_DIM), lambda h, b: (b, h)),
            pl.BlockSpec((9 * TM, NA_HEAD_DIM), lambda h, b: (b, h)),
            pl.BlockSpec((NA_HEAD_DIM, 9 * TM), lambda h, b: (h, b)),
            pl.BlockSpec((1, 3, NA_WIN_ROWS * GRID_W, TM), lambda h, b: (h, 0, 0, 0)),
        ],
        out_specs=pl.BlockSpec((rows, NA_HEAD_DIM), lambda h, b: (b, h)),
        out_shape=jax.ShapeDtypeStruct((B * rows, NA_WIDTH), F32),
        compiler_params=_cparams(("arbitrary", "arbitrary")),
        name="na_attention",
    )(naq, nak, navt, bias)


def _na_bias(rpb):
    kh, rows, nq = NA_KH_MAX, 8 * TM // GRID_W, TM // GRID_W
    drow = np.zeros((3, NA_WIN_ROWS, nq), np.int64)
    row_ok = np.zeros((3, NA_WIN_ROWS, nq), bool)
    for g in (0, 1, 7):
        start_row, pattern = _na_chunk(g)
        for kr in range(NA_WIN_ROWS):
            for qr in range(nq):
                r = 4 * g + qr
                r_start = min(max(r - kh // 2, 0), rows - kh)
                key_row = start_row + kr
                row_ok[pattern, kr, qr] = r_start <= key_row < r_start + kh
                drow[pattern, kr, qr] = min(max(key_row - r + (kh - 1), 0), 2 * kh - 2)
    cq = np.arange(GRID_W)
    ck = np.arange(GRID_W)
    col_start = np.clip(cq - NA_KW // 2, 0, GRID_W - NA_KW)
    col_ok = (ck[:, None] >= col_start[None, :]) & (ck[:, None] < col_start[None, :] + NA_KW)
    dcol = np.clip(ck[:, None] - cq[None, :] + (NA_KW - 1), 0, 2 * NA_KW - 2)
    t = jnp.take(rpb.astype(F32), jnp.asarray(drow.reshape(-1)), axis=2)
    t = jnp.take(t, jnp.asarray(dcol.reshape(-1)), axis=3)
    t = t.reshape(-1, NA_HEADS, 3, NA_WIN_ROWS, nq, GRID_W, GRID_W).transpose(0, 1, 2, 3, 5, 4, 6)
    ok = row_ok[:, :, None, :, None] & col_ok[None, None, :, None, :]
    t = jnp.where(jnp.asarray(ok)[None, None], t * float(np.log2(np.e)), NEG_INF)
    return t.reshape(-1, NA_HEADS, 3, NA_WIN_ROWS * GRID_W, nq * GRID_W)


def _fnet_kernel(z_ref, csl_ref, csc_ref, cd_ref, w_ref, b_ref, o_ref, ab_ref, *, j0, seq):
    j = pl.program_id(1) + j0

    def small_side(row0, length):
        for g in range(FNET_GROUPS):
            c = g * FNET_GROUP_DIM
            ab = _dot(z_ref[row0:row0 + length, c:c + FNET_GROUP_DIM], cd_ref[...])
            ab_ref[0:length, c:c + FNET_GROUP_DIM] = ab[:, 0:FNET_GROUP_DIM].astype(BF16)
            ab_ref[length:2 * length, c:c + FNET_GROUP_DIM] = ab[:, FNET_GROUP_DIM:].astype(BF16)

    def long_side(cs, length):
        f = _dot(cs, ab_ref[0:2 * length, :]) * (length * FNET_GROUP_DIM) ** -0.5
        o_ref[...] = _dot(f.astype(BF16), w_ref[...]) + b_ref[...]

    if j0 == 0:
        @pl.when(j == 0)
        def _():
            small_side(0, CTX_LEN)
            long_side(csc_ref[...], CTX_LEN)

    @pl.when(j == 1)
    def _():
        small_side(CTX_LEN, seq)

    @pl.when(j >= 1)
    def _():
        long_side(csl_ref[...], seq)


def _fnet(zf, cs_lat, cs_ctx, cd, w_fnet, b_fnet, B, j0):
    seq = 8 * TM
    return pl.pallas_call(
        functools.partial(_fnet_kernel, j0=j0, seq=seq),
        grid=(B, 9 - j0),
        in_specs=[
            pl.BlockSpec((9 * TM, FNET_WIDTH), lambda b, j: (b, 0)),
            pl.BlockSpec((TM, 2 * seq), lambda b, j: (jnp.maximum(j0 + j - 1, 0), 0)),
            pl.BlockSpec((CTX_LEN, 2 * CTX_LEN), lambda b, j: (0, 0)),
            pl.BlockSpec((FNET_GROUP_DIM, 2 * FNET_GROUP_DIM), lambda b, j: (0, 0)),
            pl.BlockSpec((FNET_WIDTH, FNET_WIDTH), lambda b, j: (0, 0)),
            pl.BlockSpec((1, FNET_WIDTH), lambda b, j: (0, 0)),
        ],
        out_specs=pl.BlockSpec((TM, FNET_WIDTH), lambda b, j: (_otile(b, j, j0), 0)),
        out_shape=jax.ShapeDtypeStruct((B * (9 - j0) * TM, FNET_WIDTH), F32),
        scratch_shapes=[pltpu.VMEM((2 * seq, FNET_WIDTH), BF16)],
        compiler_params=_cparams(("arbitrary", "arbitrary")),
        name="fnet",
    )(zf, cs_lat, cs_ctx, cd, w_fnet, b_fnet.reshape(1, -1))


def _dft_cos_sin(n):
    j = jnp.arange(n, dtype=jnp.int32)[:, None]
    if n <= 64:
        ang = ((j * j.T) % n).astype(F32) * (2.0 * np.pi / n)
        return jnp.cos(ang), jnp.sin(ang)
    k1 = jnp.arange(n // 64, dtype=jnp.int32)[None, :]
    k0 = jnp.arange(64, dtype=jnp.int32)[None, :]
    a = ((j * k1 * 64) % n).astype(F32) * (2.0 * np.pi / n)
    b = ((j * k0) % n).astype(F32) * (2.0 * np.pi / n)
    ca, sa, cb, sb = jnp.cos(a), jnp.sin(a), jnp.cos(b), jnp.sin(b)
    c = ca[:, :, None] * cb[:, None, :] - sa[:, :, None] * sb[:, None, :]
    s = sa[:, :, None] * cb[:, None, :] + ca[:, :, None] * sb[:, None, :]
    return c.reshape(n, n), s.reshape(n, n)


def _dft_tables(n):
    c, s = _dft_cos_sin(n)
    return jnp.concatenate([c, s], axis=1).astype(BF16)


def _merge_kernel(*refs, n_stream, j0):
    x = _stream_tile(refs[:n_stream], j0)
    (om_ref, of_ref, on_ref, ga_ref, shf_ref, scf_ref, gout_ref, gffn_ref, wout_ref, wr_ref, br_ref,
     xn_ref, hf_ref, rw_ref, ri_ref, cnt_ref, run_ref) = refs[n_stream:]
    ym = _rms(om_ref[...], gout_ref[:, 0:MLA_WIDTH]).astype(BF16)
    yf = _rms(of_ref[...], gout_ref[:, MLA_WIDTH:MLA_WIDTH + FNET_WIDTH]).astype(BF16)
    yn = _rms(on_ref[...], gout_ref[:, MLA_WIDTH + FNET_WIDTH:]).astype(BF16)
    acc = _dot(ym, wout_ref[0:MLA_WIDTH, :])
    acc = acc + _dot(yf, wout_ref[MLA_WIDTH:MLA_WIDTH + FNET_WIDTH, :])
    acc = acc + _dot(yn, wout_ref[MLA_WIDTH + FNET_WIDTH:, :])
    xn = x + ga_ref[0] * acc
    xn_ref[...] = xn
    hf = _rms(xn, gffn_ref[...]) * (1.0 + scf_ref[0]) + shf_ref[0]
    hf_ref[...] = hf
    hi = hf.astype(BF16)
    lo = (hf - hi.astype(F32)).astype(BF16)
    a = _dot(hi, wr_ref[...])
    b = _dot(lo, wr_ref[...])
    small = a[:, ROUTER_COLS:] + (b[:, :ROUTER_COLS] + b[:, ROUTER_COLS:])
    _route_tile(a[:, :ROUTER_COLS] + small + br_ref[...], rw_ref, ri_ref, cnt_ref, run_ref)


def _route_tile(lg, rw_ref, ri_ref, cnt_ref, run_ref):
    first = jnp.logical_and(pl.program_id(0) == 0, pl.program_id(1) == 0)

    @pl.when(first)
    def _():
        run_ref[...] = jnp.zeros_like(run_ref)

    lane = lax.broadcasted_iota(jnp.int32, lg.shape, 1)
    neg = jnp.float32(-jnp.inf)

    def top(v):
        vmax = jnp.max(v, axis=1, keepdims=True)
        idx = jnp.min(jnp.where(v == vmax, lane, ROUTER_COLS), axis=1, keepdims=True)
        return vmax, idx

    in_groups = lane < N_GROUPS
    gl = jnp.where(in_groups, lg, neg)
    g_max, g_sel = top(gl)
    g_w = 1.0 / jnp.sum(jnp.where(in_groups, jnp.exp(gl - g_max), 0.0), axis=1, keepdims=True)
    e_lo = N_GROUPS + g_sel * EXPERTS_PER_GROUP
    el = jnp.where(jnp.logical_and(lane >= e_lo, lane < e_lo + EXPERTS_PER_GROUP), lg, neg)
    e1_max, i1 = top(el)
    e2_max, i2 = top(jnp.where(lane == i1, neg, el))
    t = jnp.exp(e2_max - e1_max)
    w0 = g_w / (1.0 + t)
    w1 = w0 * t
    rw_ref[...] = jnp.where(lane == 0, w0, jnp.where(lane == 1, w1, 0.0))

    row = lax.broadcasted_iota(jnp.int32, (TM, TM), 0)
    col = lax.broadcasted_iota(jnp.int32, (TM, TM), 1)
    tri = jnp.where(row >= col, 1.0, 0.0).astype(BF16)
    hot0 = lane == i1
    hot1 = lane == i2
    c0 = _dot(tri, jnp.where(hot0, 1.0, 0.0).astype(BF16))
    c1 = _dot(tri, jnp.where(hot1, 1.0, 0.0).astype(BF16))
    run = run_ref[...]
    tot0 = c0[TM - 1:TM, :]
    rank0 = jnp.sum(jnp.where(hot0, run + c0 - 1.0, 0.0), axis=1, keepdims=True)
    rank1 = jnp.sum(jnp.where(hot1, run + tot0 + c1 - 1.0, 0.0), axis=1, keepdims=True)
    run = run + tot0 + c1[TM - 1:TM, :]
    run_ref[...] = run
    cnt_ref[...] = jnp.broadcast_to(run, cnt_ref.shape).astype(jnp.int32)
    ri_ref[...] = jnp.where(lane == 0, i1 - N_GROUPS, jnp.where(lane == 1, i2 - N_GROUPS, jnp.where(
        lane == 2, rank0.astype(jnp.int32), jnp.where(lane == 3, rank1.astype(jnp.int32), 0))))


def _merge(o_mla, o_f, o_na, xs, mods, g_out, g_ffn, w_out, w_router, b_router, B, j0):
    T = B * (9 - j0) * TM
    row = lambda b, j: (_otile(b, j, j0), 0)
    const = lambda b, j: (0, 0)
    stream_specs, stream_args = _stream_specs(xs, j0)
    return pl.pallas_call(
        functools.partial(_merge_kernel, n_stream=len(stream_args), j0=j0),
        grid=(B, 9 - j0),
        in_specs=stream_specs + [
            pl.BlockSpec((TM, MLA_WIDTH), row),
            pl.BlockSpec((TM, FNET_WIDTH), row),
            pl.BlockSpec((TM, NA_WIDTH), row),
            _mod_spec(2, j0),
            _mod_spec(3, j0),
            _mod_spec(4, j0),
            pl.BlockSpec((1, D), const),
            pl.BlockSpec((1, D), const),
            pl.BlockSpec((D, D), const),
            pl.BlockSpec((D, 2 * ROUTER_COLS), const),
            pl.BlockSpec((1, ROUTER_COLS), const),
        ],
        out_specs=[
            pl.BlockSpec((TM, D), row),
            pl.BlockSpec((TM, D), row),
            pl.BlockSpec((TM, ROUTER_COLS), row),
            pl.BlockSpec((TM, ROUTER_COLS), row),
            pl.BlockSpec((8, ROUTER_COLS), const),
        ],
        out_shape=[
            jax.ShapeDtypeStruct((T, D), F32),
            jax.ShapeDtypeStruct((T, D), F32),
            jax.ShapeDtypeStruct((T, ROUTER_COLS), F32),
            jax.ShapeDtypeStruct((T, ROUTER_COLS), jnp.int32),
            jax.ShapeDtypeStruct((8, ROUTER_COLS), jnp.int32),
        ],
        scratch_shapes=[pltpu.VMEM((1, ROUTER_COLS), F32)],
        compiler_params=_cparams(("arbitrary", "arbitrary")),
        name="merge",
    )(*stream_args, o_mla, o_f, o_na, mods, mods, mods, g_out.reshape(1, D), g_ffn.reshape(1, D),
      w_out, w_router, b_router)


def _plan_kernel(cnt_ref, eid_ref, rank_ref, slot_ref, src_ref, tile_ref, exp_ref, lo_ref, hi_ref,
                 flag_ref, ni_ref, gs_ref, *, n_pairs, max_items):
    def starts(e, acc):
        gs_ref[e] = acc
        return acc + cnt_ref[e]

    lax.fori_loop(0, N_EXPERTS, starts, 0)

    def place(p, c):
        s = gs_ref[eid_ref[p]] + rank_ref[p]
        slot_ref[p] = s
        src_ref[s] = lax.shift_right_logical(p, 1)
        return c

    lax.fori_loop(0, n_pairs, place, 0, unroll=8)

    last = N_EXPERTS - 1

    def group_end(e):
        return gs_ref[e] + cnt_ref[e]

    def next_nonempty(e):
        return lax.while_loop(lambda x: jnp.logical_and(x < last, cnt_ref[jnp.minimum(x, last)] == 0),
                              lambda x: x + 1, e)

    def tile_items(t, carry):
        i, e, prev = carry
        row0 = t * TM
        e = lax.while_loop(lambda x: group_end(x) <= row0, lambda x: x + 1, e)

        def emit(state):
            i, e, prev, first, _ = state
            tile_ref[i] = t
            exp_ref[i] = e
            lo_ref[i] = jnp.clip(gs_ref[e] - row0, 0, TM)
            hi_ref[i] = jnp.clip(group_end(e) - row0, 0, TM)
            flag_ref[i] = first + 2 * (e != prev).astype(jnp.int32)
            done = group_end(e) >= row0 + TM
            e_next = jnp.where(done, e, next_nonempty(e + 1))
            return i + 1, e_next, e, jnp.int32(0), done

        i, e, prev, _, _ = lax.while_loop(lambda s: jnp.logical_not(s[4]), emit,
                                          (i, e, prev, jnp.int32(1), jnp.bool_(False)))
        return i, e, prev

    n_items, _, _ = lax.fori_loop(0, n_pairs // TM, tile_items,
                                  (jnp.int32(0), jnp.int32(0), jnp.int32(-1)))
    ni_ref[0] = n_items

    def pad(i, c):
        tile_ref[i] = tile_ref[n_items - 1]
        exp_ref[i] = exp_ref[n_items - 1]
        lo_ref[i] = 0
        hi_ref[i] = 0
        flag_ref[i] = 0
        return c

    lax.fori_loop(n_items, max_items, pad, 0)


def _plan(counts, eid, rank):
    n_pairs = eid.shape[0]
    max_items = n_pairs // TM + N_EXPERTS - 1
    smem = pl.BlockSpec(memory_space=pltpu.SMEM)
    i32 = lambda n: jax.ShapeDtypeStruct((n,), jnp.int32)
    slot, src, it_tile, it_exp, it_lo, it_hi, flags, n_items = pl.pallas_call(
        functools.partial(_plan_kernel, n_pairs=n_pairs, max_items=max_items),
        in_specs=[smem] * 3,
        out_specs=[smem] * 8,
        out_shape=[i32(n_pairs), i32(n_pairs)] + [i32(max_items)] * 5 + [i32(1)],
        scratch_shapes=[pltpu.SMEM((N_EXPERTS,), jnp.int32)],
        name="moe_plan",
    )(counts, eid, rank)
    return (it_tile, it_exp, it_lo, it_hi, flags, n_items, src), slot


def _moe_kernel(tile_ref, exp_ref, lo_ref, hi_ref, flag_ref, ni_ref, src_ref,
                hf_hbm, wg_ref, wu_ref, wd_ref, y_ref, xbuf, wgb, wub, wdb, sem, *, n_tiles):
    i = pl.program_id(0)
    t = tile_ref[i]
    slot = lax.rem(t, 2)

    def gather_start(tile, buf, unrolled):
        base = tile * TM

        def issue(r):
            tok = src_ref[base + r]
            pltpu.make_async_copy(hf_hbm.at[pl.ds(tok, 1)], xbuf.at[buf, pl.ds(r, 1)], sem.at[buf]).start()

        if unrolled:
            for r in range(TM):
                issue(r)
        else:
            lax.fori_loop(0, TM, lambda r, c: (issue(r), c)[1], 0)

    def gather_wait(buf):
        pltpu.make_async_copy(hf_hbm.at[pl.ds(0, TM)], xbuf.at[buf], sem.at[buf]).wait()

    @pl.when(i < ni_ref[0])
    def _():
        first_visit = (flag_ref[i] & 1) != 0
        new_expert = (flag_ref[i] & 2) != 0

        @pl.when(i == 0)
        def _():
            gather_start(0, 0, False)

        @pl.when(first_visit)
        def _():
            gather_wait(slot)

        @pl.when(jnp.logical_and(first_visit, t + 1 < n_tiles))
        def _():
            gather_start(t + 1, 1 - slot, True)

        @pl.when(new_expert)
        def _():
            wgb[...] = wg_ref[0].astype(BF16)
            wub[...] = wu_ref[0].astype(BF16)
            wdb[...] = wd_ref[0].astype(BF16)

        x = xbuf[slot].astype(BF16)
        a = _dot(x, wgb[...])
        u = _dot(x, wub[...])
        row = lax.broadcasted_iota(jnp.int32, (TM, 1), 0)
        mine = jnp.logical_and(row >= lo_ref[i], row < hi_ref[i])
        h = jnp.where(mine, (a * jax.nn.sigmoid(a)) * u, 0.0)
        yv = _dot(h.astype(BF16), wdb[...])

        @pl.when(first_visit)
        def _():
            y_ref[...] = yv

        @pl.when(jnp.logical_not(first_visit))
        def _():
            y_ref[...] += yv


def _moe(hf, meta, w_gate, w_up, w_down, layer):
    n_rows = meta[6].shape[0]
    n_tiles = n_rows // TM
    max_items = meta[0].shape[0]
    e0 = layer * N_EXPERTS
    wmap = lambda i, tile, exp, lo, hi, fl, ni, src: (e0 + exp[i], 0, 0)
    grid_spec = pltpu.PrefetchScalarGridSpec(
        num_scalar_prefetch=7,
        grid=(max_items,),
        in_specs=[
            pl.BlockSpec(memory_space=pl.ANY),
            pl.BlockSpec((1, D, D_EXPERT), wmap),
            pl.BlockSpec((1, D, D_EXPERT), wmap),
            pl.BlockSpec((1, D_EXPERT, D), wmap),
        ],
        out_specs=pl.BlockSpec((TM, D), lambda i, tile, *_: (tile[i], 0)),
        scratch_shapes=[
            pltpu.VMEM((2, TM, D), F32),
            pltpu.VMEM((D, D_EXPERT), BF16),
            pltpu.VMEM((D, D_EXPERT), BF16),
            pltpu.VMEM((D_EXPERT, D), BF16),
            pltpu.SemaphoreType.DMA((2,)),
        ],
    )
    return pl.pallas_call(
        functools.partial(_moe_kernel, n_tiles=n_tiles),
        grid_spec=grid_spec,
        out_shape=jax.ShapeDtypeStruct((n_rows, D), F32),
        compiler_params=_cparams(("arbitrary",)),
        name="moe_experts",
    )(*meta, hf, w_gate.reshape(-1, D, D_EXPERT), w_up.reshape(-1, D, D_EXPERT),
      w_down.reshape(-1, D_EXPERT, D))


def _final_kernel(pos_ref, x_ref, gf_ref, w_ref, g_ref, y_hbm, o_ref, ybuf, sem, *, n_steps):
    step = pl.program_id(0) * 8 + pl.program_id(1)
    out = _moe_residual(pos_ref, y_hbm, ybuf, sem, x_ref[...], gf_ref[0], w_ref[...], step, n_steps)
    o_ref[0] = _rms(out, g_ref[...])


def _final(pos, wts, xn, mods, y, g_final, B):
    tile = lambda b, j, p: (b * 8 + j, 0)
    grid_spec = pltpu.PrefetchScalarGridSpec(
        num_scalar_prefetch=1,
        grid=(B, 8),
        in_specs=[
            pl.BlockSpec((TM, D), tile),
            pl.BlockSpec((1, 1, D), lambda b, j, p: (b * 6 + 5, 0, 0)),
            pl.BlockSpec((TM, 2), tile),
            pl.BlockSpec((1, D), lambda b, j, p: (0, 0)),
            pl.BlockSpec(memory_space=pl.ANY),
        ],
        out_specs=pl.BlockSpec((1, TM, D), lambda b, j, p: (b, j, 0)),
        scratch_shapes=_MOE_GATHER_SCRATCH,
    )
    return pl.pallas_call(
        functools.partial(_final_kernel, n_steps=B * 8),
        grid_spec=grid_spec,
        out_shape=jax.ShapeDtypeStruct((B, 8 * TM, D), F32),
        compiler_params=_cparams(("arbitrary", "arbitrary")),
        name="final",
    )(pos, xn, mods, wts, g_final.reshape(1, D), y)


def _deinterleave(w):
    pairs = w.reshape(w.shape[:-1] + (w.shape[-1] // 2, 2))
    even, odd = pairs[..., 0], pairs[..., 1]
    return jnp.concatenate([even, odd, even, odd], axis=-1)


def _prep_w_in(w_in):
    sizes = (Q_LORA, KV_LORA, QK_ROPE, FNET_WIDTH, NA_WIDTH, NA_WIDTH, NA_WIDTH)
    starts = np.concatenate([[0], np.cumsum(sizes)])
    seg = lambda i: w_in[:, starts[i]:starts[i + 1]]
    return jnp.concatenate([seg(0), seg(3), seg(4), seg(5), seg(6), seg(1), _deinterleave(seg(2))],
                           axis=1).astype(BF16)


def _prep_w_uq(w_uq):
    w = w_uq.reshape(Q_LORA, MLA_HEADS, QK_NOPE + QK_ROPE)
    w = jnp.concatenate([w[..., :QK_NOPE], _deinterleave(w[..., QK_NOPE:])], axis=-1)
    return w.reshape(Q_LORA, MLA_HEADS * 256).astype(BF16)


def _rope_tables(seq):
    half = QK_ROPE // 2
    inv_freq = ROPE_THETA ** (-jnp.arange(0, half, 2, dtype=F32) / half)
    t = jnp.arange(seq, dtype=jnp.int32)
    row = (t // GRID_W).astype(F32)
    col = (t % GRID_W).astype(F32)
    ang = jnp.concatenate([row[:, None] * inv_freq, col[:, None] * inv_freq], axis=-1)
    cos, sin = jnp.cos(ang), jnp.sin(ang)
    zeros = jnp.zeros((seq, 64), F32)
    cos_l = jnp.concatenate([cos, cos, zeros], axis=1)
    sin_l = jnp.concatenate([-sin, sin, zeros], axis=1)
    cos_c = jnp.concatenate([jnp.ones((CTX_LEN, 64), F32), jnp.zeros((CTX_LEN, 64), F32)], axis=1)
    sin_c = jnp.zeros((CTX_LEN, 128), F32)
    return jnp.concatenate([cos_c, cos_l], axis=0), jnp.concatenate([sin_c, sin_l], axis=0)


def kernel(x, c, ctx, c_ctx, w_ada, b_ada, g_attn, g_ffn, w_in, g_q, w_uq, g_kv, w_ukv, w_fnet, b_fnet,
           na_rpb, g_out, w_out, w_rg, b_rg, w_re, b_re, w_gate, w_up, w_down, g_final):
    B, S, _ = x.shape
    L = w_ada.shape[0]
    assert ctx.shape[1] == CTX_LEN == TM and S == 8 * TM and B <= 4
    T = B * 9 * TM

    cond8 = jnp.zeros((8, D), F32).at[:B].set(c).at[4].set(c_ctx)
    mods_all = _modulation(cond8, w_ada, b_ada)
    na_bias = _na_bias(na_rpb)
    cos_t, sin_t = _rope_tables(S)
    cs_lat = _dft_tables(S)
    cs_ctx = _dft_tables(CTX_LEN)
    cd_c, cd_s = _dft_cos_sin(FNET_GROUP_DIM)
    cd = jnp.concatenate([cd_c, -cd_s], axis=1).astype(BF16)

    xs = (ctx.reshape(B * CTX_LEN, D), x.reshape(B * S, D))
    pending = None
    for l in range(L):
        last = l == L - 1
        j0 = 1 if last else 0
        mods = mods_all[l].reshape(48, 1, D)
        w_in_ext = _prep_w_in(w_in[l])
        w_uq_ext = _prep_w_uq(w_uq[l])
        w_router = jnp.zeros((D, ROUTER_COLS), F32).at[:, :N_GROUPS].set(w_rg[l]) \
            .at[:, N_GROUPS:N_GROUPS + N_EXPERTS].set(w_re[l])
        w_scaled = w_router * 65537.0
        w_router_hi = w_scaled - (w_scaled - w_router)
        w_router = jnp.concatenate([w_router_hi, w_router - w_router_hi], axis=1).astype(BF16)
        b_router = jnp.zeros((1, ROUTER_COLS), F32).at[0, :N_GROUPS].set(b_rg[l]) \
            .at[0, N_GROUPS:N_GROUPS + N_EXPERTS].set(b_re[l])

        if pending is None:
            zq, zkv, zf, naq, nak, navt = _in_projection(xs, mods, g_attn[l], w_in_ext, B)
        else:
            zq, zkv, zf, naq, nak, navt, xs = _in_projection(xs, mods, g_attn[l], w_in_ext, B, moe=pending)
        q, k, vt = _mla_projection(zq, zkv, g_q[l], g_kv[l], w_uq_ext, w_ukv[l].astype(BF16),
                                   cos_t, sin_t, B)
        o_mla = _mla_attention(q, k, vt, B, j0)
        o_na = _na_attention(naq, nak, navt, na_bias[l], B, j0)
        w_f = w_fnet[l].astype(BF16)
        o_f = _fnet(zf, cs_lat, cs_ctx, cd, w_f, b_fnet[l], B, j0)
        xn, hf, route_w, route_i, counts = _merge(o_mla, o_f, o_na, xs, mods, g_out[l], g_ffn[l],
                                                  w_out[l].astype(BF16), w_router, b_router, B, j0)
        wts = route_w[:, 0:2]
        meta, slot = _plan(counts[0, N_GROUPS:N_GROUPS + N_EXPERTS], route_i[:, 0:2].reshape(-1),
                           route_i[:, 2:4].reshape(-1))
        y = _moe(hf, meta, w_gate, w_up, w_down, l)
        if last:
            return _final(slot, wts, xn, mods, y, g_final, B)
        xs, pending = xn, (slot, wts, y, mods)
```

```python
import functools

import numpy as np
import jax
import jax.numpy as jnp
from jax import lax
from jax.experimental import pallas as pl
from jax.experimental.pallas import tpu as pltpu

F32 = jnp.float32
BF16 = jnp.bfloat16

D = 2048
GRID_W = 64
CTX_LEN = 256
EPS = 1e-6
NEG_INF = -1e30
ROPE_THETA = 10000.0

V_DIM = 128
MLA_WIDTH = D // 2
MLA_HEADS = MLA_WIDTH // V_DIM
QK_NOPE = 128
QK_ROPE = 64
Q_LORA = D // 4
KV_LORA = D // 8
FNET_WIDTH = D // 4
FNET_GROUP_DIM = 128
FNET_GROUPS = FNET_WIDTH // FNET_GROUP_DIM
NA_WIDTH = D // 4
NA_HEAD_DIM = 128
NA_HEADS = NA_WIDTH // NA_HEAD_DIM
NA_KH_MAX = 8
NA_KW = 16
N_GROUPS = 4
EXPERTS_PER_GROUP = 8
N_EXPERTS = N_GROUPS * EXPERTS_PER_GROUP
D_EXPERT = D // 4

MLA_QSCALE = (QK_NOPE + QK_ROPE) ** -0.5 * float(np.log2(np.e))
NA_QSCALE = NA_HEAD_DIM ** -0.5 * float(np.log2(np.e))
NA_WIN_ROWS = 12
TM = 256
IN_EXT = 5 * 512 + 256 + 128
ROUTER_COLS = 128
VMEM_LIMIT = 56 * 1024 * 1024


def _cparams(sem):
    return pltpu.CompilerParams(dimension_semantics=sem, vmem_limit_bytes=VMEM_LIMIT)


def _rms(v, g):
    return v * lax.rsqrt(jnp.mean(v * v, axis=-1, keepdims=True) + EPS) * g


def _dot(a, b):
    return jnp.dot(a, b, preferred_element_type=F32)


def _dot_nt(a, b):
    return lax.dot_general(a, b, (((1,), (1,)), ((), ())), preferred_element_type=F32)


def _mod_kernel(c_ref, w_ref, b_ref, o_ref):
    c = c_ref[...]
    s = c * jax.nn.sigmoid(c)
    o_ref[0] = _dot(s.astype(BF16), w_ref[0].astype(BF16)) + b_ref[0]


def _modulation(cond8, w_ada, b_ada):
    L = w_ada.shape[0]
    tn = 1024
    return pl.pallas_call(
        _mod_kernel,
        grid=(L, 6 * D // tn),
        in_specs=[
            pl.BlockSpec((8, D), lambda l, n: (0, 0)),
            pl.BlockSpec((1, D, tn), lambda l, n: (l, 0, n)),
            pl.BlockSpec((1, 1, tn), lambda l, n: (l, 0, n)),
        ],
        out_specs=pl.BlockSpec((1, 8, tn), lambda l, n: (l, 0, n)),
        out_shape=jax.ShapeDtypeStruct((L, 8, 6 * D), F32),
        compiler_params=_cparams(("arbitrary", "arbitrary")),
        name="modulation",
    )(cond8, w_ada, b_ada.reshape(L, 1, 6 * D))


def _tile_of(b, j, j0):
    return b * 9 + j0 + j


def _otile(b, j, j0):
    return b * (9 - j0) + j


def _mod_row(j, b, j0):
    return jnp.where(j0 + j == 0, 4, b)


def _mod_spec(k, j0):
    return pl.BlockSpec((1, 1, D), lambda b, j: (_mod_row(j, b, j0) * 6 + k, 0, 0))


def _stream_specs(xs, j0):
    if isinstance(xs, tuple):
        return [pl.BlockSpec((TM, D), lambda b, j, *_: (b, 0)),
                pl.BlockSpec((TM, D), lambda b, j, *_: (b * 8 + jnp.maximum(j0 + j - 1, 0), 0))], list(xs)
    return [pl.BlockSpec((TM, D), lambda b, j, *_: (_tile_of(b, j, j0), 0))], [xs]


def _stream_tile(refs, j0):
    if len(refs) == 1:
        return refs[0][...]
    return jnp.where(pl.program_id(1) + j0 == 0, refs[0][...], refs[1][...])


def _moe_residual(pos_ref, y_hbm, ybuf, sem, x, gate, w, step, n_steps):
    buf = lax.rem(step, 2)

    def start(tile, b, unrolled):
        base = tile * TM

        def issue(r):
            for k in range(2):
                pltpu.make_async_copy(y_hbm.at[pl.ds(pos_ref[2 * (base + r) + k], 1)],
                                      ybuf.at[b, k, pl.ds(r, 1)], sem.at[b]).start()

        if unrolled:
            for r in range(TM):
                issue(r)
        else:
            lax.fori_loop(0, TM, lambda r, c: (issue(r), c)[1], 0)

    def wait(b):
        for k in range(2):
            pltpu.make_async_copy(y_hbm.at[pl.ds(0, TM)], ybuf.at[b, k], sem.at[b]).wait()

    @pl.when(step == 0)
    def _():
        start(0, 0, False)

    start(jnp.minimum(step + 1, n_steps - 1), 1 - buf, True)
    wait(buf)
    out = x + gate * (w[:, 0:1] * ybuf[buf, 0] + w[:, 1:2] * ybuf[buf, 1])

    @pl.when(step == n_steps - 1)
    def _():
        wait(1 - buf)

    return out


_MOE_GATHER_SCRATCH = [pltpu.VMEM((2, 2, TM, D), F32), pltpu.SemaphoreType.DMA((2,))]


def _inproj_body(x, sh_ref, sc_ref, g_ref, w_ref, zq_ref, zkv_ref, zf_ref, naq_ref, nak_ref, navt_ref):
    h = _rms(x, g_ref[...])
    h = h * (1.0 + sc_ref[0]) + sh_ref[0]
    z = _dot(h.astype(BF16), w_ref[...])
    zq_ref[...] = z[:, 0:512]
    zf_ref[...] = z[:, 512:1024].astype(BF16)
    naq_ref[...] = (z[:, 1024:1536] * NA_QSCALE).astype(BF16)
    nak_ref[...] = z[:, 1536:2048].astype(BF16)
    for hd in range(NA_HEADS):
        c = 2048 + hd * NA_HEAD_DIM
        navt_ref[hd * NA_HEAD_DIM:(hd + 1) * NA_HEAD_DIM, :] = z[:, c:c + NA_HEAD_DIM].T.astype(BF16)
    zkv_ref[...] = z[:, 2560:IN_EXT]


def _inproj_kernel(*refs, n_stream):
    _inproj_body(_stream_tile(refs[:n_stream], 0), *refs[n_stream:])


def _inproj_moe_kernel(pos_ref, x_ref, gf_ref, wts_ref, y_hbm, *refs, n_steps):
    *refs, xs_ref, ybuf, sem = refs
    step = pl.program_id(0) * 9 + pl.program_id(1)
    x = _moe_residual(pos_ref, y_hbm, ybuf, sem, x_ref[...], gf_ref[0], wts_ref[...], step, n_steps)
    xs_ref[...] = x
    _inproj_body(x, *refs)


def _in_projection(xs, mods, g_attn, w_in_ext, B, moe=None):
    T = B * 9 * TM
    row = lambda b, j, *_: (_tile_of(b, j, 0), 0)
    const = lambda b, j, *_: (0, 0)
    mod = lambda k: pl.BlockSpec((1, 1, D), lambda b, j, *_: (_mod_row(j, b, 0) * 6 + k, 0, 0))
    in_specs = [
        pl.BlockSpec((TM, D), row),
        mod(0),
        mod(1),
        pl.BlockSpec((1, D), const),
        pl.BlockSpec((D, IN_EXT), const),
    ]
    out_specs = [
        pl.BlockSpec((TM, 512), row),
        pl.BlockSpec((TM, 384), row),
        pl.BlockSpec((TM, 512), row),
        pl.BlockSpec((TM, NA_WIDTH), row),
        pl.BlockSpec((TM, NA_WIDTH), row),
        pl.BlockSpec((NA_WIDTH, TM), lambda b, j, *_: (0, _tile_of(b, j, 0))),
    ]
    out_shape = [
        jax.ShapeDtypeStruct((T, 512), F32),
        jax.ShapeDtypeStruct((T, 384), F32),
        jax.ShapeDtypeStruct((T, 512), BF16),
        jax.ShapeDtypeStruct((T, NA_WIDTH), BF16),
        jax.ShapeDtypeStruct((T, NA_WIDTH), BF16),
        jax.ShapeDtypeStruct((NA_WIDTH, T), BF16),
    ]
    args = [xs, mods, mods, g_attn.reshape(1, D), w_in_ext]
    if moe is None:
        stream_specs, stream_args = _stream_specs(xs, 0)
        return pl.pallas_call(
            functools.partial(_inproj_kernel, n_stream=len(stream_args)),
            grid=(B, 9),
            in_specs=stream_specs + in_specs[1:],
            out_specs=out_specs,
            out_shape=out_shape,
            compiler_params=_cparams(("arbitrary", "arbitrary")),
            name="in_projection",
        )(*stream_args, *args[1:])
    pos, wts, y, mods_prev = moe
    in_specs = [in_specs[0], mod(5), pl.BlockSpec((TM, 2), row), pl.BlockSpec(memory_space=pl.ANY)] \
        + in_specs[1:]
    grid_spec = pltpu.PrefetchScalarGridSpec(
        num_scalar_prefetch=1,
        grid=(B, 9),
        in_specs=in_specs,
        out_specs=out_specs + [pl.BlockSpec((TM, D), row)],
        scratch_shapes=_MOE_GATHER_SCRATCH,
    )
    return pl.pallas_call(
        functools.partial(_inproj_moe_kernel, n_steps=B * 9),
        grid_spec=grid_spec,
        out_shape=out_shape + [jax.ShapeDtypeStruct((T, D), F32)],
        compiler_params=_cparams(("arbitrary", "arbitrary")),
        name="in_projection_moe",
    )(pos, xs, mods_prev, wts, y, *args[1:])


def _rope(r, cos_t, sin_t):
    return r * cos_t + pltpu.roll(r, 32, 1) * sin_t


def _mlaproj_kernel(zq_ref, zkv_ref, gq_ref, gkv_ref, wq_ref, wkv_ref, cos_ref, sin_ref,
                    q_ref, k_ref, vt_ref):
    cos_t = cos_ref[...]
    sin_t = sin_ref[...]
    q = _dot(_rms(zq_ref[...], gq_ref[...]).astype(BF16), wq_ref[...]) * MLA_QSCALE
    kv = _dot(_rms(zkv_ref[:, 0:KV_LORA], gkv_ref[...]).astype(BF16), wkv_ref[...])
    k_rope = _rope(zkv_ref[:, KV_LORA:KV_LORA + 128], cos_t, sin_t).astype(BF16)
    for h in range(MLA_HEADS):
        c = h * 256
        q_ref[:, c:c + 128] = q[:, c:c + 128].astype(BF16)
        q_ref[:, c + 128:c + 256] = _rope(q[:, c + 128:c + 256], cos_t, sin_t).astype(BF16)
        k_ref[:, c:c + 128] = kv[:, c:c + 128].astype(BF16)
        k_ref[:, c + 128:c + 256] = k_rope
        vt_ref[h * 128:(h + 1) * 128, :] = kv[:, c + 128:c + 256].T.astype(BF16)


def _mla_projection(zq, zkv, g_q, g_kv, w_uq_ext, w_ukv, cos_t, sin_t, B):
    T = zq.shape[0]
    row = lambda b, j: (_tile_of(b, j, 0), 0)
    const = lambda b, j: (0, 0)
    return pl.pallas_call(
        _mlaproj_kernel,
        grid=(B, 9),
        in_specs=[
            pl.BlockSpec((TM, 512), row),
            pl.BlockSpec((TM, 384), row),
            pl.BlockSpec((1, Q_LORA), const),
            pl.BlockSpec((1, KV_LORA), const),
            pl.BlockSpec((Q_LORA, 2048), const),
            pl.BlockSpec((KV_LORA, 2048), const),
            pl.BlockSpec((TM, 128), lambda b, j: (j, 0)),
            pl.BlockSpec((TM, 128), lambda b, j: (j, 0)),
        ],
        out_specs=[
            pl.BlockSpec((TM, 2048), row),
            pl.BlockSpec((TM, 2048), row),
            pl.BlockSpec((MLA_WIDTH, TM), lambda b, j: (0, _tile_of(b, j, 0))),
        ],
        out_shape=[
            jax.ShapeDtypeStruct((T, 2048), BF16),
            jax.ShapeDtypeStruct((T, 2048), BF16),
            jax.ShapeDtypeStruct((MLA_WIDTH, T), BF16),
        ],
        compiler_params=_cparams(("arbitrary", "arbitrary")),
        name="mla_projection",
    )(zq, zkv, g_q.reshape(1, -1), g_kv.reshape(1, -1), w_uq_ext, w_ukv, cos_t, sin_t)


def _mla_attn_kernel(q_ref, k_ref, vt_ref, o_ref, *, j0):
    def attend(q0, nq, nk, o0):
        st = _dot_nt(k_ref[0:nk, :], q_ref[q0:q0 + nq, :])
        m = jnp.max(st, axis=0, keepdims=True)
        p = jnp.exp2(st - m)
        l = jnp.sum(p, axis=0, keepdims=True)
        ot = _dot(vt_ref[:, 0:nk], p.astype(BF16)) / l
        o_ref[o0:o0 + nq, :] = ot.T

    if j0 == 0:
        attend(0, CTX_LEN, CTX_LEN, 0)
    for c in range(4):
        attend(CTX_LEN + 2 * c * TM, 2 * TM, 9 * TM, (1 - j0 + 2 * c) * TM)


def _mla_attention(q, k, vt, B, j0):
    rows = (9 - j0) * TM
    return pl.pallas_call(
        functools.partial(_mla_attn_kernel, j0=j0),
        grid=(B, MLA_HEADS),
        in_specs=[
            pl.BlockSpec((9 * TM, 256), lambda b, h: (b, h)),
            pl.BlockSpec((9 * TM, 256), lambda b, h: (b, h)),
            pl.BlockSpec((V_DIM, 9 * TM), lambda b, h: (h, b)),
        ],
        out_specs=pl.BlockSpec((rows, V_DIM), lambda b, h: (b, h)),
        out_shape=jax.ShapeDtypeStruct((B * rows, MLA_WIDTH), F32),
        compiler_params=_cparams(("arbitrary", "arbitrary")),
        name="mla_attention",
    )(q, k, vt)


def _na_chunk(g):
    start_row = min(max(4 * g - 4, 0), 8 * TM // GRID_W - NA_WIN_ROWS)
    pattern = 0 if g == 0 else (2 if g == 7 else 1)
    return start_row, pattern


def _na_kernel(q_ref, k_ref, vt_ref, bias_ref, o_ref, *, j0):
    def finish(parts, o0):
        m = None
        for st, _ in parts:
            pm = jnp.max(st, axis=0, keepdims=True)
            m = pm if m is None else jnp.maximum(m, pm)
        l = None
        ot = None
        for st, vt in parts:
            p = jnp.exp2(st - m)
            pl_sum = jnp.sum(p, axis=0, keepdims=True)
            po = _dot(vt, p.astype(BF16))
            l = pl_sum if l is None else l + pl_sum
            ot = po if ot is None else ot + po
        o_ref[o0:o0 + TM, :] = (ot / l).T

    if j0 == 0:
        st = _dot_nt(k_ref[0:CTX_LEN, :], q_ref[0:CTX_LEN, :])
        finish([(st, vt_ref[:, 0:CTX_LEN])], 0)
    for g in range(8):
        start_row, pattern = _na_chunk(g)
        k0 = CTX_LEN + start_row * GRID_W
        nk = NA_WIN_ROWS * GRID_W
        q = q_ref[CTX_LEN + g * TM:CTX_LEN + (g + 1) * TM, :]
        st_loc = _dot_nt(k_ref[k0:k0 + nk, :], q) + bias_ref[0, pattern]
        st_ctx = _dot_nt(k_ref[0:CTX_LEN, :], q)
        finish([(st_loc, vt_ref[:, k0:k0 + nk]), (st_ctx, vt_ref[:, 0:CTX_LEN])], (1 - j0 + g) * TM)


def _na_attention(naq, nak, navt, bias, B, j0):
    rows = (9 - j0) * TM
    return pl.pallas_call(
        functools.partial(_na_kernel, j0=j0),
        grid=(NA_HEADS, B),
        in_specs=[
            pl.BlockSpec((9 * TM, NA_HEAD_DIM), lambda h, b: (b, h)),
            pl.BlockSpec((9 * TM, NA_HEAD_DIM), lambda h, b: (b, h)),
            pl.BlockSpec((NA_HEAD_DIM, 9 * TM), lambda h, b: (h, b)),
            pl.BlockSpec((1, 3, NA_WIN_ROWS * GRID_W, TM), lambda h, b: (h, 0, 0, 0)),
        ],
        out_specs=pl.BlockSpec((rows, NA_HEAD_DIM), lambda h, b: (b, h)),
        out_shape=jax.ShapeDtypeStruct((B * rows, NA_WIDTH), F32),
        compiler_params=_cparams(("arbitrary", "arbitrary")),
        name="na_attention",
    )(naq, nak, navt, bias)


def _na_bias(rpb):
    kh, rows, nq = NA_KH_MAX, 8 * TM // GRID_W, TM // GRID_W
    cq = np.arange(GRID_W)
    ck = np.arange(GRID_W)
    col_start = np.clip(cq - NA_KW // 2, 0, GRID_W - NA_KW)
    col_ok = (ck[:, None] >= col_start[None, :]) & (ck[:, None] < col_start[None, :] + NA_KW)
    dcol = np.clip(ck[:, None] - cq[None, :] + (NA_KW - 1), 0, 2 * NA_KW - 2)
    blocks = jnp.take(rpb.astype(F32) * float(np.log2(np.e)), jnp.asarray(dcol.reshape(-1)), axis=3)
    blocks = blocks.reshape(rpb.shape[:3] + (GRID_W, GRID_W))
    blocks = jnp.where(jnp.asarray(col_ok), blocks, NEG_INF)
    masked = jnp.full(rpb.shape[:2] + (GRID_W, GRID_W), NEG_INF, F32)
    patterns = []
    for g in (0, 1, 7):
        start_row, _ = _na_chunk(g)
        key_rows = []
        for kr in range(NA_WIN_ROWS):
            key_row = start_row + kr
            row = []
            for qr in range(nq):
                r = 4 * g + qr
                r_start = min(max(r - kh // 2, 0), rows - kh)
                in_rows = r_start <= key_row < r_start + kh
                row.append(blocks[:, :, key_row - r + (kh - 1)] if in_rows else masked)
            key_rows.append(jnp.concatenate(row, axis=-1))
        patterns.append(jnp.concatenate(key_rows, axis=-2))
    return jnp.stack(patterns, axis=2)


def _fnet_kernel(z_ref, csl_ref, csc_ref, cd_ref, w_ref, b_ref, o_ref, ab_ref, *, j0, seq):
    j = pl.program_id(1) + j0

    def small_side(row0, length):
        for g in range(FNET_GROUPS):
            c = g * FNET_GROUP_DIM
            ab = _dot(z_ref[row0:row0 + length, c:c + FNET_GROUP_DIM], cd_ref[...])
            ab_ref[0:length, c:c + FNET_GROUP_DIM] = ab[:, 0:FNET_GROUP_DIM].astype(BF16)
            ab_ref[length:2 * length, c:c + FNET_GROUP_DIM] = ab[:, FNET_GROUP_DIM:].astype(BF16)

    def long_side(cs, length):
        f = _dot(cs, ab_ref[0:2 * length, :]) * (length * FNET_GROUP_DIM) ** -0.5
        o_ref[...] = _dot(f.astype(BF16), w_ref[...]) + b_ref[...]

    if j0 == 0:
        @pl.when(j == 0)
        def _():
            small_side(0, CTX_LEN)
            long_side(csc_ref[...], CTX_LEN)

    @pl.when(j == 1)
    def _():
        small_side(CTX_LEN, seq)

    @pl.when(j >= 1)
    def _():
        long_side(csl_ref[...], seq)


def _fnet(zf, cs_lat, cs_ctx, cd, w_fnet, b_fnet, B, j0):
    seq = 8 * TM
    return pl.pallas_call(
        functools.partial(_fnet_kernel, j0=j0, seq=seq),
        grid=(B, 9 - j0),
        in_specs=[
            pl.BlockSpec((9 * TM, FNET_WIDTH), lambda b, j: (b, 0)),
            pl.BlockSpec((TM, 2 * seq), lambda b, j: (jnp.maximum(j0 + j - 1, 0), 0)),
            pl.BlockSpec((CTX_LEN, 2 * CTX_LEN), lambda b, j: (0, 0)),
            pl.BlockSpec((FNET_GROUP_DIM, 2 * FNET_GROUP_DIM), lambda b, j: (0, 0)),
            pl.BlockSpec((FNET_WIDTH, FNET_WIDTH), lambda b, j: (0, 0)),
            pl.BlockSpec((1, FNET_WIDTH), lambda b, j: (0, 0)),
        ],
        out_specs=pl.BlockSpec((TM, FNET_WIDTH), lambda b, j: (_otile(b, j, j0), 0)),
        out_shape=jax.ShapeDtypeStruct((B * (9 - j0) * TM, FNET_WIDTH), F32),
        scratch_shapes=[pltpu.VMEM((2 * seq, FNET_WIDTH), BF16)],
        compiler_params=_cparams(("arbitrary", "arbitrary")),
        name="fnet",
    )(zf, cs_lat, cs_ctx, cd, w_fnet, b_fnet.reshape(1, -1))


def _dft_cos_sin(n):
    j = jnp.arange(n, dtype=jnp.int32)[:, None]
    if n <= 64:
        ang = ((j * j.T) % n).astype(F32) * (2.0 * np.pi / n)
        return jnp.cos(ang), jnp.sin(ang)
    k1 = jnp.arange(n // 64, dtype=jnp.int32)[None, :]
    k0 = jnp.arange(64, dtype=jnp.int32)[None, :]
    a = ((j * k1 * 64) % n).astype(F32) * (2.0 * np.pi / n)
    b = ((j * k0) % n).astype(F32) * (2.0 * np.pi / n)
    ca, sa, cb, sb = jnp.cos(a), jnp.sin(a), jnp.cos(b), jnp.sin(b)
    c = ca[:, :, None] * cb[:, None, :] - sa[:, :, None] * sb[:, None, :]
    s = sa[:, :, None] * cb[:, None, :] + ca[:, :, None] * sb[:, None, :]
    return c.reshape(n, n), s.reshape(n, n)


def _dft_tables(n):
    c, s = _dft_cos_sin(n)
    return jnp.concatenate([c, s], axis=1).astype(BF16)


def _merge_kernel(*refs, n_stream, j0):
    x = _stream_tile(refs[:n_stream], j0)
    (om_ref, of_ref, on_ref, ga_ref, shf_ref, scf_ref, gout_ref, gffn_ref, wout_ref, wr_ref, br_ref,
     xn_ref, hf_ref, rw_ref, ri_ref, cnt_ref, run_ref) = refs[n_stream:]
    ym = _rms(om_ref[...], gout_ref[:, 0:MLA_WIDTH]).astype(BF16)
    yf = _rms(of_ref[...], gout_ref[:, MLA_WIDTH:MLA_WIDTH + FNET_WIDTH]).astype(BF16)
    yn = _rms(on_ref[...], gout_ref[:, MLA_WIDTH + FNET_WIDTH:]).astype(BF16)
    acc = _dot(ym, wout_ref[0:MLA_WIDTH, :])
    acc = acc + _dot(yf, wout_ref[MLA_WIDTH:MLA_WIDTH + FNET_WIDTH, :])
    acc = acc + _dot(yn, wout_ref[MLA_WIDTH + FNET_WIDTH:, :])
    xn = x + ga_ref[0] * acc
    xn_ref[...] = xn
    hf = _rms(xn, gffn_ref[...]) * (1.0 + scf_ref[0]) + shf_ref[0]
    hf_ref[...] = hf
    hi = hf.astype(BF16)
    lo = (hf - hi.astype(F32)).astype(BF16)
    a = _dot(hi, wr_ref[...])
    b = _dot(lo, wr_ref[...])
    small = a[:, ROUTER_COLS:] + (b[:, :ROUTER_COLS] + b[:, ROUTER_COLS:])
    _route_tile(a[:, :ROUTER_COLS] + small + br_ref[...], rw_ref, ri_ref, cnt_ref, run_ref)


def _route_tile(lg, rw_ref, ri_ref, cnt_ref, run_ref):
    first = jnp.logical_and(pl.program_id(0) == 0, pl.program_id(1) == 0)

    @pl.when(first)
    def _():
        run_ref[...] = jnp.zeros_like(run_ref)

    lane = lax.broadcasted_iota(jnp.int32, lg.shape, 1)
    neg = jnp.float32(-jnp.inf)

    def top(v):
        vmax = jnp.max(v, axis=1, keepdims=True)
        idx = jnp.min(jnp.where(v == vmax, lane, ROUTER_COLS), axis=1, keepdims=True)
        return vmax, idx

    in_groups = lane < N_GROUPS
    gl = jnp.where(in_groups, lg, neg)
    g_max, g_sel = top(gl)
    g_w = 1.0 / jnp.sum(jnp.where(in_groups, jnp.exp(gl - g_max), 0.0), axis=1, keepdims=True)
    e_lo = N_GROUPS + g_sel * EXPERTS_PER_GROUP
    el = jnp.where(jnp.logical_and(lane >= e_lo, lane < e_lo + EXPERTS_PER_GROUP), lg, neg)
    e1_max, i1 = top(el)
    e2_max, i2 = top(jnp.where(lane == i1, neg, el))
    t = jnp.exp(e2_max - e1_max)
    w0 = g_w / (1.0 + t)
    w1 = w0 * t
    rw_ref[...] = jnp.where(lane == 0, w0, jnp.where(lane == 1, w1, 0.0))

    row = lax.broadcasted_iota(jnp.int32, (TM, TM), 0)
    col = lax.broadcasted_iota(jnp.int32, (TM, TM), 1)
    tri = jnp.where(row >= col, 1.0, 0.0).astype(BF16)
    hot0 = lane == i1
    hot1 = lane == i2
    c0 = _dot(tri, jnp.where(hot0, 1.0, 0.0).astype(BF16))
    c1 = _dot(tri, jnp.where(hot1, 1.0, 0.0).astype(BF16))
    run = run_ref[...]
    tot0 = c0[TM - 1:TM, :]
    rank0 = jnp.sum(jnp.where(hot0, run + c0 - 1.0, 0.0), axis=1, keepdims=True)
    rank1 = jnp.sum(jnp.where(hot1, run + tot0 + c1 - 1.0, 0.0), axis=1, keepdims=True)
    run = run + tot0 + c1[TM - 1:TM, :]
    run_ref[...] = run
    cnt_ref[...] = jnp.broadcast_to(run, cnt_ref.shape).astype(jnp.int32)
    ri_ref[...] = jnp.where(lane == 0, i1 - N_GROUPS, jnp.where(lane == 1, i2 - N_GROUPS, jnp.where(
        lane == 2, rank0.astype(jnp.int32), jnp.where(lane == 3, rank1.astype(jnp.int32), 0))))


def _merge(o_mla, o_f, o_na, xs, mods, g_out, g_ffn, w_out, w_router, b_router, B, j0):
    T = B * (9 - j0) * TM
    row = lambda b, j: (_otile(b, j, j0), 0)
    const = lambda b, j: (0, 0)
    stream_specs, stream_args = _stream_specs(xs, j0)
    return pl.pallas_call(
        functools.partial(_merge_kernel, n_stream=len(stream_args), j0=j0),
        grid=(B, 9 - j0),
        in_specs=stream_specs + [
            pl.BlockSpec((TM, MLA_WIDTH), row),
            pl.BlockSpec((TM, FNET_WIDTH), row),
            pl.BlockSpec((TM, NA_WIDTH), row),
            _mod_spec(2, j0),
            _mod_spec(3, j0),
            _mod_spec(4, j0),
            pl.BlockSpec((1, D), const),
            pl.BlockSpec((1, D), const),
            pl.BlockSpec((D, D), const),
            pl.BlockSpec((D, 2 * ROUTER_COLS), const),
            pl.BlockSpec((1, ROUTER_COLS), const),
        ],
        out_specs=[
            pl.BlockSpec((TM, D), row),
            pl.BlockSpec((TM, D), row),
            pl.BlockSpec((TM, ROUTER_COLS), row),
            pl.BlockSpec((TM, ROUTER_COLS), row),
            pl.BlockSpec((8, ROUTER_COLS), const),
        ],
        out_shape=[
            jax.ShapeDtypeStruct((T, D), F32),
            jax.ShapeDtypeStruct((T, D), F32),
            jax.ShapeDtypeStruct((T, ROUTER_COLS), F32),
            jax.ShapeDtypeStruct((T, ROUTER_COLS), jnp.int32),
            jax.ShapeDtypeStruct((8, ROUTER_COLS), jnp.int32),
        ],
        scratch_shapes=[pltpu.VMEM((1, ROUTER_COLS), F32)],
        compiler_params=_cparams(("arbitrary", "arbitrary")),
        name="merge",
    )(*stream_args, o_mla, o_f, o_na, mods, mods, mods, g_out.reshape(1, D), g_ffn.reshape(1, D),
      w_out, w_router, b_router)


def _plan_kernel(cnt_ref, slot_ref, src_ref, tile_ref, exp_ref, lo_ref, hi_ref, flag_ref, ni_ref, gs_ref,
                 *, n_pairs, max_items):
    def starts(e, acc):
        gs_ref[e] = acc
        return acc + cnt_ref[e]

    lax.fori_loop(0, N_EXPERTS, starts, 0)

    def place(p, c):
        src_ref[slot_ref[p]] = lax.shift_right_logical(p, 1)
        return c

    lax.fori_loop(0, n_pairs, place, 0, unroll=16)

    last = N_EXPERTS - 1

    def group_end(e):
        return gs_ref[e] + cnt_ref[e]

    def next_nonempty(e):
        return lax.while_loop(lambda x: jnp.logical_and(x < last, cnt_ref[jnp.minimum(x, last)] == 0),
                              lambda x: x + 1, e)

    def tile_items(t, carry):
        i, e, prev = carry
        row0 = t * TM
        e = lax.while_loop(lambda x: group_end(x) <= row0, lambda x: x + 1, e)

        def emit(state):
            i, e, prev, first, _ = state
            tile_ref[i] = t
            exp_ref[i] = e
            lo_ref[i] = jnp.clip(gs_ref[e] - row0, 0, TM)
            hi_ref[i] = jnp.clip(group_end(e) - row0, 0, TM)
            flag_ref[i] = first + 2 * (e != prev).astype(jnp.int32)
            done = group_end(e) >= row0 + TM
            e_next = jnp.where(done, e, next_nonempty(e + 1))
            return i + 1, e_next, e, jnp.int32(0), done

        i, e, prev, _, _ = lax.while_loop(lambda s: jnp.logical_not(s[4]), emit,
                                          (i, e, prev, jnp.int32(1), jnp.bool_(False)))
        return i, e, prev

    n_items, _, _ = lax.fori_loop(0, n_pairs // TM, tile_items,
                                  (jnp.int32(0), jnp.int32(0), jnp.int32(-1)))
    ni_ref[0] = n_items

    def pad(i, c):
        tile_ref[i] = tile_ref[n_items - 1]
        exp_ref[i] = exp_ref[n_items - 1]
        lo_ref[i] = 0
        hi_ref[i] = 0
        flag_ref[i] = 0
        return c

    lax.fori_loop(n_items, max_items, pad, 0)


def _plan(counts, eid, rank):
    n_pairs = eid.shape[0]
    max_items = n_pairs // TM + N_EXPERTS - 1
    g_start = jnp.cumsum(counts) - counts
    experts = jnp.arange(N_EXPERTS, dtype=jnp.int32)
    slot = jnp.sum(jnp.where(eid[:, None] == experts[None, :], g_start[None, :], 0), axis=1) + rank
    smem = pl.BlockSpec(memory_space=pltpu.SMEM)
    i32 = lambda n: jax.ShapeDtypeStruct((n,), jnp.int32)
    src, it_tile, it_exp, it_lo, it_hi, flags, n_items = pl.pallas_call(
        functools.partial(_plan_kernel, n_pairs=n_pairs, max_items=max_items),
        in_specs=[smem] * 2,
        out_specs=[smem] * 7,
        out_shape=[i32(n_pairs)] + [i32(max_items)] * 5 + [i32(1)],
        scratch_shapes=[pltpu.SMEM((N_EXPERTS,), jnp.int32)],
        name="moe_plan",
    )(counts, slot)
    return (it_tile, it_exp, it_lo, it_hi, flags, n_items, src), slot


def _moe_kernel(tile_ref, exp_ref, lo_ref, hi_ref, flag_ref, ni_ref, src_ref,
                hf_hbm, wg_ref, wu_ref, wd_ref, y_ref, xbuf, wgb, wub, wdb, sem, *, n_tiles):
    i = pl.program_id(0)
    t = tile_ref[i]
    slot = lax.rem(t, 2)

    def gather_start(tile, buf, unrolled):
        base = tile * TM

        def issue(r):
            tok = src_ref[base + r]
            pltpu.make_async_copy(hf_hbm.at[pl.ds(tok, 1)], xbuf.at[buf, pl.ds(r, 1)], sem.at[buf]).start()

        if unrolled:
            for r in range(TM):
                issue(r)
        else:
            lax.fori_loop(0, TM, lambda r, c: (issue(r), c)[1], 0)

    def gather_wait(buf):
        pltpu.make_async_copy(hf_hbm.at[pl.ds(0, TM)], xbuf.at[buf], sem.at[buf]).wait()

    @pl.when(i < ni_ref[0])
    def _():
        first_visit = (flag_ref[i] & 1) != 0
        new_expert = (flag_ref[i] & 2) != 0

        @pl.when(i == 0)
        def _():
            gather_start(0, 0, False)

        @pl.when(first_visit)
        def _():
            gather_wait(slot)

        @pl.when(jnp.logical_and(first_visit, t + 1 < n_tiles))
        def _():
            gather_start(t + 1, 1 - slot, True)

        @pl.when(new_expert)
        def _():
            wgb[...] = wg_ref[0].astype(BF16)
            wub[...] = wu_ref[0].astype(BF16)
            wdb[...] = wd_ref[0].astype(BF16)

        x = xbuf[slot].astype(BF16)
        a = _dot(x, wgb[...])
        u = _dot(x, wub[...])
        row = lax.broadcasted_iota(jnp.int32, (TM, 1), 0)
        mine = jnp.logical_and(row >= lo_ref[i], row < hi_ref[i])
        h = jnp.where(mine, (a * jax.nn.sigmoid(a)) * u, 0.0)
        yv = _dot(h.astype(BF16), wdb[...])

        @pl.when(first_visit)
        def _():
            y_ref[...] = yv

        @pl.when(jnp.logical_not(first_visit))
        def _():
            y_ref[...] += yv


def _moe(hf, meta, w_gate, w_up, w_down, layer):
    n_rows = meta[6].shape[0]
    n_tiles = n_rows // TM
    max_items = meta[0].shape[0]
    e0 = layer * N_EXPERTS
    wmap = lambda i, tile, exp, lo, hi, fl, ni, src: (e0 + exp[i], 0, 0)
    grid_spec = pltpu.PrefetchScalarGridSpec(
        num_scalar_prefetch=7,
        grid=(max_items,),
        in_specs=[
            pl.BlockSpec(memory_space=pl.ANY),
            pl.BlockSpec((1, D, D_EXPERT), wmap),
            pl.BlockSpec((1, D, D_EXPERT), wmap),
            pl.BlockSpec((1, D_EXPERT, D), wmap),
        ],
        out_specs=pl.BlockSpec((TM, D), lambda i, tile, *_: (tile[i], 0)),
        scratch_shapes=[
            pltpu.VMEM((2, TM, D), F32),
            pltpu.VMEM((D, D_EXPERT), BF16),
            pltpu.VMEM((D, D_EXPERT), BF16),
            pltpu.VMEM((D_EXPERT, D), BF16),
            pltpu.SemaphoreType.DMA((2,)),
        ],
    )
    return pl.pallas_call(
        functools.partial(_moe_kernel, n_tiles=n_tiles),
        grid_spec=grid_spec,
        out_shape=jax.ShapeDtypeStruct((n_rows, D), F32),
        compiler_params=_cparams(("arbitrary",)),
        name="moe_experts",
    )(*meta, hf, w_gate.reshape(-1, D, D_EXPERT), w_up.reshape(-1, D, D_EXPERT),
      w_down.reshape(-1, D_EXPERT, D))


def _final_kernel(pos_ref, x_ref, gf_ref, w_ref, g_ref, y_hbm, o_ref, ybuf, sem, *, n_steps):
    step = pl.program_id(0) * 8 + pl.program_id(1)
    out = _moe_residual(pos_ref, y_hbm, ybuf, sem, x_ref[...], gf_ref[0], w_ref[...], step, n_steps)
    o_ref[0] = _rms(out, g_ref[...])


def _final(pos, wts, xn, mods, y, g_final, B):
    tile = lambda b, j, p: (b * 8 + j, 0)
    grid_spec = pltpu.PrefetchScalarGridSpec(
        num_scalar_prefetch=1,
        grid=(B, 8),
        in_specs=[
            pl.BlockSpec((TM, D), tile),
            pl.BlockSpec((1, 1, D), lambda b, j, p: (b * 6 + 5, 0, 0)),
            pl.BlockSpec((TM, 2), tile),
            pl.BlockSpec((1, D), lambda b, j, p: (0, 0)),
            pl.BlockSpec(memory_space=pl.ANY),
        ],
        out_specs=pl.BlockSpec((1, TM, D), lambda b, j, p: (b, j, 0)),
        scratch_shapes=_MOE_GATHER_SCRATCH,
    )
    return pl.pallas_call(
        functools.partial(_final_kernel, n_steps=B * 8),
        grid_spec=grid_spec,
        out_shape=jax.ShapeDtypeStruct((B, 8 * TM, D), F32),
        compiler_params=_cparams(("arbitrary", "arbitrary")),
        name="final",
    )(pos, xn, mods, wts, g_final.reshape(1, D), y)


def _deinterleave(w):
    pairs = w.reshape(w.shape[:-1] + (w.shape[-1] // 2, 2))
    even, odd = pairs[..., 0], pairs[..., 1]
    return jnp.concatenate([even, odd, even, odd], axis=-1)


def _prep_w_in(w_in):
    sizes = (Q_LORA, KV_LORA, QK_ROPE, FNET_WIDTH, NA_WIDTH, NA_WIDTH, NA_WIDTH)
    starts = np.concatenate([[0], np.cumsum(sizes)])
    seg = lambda i: w_in[:, starts[i]:starts[i + 1]]
    return jnp.concatenate([seg(0), seg(3), seg(4), seg(5), seg(6), seg(1), _deinterleave(seg(2))],
                           axis=1).astype(BF16)


def _prep_w_uq(w_uq):
    w = w_uq.reshape(Q_LORA, MLA_HEADS, QK_NOPE + QK_ROPE)
    w = jnp.concatenate([w[..., :QK_NOPE], _deinterleave(w[..., QK_NOPE:])], axis=-1)
    return w.reshape(Q_LORA, MLA_HEADS * 256).astype(BF16)


def _rope_tables(seq):
    half = QK_ROPE // 2
    inv_freq = ROPE_THETA ** (-jnp.arange(0, half, 2, dtype=F32) / half)
    t = jnp.arange(seq, dtype=jnp.int32)
    row = (t // GRID_W).astype(F32)
    col = (t % GRID_W).astype(F32)
    ang = jnp.concatenate([row[:, None] * inv_freq, col[:, None] * inv_freq], axis=-1)
    cos, sin = jnp.cos(ang), jnp.sin(ang)
    zeros = jnp.zeros((seq, 64), F32)
    cos_l = jnp.concatenate([cos, cos, zeros], axis=1)
    sin_l = jnp.concatenate([-sin, sin, zeros], axis=1)
    cos_c = jnp.concatenate([jnp.ones((CTX_LEN, 64), F32), jnp.zeros((CTX_LEN, 64), F32)], axis=1)
    sin_c = jnp.zeros((CTX_LEN, 128), F32)
    return jnp.concatenate([cos_c, cos_l], axis=0), jnp.concatenate([sin_c, sin_l], axis=0)


def kernel(x, c, ctx, c_ctx, w_ada, b_ada, g_attn, g_ffn, w_in, g_q, w_uq, g_kv, w_ukv, w_fnet, b_fnet,
           na_rpb, g_out, w_out, w_rg, b_rg, w_re, b_re, w_gate, w_up, w_down, g_final):
    B, S, _ = x.shape
    L = w_ada.shape[0]
    assert ctx.shape[1] == CTX_LEN == TM and S == 8 * TM and B <= 4
    T = B * 9 * TM

    cond8 = jnp.zeros((8, D), F32).at[:B].set(c).at[4].set(c_ctx)
    mods_all = _modulation(cond8, w_ada, b_ada)
    na_bias = _na_bias(na_rpb)
    cos_t, sin_t = _rope_tables(S)
    cs_lat = _dft_tables(S)
    cs_ctx = _dft_tables(CTX_LEN)
    cd_c, cd_s = _dft_cos_sin(FNET_GROUP_DIM)
    cd = jnp.concatenate([cd_c, -cd_s], axis=1).astype(BF16)

    xs = (ctx.reshape(B * CTX_LEN, D), x.reshape(B * S, D))
    pending = None
    for l in range(L):
        last = l == L - 1
        j0 = 1 if last else 0
        mods = mods_all[l].reshape(48, 1, D)
        w_in_ext = _prep_w_in(w_in[l])
        w_uq_ext = _prep_w_uq(w_uq[l])
        w_router = jnp.zeros((D, ROUTER_COLS), F32).at[:, :N_GROUPS].set(w_rg[l]) \
            .at[:, N_GROUPS:N_GROUPS + N_EXPERTS].set(w_re[l])
        w_scaled = w_router * 65537.0
        w_router_hi = w_scaled - (w_scaled - w_router)
        w_router = jnp.concatenate([w_router_hi, w_router - w_router_hi], axis=1).astype(BF16)
        b_router = jnp.zeros((1, ROUTER_COLS), F32).at[0, :N_GROUPS].set(b_rg[l]) \
            .at[0, N_GROUPS:N_GROUPS + N_EXPERTS].set(b_re[l])

        if pending is None:
            zq, zkv, zf, naq, nak, navt = _in_projection(xs, mods, g_attn[l], w_in_ext, B)
        else:
            zq, zkv, zf, naq, nak, navt, xs = _in_projection(xs, mods, g_attn[l], w_in_ext, B, moe=pending)
        q, k, vt = _mla_projection(zq, zkv, g_q[l], g_kv[l], w_uq_ext, w_ukv[l].astype(BF16),
                                   cos_t, sin_t, B)
        o_mla = _mla_attention(q, k, vt, B, j0)
        o_na = _na_attention(naq, nak, navt, na_bias[l], B, j0)
        w_f = w_fnet[l].astype(BF16)
        o_f = _fnet(zf, cs_lat, cs_ctx, cd, w_f, b_fnet[l], B, j0)
        xn, hf, route_w, route_i, counts = _merge(o_mla, o_f, o_na, xs, mods, g_out[l], g_ffn[l],
                                                  w_out[l].astype(BF16), w_router, b_router, B, j0)
        wts = route_w[:, 0:2]
        meta, slot = _plan(counts[0, N_GROUPS:N_GROUPS + N_EXPERTS], route_i[:, 0:2].reshape(-1),
                           route_i[:, 2:4].reshape(-1))
        y = _moe(hf, meta, w_gate, w_up, w_down, l)
        if last:
            return _final(slot, wts, xn, mods, y, g_final, B)
        xs, pending = xn, (slot, wts, y, mods)
```

```python
import functools

import numpy as np
import jax
import jax.numpy as jnp
from jax import lax
from jax.experimental import pallas as pl
from jax.experimental.pallas import tpu as pltpu

F32 = jnp.float32
BF16 = jnp.bfloat16

D = 2048
GRID_W = 64
CTX_LEN = 256
EPS = 1e-6
NEG_INF = -1e30
ROPE_THETA = 10000.0

V_DIM = 128
MLA_WIDTH = D // 2
MLA_HEADS = MLA_WIDTH // V_DIM
QK_NOPE = 128
QK_ROPE = 64
Q_LORA = D // 4
KV_LORA = D // 8
FNET_WIDTH = D // 4
FNET_GROUP_DIM = 128
FNET_GROUPS = FNET_WIDTH // FNET_GROUP_DIM
NA_WIDTH = D // 4
NA_HEAD_DIM = 128
NA_HEADS = NA_WIDTH // NA_HEAD_DIM
NA_KH_MAX = 8
NA_KW = 16
N_GROUPS = 4
EXPERTS_PER_GROUP = 8
N_EXPERTS = N_GROUPS * EXPERTS_PER_GROUP
D_EXPERT = D // 4

MLA_QSCALE = (QK_NOPE + QK_ROPE) ** -0.5 * float(np.log2(np.e))
MLA_CHUNK_TILES = 8
NA_QSCALE = NA_HEAD_DIM ** -0.5 * float(np.log2(np.e))
NA_WIN_ROWS = 12
TM = 256
IN_EXT = 5 * 512 + 256 + 128
ROUTER_COLS = 128
VMEM_LIMIT = 56 * 1024 * 1024


def _cparams(sem):
    return pltpu.CompilerParams(dimension_semantics=sem, vmem_limit_bytes=VMEM_LIMIT)


def _rms(v, g):
    return v * lax.rsqrt(jnp.mean(v * v, axis=-1, keepdims=True) + EPS) * g


def _dot(a, b):
    return jnp.dot(a, b, preferred_element_type=F32)


def _dot_nt(a, b):
    return lax.dot_general(a, b, (((1,), (1,)), ((), ())), preferred_element_type=F32)


def _mod_kernel(c_ref, w_ref, b_ref, o_ref):
    c = c_ref[...]
    s = c * jax.nn.sigmoid(c)
    o_ref[0] = _dot(s.astype(BF16), w_ref[0].astype(BF16)) + b_ref[0]


def _modulation(cond8, w_ada, b_ada):
    L = w_ada.shape[0]
    tn = 1024
    return pl.pallas_call(
        _mod_kernel,
        grid=(L, 6 * D // tn),
        in_specs=[
            pl.BlockSpec((8, D), lambda l, n: (0, 0)),
            pl.BlockSpec((1, D, tn), lambda l, n: (l, 0, n)),
            pl.BlockSpec((1, 1, tn), lambda l, n: (l, 0, n)),
        ],
        out_specs=pl.BlockSpec((1, 8, tn), lambda l, n: (l, 0, n)),
        out_shape=jax.ShapeDtypeStruct((L, 8, 6 * D), F32),
        compiler_params=_cparams(("arbitrary", "arbitrary")),
        name="modulation",
    )(cond8, w_ada, b_ada.reshape(L, 1, 6 * D))


def _tile_of(b, j, j0):
    return b * 9 + j0 + j


def _otile(b, j, j0):
    return b * (9 - j0) + j


def _mod_row(j, b, j0):
    return jnp.where(j0 + j == 0, 4, b)


def _mod_spec(k, j0):
    return pl.BlockSpec((1, 1, D), lambda b, j: (_mod_row(j, b, j0) * 6 + k, 0, 0))


def _stream_specs(xs, j0):
    if isinstance(xs, tuple):
        return [pl.BlockSpec((TM, D), lambda b, j, *_: (b, 0)),
                pl.BlockSpec((TM, D), lambda b, j, *_: (b * 8 + jnp.maximum(j0 + j - 1, 0), 0))], list(xs)
    return [pl.BlockSpec((TM, D), lambda b, j, *_: (_tile_of(b, j, j0), 0))], [xs]


def _stream_tile(refs, j0):
    if len(refs) == 1:
        return refs[0][...]
    return jnp.where(pl.program_id(1) + j0 == 0, refs[0][...], refs[1][...])


def _moe_residual(pos_ref, y_hbm, ybuf, sem, x, gate, w, step, n_steps):
    buf = lax.rem(step, 2)

    def start(tile, b, unrolled):
        base = tile * TM

        def issue(r):
            for k in range(2):
                pltpu.make_async_copy(y_hbm.at[pl.ds(pos_ref[2 * (base + r) + k], 1)],
                                      ybuf.at[b, k, pl.ds(r, 1)], sem.at[b]).start()

        if unrolled:
            for r in range(TM):
                issue(r)
        else:
            lax.fori_loop(0, TM, lambda r, c: (issue(r), c)[1], 0)

    def wait(b):
        for k in range(2):
            pltpu.make_async_copy(y_hbm.at[pl.ds(0, TM)], ybuf.at[b, k], sem.at[b]).wait()

    @pl.when(step == 0)
    def _():
        start(0, 0, False)

    start(jnp.minimum(step + 1, n_steps - 1), 1 - buf, True)
    wait(buf)
    out = x + gate * (w[:, 0:1] * ybuf[buf, 0] + w[:, 1:2] * ybuf[buf, 1])

    @pl.when(step == n_steps - 1)
    def _():
        wait(1 - buf)

    return out


_MOE_GATHER_SCRATCH = [pltpu.VMEM((2, 2, TM, D), F32), pltpu.SemaphoreType.DMA((2,))]


def _inproj_body(x, sh_ref, sc_ref, g_ref, w_ref, zq_ref, zkv_ref, zf_ref, naq_ref, nak_ref, navt_ref):
    h = _rms(x, g_ref[...])
    h = h * (1.0 + sc_ref[0]) + sh_ref[0]
    z = _dot(h.astype(BF16), w_ref[...])
    zq_ref[...] = z[:, 0:512]
    zf_ref[...] = z[:, 512:1024].astype(BF16)
    naq_ref[...] = (z[:, 1024:1536] * NA_QSCALE).astype(BF16)
    nak_ref[...] = z[:, 1536:2048].astype(BF16)
    for hd in range(NA_HEADS):
        c = 2048 + hd * NA_HEAD_DIM
        navt_ref[hd * NA_HEAD_DIM:(hd + 1) * NA_HEAD_DIM, :] = z[:, c:c + NA_HEAD_DIM].T.astype(BF16)
    zkv_ref[...] = z[:, 2560:IN_EXT]


def _inproj_kernel(*refs, n_stream):
    _inproj_body(_stream_tile(refs[:n_stream], 0), *refs[n_stream:])


def _inproj_moe_kernel(pos_ref, x_ref, gf_ref, wts_ref, y_hbm, *refs, n_steps):
    *refs, xs_ref, ybuf, sem = refs
    step = pl.program_id(0) * 9 + pl.program_id(1)
    x = _moe_residual(pos_ref, y_hbm, ybuf, sem, x_ref[...], gf_ref[0], wts_ref[...], step, n_steps)
    xs_ref[...] = x
    _inproj_body(x, *refs)


def _in_projection(xs, mods, g_attn, w_in_ext, B, moe=None):
    T = B * 9 * TM
    row = lambda b, j, *_: (_tile_of(b, j, 0), 0)
    const = lambda b, j, *_: (0, 0)
    mod = lambda k: pl.BlockSpec((1, 1, D), lambda b, j, *_: (_mod_row(j, b, 0) * 6 + k, 0, 0))
    in_specs = [
        pl.BlockSpec((TM, D), row),
        mod(0),
        mod(1),
        pl.BlockSpec((1, D), const),
        pl.BlockSpec((D, IN_EXT), const),
    ]
    out_specs = [
        pl.BlockSpec((TM, 512), row),
        pl.BlockSpec((TM, 384), row),
        pl.BlockSpec((TM, 512), row),
        pl.BlockSpec((TM, NA_WIDTH), row),
        pl.BlockSpec((TM, NA_WIDTH), row),
        pl.BlockSpec((NA_WIDTH, TM), lambda b, j, *_: (0, _tile_of(b, j, 0))),
    ]
    out_shape = [
        jax.ShapeDtypeStruct((T, 512), F32),
        jax.ShapeDtypeStruct((T, 384), F32),
        jax.ShapeDtypeStruct((T, 512), BF16),
        jax.ShapeDtypeStruct((T, NA_WIDTH), BF16),
        jax.ShapeDtypeStruct((T, NA_WIDTH), BF16),
        jax.ShapeDtypeStruct((NA_WIDTH, T), BF16),
    ]
    args = [xs, mods, mods, g_attn.reshape(1, D), w_in_ext]
    if moe is None:
        stream_specs, stream_args = _stream_specs(xs, 0)
        return pl.pallas_call(
            functools.partial(_inproj_kernel, n_stream=len(stream_args)),
            grid=(B, 9),
            in_specs=stream_specs + in_specs[1:],
            out_specs=out_specs,
            out_shape=out_shape,
            compiler_params=_cparams(("arbitrary", "arbitrary")),
            name="in_projection",
        )(*stream_args, *args[1:])
    pos, wts, y, mods_prev = moe
    in_specs = [in_specs[0], mod(5), pl.BlockSpec((TM, 2), row), pl.BlockSpec(memory_space=pl.ANY)] \
        + in_specs[1:]
    grid_spec = pltpu.PrefetchScalarGridSpec(
        num_scalar_prefetch=1,
        grid=(B, 9),
        in_specs=in_specs,
        out_specs=out_specs + [pl.BlockSpec((TM, D), row)],
        scratch_shapes=_MOE_GATHER_SCRATCH,
    )
    return pl.pallas_call(
        functools.partial(_inproj_moe_kernel, n_steps=B * 9),
        grid_spec=grid_spec,
        out_shape=out_shape + [jax.ShapeDtypeStruct((T, D), F32)],
        compiler_params=_cparams(("arbitrary", "arbitrary")),
        name="in_projection_moe",
    )(pos, xs, mods_prev, wts, y, *args[1:])


def _rope(r, cos_t, sin_t):
    return r * cos_t + pltpu.roll(r, 32, 1) * sin_t


def _mlaproj_kernel(zq_ref, zkv_ref, gq_ref, gkv_ref, wq_ref, wkv_ref, cos_ref, sin_ref,
                    q_ref, k_ref, vt_ref):
    cos_t = cos_ref[...]
    sin_t = sin_ref[...]
    q = _dot(_rms(zq_ref[...], gq_ref[...]).astype(BF16), wq_ref[...]) * MLA_QSCALE
    kv = _dot(_rms(zkv_ref[:, 0:KV_LORA], gkv_ref[...]).astype(BF16), wkv_ref[...])
    k_rope = _rope(zkv_ref[:, KV_LORA:KV_LORA + 128], cos_t, sin_t).astype(BF16)
    for h in range(MLA_HEADS):
        c = h * 256
        q_ref[:, c:c + 128] = q[:, c:c + 128].astype(BF16)
        q_ref[:, c + 128:c + 256] = _rope(q[:, c + 128:c + 256], cos_t, sin_t).astype(BF16)
        k_ref[:, c:c + 128] = kv[:, c:c + 128].astype(BF16)
        k_ref[:, c + 128:c + 256] = k_rope
        vt_ref[h * 128:(h + 1) * 128, :] = kv[:, c + 128:c + 256].T.astype(BF16)


def _mla_projection(zq, zkv, g_q, g_kv, w_uq_ext, w_ukv, cos_t, sin_t, B):
    T = zq.shape[0]
    row = lambda b, j: (_tile_of(b, j, 0), 0)
    const = lambda b, j: (0, 0)
    return pl.pallas_call(
        _mlaproj_kernel,
        grid=(B, 9),
        in_specs=[
            pl.BlockSpec((TM, 512), row),
            pl.BlockSpec((TM, 384), row),
            pl.BlockSpec((1, Q_LORA), const),
            pl.BlockSpec((1, KV_LORA), const),
            pl.BlockSpec((Q_LORA, 2048), const),
            pl.BlockSpec((KV_LORA, 2048), const),
            pl.BlockSpec((TM, 128), lambda b, j: (j, 0)),
            pl.BlockSpec((TM, 128), lambda b, j: (j, 0)),
        ],
        out_specs=[
            pl.BlockSpec((TM, 2048), row),
            pl.BlockSpec((TM, 2048), row),
            pl.BlockSpec((MLA_WIDTH, TM), lambda b, j: (0, _tile_of(b, j, 0))),
        ],
        out_shape=[
            jax.ShapeDtypeStruct((T, 2048), BF16),
            jax.ShapeDtypeStruct((T, 2048), BF16),
            jax.ShapeDtypeStruct((MLA_WIDTH, T), BF16),
        ],
        compiler_params=_cparams(("arbitrary", "arbitrary")),
        name="mla_projection",
    )(zq, zkv, g_q.reshape(1, -1), g_kv.reshape(1, -1), w_uq_ext, w_ukv, cos_t, sin_t)


def _mla_attn_kernel(q_ref, k_ref, vt_ref, o_ref, *, j0):
    def attend(q0, nq, nk, o0):
        st = _dot_nt(k_ref[0:nk, :], q_ref[q0:q0 + nq, :])
        m = jnp.max(st, axis=0, keepdims=True)
        p = jnp.exp2(st - m)
        l = jnp.sum(p, axis=0, keepdims=True)
        ot = _dot(vt_ref[:, 0:nk], p.astype(BF16)) / l
        o_ref[o0:o0 + nq, :] = ot.T

    if j0 == 0:
        attend(0, CTX_LEN, CTX_LEN, 0)
    for c in range(8 // MLA_CHUNK_TILES):
        attend(CTX_LEN + MLA_CHUNK_TILES * c * TM, MLA_CHUNK_TILES * TM, 9 * TM,
               (1 - j0 + MLA_CHUNK_TILES * c) * TM)


def _mla_attention(q, k, vt, B, j0):
    rows = (9 - j0) * TM
    return pl.pallas_call(
        functools.partial(_mla_attn_kernel, j0=j0),
        grid=(B, MLA_HEADS),
        in_specs=[
            pl.BlockSpec((9 * TM, 256), lambda b, h: (b, h)),
            pl.BlockSpec((9 * TM, 256), lambda b, h: (b, h)),
            pl.BlockSpec((V_DIM, 9 * TM), lambda b, h: (h, b)),
        ],
        out_specs=pl.BlockSpec((rows, V_DIM), lambda b, h: (b, h)),
        out_shape=jax.ShapeDtypeStruct((B * rows, MLA_WIDTH), F32),
        compiler_params=_cparams(("arbitrary", "arbitrary")),
        name="mla_attention",
    )(q, k, vt)


def _na_chunk(g):
    start_row = min(max(4 * g - 4, 0), 8 * TM // GRID_W - NA_WIN_ROWS)
    pattern = 0 if g == 0 else (2 if g == 7 else 1)
    return start_row, pattern


def _na_kernel(q_ref, k_ref, vt_ref, bias_ref, o_ref, *, j0):
    def finish(parts, o0):
        m = None
        for st, _ in parts:
            pm = jnp.max(st, axis=0, keepdims=True)
            m = pm if m is None else jnp.maximum(m, pm)
        l = None
        ot = None
        for st, vt in parts:
            p = jnp.exp2(st - m)
            pl_sum = jnp.sum(p, axis=0, keepdims=True)
            po = _dot(vt, p.astype(BF16))
            l = pl_sum if l is None else l + pl_sum
            ot = po if ot is None else ot + po
        o_ref[o0:o0 + TM, :] = (ot / l).T

    if j0 == 0:
        st = _dot_nt(k_ref[0:CTX_LEN, :], q_ref[0:CTX_LEN, :])
        finish([(st, vt_ref[:, 0:CTX_LEN])], 0)
    for g in range(8):
        start_row, pattern = _na_chunk(g)
        k0 = CTX_LEN + start_row * GRID_W
        nk = NA_WIN_ROWS * GRID_W
        q = q_ref[CTX_LEN + g * TM:CTX_LEN + (g + 1) * TM, :]
        st_loc = _dot_nt(k_ref[k0:k0 + nk, :], q) + bias_ref[0, pattern]
        st_ctx = _dot_nt(k_ref[0:CTX_LEN, :], q)
        finish([(st_loc, vt_ref[:, k0:k0 + nk]), (st_ctx, vt_ref[:, 0:CTX_LEN])], (1 - j0 + g) * TM)


def _na_attention(naq, nak, navt, bias, B, j0):
    rows = (9 - j0) * TM
    return pl.pallas_call(
        functools.partial(_na_kernel, j0=j0),
        grid=(NA_HEADS, B),
        in_specs=[
            pl.BlockSpec((9 * TM, NA_HEAD_DIM), lambda h, b: (b, h)),
            pl.BlockSpec((9 * TM, NA_HEAD_DIM), lambda h, b: (b, h)),
            pl.BlockSpec((NA_HEAD_DIM, 9 * TM), lambda h, b: (h, b)),
            pl.BlockSpec((1, 3, NA_WIN_ROWS * GRID_W, TM), lambda h, b: (h, 0, 0, 0)),
        ],
        out_specs=pl.BlockSpec((rows, NA_HEAD_DIM), lambda h, b: (b, h)),
        out_shape=jax.ShapeDtypeStruct((B * rows, NA_WIDTH), F32),
        compiler_params=_cparams(("arbitrary", "arbitrary")),
        name="na_attention",
    )(naq, nak, navt, bias)


def _na_bias(rpb):
    kh, rows, nq = NA_KH_MAX, 8 * TM // GRID_W, TM // GRID_W
    cq = np.arange(GRID_W)
    ck = np.arange(GRID_W)
    col_start = np.clip(cq - NA_KW // 2, 0, GRID_W - NA_KW)
    col_ok = (ck[:, None] >= col_start[None, :]) & (ck[:, None] < col_start[None, :] + NA_KW)
    dcol = np.clip(ck[:, None] - cq[None, :] + (NA_KW - 1), 0, 2 * NA_KW - 2)
    blocks = jnp.take(rpb.astype(F32) * float(np.log2(np.e)), jnp.asarray(dcol.reshape(-1)), axis=3)
    blocks = blocks.reshape(rpb.shape[:3] + (GRID_W, GRID_W))
    blocks = jnp.where(jnp.asarray(col_ok), blocks, NEG_INF)
    masked = jnp.full(rpb.shape[:2] + (GRID_W, GRID_W), NEG_INF, F32)
    patterns = []
    for g in (0, 1, 7):
        start_row, _ = _na_chunk(g)
        key_rows = []
        for kr in range(NA_WIN_ROWS):
            key_row = start_row + kr
            row = []
            for qr in range(nq):
                r = 4 * g + qr
                r_start = min(max(r - kh // 2, 0), rows - kh)
                in_rows = r_start <= key_row < r_start + kh
                row.append(blocks[:, :, key_row - r + (kh - 1)] if in_rows else masked)
            key_rows.append(jnp.concatenate(row, axis=-1))
        patterns.append(jnp.concatenate(key_rows, axis=-2))
    return jnp.stack(patterns, axis=2)


def _fnet_kernel(z_ref, csl_ref, csc_ref, cd_ref, w_ref, b_ref, o_ref, ab_ref, *, j0, seq):
    j = pl.program_id(1) + j0

    def small_side(row0, length):
        for g in range(FNET_GROUPS):
            c = g * FNET_GROUP_DIM
            ab = _dot(z_ref[row0:row0 + length, c:c + FNET_GROUP_DIM], cd_ref[...])
            ab_ref[0:length, c:c + FNET_GROUP_DIM] = ab[:, 0:FNET_GROUP_DIM].astype(BF16)
            ab_ref[length:2 * length, c:c + FNET_GROUP_DIM] = ab[:, FNET_GROUP_DIM:].astype(BF16)

    def long_side(cs, length):
        f = _dot(cs, ab_ref[0:2 * length, :]) * (length * FNET_GROUP_DIM) ** -0.5
        o_ref[...] = _dot(f.astype(BF16), w_ref[...]) + b_ref[...]

    if j0 == 0:
        @pl.when(j == 0)
        def _():
            small_side(0, CTX_LEN)
            long_side(csc_ref[...], CTX_LEN)

    @pl.when(j == 1)
    def _():
        small_side(CTX_LEN, seq)

    @pl.when(j >= 1)
    def _():
        long_side(csl_ref[...], seq)


def _fnet(zf, cs_lat, cs_ctx, cd, w_fnet, b_fnet, B, j0):
    seq = 8 * TM
    return pl.pallas_call(
        functools.partial(_fnet_kernel, j0=j0, seq=seq),
        grid=(B, 9 - j0),
        in_specs=[
            pl.BlockSpec((9 * TM, FNET_WIDTH), lambda b, j: (b, 0)),
            pl.BlockSpec((TM, 2 * seq), lambda b, j: (jnp.maximum(j0 + j - 1, 0), 0)),
            pl.BlockSpec((CTX_LEN, 2 * CTX_LEN), lambda b, j: (0, 0)),
            pl.BlockSpec((FNET_GROUP_DIM, 2 * FNET_GROUP_DIM), lambda b, j: (0, 0)),
            pl.BlockSpec((FNET_WIDTH, FNET_WIDTH), lambda b, j: (0, 0)),
            pl.BlockSpec((1, FNET_WIDTH), lambda b, j: (0, 0)),
        ],
        out_specs=pl.BlockSpec((TM, FNET_WIDTH), lambda b, j: (_otile(b, j, j0), 0)),
        out_shape=jax.ShapeDtypeStruct((B * (9 - j0) * TM, FNET_WIDTH), F32),
        scratch_shapes=[pltpu.VMEM((2 * seq, FNET_WIDTH), BF16)],
        compiler_params=_cparams(("arbitrary", "arbitrary")),
        name="fnet",
    )(zf, cs_lat, cs_ctx, cd, w_fnet, b_fnet.reshape(1, -1))


def _dft_cos_sin(n):
    j = jnp.arange(n, dtype=jnp.int32)[:, None]
    if n <= 64:
        ang = ((j * j.T) % n).astype(F32) * (2.0 * np.pi / n)
        return jnp.cos(ang), jnp.sin(ang)
    k1 = jnp.arange(n // 64, dtype=jnp.int32)[None, :]
    k0 = jnp.arange(64, dtype=jnp.int32)[None, :]
    a = ((j * k1 * 64) % n).astype(F32) * (2.0 * np.pi / n)
    b = ((j * k0) % n).astype(F32) * (2.0 * np.pi / n)
    ca, sa, cb, sb = jnp.cos(a), jnp.sin(a), jnp.cos(b), jnp.sin(b)
    c = ca[:, :, None] * cb[:, None, :] - sa[:, :, None] * sb[:, None, :]
    s = sa[:, :, None] * cb[:, None, :] + ca[:, :, None] * sb[:, None, :]
    return c.reshape(n, n), s.reshape(n, n)


def _dft_tables(n):
    c, s = _dft_cos_sin(n)
    return jnp.concatenate([c, s], axis=1).astype(BF16)


def _merge_kernel(*refs, n_stream, j0):
    x = _stream_tile(refs[:n_stream], j0)
    (om_ref, of_ref, on_ref, ga_ref, shf_ref, scf_ref, gout_ref, gffn_ref, wout_ref, wr_ref, br_ref,
     xn_ref, hf_ref, rw_ref, ri_ref, cnt_ref, run_ref) = refs[n_stream:]
    ym = _rms(om_ref[...], gout_ref[:, 0:MLA_WIDTH]).astype(BF16)
    yf = _rms(of_ref[...], gout_ref[:, MLA_WIDTH:MLA_WIDTH + FNET_WIDTH]).astype(BF16)
    yn = _rms(on_ref[...], gout_ref[:, MLA_WIDTH + FNET_WIDTH:]).astype(BF16)
    acc = _dot(ym, wout_ref[0:MLA_WIDTH, :])
    acc = acc + _dot(yf, wout_ref[MLA_WIDTH:MLA_WIDTH + FNET_WIDTH, :])
    acc = acc + _dot(yn, wout_ref[MLA_WIDTH + FNET_WIDTH:, :])
    xn = x + ga_ref[0] * acc
    xn_ref[...] = xn
    hf = _rms(xn, gffn_ref[...]) * (1.0 + scf_ref[0]) + shf_ref[0]
    hf_ref[...] = hf
    hi = hf.astype(BF16)
    lo = (hf - hi.astype(F32)).astype(BF16)
    a = _dot(hi, wr_ref[...])
    b = _dot(lo, wr_ref[...])
    small = a[:, ROUTER_COLS:] + (b[:, :ROUTER_COLS] + b[:, ROUTER_COLS:])
    _route_tile(a[:, :ROUTER_COLS] + small + br_ref[...], rw_ref, ri_ref, cnt_ref, run_ref)


def _route_tile(lg, rw_ref, ri_ref, cnt_ref, run_ref):
    first = jnp.logical_and(pl.program_id(0) == 0, pl.program_id(1) == 0)

    @pl.when(first)
    def _():
        run_ref[...] = jnp.zeros_like(run_ref)

    lane = lax.broadcasted_iota(jnp.int32, lg.shape, 1)
    neg = jnp.float32(-jnp.inf)

    def top(v):
        vmax = jnp.max(v, axis=1, keepdims=True)
        idx = jnp.min(jnp.where(v == vmax, lane, ROUTER_COLS), axis=1, keepdims=True)
        return vmax, idx

    in_groups = lane < N_GROUPS
    gl = jnp.where(in_groups, lg, neg)
    g_max, g_sel = top(gl)
    g_w = 1.0 / jnp.sum(jnp.where(in_groups, jnp.exp(gl - g_max), 0.0), axis=1, keepdims=True)
    e_lo = N_GROUPS + g_sel * EXPERTS_PER_GROUP
    el = jnp.where(jnp.logical_and(lane >= e_lo, lane < e_lo + EXPERTS_PER_GROUP), lg, neg)
    e1_max, i1 = top(el)
    e2_max, i2 = top(jnp.where(lane == i1, neg, el))
    t = jnp.exp(e2_max - e1_max)
    w0 = g_w / (1.0 + t)
    w1 = w0 * t
    rw_ref[...] = jnp.where(lane == 0, w0, jnp.where(lane == 1, w1, 0.0))

    row = lax.broadcasted_iota(jnp.int32, (TM, TM), 0)
    col = lax.broadcasted_iota(jnp.int32, (TM, TM), 1)
    tri = jnp.where(row >= col, 1.0, 0.0).astype(BF16)
    hot0 = lane == i1
    hot1 = lane == i2
    c0 = _dot(tri, jnp.where(hot0, 1.0, 0.0).astype(BF16))
    c1 = _dot(tri, jnp.where(hot1, 1.0, 0.0).astype(BF16))
    run = run_ref[...]
    tot0 = c0[TM - 1:TM, :]
    rank0 = jnp.sum(jnp.where(hot0, run + c0 - 1.0, 0.0), axis=1, keepdims=True)
    rank1 = jnp.sum(jnp.where(hot1, run + tot0 + c1 - 1.0, 0.0), axis=1, keepdims=True)
    run = run + tot0 + c1[TM - 1:TM, :]
    run_ref[...] = run
    cnt_ref[...] = jnp.broadcast_to(run, cnt_ref.shape).astype(jnp.int32)
    ri_ref[...] = jnp.where(lane == 0, i1 - N_GROUPS, jnp.where(lane == 1, i2 - N_GROUPS, jnp.where(
        lane == 2, rank0.astype(jnp.int32), jnp.where(lane == 3, rank1.astype(jnp.int32), 0))))


def _merge(o_mla, o_f, o_na, xs, mods, g_out, g_ffn, w_out, w_router, b_router, B, j0):
    T = B * (9 - j0) * TM
    row = lambda b, j: (_otile(b, j, j0), 0)
    const = lambda b, j: (0, 0)
    stream_specs, stream_args = _stream_specs(xs, j0)
    return pl.pallas_call(
        functools.partial(_merge_kernel, n_stream=len(stream_args), j0=j0),
        grid=(B, 9 - j0),
        in_specs=stream_specs + [
            pl.BlockSpec((TM, MLA_WIDTH), row),
            pl.BlockSpec((TM, FNET_WIDTH), row),
            pl.BlockSpec((TM, NA_WIDTH), row),
            _mod_spec(2, j0),
            _mod_spec(3, j0),
            _mod_spec(4, j0),
            pl.BlockSpec((1, D), const),
            pl.BlockSpec((1, D), const),
            pl.BlockSpec((D, D), const),
            pl.BlockSpec((D, 2 * ROUTER_COLS), const),
            pl.BlockSpec((1, ROUTER_COLS), const),
        ],
        out_specs=[
            pl.BlockSpec((TM, D), row),
            pl.BlockSpec((TM, D), row),
            pl.BlockSpec((TM, ROUTER_COLS), row),
            pl.BlockSpec((TM, ROUTER_COLS), row),
            pl.BlockSpec((8, ROUTER_COLS), const),
        ],
        out_shape=[
            jax.ShapeDtypeStruct((T, D), F32),
            jax.ShapeDtypeStruct((T, D), F32),
            jax.ShapeDtypeStruct((T, ROUTER_COLS), F32),
            jax.ShapeDtypeStruct((T, ROUTER_COLS), jnp.int32),
            jax.ShapeDtypeStruct((8, ROUTER_COLS), jnp.int32),
        ],
        scratch_shapes=[pltpu.VMEM((1, ROUTER_COLS), F32)],
        compiler_params=_cparams(("arbitrary", "arbitrary")),
        name="merge",
    )(*stream_args, o_mla, o_f, o_na, mods, mods, mods, g_out.reshape(1, D), g_ffn.reshape(1, D),
      w_out, w_router, b_router)


def _plan_kernel(cnt_ref, slot_ref, src_ref, tile_ref, exp_ref, lo_ref, hi_ref, flag_ref, nxt_ref, ni_ref,
                 gs_ref, *, n_pairs, max_items):
    def starts(e, acc):
        gs_ref[e] = acc
        return acc + cnt_ref[e]

    lax.fori_loop(0, N_EXPERTS, starts, 0)

    def place(p, c):
        src_ref[slot_ref[p]] = lax.shift_right_logical(p, 1)
        return c

    lax.fori_loop(0, n_pairs, place, 0, unroll=16)

    last = N_EXPERTS - 1

    def group_end(e):
        return gs_ref[e] + cnt_ref[e]

    def next_nonempty(e):
        return lax.while_loop(lambda x: jnp.logical_and(x < last, cnt_ref[jnp.minimum(x, last)] == 0),
                              lambda x: x + 1, e)

    def tile_items(t, carry):
        i, e, prev = carry
        row0 = t * TM
        e = lax.while_loop(lambda x: group_end(x) <= row0, lambda x: x + 1, e)

        def emit(state):
            i, e, prev, first, _ = state
            tile_ref[i] = t
            exp_ref[i] = e
            lo_ref[i] = jnp.clip(gs_ref[e] - row0, 0, TM)
            hi_ref[i] = jnp.clip(group_end(e) - row0, 0, TM)
            flag_ref[i] = first + 2 * (e != prev).astype(jnp.int32)
            done = group_end(e) >= row0 + TM
            e_next = jnp.where(done, e, next_nonempty(e + 1))
            return i + 1, e_next, e, jnp.int32(0), done

        i, e, prev, _, _ = lax.while_loop(lambda s: jnp.logical_not(s[4]), emit,
                                          (i, e, prev, jnp.int32(1), jnp.bool_(False)))
        return i, e, prev

    n_items, _, _ = lax.fori_loop(0, n_pairs // TM, tile_items,
                                  (jnp.int32(0), jnp.int32(0), jnp.int32(-1)))
    ni_ref[0] = n_items

    def pad(i, c):
        tile_ref[i] = tile_ref[n_items - 1]
        exp_ref[i] = exp_ref[n_items - 1]
        lo_ref[i] = 0
        hi_ref[i] = 0
        flag_ref[i] = 0
        nxt_ref[i] = -1
        return c

    lax.fori_loop(n_items, max_items, pad, 0)

    def parity(i, par):
        par = jnp.where((flag_ref[i] & 2) != 0, 1 - par, par)
        flag_ref[i] = flag_ref[i] + 4 * par
        return par

    lax.fori_loop(0, n_items, parity, jnp.int32(1))

    def lookahead(k, following):
        i = n_items - 1 - k
        nxt_ref[i] = following
        return jnp.where((flag_ref[i] & 2) != 0, exp_ref[i], following)

    lax.fori_loop(0, n_items, lookahead, jnp.int32(-1))


def _plan(counts, eid, rank):
    n_pairs = eid.shape[0]
    max_items = n_pairs // TM + N_EXPERTS - 1
    g_start = jnp.cumsum(counts) - counts
    experts = jnp.arange(N_EXPERTS, dtype=jnp.int32)
    slot = jnp.sum(jnp.where(eid[:, None] == experts[None, :], g_start[None, :], 0), axis=1) + rank
    smem = pl.BlockSpec(memory_space=pltpu.SMEM)
    i32 = lambda n: jax.ShapeDtypeStruct((n,), jnp.int32)
    src, it_tile, it_exp, it_lo, it_hi, flags, nxt, n_items = pl.pallas_call(
        functools.partial(_plan_kernel, n_pairs=n_pairs, max_items=max_items),
        in_specs=[smem] * 2,
        out_specs=[smem] * 8,
        out_shape=[i32(n_pairs)] + [i32(max_items)] * 6 + [i32(1)],
        scratch_shapes=[pltpu.SMEM((N_EXPERTS,), jnp.int32)],
        name="moe_plan",
    )(counts, slot)
    return (it_tile, it_exp, it_lo, it_hi, flags, nxt, n_items, src), slot


def _moe_kernel(tile_ref, exp_ref, lo_ref, hi_ref, flag_ref, nxt_ref, ni_ref, src_ref,
                hf_hbm, wg_hbm, wu_hbm, wd_hbm, y_ref, xbuf, wg_buf, wu_buf, wd_buf, wgb, wub, wdb,
                sem, wsem, *, n_tiles, e0):
    i = pl.program_id(0)
    t = tile_ref[i]
    slot = lax.rem(t, 2)

    def gather_start(tile, buf, unrolled):
        base = tile * TM

        def issue(r):
            tok = src_ref[base + r]
            pltpu.make_async_copy(hf_hbm.at[pl.ds(tok, 1)], xbuf.at[buf, pl.ds(r, 1)], sem.at[buf]).start()

        if unrolled:
            for r in range(TM):
                issue(r)
        else:
            lax.fori_loop(0, TM, lambda r, c: (issue(r), c)[1], 0)

    def gather_wait(buf):
        pltpu.make_async_copy(hf_hbm.at[pl.ds(0, TM)], xbuf.at[buf], sem.at[buf]).wait()

    def weight_copies(expert, b):
        e = e0 + expert
        return (pltpu.make_async_copy(wg_hbm.at[e], wg_buf.at[b], wsem.at[b]),
                pltpu.make_async_copy(wu_hbm.at[e], wu_buf.at[b], wsem.at[b]),
                pltpu.make_async_copy(wd_hbm.at[e], wd_buf.at[b], wsem.at[b]))

    def weights_start(expert, b):
        for cp in weight_copies(expert, b):
            cp.start()

    def weights_wait(b):
        for cp in weight_copies(0, b):
            cp.wait()

    @pl.when(i < ni_ref[0])
    def _():
        first_visit = (flag_ref[i] & 1) != 0
        new_expert = (flag_ref[i] & 2) != 0
        wslot = lax.shift_right_logical(flag_ref[i], 2) & 1

        @pl.when(i == 0)
        def _():
            gather_start(0, 0, False)

        @pl.when(first_visit)
        def _():
            gather_wait(slot)

        @pl.when(jnp.logical_and(first_visit, t + 1 < n_tiles))
        def _():
            gather_start(t + 1, 1 - slot, True)

        @pl.when(i == 0)
        def _():
            weights_start(exp_ref[0], 0)

        @pl.when(new_expert)
        def _():
            @pl.when(nxt_ref[i] >= 0)
            def _():
                weights_start(nxt_ref[i], 1 - wslot)

            weights_wait(wslot)
            wgb[...] = wg_buf[wslot].astype(BF16)
            wub[...] = wu_buf[wslot].astype(BF16)
            wdb[...] = wd_buf[wslot].astype(BF16)

        x = xbuf[slot].astype(BF16)
        a = _dot(x, wgb[...])
        u = _dot(x, wub[...])
        row = lax.broadcasted_iota(jnp.int32, (TM, 1), 0)
        mine = jnp.logical_and(row >= lo_ref[i], row < hi_ref[i])
        h = jnp.where(mine, (a * jax.nn.sigmoid(a)) * u, 0.0)
        yv = _dot(h.astype(BF16), wdb[...])

        @pl.when(first_visit)
        def _():
            y_ref[...] = yv

        @pl.when(jnp.logical_not(first_visit))
        def _():
            y_ref[...] += yv


def _moe(hf, meta, w_gate, w_up, w_down, layer):
    n_rows = meta[-1].shape[0]
    n_tiles = n_rows // TM
    max_items = meta[0].shape[0]
    grid_spec = pltpu.PrefetchScalarGridSpec(
        num_scalar_prefetch=len(meta),
        grid=(max_items,),
        in_specs=[pl.BlockSpec(memory_space=pl.ANY)] * 4,
        out_specs=pl.BlockSpec((TM, D), lambda i, tile, *_: (tile[i], 0)),
        scratch_shapes=[
            pltpu.VMEM((2, TM, D), F32),
            pltpu.VMEM((2, D, D_EXPERT), F32),
            pltpu.VMEM((2, D, D_EXPERT), F32),
            pltpu.VMEM((2, D_EXPERT, D), F32),
            pltpu.VMEM((D, D_EXPERT), BF16),
            pltpu.VMEM((D, D_EXPERT), BF16),
            pltpu.VMEM((D_EXPERT, D), BF16),
            pltpu.SemaphoreType.DMA((2,)),
            pltpu.SemaphoreType.DMA((2,)),
        ],
    )
    return pl.pallas_call(
        functools.partial(_moe_kernel, n_tiles=n_tiles, e0=layer * N_EXPERTS),
        grid_spec=grid_spec,
        out_shape=jax.ShapeDtypeStruct((n_rows, D), F32),
        compiler_params=_cparams(("arbitrary",)),
        name="moe_experts",
    )(*meta, hf, w_gate.reshape(-1, D, D_EXPERT), w_up.reshape(-1, D, D_EXPERT),
      w_down.reshape(-1, D_EXPERT, D))


def _final_kernel(pos_ref, x_ref, gf_ref, w_ref, g_ref, y_hbm, o_ref, ybuf, sem, *, n_steps):
    step = pl.program_id(0) * 8 + pl.program_id(1)
    out = _moe_residual(pos_ref, y_hbm, ybuf, sem, x_ref[...], gf_ref[0], w_ref[...], step, n_steps)
    o_ref[0] = _rms(out, g_ref[...])


def _final(pos, wts, xn, mods, y, g_final, B):
    tile = lambda b, j, p: (b * 8 + j, 0)
    grid_spec = pltpu.PrefetchScalarGridSpec(
        num_scalar_prefetch=1,
        grid=(B, 8),
        in_specs=[
            pl.BlockSpec((TM, D), tile),
            pl.BlockSpec((1, 1, D), lambda b, j, p: (b * 6 + 5, 0, 0)),
            pl.BlockSpec((TM, 2), tile),
            pl.BlockSpec((1, D), lambda b, j, p: (0, 0)),
            pl.BlockSpec(memory_space=pl.ANY),
        ],
        out_specs=pl.BlockSpec((1, TM, D), lambda b, j, p: (b, j, 0)),
        scratch_shapes=_MOE_GATHER_SCRATCH,
    )
    return pl.pallas_call(
        functools.partial(_final_kernel, n_steps=B * 8),
        grid_spec=grid_spec,
        out_shape=jax.ShapeDtypeStruct((B, 8 * TM, D), F32),
        compiler_params=_cparams(("arbitrary", "arbitrary")),
        name="final",
    )(pos, xn, mods, wts, g_final.reshape(1, D), y)


def _deinterleave(w):
    pairs = w.reshape(w.shape[:-1] + (w.shape[-1] // 2, 2))
    even, odd = pairs[..., 0], pairs[..., 1]
    return jnp.concatenate([even, odd, even, odd], axis=-1)


def _prep_w_in(w_in):
    sizes = (Q_LORA, KV_LORA, QK_ROPE, FNET_WIDTH, NA_WIDTH, NA_WIDTH, NA_WIDTH)
    starts = np.concatenate([[0], np.cumsum(sizes)])
    seg = lambda i: w_in[:, starts[i]:starts[i + 1]]
    return jnp.concatenate([seg(0), seg(3), seg(4), seg(5), seg(6), seg(1), _deinterleave(seg(2))],
                           axis=1).astype(BF16)


def _prep_w_uq(w_uq):
    w = w_uq.reshape(Q_LORA, MLA_HEADS, QK_NOPE + QK_ROPE)
    w = jnp.concatenate([w[..., :QK_NOPE], _deinterleave(w[..., QK_NOPE:])], axis=-1)
    return w.reshape(Q_LORA, MLA_HEADS * 256).astype(BF16)


def _rope_tables(seq):
    half = QK_ROPE // 2
    inv_freq = ROPE_THETA ** (-jnp.arange(0, half, 2, dtype=F32) / half)
    t = jnp.arange(seq, dtype=jnp.int32)
    row = (t // GRID_W).astype(F32)
    col = (t % GRID_W).astype(F32)
    ang = jnp.concatenate([row[:, None] * inv_freq, col[:, None] * inv_freq], axis=-1)
    cos, sin = jnp.cos(ang), jnp.sin(ang)
    zeros = jnp.zeros((seq, 64), F32)
    cos_l = jnp.concatenate([cos, cos, zeros], axis=1)
    sin_l = jnp.concatenate([-sin, sin, zeros], axis=1)
    cos_c = jnp.concatenate([jnp.ones((CTX_LEN, 64), F32), jnp.zeros((CTX_LEN, 64), F32)], axis=1)
    sin_c = jnp.zeros((CTX_LEN, 128), F32)
    return jnp.concatenate([cos_c, cos_l], axis=0), jnp.concatenate([sin_c, sin_l], axis=0)


def kernel(x, c, ctx, c_ctx, w_ada, b_ada, g_attn, g_ffn, w_in, g_q, w_uq, g_kv, w_ukv, w_fnet, b_fnet,
           na_rpb, g_out, w_out, w_rg, b_rg, w_re, b_re, w_gate, w_up, w_down, g_final):
    B, S, _ = x.shape
    L = w_ada.shape[0]
    assert ctx.shape[1] == CTX_LEN == TM and S == 8 * TM and B <= 4
    T = B * 9 * TM

    cond8 = jnp.zeros((8, D), F32).at[:B].set(c).at[4].set(c_ctx)
    mods_all = _modulation(cond8, w_ada, b_ada)
    na_bias = _na_bias(na_rpb)
    cos_t, sin_t = _rope_tables(S)
    cs_lat = _dft_tables(S)
    cs_ctx = _dft_tables(CTX_LEN)
    cd_c, cd_s = _dft_cos_sin(FNET_GROUP_DIM)
    cd = jnp.concatenate([cd_c, -cd_s], axis=1).astype(BF16)

    xs = (ctx.reshape(B * CTX_LEN, D), x.reshape(B * S, D))
    pending = None
    for l in range(L):
        last = l == L - 1
        j0 = 1 if last else 0
        mods = mods_all[l].reshape(48, 1, D)
        w_in_ext = _prep_w_in(w_in[l])
        w_uq_ext = _prep_w_uq(w_uq[l])
        w_router = jnp.zeros((D, ROUTER_COLS), F32).at[:, :N_GROUPS].set(w_rg[l]) \
            .at[:, N_GROUPS:N_GROUPS + N_EXPERTS].set(w_re[l])
        w_scaled = w_router * 65537.0
        w_router_hi = w_scaled - (w_scaled - w_router)
        w_router = jnp.concatenate([w_router_hi, w_router - w_router_hi], axis=1).astype(BF16)
        b_router = jnp.zeros((1, ROUTER_COLS), F32).at[0, :N_GROUPS].set(b_rg[l]) \
            .at[0, N_GROUPS:N_GROUPS + N_EXPERTS].set(b_re[l])

        if pending is None:
            zq, zkv, zf, naq, nak, navt = _in_projection(xs, mods, g_attn[l], w_in_ext, B)
        else:
            zq, zkv, zf, naq, nak, navt, xs = _in_projection(xs, mods, g_attn[l], w_in_ext, B, moe=pending)
        q, k, vt = _mla_projection(zq, zkv, g_q[l], g_kv[l], w_uq_ext, w_ukv[l].astype(BF16),
                                   cos_t, sin_t, B)
        o_mla = _mla_attention(q, k, vt, B, j0)
        o_na = _na_attention(naq, nak, navt, na_bias[l], B, j0)
        w_f = w_fnet[l].astype(BF16)
        o_f = _fnet(zf, cs_lat, cs_ctx, cd, w_f, b_fnet[l], B, j0)
        xn, hf, route_w, route_i, counts = _merge(o_mla, o_f, o_na, xs, mods, g_out[l], g_ffn[l],
                                                  w_out[l].astype(BF16), w_router, b_router, B, j0)
        wts = route_w[:, 0:2]
        meta, slot = _plan(counts[0, N_GROUPS:N_GROUPS + N_EXPERTS], route_i[:, 0:2].reshape(-1),
                           route_i[:, 2:4].reshape(-1))
        y = _moe(hf, meta, w_gate, w_up, w_down, l)
        if last:
            return _final(slot, wts, xn, mods, y, g_final, B)
        xs, pending = xn, (slot, wts, y, mods)
```

```python
import functools

import numpy as np
import jax
import jax.numpy as jnp
from jax import lax
from jax.experimental import pallas as pl
from jax.experimental.pallas import tpu as pltpu

F32 = jnp.float32
BF16 = jnp.bfloat16

D = 2048
GRID_W = 64
CTX_LEN = 256
EPS = 1e-6
NEG_INF = -1e30
ROPE_THETA = 10000.0

V_DIM = 128
MLA_WIDTH = D // 2
MLA_HEADS = MLA_WIDTH // V_DIM
QK_NOPE = 128
QK_ROPE = 64
Q_LORA = D // 4
KV_LORA = D // 8
FNET_WIDTH = D // 4
FNET_GROUP_DIM = 128
FNET_GROUPS = FNET_WIDTH // FNET_GROUP_DIM
NA_WIDTH = D // 4
NA_HEAD_DIM = 128
NA_HEADS = NA_WIDTH // NA_HEAD_DIM
NA_KH_MAX = 8
NA_KW = 16
N_GROUPS = 4
EXPERTS_PER_GROUP = 8
N_EXPERTS = N_GROUPS * EXPERTS_PER_GROUP
D_EXPERT = D // 4

MLA_QSCALE = (QK_NOPE + QK_ROPE) ** -0.5 * float(np.log2(np.e))
MLA_CHUNK_TILES = 8
NA_QSCALE = NA_HEAD_DIM ** -0.5 * float(np.log2(np.e))
NA_WIN_ROWS = 12
TM = 256
IN_EXT = 5 * 512 + 256 + 128
ROUTER_COLS = 128
VMEM_LIMIT = 56 * 1024 * 1024


def _cparams(sem):
    return pltpu.CompilerParams(dimension_semantics=sem, vmem_limit_bytes=VMEM_LIMIT)


def _rms(v, g):
    return v * lax.rsqrt(jnp.mean(v * v, axis=-1, keepdims=True) + EPS) * g


def _dot(a, b):
    return jnp.dot(a, b, preferred_element_type=F32)


def _dot_nt(a, b):
    return lax.dot_general(a, b, (((1,), (1,)), ((), ())), preferred_element_type=F32)


def _mod_kernel(c_ref, w_ref, b_ref, o_ref):
    c = c_ref[...]
    s = c * jax.nn.sigmoid(c)
    o_ref[0] = _dot(s.astype(BF16), w_ref[0].astype(BF16)) + b_ref[0]


def _modulation(cond8, w_ada, b_ada):
    L = w_ada.shape[0]
    tn = 1024
    return pl.pallas_call(
        _mod_kernel,
        grid=(L, 6 * D // tn),
        in_specs=[
            pl.BlockSpec((8, D), lambda l, n: (0, 0)),
            pl.BlockSpec((1, D, tn), lambda l, n: (l, 0, n)),
            pl.BlockSpec((1, 1, tn), lambda l, n: (l, 0, n)),
        ],
        out_specs=pl.BlockSpec((1, 8, tn), lambda l, n: (l, 0, n)),
        out_shape=jax.ShapeDtypeStruct((L, 8, 6 * D), F32),
        compiler_params=_cparams(("arbitrary", "arbitrary")),
        name="modulation",
    )(cond8, w_ada, b_ada.reshape(L, 1, 6 * D))


def _tile_of(b, j, j0):
    return b * 9 + j0 + j


def _otile(b, j, j0):
    return b * (9 - j0) + j


def _mod_row(j, b, j0):
    return jnp.where(j0 + j == 0, 4, b)


def _mod_spec(k, j0):
    return pl.BlockSpec((1, 1, D), lambda b, j: (_mod_row(j, b, j0) * 6 + k, 0, 0))


def _stream_specs(xs, j0):
    if isinstance(xs, tuple):
        return [pl.BlockSpec((TM, D), lambda b, j, *_: (b, 0)),
                pl.BlockSpec((TM, D), lambda b, j, *_: (b * 8 + jnp.maximum(j0 + j - 1, 0), 0))], list(xs)
    return [pl.BlockSpec((TM, D), lambda b, j, *_: (_tile_of(b, j, j0), 0))], [xs]


def _stream_tile(refs, j0):
    if len(refs) == 1:
        return refs[0][...]
    return jnp.where(pl.program_id(1) + j0 == 0, refs[0][...], refs[1][...])


def _moe_residual(pos_ref, y_hbm, ybuf, sem, x, gate, w, step, n_steps):
    buf = lax.rem(step, 2)

    def start(tile, b, unrolled):
        base = tile * TM

        def issue(r):
            for k in range(2):
                pltpu.make_async_copy(y_hbm.at[pl.ds(pos_ref[2 * (base + r) + k], 1)],
                                      ybuf.at[b, k, pl.ds(r, 1)], sem.at[b]).start()

        if unrolled:
            for r in range(TM):
                issue(r)
        else:
            lax.fori_loop(0, TM, lambda r, c: (issue(r), c)[1], 0)

    def wait(b):
        for k in range(2):
            pltpu.make_async_copy(y_hbm.at[pl.ds(0, TM)], ybuf.at[b, k], sem.at[b]).wait()

    @pl.when(step == 0)
    def _():
        start(0, 0, False)

    for b in range(2):
        @pl.when(buf == 1 - b)
        def _():
            start(jnp.minimum(step + 1, n_steps - 1), b, True)

    wait(buf)
    out = x + gate * (w[:, 0:1] * ybuf[buf, 0] + w[:, 1:2] * ybuf[buf, 1])

    @pl.when(step == n_steps - 1)
    def _():
        wait(1 - buf)

    return out


_MOE_GATHER_SCRATCH = [pltpu.VMEM((2, 2, TM, D), F32), pltpu.SemaphoreType.DMA((2,))]


def _rope(r, cos_t, sin_t):
    return r * cos_t + pltpu.roll(r, 32, 1) * sin_t


def _inproj_body(x, sh_ref, sc_ref, g_ref, w_ref, gq_ref, gkv_ref, wq_ref, wkv_ref, cos_ref, sin_ref,
                 zf_ref, naq_ref, nak_ref, navt_ref, q_ref, k_ref, vt_ref):
    h = _rms(x, g_ref[...])
    h = h * (1.0 + sc_ref[0]) + sh_ref[0]
    z = _dot(h.astype(BF16), w_ref[...])
    zf_ref[...] = z[:, 512:1024].astype(BF16)
    naq_ref[...] = (z[:, 1024:1536] * NA_QSCALE).astype(BF16)
    nak_ref[...] = z[:, 1536:2048].astype(BF16)
    for hd in range(NA_HEADS):
        c = 2048 + hd * NA_HEAD_DIM
        navt_ref[hd * NA_HEAD_DIM:(hd + 1) * NA_HEAD_DIM, :] = z[:, c:c + NA_HEAD_DIM].T.astype(BF16)
    cos_t = cos_ref[...]
    sin_t = sin_ref[...]
    q = _dot(_rms(z[:, 0:Q_LORA], gq_ref[...]).astype(BF16), wq_ref[...]) * MLA_QSCALE
    kv = _dot(_rms(z[:, 2560:2560 + KV_LORA], gkv_ref[...]).astype(BF16), wkv_ref[...])
    k_rope = _rope(z[:, 2560 + KV_LORA:IN_EXT], cos_t, sin_t).astype(BF16)
    for hd in range(MLA_HEADS):
        c = hd * 256
        q_ref[:, c:c + 128] = q[:, c:c + 128].astype(BF16)
        q_ref[:, c + 128:c + 256] = _rope(q[:, c + 128:c + 256], cos_t, sin_t).astype(BF16)
        k_ref[:, c:c + 128] = kv[:, c:c + 128].astype(BF16)
        k_ref[:, c + 128:c + 256] = k_rope
        vt_ref[hd * 128:(hd + 1) * 128, :] = kv[:, c + 128:c + 256].T.astype(BF16)


def _inproj_kernel(*refs, n_stream):
    _inproj_body(_stream_tile(refs[:n_stream], 0), *refs[n_stream:])


def _inproj_moe_kernel(pos_ref, x_ref, gf_ref, wts_ref, y_hbm, *refs, n_steps):
    *refs, xs_ref, ybuf, sem = refs
    step = pl.program_id(0) * 9 + pl.program_id(1)
    x = _moe_residual(pos_ref, y_hbm, ybuf, sem, x_ref[...], gf_ref[0], wts_ref[...], step, n_steps)
    xs_ref[...] = x
    _inproj_body(x, *refs)


def _in_projection(xs, mods, g_attn, w_in_ext, mla, B, moe=None):
    T = B * 9 * TM
    row = lambda b, j, *_: (_tile_of(b, j, 0), 0)
    col = lambda b, j, *_: (0, _tile_of(b, j, 0))
    const = lambda b, j, *_: (0, 0)
    mod = lambda k: pl.BlockSpec((1, 1, D), lambda b, j, *_: (_mod_row(j, b, 0) * 6 + k, 0, 0))
    once = dict(pipeline_mode=pl.Buffered(1))
    g_q, g_kv, w_uq_ext, w_ukv, cos_t, sin_t = mla
    in_specs = [
        pl.BlockSpec((TM, D), row),
        mod(0),
        mod(1),
        pl.BlockSpec((1, D), const),
        pl.BlockSpec((D, IN_EXT), const, **once),
        pl.BlockSpec((1, Q_LORA), const),
        pl.BlockSpec((1, KV_LORA), const),
        pl.BlockSpec((Q_LORA, MLA_HEADS * 256), const, **once),
        pl.BlockSpec((KV_LORA, MLA_HEADS * 256), const, **once),
        pl.BlockSpec((TM, 128), lambda b, j, *_: (j, 0)),
        pl.BlockSpec((TM, 128), lambda b, j, *_: (j, 0)),
    ]
    out_specs = [
        pl.BlockSpec((TM, FNET_WIDTH), row),
        pl.BlockSpec((TM, NA_WIDTH), row),
        pl.BlockSpec((TM, NA_WIDTH), row),
        pl.BlockSpec((NA_WIDTH, TM), col),
        pl.BlockSpec((TM, MLA_HEADS * 256), row),
        pl.BlockSpec((TM, MLA_HEADS * 256), row),
        pl.BlockSpec((MLA_WIDTH, TM), col),
    ]
    out_shape = [
        jax.ShapeDtypeStruct((T, FNET_WIDTH), BF16),
        jax.ShapeDtypeStruct((T, NA_WIDTH), BF16),
        jax.ShapeDtypeStruct((T, NA_WIDTH), BF16),
        jax.ShapeDtypeStruct((NA_WIDTH, T), BF16),
        jax.ShapeDtypeStruct((T, MLA_HEADS * 256), BF16),
        jax.ShapeDtypeStruct((T, MLA_HEADS * 256), BF16),
        jax.ShapeDtypeStruct((MLA_WIDTH, T), BF16),
    ]
    args = [xs, mods, mods, g_attn.reshape(1, D), w_in_ext, g_q.reshape(1, -1), g_kv.reshape(1, -1),
            w_uq_ext, w_ukv, cos_t, sin_t]
    if moe is None:
        stream_specs, stream_args = _stream_specs(xs, 0)
        return pl.pallas_call(
            functools.partial(_inproj_kernel, n_stream=len(stream_args)),
            grid=(B, 9),
            in_specs=stream_specs + in_specs[1:],
            out_specs=out_specs,
            out_shape=out_shape,
            compiler_params=_cparams(("arbitrary", "arbitrary")),
            name="in_projection",
        )(*stream_args, *args[1:])
    pos, wts, y, mods_prev = moe
    in_specs = [in_specs[0], mod(5), pl.BlockSpec((TM, 2), row), pl.BlockSpec(memory_space=pl.ANY)] \
        + in_specs[1:]
    grid_spec = pltpu.PrefetchScalarGridSpec(
        num_scalar_prefetch=1,
        grid=(B, 9),
        in_specs=in_specs,
        out_specs=out_specs + [pl.BlockSpec((TM, D), row)],
        scratch_shapes=_MOE_GATHER_SCRATCH,
    )
    return pl.pallas_call(
        functools.partial(_inproj_moe_kernel, n_steps=B * 9),
        grid_spec=grid_spec,
        out_shape=out_shape + [jax.ShapeDtypeStruct((T, D), F32)],
        compiler_params=_cparams(("arbitrary", "arbitrary")),
        name="in_projection_moe",
    )(pos, xs, mods_prev, wts, y, *args[1:])


def _mla_attn_kernel(q_ref, k_ref, vt_ref, o_ref, *, j0):
    def attend(q0, nq, nk, o0):
        st = _dot_nt(k_ref[0:nk, :], q_ref[q0:q0 + nq, :])
        m = jnp.max(st, axis=0, keepdims=True)
        p = jnp.exp2(st - m)
        l = jnp.sum(p, axis=0, keepdims=True)
        ot = _dot(vt_ref[:, 0:nk], p.astype(BF16)) / l
        o_ref[o0:o0 + nq, :] = ot.T

    if j0 == 0:
        attend(0, CTX_LEN, CTX_LEN, 0)
    for c in range(8 // MLA_CHUNK_TILES):
        attend(CTX_LEN + MLA_CHUNK_TILES * c * TM, MLA_CHUNK_TILES * TM, 9 * TM,
               (1 - j0 + MLA_CHUNK_TILES * c) * TM)


def _mla_attention(q, k, vt, B, j0):
    rows = (9 - j0) * TM
    return pl.pallas_call(
        functools.partial(_mla_attn_kernel, j0=j0),
        grid=(B, MLA_HEADS),
        in_specs=[
            pl.BlockSpec((9 * TM, 256), lambda b, h: (b, h)),
            pl.BlockSpec((9 * TM, 256), lambda b, h: (b, h)),
            pl.BlockSpec((V_DIM, 9 * TM), lambda b, h: (h, b)),
        ],
        out_specs=pl.BlockSpec((rows, V_DIM), lambda b, h: (b, h)),
        out_shape=jax.ShapeDtypeStruct((B * rows, MLA_WIDTH), F32),
        compiler_params=_cparams(("arbitrary", "arbitrary")),
        name="mla_attention",
    )(q, k, vt)


def _na_chunk(g):
    start_row = min(max(4 * g - 4, 0), 8 * TM // GRID_W - NA_WIN_ROWS)
    pattern = 0 if g == 0 else (2 if g == 7 else 1)
    return start_row, pattern


def _na_kernel(q_ref, k_ref, vt_ref, bias_ref, o_ref, *, j0):
    def finish(parts, o0):
        m = None
        for st, _ in parts:
            pm = jnp.max(st, axis=0, keepdims=True)
            m = pm if m is None else jnp.maximum(m, pm)
        l = None
        ot = None
        for st, vt in parts:
            p = jnp.exp2(st - m)
            pl_sum = jnp.sum(p, axis=0, keepdims=True)
            po = _dot(vt, p.astype(BF16))
            l = pl_sum if l is None else l + pl_sum
            ot = po if ot is None else ot + po
        o_ref[o0:o0 + TM, :] = (ot / l).T

    if j0 == 0:
        st = _dot_nt(k_ref[0:CTX_LEN, :], q_ref[0:CTX_LEN, :])
        finish([(st, vt_ref[:, 0:CTX_LEN])], 0)
    for g in range(8):
        start_row, pattern = _na_chunk(g)
        k0 = CTX_LEN + start_row * GRID_W
        nk = NA_WIN_ROWS * GRID_W
        q = q_ref[CTX_LEN + g * TM:CTX_LEN + (g + 1) * TM, :]
        st_loc = _dot_nt(k_ref[k0:k0 + nk, :], q) + bias_ref[0, pattern]
        st_ctx = _dot_nt(k_ref[0:CTX_LEN, :], q)
        finish([(st_loc, vt_ref[:, k0:k0 + nk]), (st_ctx, vt_ref[:, 0:CTX_LEN])], (1 - j0 + g) * TM)


def _na_attention(naq, nak, navt, bias, B, j0):
    rows = (9 - j0) * TM
    return pl.pallas_call(
        functools.partial(_na_kernel, j0=j0),
        grid=(NA_HEADS, B),
        in_specs=[
            pl.BlockSpec((9 * TM, NA_HEAD_DIM), lambda h, b: (b, h)),
            pl.BlockSpec((9 * TM, NA_HEAD_DIM), lambda h, b: (b, h)),
            pl.BlockSpec((NA_HEAD_DIM, 9 * TM), lambda h, b: (h, b)),
            pl.BlockSpec((1, 3, NA_WIN_ROWS * GRID_W, TM), lambda h, b: (h, 0, 0, 0)),
        ],
        out_specs=pl.BlockSpec((rows, NA_HEAD_DIM), lambda h, b: (b, h)),
        out_shape=jax.ShapeDtypeStruct((B * rows, NA_WIDTH), F32),
        compiler_params=_cparams(("arbitrary", "arbitrary")),
        name="na_attention",
    )(naq, nak, navt, bias)


def _na_bias(rpb):
    kh, rows, nq = NA_KH_MAX, 8 * TM // GRID_W, TM // GRID_W
    cq = np.arange(GRID_W)
    ck = np.arange(GRID_W)
    col_start = np.clip(cq - NA_KW // 2, 0, GRID_W - NA_KW)
    col_ok = (ck[:, None] >= col_start[None, :]) & (ck[:, None] < col_start[None, :] + NA_KW)
    dcol = np.clip(ck[:, None] - cq[None, :] + (NA_KW - 1), 0, 2 * NA_KW - 2)
    blocks = jnp.take(rpb.astype(F32) * float(np.log2(np.e)), jnp.asarray(dcol.reshape(-1)), axis=3)
    blocks = blocks.reshape(rpb.shape[:3] + (GRID_W, GRID_W))
    blocks = jnp.where(jnp.asarray(col_ok), blocks, NEG_INF)
    masked = jnp.full(rpb.shape[:2] + (GRID_W, GRID_W), NEG_INF, F32)
    patterns = []
    for g in (0, 1, 7):
        start_row, _ = _na_chunk(g)
        key_rows = []
        for kr in range(NA_WIN_ROWS):
            key_row = start_row + kr
            row = []
            for qr in range(nq):
                r = 4 * g + qr
                r_start = min(max(r - kh // 2, 0), rows - kh)
                in_rows = r_start <= key_row < r_start + kh
                row.append(blocks[:, :, key_row - r + (kh - 1)] if in_rows else masked)
            key_rows.append(jnp.concatenate(row, axis=-1))
        patterns.append(jnp.concatenate(key_rows, axis=-2))
    return jnp.stack(patterns, axis=2)


def _fnet_kernel(z_ref, csl_ref, csc_ref, cd_ref, w_ref, b_ref, o_ref, ab_ref, *, j0, seq):
    j = pl.program_id(1) + j0

    def small_side(row0, length):
        for g in range(FNET_GROUPS):
            c = g * FNET_GROUP_DIM
            ab = _dot(z_ref[row0:row0 + length, c:c + FNET_GROUP_DIM], cd_ref[...])
            ab_ref[0:length, c:c + FNET_GROUP_DIM] = ab[:, 0:FNET_GROUP_DIM].astype(BF16)
            ab_ref[length:2 * length, c:c + FNET_GROUP_DIM] = ab[:, FNET_GROUP_DIM:].astype(BF16)

    def long_side(cs, length):
        f = _dot(cs, ab_ref[0:2 * length, :]) * (length * FNET_GROUP_DIM) ** -0.5
        o_ref[...] = _dot(f.astype(BF16), w_ref[...]) + b_ref[...]

    if j0 == 0:
        @pl.when(j == 0)
        def _():
            small_side(0, CTX_LEN)
            long_side(csc_ref[...], CTX_LEN)

    @pl.when(j == 1)
    def _():
        small_side(CTX_LEN, seq)

    @pl.when(j >= 1)
    def _():
        long_side(csl_ref[...], seq)


def _fnet(zf, cs_lat, cs_ctx, cd, w_fnet, b_fnet, B, j0):
    seq = 8 * TM
    return pl.pallas_call(
        functools.partial(_fnet_kernel, j0=j0, seq=seq),
        grid=(B, 9 - j0),
        in_specs=[
            pl.BlockSpec((9 * TM, FNET_WIDTH), lambda b, j: (b, 0)),
            pl.BlockSpec((TM, 2 * seq), lambda b, j: (jnp.maximum(j0 + j - 1, 0), 0)),
            pl.BlockSpec((CTX_LEN, 2 * CTX_LEN), lambda b, j: (0, 0)),
            pl.BlockSpec((FNET_GROUP_DIM, 2 * FNET_GROUP_DIM), lambda b, j: (0, 0)),
            pl.BlockSpec((FNET_WIDTH, FNET_WIDTH), lambda b, j: (0, 0)),
            pl.BlockSpec((1, FNET_WIDTH), lambda b, j: (0, 0)),
        ],
        out_specs=pl.BlockSpec((TM, FNET_WIDTH), lambda b, j: (_otile(b, j, j0), 0)),
        out_shape=jax.ShapeDtypeStruct((B * (9 - j0) * TM, FNET_WIDTH), F32),
        scratch_shapes=[pltpu.VMEM((2 * seq, FNET_WIDTH), BF16)],
        compiler_params=_cparams(("arbitrary", "arbitrary")),
        name="fnet",
    )(zf, cs_lat, cs_ctx, cd, w_fnet, b_fnet.reshape(1, -1))


def _dft_cos_sin(n):
    j = jnp.arange(n, dtype=jnp.int32)[:, None]
    if n <= 64:
        ang = ((j * j.T) % n).astype(F32) * (2.0 * np.pi / n)
        return jnp.cos(ang), jnp.sin(ang)
    k1 = jnp.arange(n // 64, dtype=jnp.int32)[None, :]
    k0 = jnp.arange(64, dtype=jnp.int32)[None, :]
    a = ((j * k1 * 64) % n).astype(F32) * (2.0 * np.pi / n)
    b = ((j * k0) % n).astype(F32) * (2.0 * np.pi / n)
    ca, sa, cb, sb = jnp.cos(a), jnp.sin(a), jnp.cos(b), jnp.sin(b)
    c = ca[:, :, None] * cb[:, None, :] - sa[:, :, None] * sb[:, None, :]
    s = sa[:, :, None] * cb[:, None, :] + ca[:, :, None] * sb[:, None, :]
    return c.reshape(n, n), s.reshape(n, n)


def _dft_tables(n):
    c, s = _dft_cos_sin(n)
    return jnp.concatenate([c, s], axis=1).astype(BF16)


def _merge_kernel(*refs, n_stream, j0):
    x = _stream_tile(refs[:n_stream], j0)
    (om_ref, of_ref, on_ref, ga_ref, shf_ref, scf_ref, gout_ref, gffn_ref, wout_ref, wr_ref, br_ref,
     xn_ref, hf_ref, rw_ref, ri_ref, cnt_ref, run_ref) = refs[n_stream:]
    ym = _rms(om_ref[...], gout_ref[:, 0:MLA_WIDTH]).astype(BF16)
    yf = _rms(of_ref[...], gout_ref[:, MLA_WIDTH:MLA_WIDTH + FNET_WIDTH]).astype(BF16)
    yn = _rms(on_ref[...], gout_ref[:, MLA_WIDTH + FNET_WIDTH:]).astype(BF16)
    acc = _dot(ym, wout_ref[0:MLA_WIDTH, :])
    acc = acc + _dot(yf, wout_ref[MLA_WIDTH:MLA_WIDTH + FNET_WIDTH, :])
    acc = acc + _dot(yn, wout_ref[MLA_WIDTH + FNET_WIDTH:, :])
    xn = x + ga_ref[0] * acc
    xn_ref[...] = xn
    hf = _rms(xn, gffn_ref[...]) * (1.0 + scf_ref[0]) + shf_ref[0]
    hf_ref[...] = hf
    hi = hf.astype(BF16)
    lo = (hf - hi.astype(F32)).astype(BF16)
    a = _dot(hi, wr_ref[...])
    b = _dot(lo, wr_ref[...])
    small = a[:, ROUTER_COLS:] + (b[:, :ROUTER_COLS] + b[:, ROUTER_COLS:])
    _route_tile(a[:, :ROUTER_COLS] + small + br_ref[...], rw_ref, ri_ref, cnt_ref, run_ref)


def _route_tile(lg, rw_ref, ri_ref, cnt_ref, run_ref):
    first = jnp.logical_and(pl.program_id(0) == 0, pl.program_id(1) == 0)

    @pl.when(first)
    def _():
        run_ref[...] = jnp.zeros_like(run_ref)

    lane = lax.broadcasted_iota(jnp.int32, lg.shape, 1)
    neg = jnp.float32(-jnp.inf)

    def top(v):
        vmax = jnp.max(v, axis=1, keepdims=True)
        idx = jnp.min(jnp.where(v == vmax, lane, ROUTER_COLS), axis=1, keepdims=True)
        return vmax, idx

    in_groups = lane < N_GROUPS
    gl = jnp.where(in_groups, lg, neg)
    g_max, g_sel = top(gl)
    g_w = 1.0 / jnp.sum(jnp.where(in_groups, jnp.exp(gl - g_max), 0.0), axis=1, keepdims=True)
    e_lo = N_GROUPS + g_sel * EXPERTS_PER_GROUP
    el = jnp.where(jnp.logical_and(lane >= e_lo, lane < e_lo + EXPERTS_PER_GROUP), lg, neg)
    e1_max, i1 = top(el)
    e2_max, i2 = top(jnp.where(lane == i1, neg, el))
    t = jnp.exp(e2_max - e1_max)
    w0 = g_w / (1.0 + t)
    w1 = w0 * t
    rw_ref[...] = jnp.where(lane == 0, w0, jnp.where(lane == 1, w1, 0.0))

    row = lax.broadcasted_iota(jnp.int32, (TM, TM), 0)
    col = lax.broadcasted_iota(jnp.int32, (TM, TM), 1)
    tri = jnp.where(row >= col, 1.0, 0.0).astype(BF16)
    hot0 = lane == i1
    hot1 = lane == i2
    c0 = _dot(tri, jnp.where(hot0, 1.0, 0.0).astype(BF16))
    c1 = _dot(tri, jnp.where(hot1, 1.0, 0.0).astype(BF16))
    run = run_ref[...]
    tot0 = c0[TM - 1:TM, :]
    rank0 = jnp.sum(jnp.where(hot0, run + c0 - 1.0, 0.0), axis=1, keepdims=True)
    rank1 = jnp.sum(jnp.where(hot1, run + tot0 + c1 - 1.0, 0.0), axis=1, keepdims=True)
    run = run + tot0 + c1[TM - 1:TM, :]
    run_ref[...] = run
    cnt_ref[...] = jnp.broadcast_to(run, cnt_ref.shape).astype(jnp.int32)
    ri_ref[...] = jnp.where(lane == 0, i1 - N_GROUPS, jnp.where(lane == 1, i2 - N_GROUPS, jnp.where(
        lane == 2, rank0.astype(jnp.int32), jnp.where(lane == 3, rank1.astype(jnp.int32), 0))))


def _merge(o_mla, o_f, o_na, xs, mods, g_out, g_ffn, w_out, w_router, b_router, B, j0):
    T = B * (9 - j0) * TM
    row = lambda b, j: (_otile(b, j, j0), 0)
    const = lambda b, j: (0, 0)
    stream_specs, stream_args = _stream_specs(xs, j0)
    return pl.pallas_call(
        functools.partial(_merge_kernel, n_stream=len(stream_args), j0=j0),
        grid=(B, 9 - j0),
        in_specs=stream_specs + [
            pl.BlockSpec((TM, MLA_WIDTH), row),
            pl.BlockSpec((TM, FNET_WIDTH), row),
            pl.BlockSpec((TM, NA_WIDTH), row),
            _mod_spec(2, j0),
            _mod_spec(3, j0),
            _mod_spec(4, j0),
            pl.BlockSpec((1, D), const),
            pl.BlockSpec((1, D), const),
            pl.BlockSpec((D, D), const),
            pl.BlockSpec((D, 2 * ROUTER_COLS), const),
            pl.BlockSpec((1, ROUTER_COLS), const),
        ],
        out_specs=[
            pl.BlockSpec((TM, D), row),
            pl.BlockSpec((TM, D), row),
            pl.BlockSpec((TM, ROUTER_COLS), row),
            pl.BlockSpec((TM, ROUTER_COLS), row),
            pl.BlockSpec((8, ROUTER_COLS), const),
        ],
        out_shape=[
            jax.ShapeDtypeStruct((T, D), F32),
            jax.ShapeDtypeStruct((T, D), F32),
            jax.ShapeDtypeStruct((T, ROUTER_COLS), F32),
            jax.ShapeDtypeStruct((T, ROUTER_COLS), jnp.int32),
            jax.ShapeDtypeStruct((8, ROUTER_COLS), jnp.int32),
        ],
        scratch_shapes=[pltpu.VMEM((1, ROUTER_COLS), F32)],
        compiler_params=_cparams(("arbitrary", "arbitrary")),
        name="merge",
    )(*stream_args, o_mla, o_f, o_na, mods, mods, mods, g_out.reshape(1, D), g_ffn.reshape(1, D),
      w_out, w_router, b_router)


def _plan_kernel(cnt_ref, slot_ref, src_ref, tile_ref, exp_ref, lo_ref, hi_ref, flag_ref, nxt_ref, ni_ref,
                 gs_ref, *, n_pairs, max_items):
    def starts(e, acc):
        gs_ref[e] = acc
        return acc + cnt_ref[e]

    lax.fori_loop(0, N_EXPERTS, starts, 0)

    def place(p, c):
        src_ref[slot_ref[p]] = lax.shift_right_logical(p, 1)
        return c

    lax.fori_loop(0, n_pairs, place, 0, unroll=16)

    last = N_EXPERTS - 1

    def group_end(e):
        return gs_ref[e] + cnt_ref[e]

    def next_nonempty(e):
        return lax.while_loop(lambda x: jnp.logical_and(x < last, cnt_ref[jnp.minimum(x, last)] == 0),
                              lambda x: x + 1, e)

    def tile_items(t, carry):
        i, e, prev = carry
        row0 = t * TM
        e = lax.while_loop(lambda x: group_end(x) <= row0, lambda x: x + 1, e)

        def emit(state):
            i, e, prev, first, _ = state
            tile_ref[i] = t
            exp_ref[i] = e
            lo_ref[i] = jnp.clip(gs_ref[e] - row0, 0, TM)
            hi_ref[i] = jnp.clip(group_end(e) - row0, 0, TM)
            flag_ref[i] = first + 2 * (e != prev).astype(jnp.int32)
            done = group_end(e) >= row0 + TM
            e_next = jnp.where(done, e, next_nonempty(e + 1))
            return i + 1, e_next, e, jnp.int32(0), done

        i, e, prev, _, _ = lax.while_loop(lambda s: jnp.logical_not(s[4]), emit,
                                          (i, e, prev, jnp.int32(1), jnp.bool_(False)))
        return i, e, prev

    n_items, _, _ = lax.fori_loop(0, n_pairs // TM, tile_items,
                                  (jnp.int32(0), jnp.int32(0), jnp.int32(-1)))
    ni_ref[0] = n_items

    def pad(i, c):
        tile_ref[i] = tile_ref[n_items - 1]
        exp_ref[i] = exp_ref[n_items - 1]
        lo_ref[i] = 0
        hi_ref[i] = 0
        flag_ref[i] = 0
        nxt_ref[i] = -1
        return c

    lax.fori_loop(n_items, max_items, pad, 0)

    def parity(i, par):
        par = jnp.where((flag_ref[i] & 2) != 0, 1 - par, par)
        flag_ref[i] = flag_ref[i] + 4 * par
        return par

    lax.fori_loop(0, n_items, parity, jnp.int32(1))

    def lookahead(k, following):
        i = n_items - 1 - k
        nxt_ref[i] = following
        return jnp.where((flag_ref[i] & 2) != 0, exp_ref[i], following)

    lax.fori_loop(0, n_items, lookahead, jnp.int32(-1))


def _plan(counts, eid, rank):
    n_pairs = eid.shape[0]
    max_items = n_pairs // TM + N_EXPERTS - 1
    g_start = jnp.cumsum(counts) - counts
    experts = jnp.arange(N_EXPERTS, dtype=jnp.int32)
    slot = jnp.sum(jnp.where(eid[:, None] == experts[None, :], g_start[None, :], 0), axis=1) + rank
    smem = pl.BlockSpec(memory_space=pltpu.SMEM)
    i32 = lambda n: jax.ShapeDtypeStruct((n,), jnp.int32)
    src, it_tile, it_exp, it_lo, it_hi, flags, nxt, n_items = pl.pallas_call(
        functools.partial(_plan_kernel, n_pairs=n_pairs, max_items=max_items),
        in_specs=[smem] * 2,
        out_specs=[smem] * 8,
        out_shape=[i32(n_pairs)] + [i32(max_items)] * 6 + [i32(1)],
        scratch_shapes=[pltpu.SMEM((N_EXPERTS,), jnp.int32)],
        name="moe_plan",
    )(counts, slot)
    return (it_tile, it_exp, it_lo, it_hi, flags, nxt, n_items, src), slot


def _moe_kernel(tile_ref, exp_ref, lo_ref, hi_ref, flag_ref, nxt_ref, ni_ref, src_ref,
                hf_hbm, wg_hbm, wu_hbm, wd_hbm, y_ref, xbuf, wg_buf, wu_buf, wd_buf, wgb, wub, wdb,
                sem, wsem, *, n_tiles, e0):
    i = pl.program_id(0)
    t = tile_ref[i]
    slot = lax.rem(t, 2)

    def gather_start(tile, buf, unrolled):
        base = tile * TM

        def issue(r):
            tok = src_ref[base + r]
            pltpu.make_async_copy(hf_hbm.at[pl.ds(tok, 1)], xbuf.at[buf, pl.ds(r, 1)], sem.at[buf]).start()

        if unrolled:
            for r in range(TM):
                issue(r)
        else:
            lax.fori_loop(0, TM, lambda r, c: (issue(r), c)[1], 0)

    def gather_wait(buf):
        pltpu.make_async_copy(hf_hbm.at[pl.ds(0, TM)], xbuf.at[buf], sem.at[buf]).wait()

    def weight_copies(expert, b):
        e = e0 + expert
        return (pltpu.make_async_copy(wg_hbm.at[e], wg_buf.at[b], wsem.at[b]),
                pltpu.make_async_copy(wu_hbm.at[e], wu_buf.at[b], wsem.at[b]),
                pltpu.make_async_copy(wd_hbm.at[e], wd_buf.at[b], wsem.at[b]))

    def weights_start(expert, b):
        for cp in weight_copies(expert, b):
            cp.start()

    def weights_wait(b):
        for cp in weight_copies(0, b):
            cp.wait()

    @pl.when(i < ni_ref[0])
    def _():
        first_visit = (flag_ref[i] & 1) != 0
        new_expert = (flag_ref[i] & 2) != 0
        wslot = lax.shift_right_logical(flag_ref[i], 2) & 1

        @pl.when(i == 0)
        def _():
            gather_start(0, 0, False)

        @pl.when(first_visit)
        def _():
            gather_wait(slot)

        for b in range(2):
            @pl.when(jnp.logical_and(jnp.logical_and(first_visit, t + 1 < n_tiles), slot == 1 - b))
            def _():
                gather_start(t + 1, b, True)

        @pl.when(i == 0)
        def _():
            weights_start(exp_ref[0], 0)

        @pl.when(new_expert)
        def _():
            @pl.when(nxt_ref[i] >= 0)
            def _():
                weights_start(nxt_ref[i], 1 - wslot)

            weights_wait(wslot)
            wgb[...] = wg_buf[wslot].astype(BF16)
            wub[...] = wu_buf[wslot].astype(BF16)
            wdb[...] = wd_buf[wslot].astype(BF16)

        x = xbuf[slot].astype(BF16)
        a = _dot(x, wgb[...])
        u = _dot(x, wub[...])
        row = lax.broadcasted_iota(jnp.int32, (TM, 1), 0)
        mine = jnp.logical_and(row >= lo_ref[i], row < hi_ref[i])
        h = jnp.where(mine, (a * jax.nn.sigmoid(a)) * u, 0.0)
        yv = _dot(h.astype(BF16), wdb[...])

        @pl.when(first_visit)
        def _():
            y_ref[...] = yv

        @pl.when(jnp.logical_not(first_visit))
        def _():
            y_ref[...] += yv


def _moe(hf, meta, w_gate, w_up, w_down, layer):
    n_rows = meta[-1].shape[0]
    n_tiles = n_rows // TM
    max_items = meta[0].shape[0]
    grid_spec = pltpu.PrefetchScalarGridSpec(
        num_scalar_prefetch=len(meta),
        grid=(max_items,),
        in_specs=[pl.BlockSpec(memory_space=pl.ANY)] * 4,
        out_specs=pl.BlockSpec((TM, D), lambda i, tile, *_: (tile[i], 0)),
        scratch_shapes=[
            pltpu.VMEM((2, TM, D), F32),
            pltpu.VMEM((2, D, D_EXPERT), F32),
            pltpu.VMEM((2, D, D_EXPERT), F32),
            pltpu.VMEM((2, D_EXPERT, D), F32),
            pltpu.VMEM((D, D_EXPERT), BF16),
            pltpu.VMEM((D, D_EXPERT), BF16),
            pltpu.VMEM((D_EXPERT, D), BF16),
            pltpu.SemaphoreType.DMA((2,)),
            pltpu.SemaphoreType.DMA((2,)),
        ],
    )
    return pl.pallas_call(
        functools.partial(_moe_kernel, n_tiles=n_tiles, e0=layer * N_EXPERTS),
        grid_spec=grid_spec,
        out_shape=jax.ShapeDtypeStruct((n_rows, D), F32),
        compiler_params=_cparams(("arbitrary",)),
        name="moe_experts",
    )(*meta, hf, w_gate.reshape(-1, D, D_EXPERT), w_up.reshape(-1, D, D_EXPERT),
      w_down.reshape(-1, D_EXPERT, D))


def _final_kernel(pos_ref, x_ref, gf_ref, w_ref, g_ref, y_hbm, o_ref, ybuf, sem, *, n_steps):
    step = pl.program_id(0) * 8 + pl.program_id(1)
    out = _moe_residual(pos_ref, y_hbm, ybuf, sem, x_ref[...], gf_ref[0], w_ref[...], step, n_steps)
    o_ref[0] = _rms(out, g_ref[...])


def _final(pos, wts, xn, mods, y, g_final, B):
    tile = lambda b, j, p: (b * 8 + j, 0)
    grid_spec = pltpu.PrefetchScalarGridSpec(
        num_scalar_prefetch=1,
        grid=(B, 8),
        in_specs=[
            pl.BlockSpec((TM, D), tile),
            pl.BlockSpec((1, 1, D), lambda b, j, p: (b * 6 + 5, 0, 0)),
            pl.BlockSpec((TM, 2), tile),
            pl.BlockSpec((1, D), lambda b, j, p: (0, 0)),
            pl.BlockSpec(memory_space=pl.ANY),
        ],
        out_specs=pl.BlockSpec((1, TM, D), lambda b, j, p: (b, j, 0)),
        scratch_shapes=_MOE_GATHER_SCRATCH,
    )
    return pl.pallas_call(
        functools.partial(_final_kernel, n_steps=B * 8),
        grid_spec=grid_spec,
        out_shape=jax.ShapeDtypeStruct((B, 8 * TM, D), F32),
        compiler_params=_cparams(("arbitrary", "arbitrary")),
        name="final",
    )(pos, xn, mods, wts, g_final.reshape(1, D), y)


def _deinterleave(w):
    pairs = w.reshape(w.shape[:-1] + (w.shape[-1] // 2, 2))
    even, odd = pairs[..., 0], pairs[..., 1]
    return jnp.concatenate([even, odd, even, odd], axis=-1)


def _prep_w_in(w_in):
    sizes = (Q_LORA, KV_LORA, QK_ROPE, FNET_WIDTH, NA_WIDTH, NA_WIDTH, NA_WIDTH)
    starts = np.concatenate([[0], np.cumsum(sizes)])
    seg = lambda i: w_in[:, starts[i]:starts[i + 1]]
    return jnp.concatenate([seg(0), seg(3), seg(4), seg(5), seg(6), seg(1), _deinterleave(seg(2))],
                           axis=1).astype(BF16)


def _prep_w_uq(w_uq):
    w = w_uq.reshape(Q_LORA, MLA_HEADS, QK_NOPE + QK_ROPE)
    w = jnp.concatenate([w[..., :QK_NOPE], _deinterleave(w[..., QK_NOPE:])], axis=-1)
    return w.reshape(Q_LORA, MLA_HEADS * 256).astype(BF16)


def _rope_tables(seq):
    half = QK_ROPE // 2
    inv_freq = ROPE_THETA ** (-jnp.arange(0, half, 2, dtype=F32) / half)
    t = jnp.arange(seq, dtype=jnp.int32)
    row = (t // GRID_W).astype(F32)
    col = (t % GRID_W).astype(F32)
    ang = jnp.concatenate([row[:, None] * inv_freq, col[:, None] * inv_freq], axis=-1)
    cos, sin = jnp.cos(ang), jnp.sin(ang)
    zeros = jnp.zeros((seq, 64), F32)
    cos_l = jnp.concatenate([cos, cos, zeros], axis=1)
    sin_l = jnp.concatenate([-sin, sin, zeros], axis=1)
    cos_c = jnp.concatenate([jnp.ones((CTX_LEN, 64), F32), jnp.zeros((CTX_LEN, 64), F32)], axis=1)
    sin_c = jnp.zeros((CTX_LEN, 128), F32)
    return jnp.concatenate([cos_c, cos_l], axis=0), jnp.concatenate([sin_c, sin_l], axis=0)


def kernel(x, c, ctx, c_ctx, w_ada, b_ada, g_attn, g_ffn, w_in, g_q, w_uq, g_kv, w_ukv, w_fnet, b_fnet,
           na_rpb, g_out, w_out, w_rg, b_rg, w_re, b_re, w_gate, w_up, w_down, g_final):
    B, S, _ = x.shape
    L = w_ada.shape[0]
    assert ctx.shape[1] == CTX_LEN == TM and S == 8 * TM and B <= 4
    T = B * 9 * TM

    cond8 = jnp.zeros((8, D), F32).at[:B].set(c).at[4].set(c_ctx)
    mods_all = _modulation(cond8, w_ada, b_ada)
    na_bias = _na_bias(na_rpb)
    cos_t, sin_t = _rope_tables(S)
    cs_lat = _dft_tables(S)
    cs_ctx = _dft_tables(CTX_LEN)
    cd_c, cd_s = _dft_cos_sin(FNET_GROUP_DIM)
    cd = jnp.concatenate([cd_c, -cd_s], axis=1).astype(BF16)

    xs = (ctx.reshape(B * CTX_LEN, D), x.reshape(B * S, D))
    pending = None
    for l in range(L):
        last = l == L - 1
        j0 = 1 if last else 0
        mods = mods_all[l].reshape(48, 1, D)
        w_in_ext = _prep_w_in(w_in[l])
        w_uq_ext = _prep_w_uq(w_uq[l])
        w_router = jnp.zeros((D, ROUTER_COLS), F32).at[:, :N_GROUPS].set(w_rg[l]) \
            .at[:, N_GROUPS:N_GROUPS + N_EXPERTS].set(w_re[l])
        w_scaled = w_router * 65537.0
        w_router_hi = w_scaled - (w_scaled - w_router)
        w_router = jnp.concatenate([w_router_hi, w_router - w_router_hi], axis=1).astype(BF16)
        b_router = jnp.zeros((1, ROUTER_COLS), F32).at[0, :N_GROUPS].set(b_rg[l]) \
            .at[0, N_GROUPS:N_GROUPS + N_EXPERTS].set(b_re[l])

        mla = (g_q[l], g_kv[l], w_uq_ext, w_ukv[l].astype(BF16), cos_t, sin_t)
        if pending is None:
            zf, naq, nak, navt, q, k, vt = _in_projection(xs, mods, g_attn[l], w_in_ext, mla, B)
        else:
            zf, naq, nak, navt, q, k, vt, xs = _in_projection(xs, mods, g_attn[l], w_in_ext, mla, B,
                                                              moe=pending)
        o_mla = _mla_attention(q, k, vt, B, j0)
        o_na = _na_attention(naq, nak, navt, na_bias[l], B, j0)
        w_f = w_fnet[l].astype(BF16)
        o_f = _fnet(zf, cs_lat, cs_ctx, cd, w_f, b_fnet[l], B, j0)
        xn, hf, route_w, route_i, counts = _merge(o_mla, o_f, o_na, xs, mods, g_out[l], g_ffn[l],
                                                  w_out[l].astype(BF16), w_router, b_router, B, j0)
        wts = route_w[:, 0:2]
        meta, slot = _plan(counts[0, N_GROUPS:N_GROUPS + N_EXPERTS], route_i[:, 0:2].reshape(-1),
                           route_i[:, 2:4].reshape(-1))
        y = _moe(hf, meta, w_gate, w_up, w_down, l)
        if last:
            return _final(slot, wts, xn, mods, y, g_final, B)
        xs, pending = xn, (slot, wts, y, mods)
```

```python
import functools

import numpy as np
import jax
import jax.numpy as jnp
from jax import lax
from jax.experimental import pallas as pl
from jax.experimental.pallas import tpu as pltpu

F32 = jnp.float32
BF16 = jnp.bfloat16

D = 2048
GRID_W = 64
CTX_LEN = 256
EPS = 1e-6
NEG_INF = -1e30
ROPE_THETA = 10000.0

V_DIM = 128
MLA_WIDTH = D // 2
MLA_HEADS = MLA_WIDTH // V_DIM
QK_NOPE = 128
QK_ROPE = 64
Q_LORA = D // 4
KV_LORA = D // 8
FNET_WIDTH = D // 4
FNET_GROUP_DIM = 128
FNET_GROUPS = FNET_WIDTH // FNET_GROUP_DIM
NA_WIDTH = D // 4
NA_HEAD_DIM = 128
NA_HEADS = NA_WIDTH // NA_HEAD_DIM
NA_KH_MAX = 8
NA_KW = 16
N_GROUPS = 4
EXPERTS_PER_GROUP = 8
N_EXPERTS = N_GROUPS * EXPERTS_PER_GROUP
D_EXPERT = D // 4

MLA_QSCALE = (QK_NOPE + QK_ROPE) ** -0.5 * float(np.log2(np.e))
MLA_CHUNK_TILES = 8
NA_QSCALE = NA_HEAD_DIM ** -0.5 * float(np.log2(np.e))
NA_WIN_ROWS = 12
TM = 256
IN_COLS = Q_LORA + KV_LORA + QK_ROPE + FNET_WIDTH + 3 * NA_WIDTH
ROUTER_COLS = 128
VMEM_LIMIT = 56 * 1024 * 1024


def _cparams(sem):
    return pltpu.CompilerParams(dimension_semantics=sem, vmem_limit_bytes=VMEM_LIMIT)


def _rms(v, g):
    return v * lax.rsqrt(jnp.mean(v * v, axis=-1, keepdims=True) + EPS) * g


def _dot(a, b):
    return jnp.dot(a, b, preferred_element_type=F32)


def _dot_nt(a, b):
    return lax.dot_general(a, b, (((1,), (1,)), ((), ())), preferred_element_type=F32)


def _mod_kernel(c_ref, w_ref, b_ref, o_ref):
    c = c_ref[...]
    s = c * jax.nn.sigmoid(c)
    o_ref[0] = _dot(s.astype(BF16), w_ref[0].astype(BF16)) + b_ref[0]


def _modulation(cond8, w_ada, b_ada):
    L = w_ada.shape[0]
    tn = 1024
    return pl.pallas_call(
        _mod_kernel,
        grid=(L, 6 * D // tn),
        in_specs=[
            pl.BlockSpec((8, D), lambda l, n: (0, 0)),
            pl.BlockSpec((1, D, tn), lambda l, n: (l, 0, n)),
            pl.BlockSpec((1, 1, tn), lambda l, n: (l, 0, n)),
        ],
        out_specs=pl.BlockSpec((1, 8, tn), lambda l, n: (l, 0, n)),
        out_shape=jax.ShapeDtypeStruct((L, 8, 6 * D), F32),
        compiler_params=_cparams(("arbitrary", "arbitrary")),
        name="modulation",
    )(cond8, w_ada, b_ada.reshape(L, 1, 6 * D))


def _tile_of(b, j, j0):
    return b * 9 + j0 + j


def _otile(b, j, j0):
    return b * (9 - j0) + j


def _mod_row(j, b, j0):
    return jnp.where(j0 + j == 0, 4, b)


def _mod_spec(k, j0):
    return pl.BlockSpec((1, 1, D), lambda b, j: (_mod_row(j, b, j0) * 6 + k, 0, 0))


def _stream_specs(xs, j0):
    if isinstance(xs, tuple):
        return [pl.BlockSpec((TM, D), lambda b, j, *_: (b, 0)),
                pl.BlockSpec((TM, D), lambda b, j, *_: (b * 8 + jnp.maximum(j0 + j - 1, 0), 0))], list(xs)
    return [pl.BlockSpec((TM, D), lambda b, j, *_: (_tile_of(b, j, j0), 0))], [xs]


def _stream_tile(refs, j0):
    if len(refs) == 1:
        return refs[0][...]
    return jnp.where(pl.program_id(1) + j0 == 0, refs[0][...], refs[1][...])


def _moe_residual(pos_ref, y_hbm, ybuf, sem, x, gate, w, step, n_steps):
    buf = lax.rem(step, 2)

    def start(tile, b, unrolled):
        base = tile * TM

        def issue(r):
            for k in range(2):
                pltpu.make_async_copy(y_hbm.at[pl.ds(pos_ref[2 * (base + r) + k], 1)],
                                      ybuf.at[b, k, pl.ds(r, 1)], sem.at[b]).start()

        if unrolled:
            for r in range(TM):
                issue(r)
        else:
            lax.fori_loop(0, TM, lambda r, c: (issue(r), c)[1], 0)

    def wait(b):
        for k in range(2):
            pltpu.make_async_copy(y_hbm.at[pl.ds(0, TM)], ybuf.at[b, k], sem.at[b]).wait()

    @pl.when(step == 0)
    def _():
        start(0, 0, False)

    for b in range(2):
        @pl.when(buf == 1 - b)
        def _():
            start(jnp.minimum(step + 1, n_steps - 1), b, True)

    wait(buf)
    out = x + gate * (w[:, 0:1] * ybuf[buf, 0] + w[:, 1:2] * ybuf[buf, 1])

    @pl.when(step == n_steps - 1)
    def _():
        wait(1 - buf)

    return out


_MOE_GATHER_SCRATCH = [pltpu.VMEM((2, 2, TM, D), F32), pltpu.SemaphoreType.DMA((2,))]


def _rope(r, cos_t, sin_t):
    return r * cos_t + pltpu.roll(r, 32, 1) * sin_t


def _inproj_body(x, sh_ref, sc_ref, g_ref, w_ref, gq_ref, gkv_ref, wq_ref, wkv_ref, cos_ref, sin_ref,
                 zf_ref, naq_ref, nak_ref, navt_ref, q_ref, k_ref, vt_ref):
    h = _rms(x, g_ref[...])
    h = h * (1.0 + sc_ref[0]) + sh_ref[0]
    z = _dot(h.astype(BF16), w_ref[0])
    c_zf =Q_LORA + KV_LORA + QK_ROPE
    zf_ref[...] = z[:, c_zf:c_zf + FNET_WIDTH].astype(BF16)
    c_na = c_zf + FNET_WIDTH
    naq_ref[...] = (z[:, c_na:c_na + NA_WIDTH] * NA_QSCALE).astype(BF16)
    nak_ref[...] = z[:, c_na + NA_WIDTH:c_na + 2 * NA_WIDTH].astype(BF16)
    for hd in range(NA_HEADS):
        c = c_na + 2 * NA_WIDTH + hd * NA_HEAD_DIM
        navt_ref[hd * NA_HEAD_DIM:(hd + 1) * NA_HEAD_DIM, :] = z[:, c:c + NA_HEAD_DIM].T.astype(BF16)
    cos_t = cos_ref[...]
    sin_t = sin_ref[...]
    q = _dot(_rms(z[:, 0:Q_LORA], gq_ref[...]).astype(BF16), wq_ref[...]) * MLA_QSCALE
    kv = _dot(_rms(z[:, Q_LORA:Q_LORA + KV_LORA], gkv_ref[...]).astype(BF16), wkv_ref[...])
    kr = z[:, Q_LORA + KV_LORA:Q_LORA + KV_LORA + 128]
    kr = jnp.where(lax.broadcasted_iota(jnp.int32, kr.shape, 1) < QK_ROPE, kr, pltpu.roll(kr, QK_ROPE, 1))
    k_rope = _rope(kr, cos_t, sin_t).astype(BF16)
    for hd in range(MLA_HEADS):
        c = hd * 256
        q_ref[:, c:c + 128] = q[:, c:c + 128].astype(BF16)
        q_ref[:, c + 128:c + 256] = _rope(q[:, c + 128:c + 256], cos_t, sin_t).astype(BF16)
        k_ref[:, c:c + 128] = kv[:, c:c + 128].astype(BF16)
        k_ref[:, c + 128:c + 256] = k_rope
        vt_ref[hd * 128:(hd + 1) * 128, :] = kv[:, c + 128:c + 256].T.astype(BF16)


def _inproj_kernel(*refs, n_stream):
    _inproj_body(_stream_tile(refs[:n_stream], 0), *refs[n_stream:])


def _inproj_moe_kernel(pos_ref, x_ref, gf_ref, wts_ref, y_hbm, *refs, n_steps):
    *refs, xs_ref, ybuf, sem = refs
    step = pl.program_id(0) * 9 + pl.program_id(1)
    x = _moe_residual(pos_ref, y_hbm, ybuf, sem, x_ref[...], gf_ref[0], wts_ref[...], step, n_steps)
    xs_ref[...] = x
    _inproj_body(x, *refs)


def _in_projection(xs, mods, g_attn, w_in_ext, layer, mla, B, moe=None):
    T = B * 9 * TM
    row = lambda b, j, *_: (_tile_of(b, j, 0), 0)
    col = lambda b, j, *_: (0, _tile_of(b, j, 0))
    const = lambda b, j, *_: (0, 0)
    mod = lambda k: pl.BlockSpec((1, 1, D), lambda b, j, *_: (_mod_row(j, b, 0) * 6 + k, 0, 0))
    once = dict(pipeline_mode=pl.Buffered(1))
    g_q, g_kv, w_uq_ext, w_ukv, cos_t, sin_t = mla
    in_specs = [
        pl.BlockSpec((TM, D), row),
        mod(0),
        mod(1),
        pl.BlockSpec((1, D), const),
        pl.BlockSpec((1, D, IN_COLS), lambda b, j, *_: (layer, 0, 0), **once),
        pl.BlockSpec((1, Q_LORA), const),
        pl.BlockSpec((1, KV_LORA), const),
        pl.BlockSpec((Q_LORA, MLA_HEADS * 256), const, **once),
        pl.BlockSpec((KV_LORA, MLA_HEADS * 256), const, **once),
        pl.BlockSpec((TM, 128), lambda b, j, *_: (j, 0)),
        pl.BlockSpec((TM, 128), lambda b, j, *_: (j, 0)),
    ]
    out_specs = [
        pl.BlockSpec((TM, FNET_WIDTH), row),
        pl.BlockSpec((TM, NA_WIDTH), row),
        pl.BlockSpec((TM, NA_WIDTH), row),
        pl.BlockSpec((NA_WIDTH, TM), col),
        pl.BlockSpec((TM, MLA_HEADS * 256), row),
        pl.BlockSpec((TM, MLA_HEADS * 256), row),
        pl.BlockSpec((MLA_WIDTH, TM), col),
    ]
    out_shape = [
        jax.ShapeDtypeStruct((T, FNET_WIDTH), BF16),
        jax.ShapeDtypeStruct((T, NA_WIDTH), BF16),
        jax.ShapeDtypeStruct((T, NA_WIDTH), BF16),
        jax.ShapeDtypeStruct((NA_WIDTH, T), BF16),
        jax.ShapeDtypeStruct((T, MLA_HEADS * 256), BF16),
        jax.ShapeDtypeStruct((T, MLA_HEADS * 256), BF16),
        jax.ShapeDtypeStruct((MLA_WIDTH, T), BF16),
    ]
    args = [xs, mods, mods, g_attn.reshape(1, D), w_in_ext, g_q.reshape(1, -1), g_kv.reshape(1, -1),
            w_uq_ext, w_ukv, cos_t, sin_t]
    if moe is None:
        stream_specs, stream_args = _stream_specs(xs, 0)
        return pl.pallas_call(
            functools.partial(_inproj_kernel, n_stream=len(stream_args)),
            grid=(B, 9),
            in_specs=stream_specs + in_specs[1:],
            out_specs=out_specs,
            out_shape=out_shape,
            compiler_params=_cparams(("arbitrary", "arbitrary")),
            name="in_projection",
        )(*stream_args, *args[1:])
    pos, wts, y, mods_prev = moe
    in_specs = [in_specs[0], mod(5), pl.BlockSpec((TM, 2), row), pl.BlockSpec(memory_space=pl.ANY)] \
        + in_specs[1:]
    grid_spec = pltpu.PrefetchScalarGridSpec(
        num_scalar_prefetch=1,
        grid=(B, 9),
        in_specs=in_specs,
        out_specs=out_specs + [pl.BlockSpec((TM, D), row)],
        scratch_shapes=_MOE_GATHER_SCRATCH,
    )
    return pl.pallas_call(
        functools.partial(_inproj_moe_kernel, n_steps=B * 9),
        grid_spec=grid_spec,
        out_shape=out_shape + [jax.ShapeDtypeStruct((T, D), F32)],
        compiler_params=_cparams(("arbitrary", "arbitrary")),
        name="in_projection_moe",
    )(pos, xs, mods_prev, wts, y, *args[1:])


def _mla_attn_kernel(q_ref, k_ref, vt_ref, o_ref, *, j0):
    def attend(q0, nq, nk, o0):
        st = _dot_nt(k_ref[0:nk, :], q_ref[q0:q0 + nq, :])
        m = jnp.max(st, axis=0, keepdims=True)
        p = jnp.exp2(st - m)
        l = jnp.sum(p, axis=0, keepdims=True)
        ot = _dot(vt_ref[:, 0:nk], p.astype(BF16)) / l
        o_ref[o0:o0 + nq, :] = ot.T

    if j0 == 0:
        attend(0, CTX_LEN, CTX_LEN, 0)
    for c in range(8 // MLA_CHUNK_TILES):
        attend(CTX_LEN + MLA_CHUNK_TILES * c * TM, MLA_CHUNK_TILES * TM, 9 * TM,
               (1 - j0 + MLA_CHUNK_TILES * c) * TM)


def _mla_attention(q, k, vt, B, j0):
    rows = (9 - j0) * TM
    return pl.pallas_call(
        functools.partial(_mla_attn_kernel, j0=j0),
        grid=(B, MLA_HEADS),
        in_specs=[
            pl.BlockSpec((9 * TM, 256), lambda b, h: (b, h)),
            pl.BlockSpec((9 * TM, 256), lambda b, h: (b, h)),
            pl.BlockSpec((V_DIM, 9 * TM), lambda b, h: (h, b)),
        ],
        out_specs=pl.BlockSpec((rows, V_DIM), lambda b, h: (b, h)),
        out_shape=jax.ShapeDtypeStruct((B * rows, MLA_WIDTH), F32),
        compiler_params=_cparams(("arbitrary", "arbitrary")),
        name="mla_attention",
    )(q, k, vt)


def _na_chunk(g):
    start_row = min(max(4 * g - 4, 0), 8 * TM // GRID_W - NA_WIN_ROWS)
    pattern = 0 if g == 0 else (2 if g == 7 else 1)
    return start_row, pattern


def _na_kernel(q_ref, k_ref, vt_ref, bias_ref, o_ref, *, j0):
    def finish(parts, o0):
        m = None
        for st, _ in parts:
            pm = jnp.max(st, axis=0, keepdims=True)
            m = pm if m is None else jnp.maximum(m, pm)
        l = None
        ot = None
        for st, vt in parts:
            p = jnp.exp2(st - m)
            pl_sum = jnp.sum(p, axis=0, keepdims=True)
            po = _dot(vt, p.astype(BF16))
            l = pl_sum if l is None else l + pl_sum
            ot = po if ot is None else ot + po
        o_ref[o0:o0 + TM, :] = (ot / l).T

    if j0 == 0:
        st = _dot_nt(k_ref[0:CTX_LEN, :], q_ref[0:CTX_LEN, :])
        finish([(st, vt_ref[:, 0:CTX_LEN])], 0)
    for g in range(8):
        start_row, pattern = _na_chunk(g)
        k0 = CTX_LEN + start_row * GRID_W
        nk = NA_WIN_ROWS * GRID_W
        q = q_ref[CTX_LEN + g * TM:CTX_LEN + (g + 1) * TM, :]
        st_loc = _dot_nt(k_ref[k0:k0 + nk, :], q) + bias_ref[0, pattern]
        st_ctx = _dot_nt(k_ref[0:CTX_LEN, :], q)
        finish([(st_loc, vt_ref[:, k0:k0 + nk]), (st_ctx, vt_ref[:, 0:CTX_LEN])], (1 - j0 + g) * TM)


def _na_attention(naq, nak, navt, bias, B, j0):
    rows = (9 - j0) * TM
    return pl.pallas_call(
        functools.partial(_na_kernel, j0=j0),
        grid=(NA_HEADS, B),
        in_specs=[
            pl.BlockSpec((9 * TM, NA_HEAD_DIM), lambda h, b: (b, h)),
            pl.BlockSpec((9 * TM, NA_HEAD_DIM), lambda h, b: (b, h)),
            pl.BlockSpec((NA_HEAD_DIM, 9 * TM), lambda h, b: (h, b)),
            pl.BlockSpec((1, 3, NA_WIN_ROWS * GRID_W, TM), lambda h, b: (h, 0, 0, 0)),
        ],
        out_specs=pl.BlockSpec((rows, NA_HEAD_DIM), lambda h, b: (b, h)),
        out_shape=jax.ShapeDtypeStruct((B * rows, NA_WIDTH), F32),
        compiler_params=_cparams(("arbitrary", "arbitrary")),
        name="na_attention",
    )(naq, nak, navt, bias)


def _na_bias(rpb):
    kh, rows, nq = NA_KH_MAX, 8 * TM // GRID_W, TM // GRID_W
    cq = np.arange(GRID_W)
    ck = np.arange(GRID_W)
    col_start = np.clip(cq - NA_KW // 2, 0, GRID_W - NA_KW)
    col_ok = (ck[:, None] >= col_start[None, :]) & (ck[:, None] < col_start[None, :] + NA_KW)
    dcol = np.clip(ck[:, None] - cq[None, :] + (NA_KW - 1), 0, 2 * NA_KW - 2)
    select = np.zeros((2 * NA_KW - 1, GRID_W * GRID_W), np.float32)
    select[dcol.reshape(-1), np.arange(GRID_W * GRID_W)] = 1.0
    blocks = jnp.einsum("lhdm,mn->lhdn", rpb.astype(F32) * float(np.log2(np.e)), jnp.asarray(select),
                        precision=lax.Precision.HIGHEST)
    blocks = blocks.reshape(rpb.shape[:3] + (GRID_W, GRID_W))
    blocks = jnp.where(jnp.asarray(col_ok), blocks, NEG_INF)
    masked = jnp.full(rpb.shape[:2] + (GRID_W, GRID_W), NEG_INF, F32)
    patterns = []
    for g in (0, 1, 7):
        start_row, _ = _na_chunk(g)
        key_rows = []
        for kr in range(NA_WIN_ROWS):
            key_row = start_row + kr
            row = []
            for qr in range(nq):
                r = 4 * g + qr
                r_start = min(max(r - kh // 2, 0), rows - kh)
                in_rows = r_start <= key_row < r_start + kh
                row.append(blocks[:, :, key_row - r + (kh - 1)] if in_rows else masked)
            key_rows.append(jnp.concatenate(row, axis=-1))
        patterns.append(jnp.concatenate(key_rows, axis=-2))
    return jnp.stack(patterns, axis=2)


def _fnet_kernel(z_ref, csl_ref, csc_ref, cd_ref, w_ref, b_ref, o_ref, ab_ref, *, j0, seq):
    j = pl.program_id(1) + j0

    def small_side(row0, length):
        for g in range(FNET_GROUPS):
            c = g * FNET_GROUP_DIM
            ab = _dot(z_ref[row0:row0 + length, c:c + FNET_GROUP_DIM], cd_ref[...])
            ab_ref[0:length, c:c + FNET_GROUP_DIM] = ab[:, 0:FNET_GROUP_DIM].astype(BF16)
            ab_ref[length:2 * length, c:c + FNET_GROUP_DIM] = ab[:, FNET_GROUP_DIM:].astype(BF16)

    def long_side(cs, length):
        f = _dot(cs, ab_ref[0:2 * length, :]) * (length * FNET_GROUP_DIM) ** -0.5
        o_ref[...] = _dot(f.astype(BF16), w_ref[...]) + b_ref[...]

    if j0 == 0:
        @pl.when(j == 0)
        def _():
            small_side(0, CTX_LEN)
            long_side(csc_ref[...], CTX_LEN)

    @pl.when(j == 1)
    def _():
        small_side(CTX_LEN, seq)

    @pl.when(j >= 1)
    def _():
        long_side(csl_ref[...], seq)


def _fnet(zf, cs_lat, cs_ctx, cd, w_fnet, b_fnet, B, j0):
    seq = 8 * TM
    return pl.pallas_call(
        functools.partial(_fnet_kernel, j0=j0, seq=seq),
        grid=(B, 9 - j0),
        in_specs=[
            pl.BlockSpec((9 * TM, FNET_WIDTH), lambda b, j: (b, 0)),
            pl.BlockSpec((TM, 2 * seq), lambda b, j: (jnp.maximum(j0 + j - 1, 0), 0)),
            pl.BlockSpec((CTX_LEN, 2 * CTX_LEN), lambda b, j: (0, 0)),
            pl.BlockSpec((FNET_GROUP_DIM, 2 * FNET_GROUP_DIM), lambda b, j: (0, 0)),
            pl.BlockSpec((FNET_WIDTH, FNET_WIDTH), lambda b, j: (0, 0)),
            pl.BlockSpec((1, FNET_WIDTH), lambda b, j: (0, 0)),
        ],
        out_specs=pl.BlockSpec((TM, FNET_WIDTH), lambda b, j: (_otile(b, j, j0), 0)),
        out_shape=jax.ShapeDtypeStruct((B * (9 - j0) * TM, FNET_WIDTH), F32),
        scratch_shapes=[pltpu.VMEM((2 * seq, FNET_WIDTH), BF16)],
        compiler_params=_cparams(("arbitrary", "arbitrary")),
        name="fnet",
    )(zf, cs_lat, cs_ctx, cd, w_fnet, b_fnet.reshape(1, -1))


def _dft_cos_sin(n):
    j = jnp.arange(n, dtype=jnp.int32)[:, None]
    if n <= 64:
        ang = ((j * j.T) % n).astype(F32) * (2.0 * np.pi / n)
        return jnp.cos(ang), jnp.sin(ang)
    k1 = jnp.arange(n // 64, dtype=jnp.int32)[None, :]
    k0 = jnp.arange(64, dtype=jnp.int32)[None, :]
    a = ((j * k1 * 64) % n).astype(F32) * (2.0 * np.pi / n)
    b = ((j * k0) % n).astype(F32) * (2.0 * np.pi / n)
    ca, sa, cb, sb = jnp.cos(a), jnp.sin(a), jnp.cos(b), jnp.sin(b)
    c = ca[:, :, None] * cb[:, None, :] - sa[:, :, None] * sb[:, None, :]
    s = sa[:, :, None] * cb[:, None, :] + ca[:, :, None] * sb[:, None, :]
    return c.reshape(n, n), s.reshape(n, n)


def _dft_tables(n):
    c, s = _dft_cos_sin(n)
    return jnp.concatenate([c, s], axis=1).astype(BF16)


def _merge_kernel(*refs, n_stream, j0):
    x = _stream_tile(refs[:n_stream], j0)
    (om_ref, of_ref, on_ref, ga_ref, shf_ref, scf_ref, gout_ref, gffn_ref, wout_ref, wr_ref, br_ref,
     xn_ref, hf_ref, rw_ref, ri_ref, cnt_ref, run_ref) = refs[n_stream:]
    ym = _rms(om_ref[...], gout_ref[:, 0:MLA_WIDTH]).astype(BF16)
    yf = _rms(of_ref[...], gout_ref[:, MLA_WIDTH:MLA_WIDTH + FNET_WIDTH]).astype(BF16)
    yn = _rms(on_ref[...], gout_ref[:, MLA_WIDTH + FNET_WIDTH:]).astype(BF16)
    acc = _dot(ym, wout_ref[0:MLA_WIDTH, :])
    acc = acc + _dot(yf, wout_ref[MLA_WIDTH:MLA_WIDTH + FNET_WIDTH, :])
    acc = acc + _dot(yn, wout_ref[MLA_WIDTH + FNET_WIDTH:, :])
    xn = x + ga_ref[0] * acc
    xn_ref[...] = xn
    hf = _rms(xn, gffn_ref[...]) * (1.0 + scf_ref[0]) + shf_ref[0]
    hf_ref[...] = hf
    hi = hf.astype(BF16)
    lo = (hf - hi.astype(F32)).astype(BF16)
    a = _dot(hi, wr_ref[...])
    b = _dot(lo, wr_ref[...])
    small = a[:, ROUTER_COLS:] + (b[:, :ROUTER_COLS] + b[:, ROUTER_COLS:])
    _route_tile(a[:, :ROUTER_COLS] + small + br_ref[...], rw_ref, ri_ref, cnt_ref, run_ref)


def _route_tile(lg, rw_ref, ri_ref, cnt_ref, run_ref):
    first = jnp.logical_and(pl.program_id(0) == 0, pl.program_id(1) == 0)

    @pl.when(first)
    def _():
        run_ref[...] = jnp.zeros_like(run_ref)

    lane = lax.broadcasted_iota(jnp.int32, lg.shape, 1)
    neg = jnp.float32(-jnp.inf)

    def top(v):
        vmax = jnp.max(v, axis=1, keepdims=True)
        idx = jnp.min(jnp.where(v == vmax, lane, ROUTER_COLS), axis=1, keepdims=True)
        return vmax, idx

    in_groups = lane < N_GROUPS
    gl = jnp.where(in_groups, lg, neg)
    g_max, g_sel = top(gl)
    g_w = 1.0 / jnp.sum(jnp.where(in_groups, jnp.exp(gl - g_max), 0.0), axis=1, keepdims=True)
    e_lo = N_GROUPS + g_sel * EXPERTS_PER_GROUP
    el = jnp.where(jnp.logical_and(lane >= e_lo, lane < e_lo + EXPERTS_PER_GROUP), lg, neg)
    e1_max, i1 = top(el)
    e2_max, i2 = top(jnp.where(lane == i1, neg, el))
    t = jnp.exp(e2_max - e1_max)
    w0 = g_w / (1.0 + t)
    w1 = w0 * t
    rw_ref[...] = jnp.where(lane == 0, w0, jnp.where(lane == 1, w1, 0.0))

    row = lax.broadcasted_iota(jnp.int32, (TM, TM), 0)
    col = lax.broadcasted_iota(jnp.int32, (TM, TM), 1)
    tri = jnp.where(row >= col, 1.0, 0.0).astype(BF16)
    hot0 = lane == i1
    hot1 = lane == i2
    c0 = _dot(tri, jnp.where(hot0, 1.0, 0.0).astype(BF16))
    c1 = _dot(tri, jnp.where(hot1, 1.0, 0.0).astype(BF16))
    run = run_ref[...]
    tot0 = c0[TM - 1:TM, :]
    rank0 = jnp.sum(jnp.where(hot0, run + c0 - 1.0, 0.0), axis=1, keepdims=True)
    rank1 = jnp.sum(jnp.where(hot1, run + tot0 + c1 - 1.0, 0.0), axis=1, keepdims=True)
    run = run + tot0 + c1[TM - 1:TM, :]
    run_ref[...] = run
    cnt_ref[...] = jnp.broadcast_to(run, cnt_ref.shape).astype(jnp.int32)
    ri_ref[...] = jnp.where(lane == 0, i1 - N_GROUPS, jnp.where(lane == 1, i2 - N_GROUPS, jnp.where(
        lane == 2, rank0.astype(jnp.int32), jnp.where(lane == 3, rank1.astype(jnp.int32), 0))))


def _merge(o_mla, o_f, o_na, xs, mods, g_out, g_ffn, w_out, w_router, b_router, B, j0):
    T = B * (9 - j0) * TM
    row = lambda b, j: (_otile(b, j, j0), 0)
    const = lambda b, j: (0, 0)
    stream_specs, stream_args = _stream_specs(xs, j0)
    return pl.pallas_call(
        functools.partial(_merge_kernel, n_stream=len(stream_args), j0=j0),
        grid=(B, 9 - j0),
        in_specs=stream_specs + [
            pl.BlockSpec((TM, MLA_WIDTH), row),
            pl.BlockSpec((TM, FNET_WIDTH), row),
            pl.BlockSpec((TM, NA_WIDTH), row),
            _mod_spec(2, j0),
            _mod_spec(3, j0),
            _mod_spec(4, j0),
            pl.BlockSpec((1, D), const),
            pl.BlockSpec((1, D), const),
            pl.BlockSpec((D, D), const),
            pl.BlockSpec((D, 2 * ROUTER_COLS), const),
            pl.BlockSpec((1, ROUTER_COLS), const),
        ],
        out_specs=[
            pl.BlockSpec((TM, D), row),
            pl.BlockSpec((TM, D), row),
            pl.BlockSpec((TM, ROUTER_COLS), row),
            pl.BlockSpec((TM, ROUTER_COLS), row),
            pl.BlockSpec((8, ROUTER_COLS), const),
        ],
        out_shape=[
            jax.ShapeDtypeStruct((T, D), F32),
            jax.ShapeDtypeStruct((T, D), F32),
            jax.ShapeDtypeStruct((T, ROUTER_COLS), F32),
            jax.ShapeDtypeStruct((T, ROUTER_COLS), jnp.int32),
            jax.ShapeDtypeStruct((8, ROUTER_COLS), jnp.int32),
        ],
        scratch_shapes=[pltpu.VMEM((1, ROUTER_COLS), F32)],
        compiler_params=_cparams(("arbitrary", "arbitrary")),
        name="merge",
    )(*stream_args, o_mla, o_f, o_na, mods, mods, mods, g_out.reshape(1, D), g_ffn.reshape(1, D),
      w_out, w_router, b_router)


def _plan_kernel(cnt_ref, slot_ref, src_ref, tile_ref, exp_ref, lo_ref, hi_ref, flag_ref, nxt_ref, ni_ref,
                 gs_ref, *, n_pairs, max_items):
    def starts(e, acc):
        gs_ref[e] = acc
        return acc + cnt_ref[e]

    lax.fori_loop(0, N_EXPERTS, starts, 0)

    def place(p, c):
        src_ref[slot_ref[p]] = lax.shift_right_logical(p, 1)
        return c

    lax.fori_loop(0, n_pairs, place, 0, unroll=16)

    last = N_EXPERTS - 1

    def group_end(e):
        return gs_ref[e] + cnt_ref[e]

    def next_nonempty(e):
        return lax.while_loop(lambda x: jnp.logical_and(x < last, cnt_ref[jnp.minimum(x, last)] == 0),
                              lambda x: x + 1, e)

    def tile_items(t, carry):
        i, e, prev = carry
        row0 = t * TM
        e = lax.while_loop(lambda x: group_end(x) <= row0, lambda x: x + 1, e)

        def emit(state):
            i, e, prev, first, _ = state
            tile_ref[i] = t
            exp_ref[i] = e
            lo_ref[i] = jnp.clip(gs_ref[e] - row0, 0, TM)
            hi_ref[i] = jnp.clip(group_end(e) - row0, 0, TM)
            flag_ref[i] = first + 2 * (e != prev).astype(jnp.int32)
            done = group_end(e) >= row0 + TM
            e_next = jnp.where(done, e, next_nonempty(e + 1))
            return i + 1, e_next, e, jnp.int32(0), done

        i, e, prev, _, _ = lax.while_loop(lambda s: jnp.logical_not(s[4]), emit,
                                          (i, e, prev, jnp.int32(1), jnp.bool_(False)))
        return i, e, prev

    n_items, _, _ = lax.fori_loop(0, n_pairs // TM, tile_items,
                                  (jnp.int32(0), jnp.int32(0), jnp.int32(-1)))
    ni_ref[0] = n_items

    def pad(i, c):
        tile_ref[i] = tile_ref[n_items - 1]
        exp_ref[i] = exp_ref[n_items - 1]
        lo_ref[i] = 0
        hi_ref[i] = 0
        flag_ref[i] = 0
        nxt_ref[i] = -1
        return c

    lax.fori_loop(n_items, max_items, pad, 0)

    def parity(i, par):
        par = jnp.where((flag_ref[i] & 2) != 0, 1 - par, par)
        flag_ref[i] = flag_ref[i] + 4 * par
        return par

    lax.fori_loop(0, n_items, parity, jnp.int32(1))

    def lookahead(k, following):
        i = n_items - 1 - k
        nxt_ref[i] = following
        return jnp.where((flag_ref[i] & 2) != 0, exp_ref[i], following)

    lax.fori_loop(0, n_items, lookahead, jnp.int32(-1))


def _plan(counts, eid, rank):
    n_pairs = eid.shape[0]
    max_items = n_pairs // TM + N_EXPERTS - 1
    g_start = jnp.cumsum(counts) - counts
    experts = jnp.arange(N_EXPERTS, dtype=jnp.int32)
    slot = jnp.sum(jnp.where(eid[:, None] == experts[None, :], g_start[None, :], 0), axis=1) + rank
    smem = pl.BlockSpec(memory_space=pltpu.SMEM)
    i32 = lambda n: jax.ShapeDtypeStruct((n,), jnp.int32)
    src, it_tile, it_exp, it_lo, it_hi, flags, nxt, n_items = pl.pallas_call(
        functools.partial(_plan_kernel, n_pairs=n_pairs, max_items=max_items),
        in_specs=[smem] * 2,
        out_specs=[smem] * 8,
        out_shape=[i32(n_pairs)] + [i32(max_items)] * 6 + [i32(1)],
        scratch_shapes=[pltpu.SMEM((N_EXPERTS,), jnp.int32)],
        name="moe_plan",
    )(counts, slot)
    return (it_tile, it_exp, it_lo, it_hi, flags, nxt, n_items, src), slot


def _moe_kernel(tile_ref, exp_ref, lo_ref, hi_ref, flag_ref, nxt_ref, ni_ref, src_ref,
                hf_hbm, wg_hbm, wu_hbm, wd_hbm, y_ref, xbuf, wg_buf, wu_buf, wd_buf, wgb, wub, wdb,
                sem, wsem, *, n_tiles, e0):
    i = pl.program_id(0)
    t = tile_ref[i]
    slot = lax.rem(t, 2)

    def gather_start(tile, buf, unrolled):
        base = tile * TM

        def issue(r):
            tok = src_ref[base + r]
            pltpu.make_async_copy(hf_hbm.at[pl.ds(tok, 1)], xbuf.at[buf, pl.ds(r, 1)], sem.at[buf]).start()

        if unrolled:
            for r in range(TM):
                issue(r)
        else:
            lax.fori_loop(0, TM, lambda r, c: (issue(r), c)[1], 0)

    def gather_wait(buf):
        pltpu.make_async_copy(hf_hbm.at[pl.ds(0, TM)], xbuf.at[buf], sem.at[buf]).wait()

    def weight_copies(expert, b):
        e = e0 + expert
        return (pltpu.make_async_copy(wg_hbm.at[e], wg_buf.at[b], wsem.at[b]),
                pltpu.make_async_copy(wu_hbm.at[e], wu_buf.at[b], wsem.at[b]),
                pltpu.make_async_copy(wd_hbm.at[e], wd_buf.at[b], wsem.at[b]))

    def weights_start(expert, b):
        for cp in weight_copies(expert, b):
            cp.start()

    def weights_wait(b):
        for cp in weight_copies(0, b):
            cp.wait()

    @pl.when(i < ni_ref[0])
    def _():
        first_visit = (flag_ref[i] & 1) != 0
        new_expert = (flag_ref[i] & 2) != 0
        wslot = lax.shift_right_logical(flag_ref[i], 2) & 1

        @pl.when(i == 0)
        def _():
            gather_start(0, 0, False)

        @pl.when(first_visit)
        def _():
            gather_wait(slot)

        for b in range(2):
            @pl.when(jnp.logical_and(jnp.logical_and(first_visit, t + 1 < n_tiles), slot == 1 - b))
            def _():
                gather_start(t + 1, b, True)

        @pl.when(i == 0)
        def _():
            weights_start(exp_ref[0], 0)

        @pl.when(new_expert)
        def _():
            @pl.when(nxt_ref[i] >= 0)
            def _():
                weights_start(nxt_ref[i], 1 - wslot)

            weights_wait(wslot)
            wgb[...] = wg_buf[wslot].astype(BF16)
            wub[...] = wu_buf[wslot].astype(BF16)
            wdb[...] = wd_buf[wslot].astype(BF16)

        x = xbuf[slot].astype(BF16)
        a = _dot(x, wgb[...])
        u = _dot(x, wub[...])
        row = lax.broadcasted_iota(jnp.int32, (TM, 1), 0)
        mine = jnp.logical_and(row >= lo_ref[i], row < hi_ref[i])
        h = jnp.where(mine, (a * jax.nn.sigmoid(a)) * u, 0.0)
        yv = _dot(h.astype(BF16), wdb[...])

        @pl.when(first_visit)
        def _():
            y_ref[...] = yv

        @pl.when(jnp.logical_not(first_visit))
        def _():
            y_ref[...] += yv


def _moe(hf, meta, w_gate, w_up, w_down, layer):
    n_rows = meta[-1].shape[0]
    n_tiles = n_rows // TM
    max_items = meta[0].shape[0]
    grid_spec = pltpu.PrefetchScalarGridSpec(
        num_scalar_prefetch=len(meta),
        grid=(max_items,),
        in_specs=[pl.BlockSpec(memory_space=pl.ANY)] * 4,
        out_specs=pl.BlockSpec((TM, D), lambda i, tile, *_: (tile[i], 0)),
        scratch_shapes=[
            pltpu.VMEM((2, TM, D), F32),
            pltpu.VMEM((2, D, D_EXPERT), F32),
            pltpu.VMEM((2, D, D_EXPERT), F32),
            pltpu.VMEM((2, D_EXPERT, D), F32),
            pltpu.VMEM((D, D_EXPERT), BF16),
            pltpu.VMEM((D, D_EXPERT), BF16),
            pltpu.VMEM((D_EXPERT, D), BF16),
            pltpu.SemaphoreType.DMA((2,)),
            pltpu.SemaphoreType.DMA((2,)),
        ],
    )
    return pl.pallas_call(
        functools.partial(_moe_kernel, n_tiles=n_tiles, e0=layer * N_EXPERTS),
        grid_spec=grid_spec,
        out_shape=jax.ShapeDtypeStruct((n_rows, D), F32),
        compiler_params=_cparams(("arbitrary",)),
        name="moe_experts",
    )(*meta, hf, w_gate.reshape(-1, D, D_EXPERT), w_up.reshape(-1, D, D_EXPERT),
      w_down.reshape(-1, D_EXPERT, D))


def _final_kernel(pos_ref, x_ref, gf_ref, w_ref, g_ref, y_hbm, o_ref, ybuf, sem, *, n_steps):
    step = pl.program_id(0) * 8 + pl.program_id(1)
    out = _moe_residual(pos_ref, y_hbm, ybuf, sem, x_ref[...], gf_ref[0], w_ref[...], step, n_steps)
    o_ref[0] = _rms(out, g_ref[...])


def _final(pos, wts, xn, mods, y, g_final, B):
    tile = lambda b, j, p: (b * 8 + j, 0)
    grid_spec = pltpu.PrefetchScalarGridSpec(
        num_scalar_prefetch=1,
        grid=(B, 8),
        in_specs=[
            pl.BlockSpec((TM, D), tile),
            pl.BlockSpec((1, 1, D), lambda b, j, p: (b * 6 + 5, 0, 0)),
            pl.BlockSpec((TM, 2), tile),
            pl.BlockSpec((1, D), lambda b, j, p: (0, 0)),
            pl.BlockSpec(memory_space=pl.ANY),
        ],
        out_specs=pl.BlockSpec((1, TM, D), lambda b, j, p: (b, j, 0)),
        scratch_shapes=_MOE_GATHER_SCRATCH,
    )
    return pl.pallas_call(
        functools.partial(_final_kernel, n_steps=B * 8),
        grid_spec=grid_spec,
        out_shape=jax.ShapeDtypeStruct((B, 8 * TM, D), F32),
        compiler_params=_cparams(("arbitrary", "arbitrary")),
        name="final",
    )(pos, xn, mods, wts, g_final.reshape(1, D), y)


def _deinterleave(w):
    pairs = w.reshape(w.shape[:-1] + (w.shape[-1] // 2, 2))
    even, odd = pairs[..., 0], pairs[..., 1]
    return jnp.concatenate([even, odd, even, odd], axis=-1)


def _prep_w_in(w_in):
    c = Q_LORA + KV_LORA
    pairs = w_in[:, :, c:c + QK_ROPE].reshape(w_in.shape[:2] + (QK_ROPE // 2, 2))
    rot = jnp.concatenate([pairs[..., 0], pairs[..., 1]], axis=-1)
    return lax.dynamic_update_slice(w_in, rot, (0, 0, c)).astype(BF16)


def _prep_w_uq(w_uq):
    w = w_uq.reshape(Q_LORA, MLA_HEADS, QK_NOPE + QK_ROPE)
    w = jnp.concatenate([w[..., :QK_NOPE], _deinterleave(w[..., QK_NOPE:])], axis=-1)
    return w.reshape(Q_LORA, MLA_HEADS * 256).astype(BF16)


def _rope_tables(seq):
    half = QK_ROPE // 2
    inv_freq = ROPE_THETA ** (-jnp.arange(0, half, 2, dtype=F32) / half)
    t = jnp.arange(seq, dtype=jnp.int32)
    row = (t // GRID_W).astype(F32)
    col = (t % GRID_W).astype(F32)
    ang = jnp.concatenate([row[:, None] * inv_freq, col[:, None] * inv_freq], axis=-1)
    cos, sin = jnp.cos(ang), jnp.sin(ang)
    zeros = jnp.zeros((seq, 64), F32)
    cos_l = jnp.concatenate([cos, cos, zeros], axis=1)
    sin_l = jnp.concatenate([-sin, sin, zeros], axis=1)
    cos_c = jnp.concatenate([jnp.ones((CTX_LEN, 64), F32), jnp.zeros((CTX_LEN, 64), F32)], axis=1)
    sin_c = jnp.zeros((CTX_LEN, 128), F32)
    return jnp.concatenate([cos_c, cos_l], axis=0), jnp.concatenate([sin_c, sin_l], axis=0)


def kernel(x, c, ctx, c_ctx, w_ada, b_ada, g_attn, g_ffn, w_in, g_q, w_uq, g_kv, w_ukv, w_fnet, b_fnet,
           na_rpb, g_out, w_out, w_rg, b_rg, w_re, b_re, w_gate, w_up, w_down, g_final):
    B, S, _ = x.shape
    L = w_ada.shape[0]
    assert ctx.shape[1] == CTX_LEN == TM and S == 8 * TM and B <= 4
    T = B * 9 * TM

    cond8 = jnp.concatenate([c, jnp.zeros((4 - B, D), F32), c_ctx[None], jnp.zeros((3, D), F32)], axis=0)
    mods_all = _modulation(cond8, w_ada, b_ada)
    w_in_ext = _prep_w_in(w_in)
    pad = ROUTER_COLS - N_GROUPS - N_EXPERTS
    w_router = jnp.concatenate([w_rg, w_re, jnp.zeros((L, D, pad), F32)], axis=2)
    w_scaled = w_router * 65537.0
    w_router_hi = w_scaled - (w_scaled - w_router)
    w_router_all = jnp.concatenate([w_router_hi, w_router - w_router_hi], axis=2).astype(BF16)
    b_router_all = jnp.concatenate([b_rg, b_re, jnp.zeros((L, pad), F32)], axis=1)
    na_bias = _na_bias(na_rpb)
    cos_t, sin_t = _rope_tables(S)
    cs_lat = _dft_tables(S)
    cs_ctx = _dft_tables(CTX_LEN)
    cd_c, cd_s = _dft_cos_sin(FNET_GROUP_DIM)
    cd = jnp.concatenate([cd_c, -cd_s], axis=1).astype(BF16)

    xs = (ctx.reshape(B * CTX_LEN, D), x.reshape(B * S, D))
    pending = None
    for l in range(L):
        last = l == L - 1
        j0 = 1 if last else 0
        mods = mods_all[l].reshape(48, 1, D)
        w_uq_ext = _prep_w_uq(w_uq[l])
        w_router, b_router = w_router_all[l], b_router_all[l:l + 1]
        mla = (g_q[l], g_kv[l], w_uq_ext, w_ukv[l].astype(BF16), cos_t, sin_t)
        if pending is None:
            zf, naq, nak, navt, q, k, vt = _in_projection(xs, mods, g_attn[l], w_in_ext, l, mla, B)
        else:
            zf, naq, nak, navt, q, k, vt, xs = _in_projection(xs, mods, g_attn[l], w_in_ext, l, mla, B,
                                                              moe=pending)
        o_mla = _mla_attention(q, k, vt, B, j0)
        o_na = _na_attention(naq, nak, navt, na_bias[l], B, j0)
        w_f = w_fnet[l].astype(BF16)
        o_f = _fnet(zf, cs_lat, cs_ctx, cd, w_f, b_fnet[l], B, j0)
        xn, hf, route_w, route_i, counts = _merge(o_mla, o_f, o_na, xs, mods, g_out[l], g_ffn[l],
                                                  w_out[l].astype(BF16), w_router, b_router, B, j0)
        wts = route_w[:, 0:2]
        meta, slot = _plan(counts[0, N_GROUPS:N_GROUPS + N_EXPERTS], route_i[:, 0:2].reshape(-1),
                           route_i[:, 2:4].reshape(-1))
        y = _moe(hf, meta, w_gate, w_up, w_down, l)
        if last:
            return _final(slot, wts, xn, mods, y, g_final, B)
        xs, pending = xn, (slot, wts, y, mods)
```

```python
import functools

import numpy as np
import jax
import jax.numpy as jnp
from jax import lax
from jax.experimental import pallas as pl
from jax.experimental.pallas import tpu as pltpu

F32 = jnp.float32
BF16 = jnp.bfloat16

D = 2048
GRID_W = 64
CTX_LEN = 256
EPS = 1e-6
NEG_INF = -1e30
ROPE_THETA = 10000.0

V_DIM = 128
MLA_WIDTH = D // 2
MLA_HEADS = MLA_WIDTH // V_DIM
QK_NOPE = 128
QK_ROPE = 64
Q_LORA = D // 4
KV_LORA = D // 8
FNET_WIDTH = D // 4
FNET_GROUP_DIM = 128
FNET_GROUPS = FNET_WIDTH // FNET_GROUP_DIM
NA_WIDTH = D // 4
NA_HEAD_DIM = 128
NA_HEADS = NA_WIDTH // NA_HEAD_DIM
NA_KH_MAX = 8
NA_KW = 16
N_GROUPS = 4
EXPERTS_PER_GROUP = 8
N_EXPERTS = N_GROUPS * EXPERTS_PER_GROUP
D_EXPERT = D // 4

MLA_QSCALE = (QK_NOPE + QK_ROPE) ** -0.5 * float(np.log2(np.e))
MLA_CHUNK_TILES = 8
NA_QSCALE = NA_HEAD_DIM ** -0.5 * float(np.log2(np.e))
NA_WIN_ROWS = 12
TM = 256
IN_COLS = Q_LORA + KV_LORA + QK_ROPE + FNET_WIDTH + 3 * NA_WIDTH
ROUTER_COLS = 128
VMEM_LIMIT = 56 * 1024 * 1024


def _cparams(sem):
    return pltpu.CompilerParams(dimension_semantics=sem, vmem_limit_bytes=VMEM_LIMIT)


def _rms(v, g):
    return v * lax.rsqrt(jnp.mean(v * v, axis=-1, keepdims=True) + EPS) * g


def _dot(a, b):
    return jnp.dot(a, b, preferred_element_type=F32)


def _dot_nt(a, b):
    return lax.dot_general(a, b, (((1,), (1,)), ((), ())), preferred_element_type=F32)


def _mod_kernel(c_ref, w_ref, b_ref, o_ref):
    c = c_ref[...]
    s = c * jax.nn.sigmoid(c)
    o_ref[0] = _dot(s.astype(BF16), w_ref[0].astype(BF16)) + b_ref[0]


def _modulation(cond8, w_ada, b_ada):
    L = w_ada.shape[0]
    tn = 1024
    return pl.pallas_call(
        _mod_kernel,
        grid=(L, 6 * D // tn),
        in_specs=[
            pl.BlockSpec((8, D), lambda l, n: (0, 0)),
            pl.BlockSpec((1, D, tn), lambda l, n: (l, 0, n)),
            pl.BlockSpec((1, 1, tn), lambda l, n: (l, 0, n)),
        ],
        out_specs=pl.BlockSpec((1, 8, tn), lambda l, n: (l, 0, n)),
        out_shape=jax.ShapeDtypeStruct((L, 8, 6 * D), F32),
        compiler_params=_cparams(("arbitrary", "arbitrary")),
        name="modulation",
    )(cond8, w_ada, b_ada.reshape(L, 1, 6 * D))


def _tile_of(b, j, j0):
    return b * 9 + j0 + j


def _otile(b, j, j0):
    return b * (9 - j0) + j


def _mod_row(j, b, j0):
    return jnp.where(j0 + j == 0, 4, b)


def _mod_spec(k, j0):
    return pl.BlockSpec((1, 1, D), lambda b, j: (_mod_row(j, b, j0) * 6 + k, 0, 0))


def _stream_specs(xs, j0):
    if isinstance(xs, tuple):
        return [pl.BlockSpec((TM, D), lambda b, j, *_: (b, 0)),
                pl.BlockSpec((TM, D), lambda b, j, *_: (b * 8 + jnp.maximum(j0 + j - 1, 0), 0))], list(xs)
    return [pl.BlockSpec((TM, D), lambda b, j, *_: (_tile_of(b, j, j0), 0))], [xs]


def _stream_tile(refs, j0):
    if len(refs) == 1:
        return refs[0][...]
    return jnp.where(pl.program_id(1) + j0 == 0, refs[0][...], refs[1][...])


def _moe_residual(pos_ref, y_hbm, ybuf, sem, x, gate, w, step, n_steps):
    buf = lax.rem(step, 2)

    def start(tile, b, unrolled):
        base = tile * TM

        def issue(r):
            for k in range(2):
                pltpu.make_async_copy(y_hbm.at[pl.ds(pos_ref[2 * (base + r) + k], 1)],
                                      ybuf.at[b, k, pl.ds(r, 1)], sem.at[b]).start()

        if unrolled:
            for r in range(TM):
                issue(r)
        else:
            lax.fori_loop(0, TM, lambda r, c: (issue(r), c)[1], 0)

    def wait(b):
        for k in range(2):
            pltpu.make_async_copy(y_hbm.at[pl.ds(0, TM)], ybuf.at[b, k], sem.at[b]).wait()

    @pl.when(step == 0)
    def _():
        start(0, 0, False)

    for b in range(2):
        @pl.when(buf == 1 - b)
        def _():
            start(jnp.minimum(step + 1, n_steps - 1), b, True)

    wait(buf)
    out = x + gate * (w[:, 0:1] * ybuf[buf, 0] + w[:, 1:2] * ybuf[buf, 1])

    @pl.when(step == n_steps - 1)
    def _():
        wait(1 - buf)

    return out


_MOE_GATHER_SCRATCH = [pltpu.VMEM((2, 2, TM, D), F32), pltpu.SemaphoreType.DMA((2,))]


def _rope(r, cos_t, sin_t):
    return r * cos_t + pltpu.roll(r, 32, 1) * sin_t


def _inproj_body(x, sh_ref, sc_ref, g_ref, w_ref, gq_ref, gkv_ref, wq_ref, wkv_ref, cos_ref, sin_ref,
                 zf_ref, naq_ref, nak_ref, navt_ref, q_ref, k_ref, vt_ref):
    h = _rms(x, g_ref[...])
    h = h * (1.0 + sc_ref[0]) + sh_ref[0]
    z = _dot_nt(h.astype(BF16), w_ref[0])
    c_zf =Q_LORA + KV_LORA + QK_ROPE
    zf_ref[...] = z[:, c_zf:c_zf + FNET_WIDTH].astype(BF16)
    c_na = c_zf + FNET_WIDTH
    naq_ref[...] = (z[:, c_na:c_na + NA_WIDTH] * NA_QSCALE).astype(BF16)
    nak_ref[...] = z[:, c_na + NA_WIDTH:c_na + 2 * NA_WIDTH].astype(BF16)
    for hd in range(NA_HEADS):
        c = c_na + 2 * NA_WIDTH + hd * NA_HEAD_DIM
        navt_ref[hd * NA_HEAD_DIM:(hd + 1) * NA_HEAD_DIM, :] = z[:, c:c + NA_HEAD_DIM].T.astype(BF16)
    cos_t = cos_ref[...]
    sin_t = sin_ref[...]
    q = _dot(_rms(z[:, 0:Q_LORA], gq_ref[...]).astype(BF16), wq_ref[...]) * MLA_QSCALE
    kv = _dot(_rms(z[:, Q_LORA:Q_LORA + KV_LORA], gkv_ref[...]).astype(BF16), wkv_ref[...])
    kr = z[:, Q_LORA + KV_LORA:Q_LORA + KV_LORA + 128]
    kr = jnp.where(lax.broadcasted_iota(jnp.int32, kr.shape, 1) < QK_ROPE, kr, pltpu.roll(kr, QK_ROPE, 1))
    k_rope = _rope(kr, cos_t, sin_t).astype(BF16)
    for hd in range(MLA_HEADS):
        c = hd * 256
        q_ref[:, c:c + 128] = q[:, c:c + 128].astype(BF16)
        q_ref[:, c + 128:c + 256] = _rope(q[:, c + 128:c + 256], cos_t, sin_t).astype(BF16)
        k_ref[:, c:c + 128] = kv[:, c:c + 128].astype(BF16)
        k_ref[:, c + 128:c + 256] = k_rope
        vt_ref[hd * 128:(hd + 1) * 128, :] = kv[:, c + 128:c + 256].T.astype(BF16)


def _inproj_kernel(*refs, n_stream):
    _inproj_body(_stream_tile(refs[:n_stream], 0), *refs[n_stream:])


def _inproj_moe_kernel(pos_ref, x_ref, gf_ref, wts_ref, y_hbm, *refs, n_steps):
    *refs, xs_ref, ybuf, sem = refs
    step = pl.program_id(0) * 9 + pl.program_id(1)
    x = _moe_residual(pos_ref, y_hbm, ybuf, sem, x_ref[...], gf_ref[0], wts_ref[...], step, n_steps)
    xs_ref[...] = x
    _inproj_body(x, *refs)


def _in_projection(xs, mods, g_attn, w_in_ext, layer, mla, B, moe=None):
    T = B * 9 * TM
    row = lambda b, j, *_: (_tile_of(b, j, 0), 0)
    col = lambda b, j, *_: (0, _tile_of(b, j, 0))
    const = lambda b, j, *_: (0, 0)
    mod = lambda k: pl.BlockSpec((1, 1, D), lambda b, j, *_: (_mod_row(j, b, 0) * 6 + k, 0, 0))
    once = dict(pipeline_mode=pl.Buffered(1))
    g_q, g_kv, w_uq_ext, w_ukv, cos_t, sin_t = mla
    in_specs = [
        pl.BlockSpec((TM, D), row),
        mod(0),
        mod(1),
        pl.BlockSpec((1, D), const),
        pl.BlockSpec((1, IN_COLS, D), lambda b, j, *_: (layer, 0, 0), **once),
        pl.BlockSpec((1, Q_LORA), const),
        pl.BlockSpec((1, KV_LORA), const),
        pl.BlockSpec((Q_LORA, MLA_HEADS * 256), const, **once),
        pl.BlockSpec((KV_LORA, MLA_HEADS * 256), const, **once),
        pl.BlockSpec((TM, 128), lambda b, j, *_: (j, 0)),
        pl.BlockSpec((TM, 128), lambda b, j, *_: (j, 0)),
    ]
    out_specs = [
        pl.BlockSpec((TM, FNET_WIDTH), row),
        pl.BlockSpec((TM, NA_WIDTH), row),
        pl.BlockSpec((TM, NA_WIDTH), row),
        pl.BlockSpec((NA_WIDTH, TM), col),
        pl.BlockSpec((TM, MLA_HEADS * 256), row),
        pl.BlockSpec((TM, MLA_HEADS * 256), row),
        pl.BlockSpec((MLA_WIDTH, TM), col),
    ]
    out_shape = [
        jax.ShapeDtypeStruct((T, FNET_WIDTH), BF16),
        jax.ShapeDtypeStruct((T, NA_WIDTH), BF16),
        jax.ShapeDtypeStruct((T, NA_WIDTH), BF16),
        jax.ShapeDtypeStruct((NA_WIDTH, T), BF16),
        jax.ShapeDtypeStruct((T, MLA_HEADS * 256), BF16),
        jax.ShapeDtypeStruct((T, MLA_HEADS * 256), BF16),
        jax.ShapeDtypeStruct((MLA_WIDTH, T), BF16),
    ]
    args = [xs, mods, mods, g_attn.reshape(1, D), w_in_ext, g_q.reshape(1, -1), g_kv.reshape(1, -1),
            w_uq_ext, w_ukv, cos_t, sin_t]
    if moe is None:
        stream_specs, stream_args = _stream_specs(xs, 0)
        return pl.pallas_call(
            functools.partial(_inproj_kernel, n_stream=len(stream_args)),
            grid=(B, 9),
            in_specs=stream_specs + in_specs[1:],
            out_specs=out_specs,
            out_shape=out_shape,
            compiler_params=_cparams(("arbitrary", "arbitrary")),
            name="in_projection",
        )(*stream_args, *args[1:])
    pos, wts, y, mods_prev = moe
    in_specs = [in_specs[0], mod(5), pl.BlockSpec((TM, 2), row), pl.BlockSpec(memory_space=pl.ANY)] \
        + in_specs[1:]
    grid_spec = pltpu.PrefetchScalarGridSpec(
        num_scalar_prefetch=1,
        grid=(B, 9),
        in_specs=in_specs,
        out_specs=out_specs + [pl.BlockSpec((TM, D), row)],
        scratch_shapes=_MOE_GATHER_SCRATCH,
    )
    return pl.pallas_call(
        functools.partial(_inproj_moe_kernel, n_steps=B * 9),
        grid_spec=grid_spec,
        out_shape=out_shape + [jax.ShapeDtypeStruct((T, D), F32)],
        compiler_params=_cparams(("arbitrary", "arbitrary")),
        name="in_projection_moe",
    )(pos, xs, mods_prev, wts, y, *args[1:])


def _mla_attn_kernel(q_ref, k_ref, vt_ref, o_ref, *, j0):
    def attend(q0, nq, nk, o0):
        st = _dot_nt(k_ref[0:nk, :], q_ref[q0:q0 + nq, :])
        m = jnp.max(st, axis=0, keepdims=True)
        p = jnp.exp2(st - m)
        l = jnp.sum(p, axis=0, keepdims=True)
        ot = _dot(vt_ref[:, 0:nk], p.astype(BF16)) / l
        o_ref[o0:o0 + nq, :] = ot.T

    if j0 == 0:
        attend(0, CTX_LEN, CTX_LEN, 0)
    for c in range(8 // MLA_CHUNK_TILES):
        attend(CTX_LEN + MLA_CHUNK_TILES * c * TM, MLA_CHUNK_TILES * TM, 9 * TM,
               (1 - j0 + MLA_CHUNK_TILES * c) * TM)


def _mla_attention(q, k, vt, B, j0):
    rows = (9 - j0) * TM
    return pl.pallas_call(
        functools.partial(_mla_attn_kernel, j0=j0),
        grid=(B, MLA_HEADS),
        in_specs=[
            pl.BlockSpec((9 * TM, 256), lambda b, h: (b, h)),
            pl.BlockSpec((9 * TM, 256), lambda b, h: (b, h)),
            pl.BlockSpec((V_DIM, 9 * TM), lambda b, h: (h, b)),
        ],
        out_specs=pl.BlockSpec((rows, V_DIM), lambda b, h: (b, h)),
        out_shape=jax.ShapeDtypeStruct((B * rows, MLA_WIDTH), F32),
        compiler_params=_cparams(("arbitrary", "arbitrary")),
        name="mla_attention",
    )(q, k, vt)


def _na_chunk(g):
    start_row = min(max(4 * g - 4, 0), 8 * TM // GRID_W - NA_WIN_ROWS)
    pattern = 0 if g == 0 else (2 if g == 7 else 1)
    return start_row, pattern


def _na_kernel(q_ref, k_ref, vt_ref, bias_ref, o_ref, *, j0):
    def finish(parts, o0):
        m = None
        for st, _ in parts:
            pm = jnp.max(st, axis=0, keepdims=True)
            m = pm if m is None else jnp.maximum(m, pm)
        l = None
        ot = None
        for st, vt in parts:
            p = jnp.exp2(st - m)
            pl_sum = jnp.sum(p, axis=0, keepdims=True)
            po = _dot(vt, p.astype(BF16))
            l = pl_sum if l is None else l + pl_sum
            ot = po if ot is None else ot + po
        o_ref[o0:o0 + TM, :] = (ot / l).T

    if j0 == 0:
        st = _dot_nt(k_ref[0:CTX_LEN, :], q_ref[0:CTX_LEN, :])
        finish([(st, vt_ref[:, 0:CTX_LEN])], 0)
    for g in range(8):
        start_row, pattern = _na_chunk(g)
        k0 = CTX_LEN + start_row * GRID_W
        nk = NA_WIN_ROWS * GRID_W
        q = q_ref[CTX_LEN + g * TM:CTX_LEN + (g + 1) * TM, :]
        st_loc = _dot_nt(k_ref[k0:k0 + nk, :], q) + bias_ref[0, pattern]
        st_ctx = _dot_nt(k_ref[0:CTX_LEN, :], q)
        finish([(st_loc, vt_ref[:, k0:k0 + nk]), (st_ctx, vt_ref[:, 0:CTX_LEN])], (1 - j0 + g) * TM)


def _na_attention(naq, nak, navt, bias, layer, B, j0):
    rows = (9 - j0) * TM
    bias = bias.reshape((-1,) + bias.shape[2:])
    return pl.pallas_call(
        functools.partial(_na_kernel, j0=j0),
        grid=(NA_HEADS, B),
        in_specs=[
            pl.BlockSpec((9 * TM, NA_HEAD_DIM), lambda h, b: (b, h)),
            pl.BlockSpec((9 * TM, NA_HEAD_DIM), lambda h, b: (b, h)),
            pl.BlockSpec((NA_HEAD_DIM, 9 * TM), lambda h, b: (h, b)),
            pl.BlockSpec((1, 3, NA_WIN_ROWS * GRID_W, TM), lambda h, b: (layer * NA_HEADS + h, 0, 0, 0)),
        ],
        out_specs=pl.BlockSpec((rows, NA_HEAD_DIM), lambda h, b: (b, h)),
        out_shape=jax.ShapeDtypeStruct((B * rows, NA_WIDTH), F32),
        compiler_params=_cparams(("arbitrary", "arbitrary")),
        name="na_attention",
    )(naq, nak, navt, bias)


def _na_bias(rpb):
    kh, rows, nq = NA_KH_MAX, 8 * TM // GRID_W, TM // GRID_W
    cq = np.arange(GRID_W)
    ck = np.arange(GRID_W)
    col_start = np.clip(cq - NA_KW // 2, 0, GRID_W - NA_KW)
    col_ok = (ck[:, None] >= col_start[None, :]) & (ck[:, None] < col_start[None, :] + NA_KW)
    dcol = np.clip(ck[:, None] - cq[None, :] + (NA_KW - 1), 0, 2 * NA_KW - 2)
    select = np.zeros((2 * NA_KW - 1, GRID_W * GRID_W), np.float32)
    select[dcol.reshape(-1), np.arange(GRID_W * GRID_W)] = 1.0
    blocks = jnp.einsum("lhdm,mn->lhdn", rpb.astype(F32) * float(np.log2(np.e)), jnp.asarray(select),
                        precision=lax.Precision.HIGHEST)
    blocks = blocks.reshape(rpb.shape[:3] + (GRID_W, GRID_W))
    blocks = jnp.where(jnp.asarray(col_ok), blocks, NEG_INF)
    masked = jnp.full(rpb.shape[:2] + (GRID_W, GRID_W), NEG_INF, F32)
    patterns = []
    for g in (0, 1, 7):
        start_row, _ = _na_chunk(g)
        key_rows = []
        for kr in range(NA_WIN_ROWS):
            key_row = start_row + kr
            row = []
            for qr in range(nq):
                r = 4 * g + qr
                r_start = min(max(r - kh // 2, 0), rows - kh)
                in_rows = r_start <= key_row < r_start + kh
                row.append(blocks[:, :, key_row - r + (kh - 1)] if in_rows else masked)
            key_rows.append(jnp.concatenate(row, axis=-1))
        patterns.append(jnp.concatenate(key_rows, axis=-2))
    return jnp.stack(patterns, axis=2)


def _fnet_kernel(z_ref, csl_ref, csc_ref, cd_ref, w_ref, b_ref, o_ref, ab_ref, *, j0, seq):
    j = pl.program_id(1) + j0

    def small_side(row0, length):
        for g in range(FNET_GROUPS):
            c = g * FNET_GROUP_DIM
            ab = _dot(z_ref[row0:row0 + length, c:c + FNET_GROUP_DIM], cd_ref[...])
            ab_ref[0:length, c:c + FNET_GROUP_DIM] = ab[:, 0:FNET_GROUP_DIM].astype(BF16)
            ab_ref[length:2 * length, c:c + FNET_GROUP_DIM] = ab[:, FNET_GROUP_DIM:].astype(BF16)

    def long_side(cs, length):
        f = _dot(cs, ab_ref[0:2 * length, :]) * (length * FNET_GROUP_DIM) ** -0.5
        o_ref[...] = _dot(f.astype(BF16), w_ref[...]) + b_ref[...]

    if j0 == 0:
        @pl.when(j == 0)
        def _():
            small_side(0, CTX_LEN)
            long_side(csc_ref[...], CTX_LEN)

    @pl.when(j == 1)
    def _():
        small_side(CTX_LEN, seq)

    @pl.when(j >= 1)
    def _():
        long_side(csl_ref[...], seq)


def _fnet(zf, cs_lat, cs_ctx, cd, w_fnet, b_fnet, B, j0):
    seq = 8 * TM
    return pl.pallas_call(
        functools.partial(_fnet_kernel, j0=j0, seq=seq),
        grid=(B, 9 - j0),
        in_specs=[
            pl.BlockSpec((9 * TM, FNET_WIDTH), lambda b, j: (b, 0)),
            pl.BlockSpec((TM, 2 * seq), lambda b, j: (jnp.maximum(j0 + j - 1, 0), 0)),
            pl.BlockSpec((CTX_LEN, 2 * CTX_LEN), lambda b, j: (0, 0)),
            pl.BlockSpec((FNET_GROUP_DIM, 2 * FNET_GROUP_DIM), lambda b, j: (0, 0)),
            pl.BlockSpec((FNET_WIDTH, FNET_WIDTH), lambda b, j: (0, 0)),
            pl.BlockSpec((1, FNET_WIDTH), lambda b, j: (0, 0)),
        ],
        out_specs=pl.BlockSpec((TM, FNET_WIDTH), lambda b, j: (_otile(b, j, j0), 0)),
        out_shape=jax.ShapeDtypeStruct((B * (9 - j0) * TM, FNET_WIDTH), F32),
        scratch_shapes=[pltpu.VMEM((2 * seq, FNET_WIDTH), BF16)],
        compiler_params=_cparams(("arbitrary", "arbitrary")),
        name="fnet",
    )(zf, cs_lat, cs_ctx, cd, w_fnet, b_fnet.reshape(1, -1))


def _dft_cos_sin(n):
    j = jnp.arange(n, dtype=jnp.int32)[:, None]
    if n <= 64:
        ang = ((j * j.T) % n).astype(F32) * (2.0 * np.pi / n)
        return jnp.cos(ang), jnp.sin(ang)
    k1 = jnp.arange(n // 64, dtype=jnp.int32)[None, :]
    k0 = jnp.arange(64, dtype=jnp.int32)[None, :]
    a = ((j * k1 * 64) % n).astype(F32) * (2.0 * np.pi / n)
    b = ((j * k0) % n).astype(F32) * (2.0 * np.pi / n)
    ca, sa, cb, sb = jnp.cos(a), jnp.sin(a), jnp.cos(b), jnp.sin(b)
    c = ca[:, :, None] * cb[:, None, :] - sa[:, :, None] * sb[:, None, :]
    s = sa[:, :, None] * cb[:, None, :] + ca[:, :, None] * sb[:, None, :]
    return c.reshape(n, n), s.reshape(n, n)


def _dft_tables(n):
    c, s = _dft_cos_sin(n)
    return jnp.concatenate([c, s], axis=1).astype(BF16)


def _merge_kernel(*refs, n_stream, j0):
    x = _stream_tile(refs[:n_stream], j0)
    (om_ref, of_ref, on_ref, ga_ref, shf_ref, scf_ref, gout_ref, gffn_ref, wout_ref, wr_ref, br_ref,
     xn_ref, hf_ref, rw_ref, ri_ref, cnt_ref, run_ref) = refs[n_stream:]
    ym = _rms(om_ref[...], gout_ref[:, 0:MLA_WIDTH]).astype(BF16)
    yf = _rms(of_ref[...], gout_ref[:, MLA_WIDTH:MLA_WIDTH + FNET_WIDTH]).astype(BF16)
    yn = _rms(on_ref[...], gout_ref[:, MLA_WIDTH + FNET_WIDTH:]).astype(BF16)
    acc = _dot(ym, wout_ref[0, 0:MLA_WIDTH, :])
    acc = acc + _dot(yf, wout_ref[0, MLA_WIDTH:MLA_WIDTH + FNET_WIDTH, :])
    acc = acc + _dot(yn, wout_ref[0, MLA_WIDTH + FNET_WIDTH:, :])
    xn = x + ga_ref[0] * acc
    xn_ref[...] = xn
    hf = _rms(xn, gffn_ref[...]) * (1.0 + scf_ref[0]) + shf_ref[0]
    hf_ref[...] = hf
    hi = hf.astype(BF16)
    lo = (hf - hi.astype(F32)).astype(BF16)
    a = _dot(hi, wr_ref[0])
    b = _dot(lo, wr_ref[0])
    small = a[:, ROUTER_COLS:] + (b[:, :ROUTER_COLS] + b[:, ROUTER_COLS:])
    _route_tile(a[:, :ROUTER_COLS] + small + br_ref[0], rw_ref, ri_ref, cnt_ref, run_ref)


def _route_tile(lg, rw_ref, ri_ref, cnt_ref, run_ref):
    first = jnp.logical_and(pl.program_id(0) == 0, pl.program_id(1) == 0)

    @pl.when(first)
    def _():
        run_ref[...] = jnp.zeros_like(run_ref)

    lane = lax.broadcasted_iota(jnp.int32, lg.shape, 1)
    neg = jnp.float32(-jnp.inf)

    def top(v):
        vmax = jnp.max(v, axis=1, keepdims=True)
        idx = jnp.min(jnp.where(v == vmax, lane, ROUTER_COLS), axis=1, keepdims=True)
        return vmax, idx

    in_groups = lane < N_GROUPS
    gl = jnp.where(in_groups, lg, neg)
    g_max, g_sel = top(gl)
    g_w = 1.0 / jnp.sum(jnp.where(in_groups, jnp.exp(gl - g_max), 0.0), axis=1, keepdims=True)
    e_lo = N_GROUPS + g_sel * EXPERTS_PER_GROUP
    el = jnp.where(jnp.logical_and(lane >= e_lo, lane < e_lo + EXPERTS_PER_GROUP), lg, neg)
    e1_max, i1 = top(el)
    e2_max, i2 = top(jnp.where(lane == i1, neg, el))
    t = jnp.exp(e2_max - e1_max)
    w0 = g_w / (1.0 + t)
    w1 = w0 * t
    rw_ref[...] = jnp.where(lane == 0, w0, jnp.where(lane == 1, w1, 0.0))

    row = lax.broadcasted_iota(jnp.int32, (TM, TM), 0)
    col = lax.broadcasted_iota(jnp.int32, (TM, TM), 1)
    tri = jnp.where(row >= col, 1.0, 0.0).astype(BF16)
    hot0 = lane == i1
    hot1 = lane == i2
    c0 = _dot(tri, jnp.where(hot0, 1.0, 0.0).astype(BF16))
    c1 = _dot(tri, jnp.where(hot1, 1.0, 0.0).astype(BF16))
    run = run_ref[...]
    tot0 = c0[TM - 1:TM, :]
    rank0 = jnp.sum(jnp.where(hot0, run + c0 - 1.0, 0.0), axis=1, keepdims=True)
    rank1 = jnp.sum(jnp.where(hot1, run + tot0 + c1 - 1.0, 0.0), axis=1, keepdims=True)
    run = run + tot0 + c1[TM - 1:TM, :]
    run_ref[...] = run
    cnt_ref[...] = jnp.broadcast_to(run, cnt_ref.shape).astype(jnp.int32)
    ri_ref[...] = jnp.where(lane == 0, i1 - N_GROUPS, jnp.where(lane == 1, i2 - N_GROUPS, jnp.where(
        lane == 2, rank0.astype(jnp.int32), jnp.where(lane == 3, rank1.astype(jnp.int32), 0))))


def _merge(o_mla, o_f, o_na, xs, mods, g_out, g_ffn, w_out, w_router, b_router, layer, B, j0):
    T = B * (9 - j0) * TM
    row = lambda b, j: (_otile(b, j, j0), 0)
    const = lambda b, j: (0, 0)
    slab = lambda b, j: (layer, 0, 0)
    stream_specs, stream_args = _stream_specs(xs, j0)
    return pl.pallas_call(
        functools.partial(_merge_kernel, n_stream=len(stream_args), j0=j0),
        grid=(B, 9 - j0),
        in_specs=stream_specs + [
            pl.BlockSpec((TM, MLA_WIDTH), row),
            pl.BlockSpec((TM, FNET_WIDTH), row),
            pl.BlockSpec((TM, NA_WIDTH), row),
            _mod_spec(2, j0),
            _mod_spec(3, j0),
            _mod_spec(4, j0),
            pl.BlockSpec((1, D), const),
            pl.BlockSpec((1, D), const),
            pl.BlockSpec((1, D, D), slab),
            pl.BlockSpec((1, D, 2 * ROUTER_COLS), slab),
            pl.BlockSpec((1, 1, ROUTER_COLS), slab),
        ],
        out_specs=[
            pl.BlockSpec((TM, D), row),
            pl.BlockSpec((TM, D), row),
            pl.BlockSpec((TM, ROUTER_COLS), row),
            pl.BlockSpec((TM, ROUTER_COLS), row),
            pl.BlockSpec((8, ROUTER_COLS), const),
        ],
        out_shape=[
            jax.ShapeDtypeStruct((T, D), F32),
            jax.ShapeDtypeStruct((T, D), F32),
            jax.ShapeDtypeStruct((T, ROUTER_COLS), F32),
            jax.ShapeDtypeStruct((T, ROUTER_COLS), jnp.int32),
            jax.ShapeDtypeStruct((8, ROUTER_COLS), jnp.int32),
        ],
        scratch_shapes=[pltpu.VMEM((1, ROUTER_COLS), F32)],
        compiler_params=_cparams(("arbitrary", "arbitrary")),
        name="merge",
    )(*stream_args, o_mla, o_f, o_na, mods, mods, mods, g_out.reshape(1, D), g_ffn.reshape(1, D),
      w_out, w_router, b_router)


def _plan_kernel(cnt_ref, slot_ref, src_ref, tile_ref, exp_ref, lo_ref, hi_ref, flag_ref, nxt_ref, ni_ref,
                 gs_ref, *, n_pairs, max_items):
    def starts(e, acc):
        gs_ref[e] = acc
        return acc + cnt_ref[e]

    lax.fori_loop(0, N_EXPERTS, starts, 0)

    def place(p, c):
        src_ref[slot_ref[p]] = lax.shift_right_logical(p, 1)
        return c

    lax.fori_loop(0, n_pairs, place, 0, unroll=16)

    last = N_EXPERTS - 1

    def group_end(e):
        return gs_ref[e] + cnt_ref[e]

    def next_nonempty(e):
        return lax.while_loop(lambda x: jnp.logical_and(x < last, cnt_ref[jnp.minimum(x, last)] == 0),
                              lambda x: x + 1, e)

    def tile_items(t, carry):
        i, e, prev = carry
        row0 = t * TM
        e = lax.while_loop(lambda x: group_end(x) <= row0, lambda x: x + 1, e)

        def emit(state):
            i, e, prev, first, _ = state
            tile_ref[i] = t
            exp_ref[i] = e
            lo_ref[i] = jnp.clip(gs_ref[e] - row0, 0, TM)
            hi_ref[i] = jnp.clip(group_end(e) - row0, 0, TM)
            flag_ref[i] = first + 2 * (e != prev).astype(jnp.int32)
            done = group_end(e) >= row0 + TM
            e_next = jnp.where(done, e, next_nonempty(e + 1))
            return i + 1, e_next, e, jnp.int32(0), done

        i, e, prev, _, _ = lax.while_loop(lambda s: jnp.logical_not(s[4]), emit,
                                          (i, e, prev, jnp.int32(1), jnp.bool_(False)))
        return i, e, prev

    n_items, _, _ = lax.fori_loop(0, n_pairs // TM, tile_items,
                                  (jnp.int32(0), jnp.int32(0), jnp.int32(-1)))
    ni_ref[0] = n_items

    def pad(i, c):
        tile_ref[i] = tile_ref[n_items - 1]
        exp_ref[i] = exp_ref[n_items - 1]
        lo_ref[i] = 0
        hi_ref[i] = 0
        flag_ref[i] = 0
        nxt_ref[i] = -1
        return c

    lax.fori_loop(n_items, max_items, pad, 0)

    def parity(i, par):
        par = jnp.where((flag_ref[i] & 2) != 0, 1 - par, par)
        flag_ref[i] = flag_ref[i] + 4 * par
        return par

    lax.fori_loop(0, n_items, parity, jnp.int32(1))

    def lookahead(k, following):
        i = n_items - 1 - k
        nxt_ref[i] = following
        return jnp.where((flag_ref[i] & 2) != 0, exp_ref[i], following)

    lax.fori_loop(0, n_items, lookahead, jnp.int32(-1))


def _plan(counts, eid, rank):
    n_pairs = eid.shape[0]
    max_items = n_pairs // TM + N_EXPERTS - 1
    g_start = jnp.cumsum(counts) - counts
    experts = jnp.arange(N_EXPERTS, dtype=jnp.int32)
    slot = jnp.sum(jnp.where(eid[:, None] == experts[None, :], g_start[None, :], 0), axis=1) + rank
    smem = pl.BlockSpec(memory_space=pltpu.SMEM)
    i32 = lambda n: jax.ShapeDtypeStruct((n,), jnp.int32)
    src, it_tile, it_exp, it_lo, it_hi, flags, nxt, n_items = pl.pallas_call(
        functools.partial(_plan_kernel, n_pairs=n_pairs, max_items=max_items),
        in_specs=[smem] * 2,
        out_specs=[smem] * 8,
        out_shape=[i32(n_pairs)] + [i32(max_items)] * 6 + [i32(1)],
        scratch_shapes=[pltpu.SMEM((N_EXPERTS,), jnp.int32)],
        name="moe_plan",
    )(counts, slot)
    return (it_tile, it_exp, it_lo, it_hi, flags, nxt, n_items, src), slot


def _moe_kernel(tile_ref, exp_ref, lo_ref, hi_ref, flag_ref, nxt_ref, ni_ref, src_ref,
                hf_hbm, wg_hbm, wu_hbm, wd_hbm, y_ref, xbuf, wg_buf, wu_buf, wd_buf, wgb, wub, wdb,
                sem, wsem, *, n_tiles, e0):
    i = pl.program_id(0)
    t = tile_ref[i]
    slot = lax.rem(t, 2)

    def gather_start(tile, buf, unrolled):
        base = tile * TM

        def issue(r):
            tok = src_ref[base + r]
            pltpu.make_async_copy(hf_hbm.at[pl.ds(tok, 1)], xbuf.at[buf, pl.ds(r, 1)], sem.at[buf]).start()

        if unrolled:
            for r in range(TM):
                issue(r)
        else:
            lax.fori_loop(0, TM, lambda r, c: (issue(r), c)[1], 0)

    def gather_wait(buf):
        pltpu.make_async_copy(hf_hbm.at[pl.ds(0, TM)], xbuf.at[buf], sem.at[buf]).wait()

    def weight_copies(expert, b):
        e = e0 + expert
        return (pltpu.make_async_copy(wg_hbm.at[e], wg_buf.at[b], wsem.at[b]),
                pltpu.make_async_copy(wu_hbm.at[e], wu_buf.at[b], wsem.at[b]),
                pltpu.make_async_copy(wd_hbm.at[e], wd_buf.at[b], wsem.at[b]))

    def weights_start(expert, b):
        for cp in weight_copies(expert, b):
            cp.start()

    def weights_wait(b):
        for cp in weight_copies(0, b):
            cp.wait()

    @pl.when(i < ni_ref[0])
    def _():
        first_visit = (flag_ref[i] & 1) != 0
        new_expert = (flag_ref[i] & 2) != 0
        wslot = lax.shift_right_logical(flag_ref[i], 2) & 1

        @pl.when(i == 0)
        def _():
            gather_start(0, 0, False)

        @pl.when(first_visit)
        def _():
            gather_wait(slot)

        for b in range(2):
            @pl.when(jnp.logical_and(jnp.logical_and(first_visit, t + 1 < n_tiles), slot == 1 - b))
            def _():
                gather_start(t + 1, b, True)

        @pl.when(i == 0)
        def _():
            weights_start(exp_ref[0], 0)

        @pl.when(new_expert)
        def _():
            @pl.when(nxt_ref[i] >= 0)
            def _():
                weights_start(nxt_ref[i], 1 - wslot)

            weights_wait(wslot)
            wgb[...] = wg_buf[wslot].astype(BF16)
            wub[...] = wu_buf[wslot].astype(BF16)
            wdb[...] = wd_buf[wslot].astype(BF16)

        x = xbuf[slot].astype(BF16)
        a = _dot(x, wgb[...])
        u = _dot(x, wub[...])
        row = lax.broadcasted_iota(jnp.int32, (TM, 1), 0)
        mine = jnp.logical_and(row >= lo_ref[i], row < hi_ref[i])
        h = jnp.where(mine, (a * jax.nn.sigmoid(a)) * u, 0.0)
        yv = _dot(h.astype(BF16), wdb[...])

        @pl.when(first_visit)
        def _():
            y_ref[...] = yv

        @pl.when(jnp.logical_not(first_visit))
        def _():
            y_ref[...] += yv


def _moe(hf, meta, w_gate, w_up, w_down, layer):
    n_rows = meta[-1].shape[0]
    n_tiles = n_rows // TM
    max_items = meta[0].shape[0]
    grid_spec = pltpu.PrefetchScalarGridSpec(
        num_scalar_prefetch=len(meta),
        grid=(max_items,),
        in_specs=[pl.BlockSpec(memory_space=pl.ANY)] * 4,
        out_specs=pl.BlockSpec((TM, D), lambda i, tile, *_: (tile[i], 0)),
        scratch_shapes=[
            pltpu.VMEM((2, TM, D), F32),
            pltpu.VMEM((2, D, D_EXPERT), F32),
            pltpu.VMEM((2, D, D_EXPERT), F32),
            pltpu.VMEM((2, D_EXPERT, D), F32),
            pltpu.VMEM((D, D_EXPERT), BF16),
            pltpu.VMEM((D, D_EXPERT), BF16),
            pltpu.VMEM((D_EXPERT, D), BF16),
            pltpu.SemaphoreType.DMA((2,)),
            pltpu.SemaphoreType.DMA((2,)),
        ],
    )
    return pl.pallas_call(
        functools.partial(_moe_kernel, n_tiles=n_tiles, e0=layer * N_EXPERTS),
        grid_spec=grid_spec,
        out_shape=jax.ShapeDtypeStruct((n_rows, D), F32),
        compiler_params=_cparams(("arbitrary",)),
        name="moe_experts",
    )(*meta, hf, w_gate.reshape(-1, D, D_EXPERT), w_up.reshape(-1, D, D_EXPERT),
      w_down.reshape(-1, D_EXPERT, D))


def _final_kernel(pos_ref, x_ref, gf_ref, w_ref, g_ref, y_hbm, o_ref, ybuf, sem, *, n_steps):
    step = pl.program_id(0) * 8 + pl.program_id(1)
    out = _moe_residual(pos_ref, y_hbm, ybuf, sem, x_ref[...], gf_ref[0], w_ref[...], step, n_steps)
    o_ref[0] = _rms(out, g_ref[...])


def _final(pos, wts, xn, mods, y, g_final, B):
    tile = lambda b, j, p: (b * 8 + j, 0)
    grid_spec = pltpu.PrefetchScalarGridSpec(
        num_scalar_prefetch=1,
        grid=(B, 8),
        in_specs=[
            pl.BlockSpec((TM, D), tile),
            pl.BlockSpec((1, 1, D), lambda b, j, p: (b * 6 + 5, 0, 0)),
            pl.BlockSpec((TM, 2), tile),
            pl.BlockSpec((1, D), lambda b, j, p: (0, 0)),
            pl.BlockSpec(memory_space=pl.ANY),
        ],
        out_specs=pl.BlockSpec((1, TM, D), lambda b, j, p: (b, j, 0)),
        scratch_shapes=_MOE_GATHER_SCRATCH,
    )
    return pl.pallas_call(
        functools.partial(_final_kernel, n_steps=B * 8),
        grid_spec=grid_spec,
        out_shape=jax.ShapeDtypeStruct((B, 8 * TM, D), F32),
        compiler_params=_cparams(("arbitrary", "arbitrary")),
        name="final",
    )(pos, xn, mods, wts, g_final.reshape(1, D), y)


def _deinterleave(w):
    pairs = w.reshape(w.shape[:-1] + (w.shape[-1] // 2, 2))
    even, odd = pairs[..., 0], pairs[..., 1]
    return jnp.concatenate([even, odd, even, odd], axis=-1)


def _prep_w_in(w_in):
    w_t = jnp.swapaxes(w_in, 1, 2)
    c = Q_LORA + KV_LORA
    pairs = w_t[:, c:c + QK_ROPE].reshape(w_t.shape[0], QK_ROPE // 2, 2, w_t.shape[2])
    rot = jnp.concatenate([pairs[:, :, 0], pairs[:, :, 1]], axis=1)
    return lax.dynamic_update_slice(w_t, rot, (0, c, 0)).astype(BF16)


def _prep_w_uq(w_uq):
    w = w_uq.reshape(Q_LORA, MLA_HEADS, QK_NOPE + QK_ROPE)
    w = jnp.concatenate([w[..., :QK_NOPE], _deinterleave(w[..., QK_NOPE:])], axis=-1)
    return w.reshape(Q_LORA, MLA_HEADS * 256).astype(BF16)


def _rope_tables(seq):
    half = QK_ROPE // 2
    inv_freq = ROPE_THETA ** (-jnp.arange(0, half, 2, dtype=F32) / half)
    t = jnp.arange(seq, dtype=jnp.int32)
    row = (t // GRID_W).astype(F32)
    col = (t % GRID_W).astype(F32)
    ang = jnp.concatenate([row[:, None] * inv_freq, col[:, None] * inv_freq], axis=-1)
    cos, sin = jnp.cos(ang), jnp.sin(ang)
    zeros = jnp.zeros((seq, 64), F32)
    cos_l = jnp.concatenate([cos, cos, zeros], axis=1)
    sin_l = jnp.concatenate([-sin, sin, zeros], axis=1)
    cos_c = jnp.concatenate([jnp.ones((CTX_LEN, 64), F32), jnp.zeros((CTX_LEN, 64), F32)], axis=1)
    sin_c = jnp.zeros((CTX_LEN, 128), F32)
    return jnp.concatenate([cos_c, cos_l], axis=0), jnp.concatenate([sin_c, sin_l], axis=0)


def kernel(x, c, ctx, c_ctx, w_ada, b_ada, g_attn, g_ffn, w_in, g_q, w_uq, g_kv, w_ukv, w_fnet, b_fnet,
           na_rpb, g_out, w_out, w_rg, b_rg, w_re, b_re, w_gate, w_up, w_down, g_final):
    B, S, _ = x.shape
    L = w_ada.shape[0]
    assert ctx.shape[1] == CTX_LEN == TM and S == 8 * TM and B <= 4
    T = B * 9 * TM

    cond8 = jnp.concatenate([c, jnp.zeros((4 - B, D), F32), c_ctx[None], jnp.zeros((3, D), F32)], axis=0)
    mods_all = _modulation(cond8, w_ada, b_ada)
    w_in_ext = _prep_w_in(w_in)
    pad = ROUTER_COLS - N_GROUPS - N_EXPERTS
    w_router = jnp.concatenate([w_rg, w_re, jnp.zeros((L, D, pad), F32)], axis=2)
    w_scaled = w_router * 65537.0
    w_router_hi = w_scaled - (w_scaled - w_router)
    w_router_all = jnp.concatenate([w_router_hi, w_router - w_router_hi], axis=2).astype(BF16)
    b_router_all = jnp.concatenate([b_rg, b_re, jnp.zeros((L, pad), F32)], axis=1).reshape(L, 1, ROUTER_COLS)
    w_out_all = w_out.astype(BF16)
    na_bias = _na_bias(na_rpb)
    cos_t, sin_t = _rope_tables(S)
    cs_lat = _dft_tables(S)
    cs_ctx = _dft_tables(CTX_LEN)
    cd_c, cd_s = _dft_cos_sin(FNET_GROUP_DIM)
    cd = jnp.concatenate([cd_c, -cd_s], axis=1).astype(BF16)

    xs = (ctx.reshape(B * CTX_LEN, D), x.reshape(B * S, D))
    pending = None
    for l in range(L):
        last = l == L - 1
        j0 = 1 if last else 0
        mods = mods_all[l].reshape(48, 1, D)
        w_uq_ext = _prep_w_uq(w_uq[l])
        mla = (g_q[l], g_kv[l], w_uq_ext, w_ukv[l].astype(BF16), cos_t, sin_t)
        if pending is None:
            zf, naq, nak, navt, q, k, vt = _in_projection(xs, mods, g_attn[l], w_in_ext, l, mla, B)
        else:
            zf, naq, nak, navt, q, k, vt, xs = _in_projection(xs, mods, g_attn[l], w_in_ext, l, mla, B,
                                                              moe=pending)
        o_mla = _mla_attention(q, k, vt, B, j0)
        o_na = _na_attention(naq, nak, navt, na_bias, l, B, j0)
        w_f = w_fnet[l].astype(BF16)
        o_f = _fnet(zf, cs_lat, cs_ctx, cd, w_f, b_fnet[l], B, j0)
        xn, hf, route_w, route_i, counts = _merge(o_mla, o_f, o_na, xs, mods, g_out[l], g_ffn[l],
                                                  w_out_all, w_router_all, b_router_all, l, B, j0)
        wts = route_w[:, 0:2]
        meta, slot = _plan(counts[0, N_GROUPS:N_GROUPS + N_EXPERTS], route_i[:, 0:2].reshape(-1),
                           route_i[:, 2:4].reshape(-1))
        y = _moe(hf, meta, w_gate, w_up, w_down, l)
        if last:
            return _final(slot, wts, xn, mods, y, g_final, B)
        xs, pending = xn, (slot, wts, y, mods)
```

```python
import functools

import numpy as np
import jax
import jax.numpy as jnp
from jax import lax
from jax.experimental import pallas as pl
from jax.experimental.pallas import tpu as pltpu

F32 = jnp.float32
BF16 = jnp.bfloat16

D = 2048
GRID_W = 64
CTX_LEN = 256
EPS = 1e-6
NEG_INF = -1e30
ROPE_THETA = 10000.0

V_DIM = 128
MLA_WIDTH = D // 2
MLA_HEADS = MLA_WIDTH // V_DIM
QK_NOPE = 128
QK_ROPE = 64
Q_LORA = D // 4
KV_LORA = D // 8
FNET_WIDTH = D // 4
FNET_GROUP_DIM = 128
FNET_GROUPS = FNET_WIDTH // FNET_GROUP_DIM
NA_WIDTH = D // 4
NA_HEAD_DIM = 128
NA_HEADS = NA_WIDTH // NA_HEAD_DIM
NA_KH_MAX = 8
NA_KW = 16
N_GROUPS = 4
EXPERTS_PER_GROUP = 8
N_EXPERTS = N_GROUPS * EXPERTS_PER_GROUP
D_EXPERT = D // 4

MLA_QSCALE = (QK_NOPE + QK_ROPE) ** -0.5 * float(np.log2(np.e))
MLA_CHUNK_TILES = 8
NA_QSCALE = NA_HEAD_DIM ** -0.5 * float(np.log2(np.e))
NA_WIN_ROWS = 12
TM = 256
IN_COLS = Q_LORA + KV_LORA + QK_ROPE + FNET_WIDTH + 3 * NA_WIDTH
ROUTER_COLS = 128
VMEM_LIMIT = 56 * 1024 * 1024


def _cparams(sem):
    return pltpu.CompilerParams(dimension_semantics=sem, vmem_limit_bytes=VMEM_LIMIT)


def _rms(v, g):
    return v * lax.rsqrt(jnp.mean(v * v, axis=-1, keepdims=True) + EPS) * g


def _dot(a, b):
    return jnp.dot(a, b, preferred_element_type=F32)


def _dot_nt(a, b):
    return lax.dot_general(a, b, (((1,), (1,)), ((), ())), preferred_element_type=F32)


def _mod_kernel(c_ref, w_ref, b_ref, o_ref):
    c = c_ref[...]
    s = c * jax.nn.sigmoid(c)
    o_ref[0] = _dot(s.astype(BF16), w_ref[0].astype(BF16)) + b_ref[0]


def _modulation(cond8, w_ada, b_ada):
    L = w_ada.shape[0]
    tn = 1024
    return pl.pallas_call(
        _mod_kernel,
        grid=(L, 6 * D // tn),
        in_specs=[
            pl.BlockSpec((8, D), lambda l, n: (0, 0)),
            pl.BlockSpec((1, D, tn), lambda l, n: (l, 0, n)),
            pl.BlockSpec((1, 1, tn), lambda l, n: (l, 0, n)),
        ],
        out_specs=pl.BlockSpec((1, 8, tn), lambda l, n: (l, 0, n)),
        out_shape=jax.ShapeDtypeStruct((L, 8, 6 * D), F32),
        compiler_params=_cparams(("arbitrary", "arbitrary")),
        name="modulation",
    )(cond8, w_ada, b_ada.reshape(L, 1, 6 * D))


def _tile_of(b, j, j0):
    return b * 9 + j0 + j


def _otile(b, j, j0):
    return b * (9 - j0) + j


def _mod_row(j, b, j0):
    return jnp.where(j0 + j == 0, 4, b)


def _mod_spec(k, j0):
    return pl.BlockSpec((1, 1, D), lambda b, j: (_mod_row(j, b, j0) * 6 + k, 0, 0))


def _stream_specs(xs, j0):
    if isinstance(xs, tuple):
        return [pl.BlockSpec((TM, D), lambda b, j, *_: (b, 0)),
                pl.BlockSpec((TM, D), lambda b, j, *_: (b * 8 + jnp.maximum(j0 + j - 1, 0), 0))], list(xs)
    return [pl.BlockSpec((TM, D), lambda b, j, *_: (_tile_of(b, j, j0), 0))], [xs]


def _stream_tile(refs, j0):
    if len(refs) == 1:
        return refs[0][...]
    return jnp.where(pl.program_id(1) + j0 == 0, refs[0][...], refs[1][...])


def _moe_residual(pos_ref, y_hbm, ybuf, sem, x, gate, w, step, n_steps):
    buf = lax.rem(step, 2)

    def start(tile, b, unrolled):
        base = tile * TM

        def issue(r):
            for k in range(2):
                pltpu.make_async_copy(y_hbm.at[pl.ds(pos_ref[2 * (base + r) + k], 1)],
                                      ybuf.at[b, k, pl.ds(r, 1)], sem.at[b]).start()

        if unrolled:
            for r in range(TM):
                issue(r)
        else:
            lax.fori_loop(0, TM, lambda r, c: (issue(r), c)[1], 0)

    def wait(b):
        for k in range(2):
            pltpu.make_async_copy(y_hbm.at[pl.ds(0, TM)], ybuf.at[b, k], sem.at[b]).wait()

    @pl.when(step == 0)
    def _():
        start(0, 0, False)

    for b in range(2):
        @pl.when(buf == 1 - b)
        def _():
            start(jnp.minimum(step + 1, n_steps - 1), b, True)

    wait(buf)
    out = x + gate * (w[:, 0:1] * ybuf[buf, 0] + w[:, 1:2] * ybuf[buf, 1])

    @pl.when(step == n_steps - 1)
    def _():
        wait(1 - buf)

    return out


_MOE_GATHER_SCRATCH = [pltpu.VMEM((2, 2, TM, D), F32), pltpu.SemaphoreType.DMA((2,))]


def _rope(r, cos_t, sin_t):
    lane = lax.broadcasted_iota(jnp.int32, r.shape, 1)
    partner = jnp.where(lane % 2 == 0, pltpu.roll(r, 127, 1), pltpu.roll(r, 1, 1))
    return r * cos_t + partner * sin_t


def _inproj_body(x, sh_ref, sc_ref, g_ref, w_ref, gq_ref, gkv_ref, wq_ref, wkv_ref, cos_ref, sin_ref,
                 zf_ref, naq_ref, nak_ref, navt_ref, q_ref, k_ref, vt_ref):
    h = _rms(x, g_ref[...])
    h = h * (1.0 + sc_ref[0]) + sh_ref[0]
    z = _dot_nt(h.astype(BF16), w_ref[0])
    c_zf =Q_LORA + KV_LORA + QK_ROPE
    zf_ref[...] = z[:, c_zf:c_zf + FNET_WIDTH].astype(BF16)
    c_na = c_zf + FNET_WIDTH
    naq_ref[...] = (z[:, c_na:c_na + NA_WIDTH] * NA_QSCALE).astype(BF16)
    nak_ref[...] = z[:, c_na + NA_WIDTH:c_na + 2 * NA_WIDTH].astype(BF16)
    for hd in range(NA_HEADS):
        c = c_na + 2 * NA_WIDTH + hd * NA_HEAD_DIM
        navt_ref[hd * NA_HEAD_DIM:(hd + 1) * NA_HEAD_DIM, :] = z[:, c:c + NA_HEAD_DIM].T.astype(BF16)
    cos_t = cos_ref[...]
    sin_t = sin_ref[...]
    q = _dot(_rms(z[:, 0:Q_LORA], gq_ref[...]).astype(BF16), wq_ref[...]) * MLA_QSCALE
    kv = _dot(_rms(z[:, Q_LORA:Q_LORA + KV_LORA], gkv_ref[...]).astype(BF16), wkv_ref[...])
    k_rope = _rope(z[:, Q_LORA + KV_LORA:Q_LORA + KV_LORA + 128], cos_t, sin_t).astype(BF16)
    for hd in range(MLA_HEADS):
        c = hd * 256
        q_ref[:, c:c + 128] = q[:, c:c + 128].astype(BF16)
        q_ref[:, c + 128:c + 256] = _rope(q[:, c + 128:c + 256], cos_t, sin_t).astype(BF16)
        k_ref[:, c:c + 128] = kv[:, c:c + 128].astype(BF16)
        k_ref[:, c + 128:c + 256] = k_rope
        vt_ref[hd * 128:(hd + 1) * 128, :] = kv[:, c + 128:c + 256].T.astype(BF16)


def _inproj_kernel(*refs, n_stream):
    _inproj_body(_stream_tile(refs[:n_stream], 0), *refs[n_stream:])


def _inproj_moe_kernel(pos_ref, x_ref, gf_ref, wts_ref, y_hbm, *refs, n_steps):
    *refs, xs_ref, ybuf, sem = refs
    step = pl.program_id(0) * 9 + pl.program_id(1)
    x = _moe_residual(pos_ref, y_hbm, ybuf, sem, x_ref[...], gf_ref[0], wts_ref[...], step, n_steps)
    xs_ref[...] = x
    _inproj_body(x, *refs)


def _in_projection(xs, mods, g_attn, w_in_ext, layer, mla, B, moe=None):
    T = B * 9 * TM
    row = lambda b, j, *_: (_tile_of(b, j, 0), 0)
    col = lambda b, j, *_: (0, _tile_of(b, j, 0))
    const = lambda b, j, *_: (0, 0)
    mod = lambda k: pl.BlockSpec((1, 1, D), lambda b, j, *_: (_mod_row(j, b, 0) * 6 + k, 0, 0))
    once = dict(pipeline_mode=pl.Buffered(1))
    g_q, g_kv, w_uq_ext, w_ukv, cos_t, sin_t = mla
    in_specs = [
        pl.BlockSpec((TM, D), row),
        mod(0),
        mod(1),
        pl.BlockSpec((1, D), const),
        pl.BlockSpec((1, IN_COLS, D), lambda b, j, *_: (layer, 0, 0), **once),
        pl.BlockSpec((1, Q_LORA), const),
        pl.BlockSpec((1, KV_LORA), const),
        pl.BlockSpec((Q_LORA, MLA_HEADS * 256), const, **once),
        pl.BlockSpec((KV_LORA, MLA_HEADS * 256), const, **once),
        pl.BlockSpec((TM, 128), lambda b, j, *_: (j, 0)),
        pl.BlockSpec((TM, 128), lambda b, j, *_: (j, 0)),
    ]
    out_specs = [
        pl.BlockSpec((TM, FNET_WIDTH), row),
        pl.BlockSpec((TM, NA_WIDTH), row),
        pl.BlockSpec((TM, NA_WIDTH), row),
        pl.BlockSpec((NA_WIDTH, TM), col),
        pl.BlockSpec((TM, MLA_HEADS * 256), row),
        pl.BlockSpec((TM, MLA_HEADS * 256), row),
        pl.BlockSpec((MLA_WIDTH, TM), col),
    ]
    out_shape = [
        jax.ShapeDtypeStruct((T, FNET_WIDTH), BF16),
        jax.ShapeDtypeStruct((T, NA_WIDTH), BF16),
        jax.ShapeDtypeStruct((T, NA_WIDTH), BF16),
        jax.ShapeDtypeStruct((NA_WIDTH, T), BF16),
        jax.ShapeDtypeStruct((T, MLA_HEADS * 256), BF16),
        jax.ShapeDtypeStruct((T, MLA_HEADS * 256), BF16),
        jax.ShapeDtypeStruct((MLA_WIDTH, T), BF16),
    ]
    args = [xs, mods, mods, g_attn.reshape(1, D), w_in_ext, g_q.reshape(1, -1), g_kv.reshape(1, -1),
            w_uq_ext, w_ukv, cos_t, sin_t]
    if moe is None:
        stream_specs, stream_args = _stream_specs(xs, 0)
        return pl.pallas_call(
            functools.partial(_inproj_kernel, n_stream=len(stream_args)),
            grid=(B, 9),
            in_specs=stream_specs + in_specs[1:],
            out_specs=out_specs,
            out_shape=out_shape,
            compiler_params=_cparams(("arbitrary", "arbitrary")),
            name="in_projection",
        )(*stream_args, *args[1:])
    pos, wts, y, mods_prev = moe
    in_specs = [in_specs[0], mod(5), pl.BlockSpec((TM, 2), row), pl.BlockSpec(memory_space=pl.ANY)] \
        + in_specs[1:]
    grid_spec = pltpu.PrefetchScalarGridSpec(
        num_scalar_prefetch=1,
        grid=(B, 9),
        in_specs=in_specs,
        out_specs=out_specs + [pl.BlockSpec((TM, D), row)],
        scratch_shapes=_MOE_GATHER_SCRATCH,
    )
    return pl.pallas_call(
        functools.partial(_inproj_moe_kernel, n_steps=B * 9),
        grid_spec=grid_spec,
        out_shape=out_shape + [jax.ShapeDtypeStruct((T, D), F32)],
        compiler_params=_cparams(("arbitrary", "arbitrary")),
        name="in_projection_moe",
    )(pos, xs, mods_prev, wts, y, *args[1:])


def _mla_attn_kernel(q_ref, k_ref, vt_ref, o_ref, *, j0):
    def attend(q0, nq, nk, o0):
        st = _dot_nt(k_ref[0:nk, :], q_ref[q0:q0 + nq, :])
        m = jnp.max(st, axis=0, keepdims=True)
        p = jnp.exp2(st - m)
        l = jnp.sum(p, axis=0, keepdims=True)
        ot = _dot(vt_ref[:, 0:nk], p.astype(BF16)) / l
        o_ref[o0:o0 + nq, :] = ot.T

    if j0 == 0:
        attend(0, CTX_LEN, CTX_LEN, 0)
    for c in range(8 // MLA_CHUNK_TILES):
        attend(CTX_LEN + MLA_CHUNK_TILES * c * TM, MLA_CHUNK_TILES * TM, 9 * TM,
               (1 - j0 + MLA_CHUNK_TILES * c) * TM)


def _mla_attention(q, k, vt, B, j0):
    rows = (9 - j0) * TM
    return pl.pallas_call(
        functools.partial(_mla_attn_kernel, j0=j0),
        grid=(B, MLA_HEADS),
        in_specs=[
            pl.BlockSpec((9 * TM, 256), lambda b, h: (b, h)),
            pl.BlockSpec((9 * TM, 256), lambda b, h: (b, h)),
            pl.BlockSpec((V_DIM, 9 * TM), lambda b, h: (h, b)),
        ],
        out_specs=pl.BlockSpec((rows, V_DIM), lambda b, h: (b, h)),
        out_shape=jax.ShapeDtypeStruct((B * rows, MLA_WIDTH), F32),
        compiler_params=_cparams(("arbitrary", "arbitrary")),
        name="mla_attention",
    )(q, k, vt)


def _na_chunk(g):
    start_row = min(max(4 * g - 4, 0), 8 * TM // GRID_W - NA_WIN_ROWS)
    pattern = 0 if g == 0 else (2 if g == 7 else 1)
    return start_row, pattern


def _na_kernel(q_ref, k_ref, vt_ref, bias_ref, o_ref, *, j0):
    def finish(parts, o0):
        m = None
        for st, _ in parts:
            pm = jnp.max(st, axis=0, keepdims=True)
            m = pm if m is None else jnp.maximum(m, pm)
        l = None
        ot = None
        for st, vt in parts:
            p = jnp.exp2(st - m)
            pl_sum = jnp.sum(p, axis=0, keepdims=True)
            po = _dot(vt, p.astype(BF16))
            l = pl_sum if l is None else l + pl_sum
            ot = po if ot is None else ot + po
        o_ref[o0:o0 + TM, :] = (ot / l).T

    if j0 == 0:
        st = _dot_nt(k_ref[0:CTX_LEN, :], q_ref[0:CTX_LEN, :])
        finish([(st, vt_ref[:, 0:CTX_LEN])], 0)
    for g in range(8):
        start_row, pattern = _na_chunk(g)
        k0 = CTX_LEN + start_row * GRID_W
        nk = NA_WIN_ROWS * GRID_W
        q = q_ref[CTX_LEN + g * TM:CTX_LEN + (g + 1) * TM, :]
        st_loc = _dot_nt(k_ref[k0:k0 + nk, :], q) + bias_ref[0, pattern]
        st_ctx = _dot_nt(k_ref[0:CTX_LEN, :], q)
        finish([(st_loc, vt_ref[:, k0:k0 + nk]), (st_ctx, vt_ref[:, 0:CTX_LEN])], (1 - j0 + g) * TM)


def _na_attention(naq, nak, navt, bias, layer, B, j0):
    rows = (9 - j0) * TM
    bias = bias.reshape((-1,) + bias.shape[2:])
    return pl.pallas_call(
        functools.partial(_na_kernel, j0=j0),
        grid=(NA_HEADS, B),
        in_specs=[
            pl.BlockSpec((9 * TM, NA_HEAD_DIM), lambda h, b: (b, h)),
            pl.BlockSpec((9 * TM, NA_HEAD_DIM), lambda h, b: (b, h)),
            pl.BlockSpec((NA_HEAD_DIM, 9 * TM), lambda h, b: (h, b)),
            pl.BlockSpec((1, 3, NA_WIN_ROWS * GRID_W, TM), lambda h, b: (layer * NA_HEADS + h, 0, 0, 0)),
        ],
        out_specs=pl.BlockSpec((rows, NA_HEAD_DIM), lambda h, b: (b, h)),
        out_shape=jax.ShapeDtypeStruct((B * rows, NA_WIDTH), F32),
        compiler_params=_cparams(("arbitrary", "arbitrary")),
        name="na_attention",
    )(naq, nak, navt, bias)


def _na_bias(rpb):
    kh, rows, nq = NA_KH_MAX, 8 * TM // GRID_W, TM // GRID_W
    cq = np.arange(GRID_W)
    ck = np.arange(GRID_W)
    col_start = np.clip(cq - NA_KW // 2, 0, GRID_W - NA_KW)
    col_ok = (ck[:, None] >= col_start[None, :]) & (ck[:, None] < col_start[None, :] + NA_KW)
    dcol = np.clip(ck[:, None] - cq[None, :] + (NA_KW - 1), 0, 2 * NA_KW - 2)
    select = np.zeros((2 * NA_KW - 1, GRID_W * GRID_W), np.float32)
    select[dcol.reshape(-1), np.arange(GRID_W * GRID_W)] = 1.0
    blocks = jnp.einsum("lhdm,mn->lhdn", rpb.astype(F32) * float(np.log2(np.e)), jnp.asarray(select),
                        precision=lax.Precision.HIGHEST)
    blocks = blocks.reshape(rpb.shape[:3] + (GRID_W, GRID_W))
    blocks = jnp.where(jnp.asarray(col_ok), blocks, NEG_INF)
    masked = jnp.full(rpb.shape[:2] + (GRID_W, GRID_W), NEG_INF, F32)
    patterns = []
    for g in (0, 1, 7):
        start_row, _ = _na_chunk(g)
        key_rows = []
        for kr in range(NA_WIN_ROWS):
            key_row = start_row + kr
            row = []
            for qr in range(nq):
                r = 4 * g + qr
                r_start = min(max(r - kh // 2, 0), rows - kh)
                in_rows = r_start <= key_row < r_start + kh
                row.append(blocks[:, :, key_row - r + (kh - 1)] if in_rows else masked)
            key_rows.append(jnp.concatenate(row, axis=-1))
        patterns.append(jnp.concatenate(key_rows, axis=-2))
    return jnp.stack(patterns, axis=2)


def _fnet_kernel(z_ref, csl_ref, csc_ref, cd_ref, w_ref, b_ref, o_ref, ab_ref, *, j0, seq):
    j = pl.program_id(1) + j0

    def small_side(row0, length):
        for g in range(FNET_GROUPS):
            c = g * FNET_GROUP_DIM
            ab = _dot(z_ref[row0:row0 + length, c:c + FNET_GROUP_DIM], cd_ref[...])
            ab_ref[0:length, c:c + FNET_GROUP_DIM] = ab[:, 0:FNET_GROUP_DIM].astype(BF16)
            ab_ref[length:2 * length, c:c + FNET_GROUP_DIM] = ab[:, FNET_GROUP_DIM:].astype(BF16)

    def long_side(cs, length):
        f = _dot(cs, ab_ref[0:2 * length, :]) * (length * FNET_GROUP_DIM) ** -0.5
        o_ref[...] = _dot(f.astype(BF16), w_ref[...]) + b_ref[...]

    if j0 == 0:
        @pl.when(j == 0)
        def _():
            small_side(0, CTX_LEN)
            long_side(csc_ref[...], CTX_LEN)

    @pl.when(j == 1)
    def _():
        small_side(CTX_LEN, seq)

    @pl.when(j >= 1)
    def _():
        long_side(csl_ref[...], seq)


def _fnet(zf, cs_lat, cs_ctx, cd, w_fnet, b_fnet, B, j0):
    seq = 8 * TM
    return pl.pallas_call(
        functools.partial(_fnet_kernel, j0=j0, seq=seq),
        grid=(B, 9 - j0),
        in_specs=[
            pl.BlockSpec((9 * TM, FNET_WIDTH), lambda b, j: (b, 0)),
            pl.BlockSpec((TM, 2 * seq), lambda b, j: (jnp.maximum(j0 + j - 1, 0), 0)),
            pl.BlockSpec((CTX_LEN, 2 * CTX_LEN), lambda b, j: (0, 0)),
            pl.BlockSpec((FNET_GROUP_DIM, 2 * FNET_GROUP_DIM), lambda b, j: (0, 0)),
            pl.BlockSpec((FNET_WIDTH, FNET_WIDTH), lambda b, j: (0, 0)),
            pl.BlockSpec((1, FNET_WIDTH), lambda b, j: (0, 0)),
        ],
        out_specs=pl.BlockSpec((TM, FNET_WIDTH), lambda b, j: (_otile(b, j, j0), 0)),
        out_shape=jax.ShapeDtypeStruct((B * (9 - j0) * TM, FNET_WIDTH), F32),
        scratch_shapes=[pltpu.VMEM((2 * seq, FNET_WIDTH), BF16)],
        compiler_params=_cparams(("arbitrary", "arbitrary")),
        name="fnet",
    )(zf, cs_lat, cs_ctx, cd, w_fnet, b_fnet.reshape(1, -1))


def _dft_cos_sin(n):
    j = jnp.arange(n, dtype=jnp.int32)[:, None]
    if n <= 64:
        ang = ((j * j.T) % n).astype(F32) * (2.0 * np.pi / n)
        return jnp.cos(ang), jnp.sin(ang)
    k1 = jnp.arange(n // 64, dtype=jnp.int32)[None, :]
    k0 = jnp.arange(64, dtype=jnp.int32)[None, :]
    a = ((j * k1 * 64) % n).astype(F32) * (2.0 * np.pi / n)
    b = ((j * k0) % n).astype(F32) * (2.0 * np.pi / n)
    ca, sa, cb, sb = jnp.cos(a), jnp.sin(a), jnp.cos(b), jnp.sin(b)
    c = ca[:, :, None] * cb[:, None, :] - sa[:, :, None] * sb[:, None, :]
    s = sa[:, :, None] * cb[:, None, :] + ca[:, :, None] * sb[:, None, :]
    return c.reshape(n, n), s.reshape(n, n)


def _dft_tables(n):
    c, s = _dft_cos_sin(n)
    return jnp.concatenate([c, s], axis=1).astype(BF16)


def _merge_kernel(*refs, n_stream, j0):
    x = _stream_tile(refs[:n_stream], j0)
    (om_ref, of_ref, on_ref, ga_ref, shf_ref, scf_ref, gout_ref, gffn_ref, wout_ref, wr_ref, br_ref,
     xn_ref, hf_ref, rw_ref, ri_ref, cnt_ref, run_ref) = refs[n_stream:]
    ym = _rms(om_ref[...], gout_ref[:, 0:MLA_WIDTH]).astype(BF16)
    yf = _rms(of_ref[...], gout_ref[:, MLA_WIDTH:MLA_WIDTH + FNET_WIDTH]).astype(BF16)
    yn = _rms(on_ref[...], gout_ref[:, MLA_WIDTH + FNET_WIDTH:]).astype(BF16)
    acc = _dot(ym, wout_ref[0, 0:MLA_WIDTH, :])
    acc = acc + _dot(yf, wout_ref[0, MLA_WIDTH:MLA_WIDTH + FNET_WIDTH, :])
    acc = acc + _dot(yn, wout_ref[0, MLA_WIDTH + FNET_WIDTH:, :])
    xn = x + ga_ref[0] * acc
    xn_ref[...] = xn
    hf = _rms(xn, gffn_ref[...]) * (1.0 + scf_ref[0]) + shf_ref[0]
    hf_ref[...] = hf
    hi = hf.astype(BF16)
    lo = (hf - hi.astype(F32)).astype(BF16)
    a = _dot(hi, wr_ref[0])
    b = _dot(lo, wr_ref[0])
    small = a[:, ROUTER_COLS:] + (b[:, :ROUTER_COLS] + b[:, ROUTER_COLS:])
    _route_tile(a[:, :ROUTER_COLS] + small + br_ref[0], rw_ref, ri_ref, cnt_ref, run_ref)


def _route_tile(lg, rw_ref, ri_ref, cnt_ref, run_ref):
    first = jnp.logical_and(pl.program_id(0) == 0, pl.program_id(1) == 0)

    @pl.when(first)
    def _():
        run_ref[...] = jnp.zeros_like(run_ref)

    lane = lax.broadcasted_iota(jnp.int32, lg.shape, 1)
    neg = jnp.float32(-jnp.inf)

    def top(v):
        vmax = jnp.max(v, axis=1, keepdims=True)
        idx = jnp.min(jnp.where(v == vmax, lane, ROUTER_COLS), axis=1, keepdims=True)
        return vmax, idx

    in_groups = lane < N_GROUPS
    gl = jnp.where(in_groups, lg, neg)
    g_max, g_sel = top(gl)
    g_w = 1.0 / jnp.sum(jnp.where(in_groups, jnp.exp(gl - g_max), 0.0), axis=1, keepdims=True)
    e_lo = N_GROUPS + g_sel * EXPERTS_PER_GROUP
    el = jnp.where(jnp.logical_and(lane >= e_lo, lane < e_lo + EXPERTS_PER_GROUP), lg, neg)
    e1_max, i1 = top(el)
    e2_max, i2 = top(jnp.where(lane == i1, neg, el))
    t = jnp.exp(e2_max - e1_max)
    w0 = g_w / (1.0 + t)
    w1 = w0 * t
    rw_ref[...] = jnp.where(lane == 0, w0, jnp.where(lane == 1, w1, 0.0))

    row = lax.broadcasted_iota(jnp.int32, (TM, TM), 0)
    col = lax.broadcasted_iota(jnp.int32, (TM, TM), 1)
    tri = jnp.where(row >= col, 1.0, 0.0).astype(BF16)
    hot0 = lane == i1
    hot1 = lane == i2
    c0 = _dot(tri, jnp.where(hot0, 1.0, 0.0).astype(BF16))
    c1 = _dot(tri, jnp.where(hot1, 1.0, 0.0).astype(BF16))
    run = run_ref[...]
    tot0 = c0[TM - 1:TM, :]
    rank0 = jnp.sum(jnp.where(hot0, run + c0 - 1.0, 0.0), axis=1, keepdims=True)
    rank1 = jnp.sum(jnp.where(hot1, run + tot0 + c1 - 1.0, 0.0), axis=1, keepdims=True)
    run = run + tot0 + c1[TM - 1:TM, :]
    run_ref[...] = run
    cnt_ref[...] = jnp.broadcast_to(run, cnt_ref.shape).astype(jnp.int32)
    ri_ref[...] = jnp.where(lane == 0, i1 - N_GROUPS, jnp.where(lane == 1, i2 - N_GROUPS, jnp.where(
        lane == 2, rank0.astype(jnp.int32), jnp.where(lane == 3, rank1.astype(jnp.int32), 0))))


def _merge(o_mla, o_f, o_na, xs, mods, g_out, g_ffn, w_out, w_router, b_router, layer, B, j0):
    T = B * (9 - j0) * TM
    row = lambda b, j: (_otile(b, j, j0), 0)
    const = lambda b, j: (0, 0)
    slab = lambda b, j: (layer, 0, 0)
    stream_specs, stream_args = _stream_specs(xs, j0)
    return pl.pallas_call(
        functools.partial(_merge_kernel, n_stream=len(stream_args), j0=j0),
        grid=(B, 9 - j0),
        in_specs=stream_specs + [
            pl.BlockSpec((TM, MLA_WIDTH), row),
            pl.BlockSpec((TM, FNET_WIDTH), row),
            pl.BlockSpec((TM, NA_WIDTH), row),
            _mod_spec(2, j0),
            _mod_spec(3, j0),
            _mod_spec(4, j0),
            pl.BlockSpec((1, D), const),
            pl.BlockSpec((1, D), const),
            pl.BlockSpec((1, D, D), slab),
            pl.BlockSpec((1, D, 2 * ROUTER_COLS), slab),
            pl.BlockSpec((1, 1, ROUTER_COLS), slab),
        ],
        out_specs=[
            pl.BlockSpec((TM, D), row),
            pl.BlockSpec((TM, D), row),
            pl.BlockSpec((TM, ROUTER_COLS), row),
            pl.BlockSpec((TM, ROUTER_COLS), row),
            pl.BlockSpec((8, ROUTER_COLS), const),
        ],
        out_shape=[
            jax.ShapeDtypeStruct((T, D), F32),
            jax.ShapeDtypeStruct((T, D), F32),
            jax.ShapeDtypeStruct((T, ROUTER_COLS), F32),
            jax.ShapeDtypeStruct((T, ROUTER_COLS), jnp.int32),
            jax.ShapeDtypeStruct((8, ROUTER_COLS), jnp.int32),
        ],
        scratch_shapes=[pltpu.VMEM((1, ROUTER_COLS), F32)],
        compiler_params=_cparams(("arbitrary", "arbitrary")),
        name="merge",
    )(*stream_args, o_mla, o_f, o_na, mods, mods, mods, g_out.reshape(1, D), g_ffn.reshape(1, D),
      w_out, w_router, b_router)


def _plan_kernel(cnt_ref, slot_ref, src_ref, tile_ref, exp_ref, lo_ref, hi_ref, flag_ref, nxt_ref, ni_ref,
                 gs_ref, *, n_pairs, max_items):
    def starts(e, acc):
        gs_ref[e] = acc
        return acc + cnt_ref[e]

    lax.fori_loop(0, N_EXPERTS, starts, 0)

    def place(p, c):
        src_ref[slot_ref[p]] = lax.shift_right_logical(p, 1)
        return c

    lax.fori_loop(0, n_pairs, place, 0, unroll=16)

    last = N_EXPERTS - 1

    def group_end(e):
        return gs_ref[e] + cnt_ref[e]

    def next_nonempty(e):
        return lax.while_loop(lambda x: jnp.logical_and(x < last, cnt_ref[jnp.minimum(x, last)] == 0),
                              lambda x: x + 1, e)

    def tile_items(t, carry):
        i, e, prev = carry
        row0 = t * TM
        e = lax.while_loop(lambda x: group_end(x) <= row0, lambda x: x + 1, e)

        def emit(state):
            i, e, prev, first, _ = state
            tile_ref[i] = t
            exp_ref[i] = e
            lo_ref[i] = jnp.clip(gs_ref[e] - row0, 0, TM)
            hi_ref[i] = jnp.clip(group_end(e) - row0, 0, TM)
            flag_ref[i] = first + 2 * (e != prev).astype(jnp.int32)
            done = group_end(e) >= row0 + TM
            e_next = jnp.where(done, e, next_nonempty(e + 1))
            return i + 1, e_next, e, jnp.int32(0), done

        i, e, prev, _, _ = lax.while_loop(lambda s: jnp.logical_not(s[4]), emit,
                                          (i, e, prev, jnp.int32(1), jnp.bool_(False)))
        return i, e, prev

    n_items, _, _ = lax.fori_loop(0, n_pairs // TM, tile_items,
                                  (jnp.int32(0), jnp.int32(0), jnp.int32(-1)))
    ni_ref[0] = n_items

    def pad(i, c):
        tile_ref[i] = tile_ref[n_items - 1]
        exp_ref[i] = exp_ref[n_items - 1]
        lo_ref[i] = 0
        hi_ref[i] = 0
        flag_ref[i] = 0
        nxt_ref[i] = -1
        return c

    lax.fori_loop(n_items, max_items, pad, 0)

    def parity(i, par):
        par = jnp.where((flag_ref[i] & 2) != 0, 1 - par, par)
        flag_ref[i] = flag_ref[i] + 4 * par
        return par

    lax.fori_loop(0, n_items, parity, jnp.int32(1))

    def lookahead(k, following):
        i = n_items - 1 - k
        nxt_ref[i] = following
        return jnp.where((flag_ref[i] & 2) != 0, exp_ref[i], following)

    lax.fori_loop(0, n_items, lookahead, jnp.int32(-1))


def _plan(counts, eid, rank):
    n_pairs = eid.shape[0]
    max_items = n_pairs // TM + N_EXPERTS - 1
    g_start = jnp.cumsum(counts) - counts
    experts = jnp.arange(N_EXPERTS, dtype=jnp.int32)
    slot = jnp.sum(jnp.where(eid[:, None] == experts[None, :], g_start[None, :], 0), axis=1) + rank
    smem = pl.BlockSpec(memory_space=pltpu.SMEM)
    i32 = lambda n: jax.ShapeDtypeStruct((n,), jnp.int32)
    src, it_tile, it_exp, it_lo, it_hi, flags, nxt, n_items = pl.pallas_call(
        functools.partial(_plan_kernel, n_pairs=n_pairs, max_items=max_items),
        in_specs=[smem] * 2,
        out_specs=[smem] * 8,
        out_shape=[i32(n_pairs)] + [i32(max_items)] * 6 + [i32(1)],
        scratch_shapes=[pltpu.SMEM((N_EXPERTS,), jnp.int32)],
        name="moe_plan",
    )(counts, slot)
    return (it_tile, it_exp, it_lo, it_hi, flags, nxt, n_items, src), slot


def _moe_kernel(tile_ref, exp_ref, lo_ref, hi_ref, flag_ref, nxt_ref, ni_ref, src_ref,
                hf_hbm, wg_hbm, wu_hbm, wd_hbm, y_ref, xbuf, wg_buf, wu_buf, wd_buf, wgb, wub, wdb,
                sem, wsem, *, n_tiles, e0):
    i = pl.program_id(0)
    t = tile_ref[i]
    slot = lax.rem(t, 2)

    def gather_start(tile, buf, unrolled):
        base = tile * TM

        def issue(r):
            tok = src_ref[base + r]
            pltpu.make_async_copy(hf_hbm.at[pl.ds(tok, 1)], xbuf.at[buf, pl.ds(r, 1)], sem.at[buf]).start()

        if unrolled:
            for r in range(TM):
                issue(r)
        else:
            lax.fori_loop(0, TM, lambda r, c: (issue(r), c)[1], 0)

    def gather_wait(buf):
        pltpu.make_async_copy(hf_hbm.at[pl.ds(0, TM)], xbuf.at[buf], sem.at[buf]).wait()

    def weight_copies(expert, b):
        e = e0 + expert
        return (pltpu.make_async_copy(wg_hbm.at[e], wg_buf.at[b], wsem.at[b]),
                pltpu.make_async_copy(wu_hbm.at[e], wu_buf.at[b], wsem.at[b]),
                pltpu.make_async_copy(wd_hbm.at[e], wd_buf.at[b], wsem.at[b]))

    def weights_start(expert, b):
        for cp in weight_copies(expert, b):
            cp.start()

    def weights_wait(b):
        for cp in weight_copies(0, b):
            cp.wait()

    @pl.when(i < ni_ref[0])
    def _():
        first_visit = (flag_ref[i] & 1) != 0
        new_expert = (flag_ref[i] & 2) != 0
        wslot = lax.shift_right_logical(flag_ref[i], 2) & 1

        @pl.when(i == 0)
        def _():
            gather_start(0, 0, False)

        @pl.when(first_visit)
        def _():
            gather_wait(slot)

        for b in range(2):
            @pl.when(jnp.logical_and(jnp.logical_and(first_visit, t + 1 < n_tiles), slot == 1 - b))
            def _():
                gather_start(t + 1, b, True)

        @pl.when(i == 0)
        def _():
            weights_start(exp_ref[0], 0)

        @pl.when(new_expert)
        def _():
            @pl.when(nxt_ref[i] >= 0)
            def _():
                weights_start(nxt_ref[i], 1 - wslot)

            weights_wait(wslot)
            wgb[...] = wg_buf[wslot].astype(BF16)
            wub[...] = wu_buf[wslot].astype(BF16)
            wdb[...] = wd_buf[wslot].astype(BF16)

        x = xbuf[slot].astype(BF16)
        a = _dot(x, wgb[...])
        u = _dot(x, wub[...])
        row = lax.broadcasted_iota(jnp.int32, (TM, 1), 0)
        mine = jnp.logical_and(row >= lo_ref[i], row < hi_ref[i])
        h = jnp.where(mine, (a * jax.nn.sigmoid(a)) * u, 0.0)
        yv = _dot(h.astype(BF16), wdb[...])

        @pl.when(first_visit)
        def _():
            y_ref[...] = yv

        @pl.when(jnp.logical_not(first_visit))
        def _():
            y_ref[...] += yv


def _moe(hf, meta, w_gate, w_up, w_down, layer):
    n_rows = meta[-1].shape[0]
    n_tiles = n_rows // TM
    max_items = meta[0].shape[0]
    grid_spec = pltpu.PrefetchScalarGridSpec(
        num_scalar_prefetch=len(meta),
        grid=(max_items,),
        in_specs=[pl.BlockSpec(memory_space=pl.ANY)] * 4,
        out_specs=pl.BlockSpec((TM, D), lambda i, tile, *_: (tile[i], 0)),
        scratch_shapes=[
            pltpu.VMEM((2, TM, D), F32),
            pltpu.VMEM((2, D, D_EXPERT), F32),
            pltpu.VMEM((2, D, D_EXPERT), F32),
            pltpu.VMEM((2, D_EXPERT, D), F32),
            pltpu.VMEM((D, D_EXPERT), BF16),
            pltpu.VMEM((D, D_EXPERT), BF16),
            pltpu.VMEM((D_EXPERT, D), BF16),
            pltpu.SemaphoreType.DMA((2,)),
            pltpu.SemaphoreType.DMA((2,)),
        ],
    )
    return pl.pallas_call(
        functools.partial(_moe_kernel, n_tiles=n_tiles, e0=layer * N_EXPERTS),
        grid_spec=grid_spec,
        out_shape=jax.ShapeDtypeStruct((n_rows, D), F32),
        compiler_params=_cparams(("arbitrary",)),
        name="moe_experts",
    )(*meta, hf, w_gate.reshape(-1, D, D_EXPERT), w_up.reshape(-1, D, D_EXPERT),
      w_down.reshape(-1, D_EXPERT, D))


def _final_kernel(pos_ref, x_ref, gf_ref, w_ref, g_ref, y_hbm, o_ref, ybuf, sem, *, n_steps):
    step = pl.program_id(0) * 8 + pl.program_id(1)
    out = _moe_residual(pos_ref, y_hbm, ybuf, sem, x_ref[...], gf_ref[0], w_ref[...], step, n_steps)
    o_ref[0] = _rms(out, g_ref[...])


def _final(pos, wts, xn, mods, y, g_final, B):
    tile = lambda b, j, p: (b * 8 + j, 0)
    grid_spec = pltpu.PrefetchScalarGridSpec(
        num_scalar_prefetch=1,
        grid=(B, 8),
        in_specs=[
            pl.BlockSpec((TM, D), tile),
            pl.BlockSpec((1, 1, D), lambda b, j, p: (b * 6 + 5, 0, 0)),
            pl.BlockSpec((TM, 2), tile),
            pl.BlockSpec((1, D), lambda b, j, p: (0, 0)),
            pl.BlockSpec(memory_space=pl.ANY),
        ],
        out_specs=pl.BlockSpec((1, TM, D), lambda b, j, p: (b, j, 0)),
        scratch_shapes=_MOE_GATHER_SCRATCH,
    )
    return pl.pallas_call(
        functools.partial(_final_kernel, n_steps=B * 8),
        grid_spec=grid_spec,
        out_shape=jax.ShapeDtypeStruct((B, 8 * TM, D), F32),
        compiler_params=_cparams(("arbitrary", "arbitrary")),
        name="final",
    )(pos, xn, mods, wts, g_final.reshape(1, D), y)


def _prep_w_in(w_in):
    return jnp.swapaxes(w_in, 1, 2).astype(BF16)


def _prep_w_uq(w_uq):
    w = w_uq.reshape(Q_LORA, MLA_HEADS, QK_NOPE + QK_ROPE)
    w = jnp.pad(w, ((0, 0), (0, 0), (0, 256 - QK_NOPE - QK_ROPE)))
    return w.reshape(Q_LORA, MLA_HEADS * 256).astype(BF16)


def _rope_tables(seq):
    half = QK_ROPE // 2
    inv_freq = ROPE_THETA ** (-jnp.arange(0, half, 2, dtype=F32) / half)
    t = jnp.arange(seq, dtype=jnp.int32)
    row = (t // GRID_W).astype(F32)
    col = (t % GRID_W).astype(F32)
    ang = jnp.concatenate([row[:, None] * inv_freq, col[:, None] * inv_freq], axis=-1)
    cos, sin = jnp.cos(ang), jnp.sin(ang)
    zeros = jnp.zeros((seq, 64), F32)
    cos_l = jnp.concatenate([jnp.repeat(cos, 2, axis=1), zeros], axis=1)
    sin_l = jnp.concatenate([jnp.stack([-sin, sin], axis=-1).reshape(seq, 64), zeros], axis=1)
    cos_c = jnp.concatenate([jnp.ones((CTX_LEN, 64), F32), jnp.zeros((CTX_LEN, 64), F32)], axis=1)
    sin_c = jnp.zeros((CTX_LEN, 128), F32)
    return jnp.concatenate([cos_c, cos_l], axis=0), jnp.concatenate([sin_c, sin_l], axis=0)


def kernel(x, c, ctx, c_ctx, w_ada, b_ada, g_attn, g_ffn, w_in, g_q, w_uq, g_kv, w_ukv, w_fnet, b_fnet,
           na_rpb, g_out, w_out, w_rg, b_rg, w_re, b_re, w_gate, w_up, w_down, g_final):
    B, S, _ = x.shape
    L = w_ada.shape[0]
    assert ctx.shape[1] == CTX_LEN == TM and S == 8 * TM and B <= 4
    T = B * 9 * TM

    cond8 = jnp.concatenate([c, jnp.zeros((4 - B, D), F32), c_ctx[None], jnp.zeros((3, D), F32)], axis=0)
    mods_all = _modulation(cond8, w_ada, b_ada)
    w_in_ext = _prep_w_in(w_in)
    pad = ROUTER_COLS - N_GROUPS - N_EXPERTS
    w_router = jnp.concatenate([w_rg, w_re, jnp.zeros((L, D, pad), F32)], axis=2)
    w_scaled = w_router * 65537.0
    w_router_hi = w_scaled - (w_scaled - w_router)
    w_router_all = jnp.concatenate([w_router_hi, w_router - w_router_hi], axis=2).astype(BF16)
    b_router_all = jnp.concatenate([b_rg, b_re, jnp.zeros((L, pad), F32)], axis=1).reshape(L, 1, ROUTER_COLS)
    w_out_all = w_out.astype(BF16)
    na_bias = _na_bias(na_rpb)
    cos_t, sin_t = _rope_tables(S)
    cs_lat = _dft_tables(S)
    cs_ctx = _dft_tables(CTX_LEN)
    cd_c, cd_s = _dft_cos_sin(FNET_GROUP_DIM)
    cd = jnp.concatenate([cd_c, -cd_s], axis=1).astype(BF16)

    xs = (ctx.reshape(B * CTX_LEN, D), x.reshape(B * S, D))
    pending = None
    for l in range(L):
        last = l == L - 1
        j0 = 1 if last else 0
        mods = mods_all[l].reshape(48, 1, D)
        w_uq_ext = _prep_w_uq(w_uq[l])
        mla = (g_q[l], g_kv[l], w_uq_ext, w_ukv[l].astype(BF16), cos_t, sin_t)
        if pending is None:
            zf, naq, nak, navt, q, k, vt = _in_projection(xs, mods, g_attn[l], w_in_ext, l, mla, B)
        else:
            zf, naq, nak, navt, q, k, vt, xs = _in_projection(xs, mods, g_attn[l], w_in_ext, l, mla, B,
                                                              moe=pending)
        o_mla = _mla_attention(q, k, vt, B, j0)
        o_na = _na_attention(naq, nak, navt, na_bias, l, B, j0)
        w_f = w_fnet[l].astype(BF16)
        o_f = _fnet(zf, cs_lat, cs_ctx, cd, w_f, b_fnet[l], B, j0)
        xn, hf, route_w, route_i, counts = _merge(o_mla, o_f, o_na, xs, mods, g_out[l], g_ffn[l],
                                                  w_out_all, w_router_all, b_router_all, l, B, j0)
        wts = route_w[:, 0:2]
        meta, slot = _plan(counts[0, N_GROUPS:N_GROUPS + N_EXPERTS], route_i[:, 0:2].reshape(-1),
                           route_i[:, 2:4].reshape(-1))
        y = _moe(hf, meta, w_gate, w_up, w_down, l)
        if last:
            return _final(slot, wts, xn, mods, y, g_final, B)
        xs, pending = xn, (slot, wts, y, mods)
```

```python
import functools

import numpy as np
import jax
import jax.numpy as jnp
from jax import lax
from jax.experimental import pallas as pl
from jax.experimental.pallas import tpu as pltpu

F32 = jnp.float32
BF16 = jnp.bfloat16

D = 2048
GRID_W = 64
CTX_LEN = 256
EPS = 1e-6
NEG_INF = -1e30
ROPE_THETA = 10000.0

V_DIM = 128
MLA_WIDTH = D // 2
MLA_HEADS = MLA_WIDTH // V_DIM
QK_NOPE = 128
QK_ROPE = 64
Q_LORA = D // 4
KV_LORA = D // 8
FNET_WIDTH = D // 4
FNET_GROUP_DIM = 128
FNET_GROUPS = FNET_WIDTH // FNET_GROUP_DIM
NA_WIDTH = D // 4
NA_HEAD_DIM = 128
NA_HEADS = NA_WIDTH // NA_HEAD_DIM
NA_KH_MAX = 8
NA_KW = 16
N_GROUPS = 4
EXPERTS_PER_GROUP = 8
N_EXPERTS = N_GROUPS * EXPERTS_PER_GROUP
D_EXPERT = D // 4

MLA_QSCALE = (QK_NOPE + QK_ROPE) ** -0.5 * float(np.log2(np.e))
MLA_CHUNK_TILES = 8
NA_QSCALE = NA_HEAD_DIM ** -0.5 * float(np.log2(np.e))
NA_WIN_ROWS = 12
TM = 256
IN_COLS = Q_LORA + KV_LORA + QK_ROPE + FNET_WIDTH + 3 * NA_WIDTH
ROUTER_COLS = 128
VMEM_LIMIT = 56 * 1024 * 1024


def _cparams(sem):
    return pltpu.CompilerParams(dimension_semantics=sem, vmem_limit_bytes=VMEM_LIMIT)


def _rms(v, g):
    return v * lax.rsqrt(jnp.mean(v * v, axis=-1, keepdims=True) + EPS) * g


def _dot(a, b):
    return jnp.dot(a, b, preferred_element_type=F32)


def _dot_nt(a, b):
    return lax.dot_general(a, b, (((1,), (1,)), ((), ())), preferred_element_type=F32)


def _mod_kernel(c_ref, w_ref, b_ref, o_ref):
    c = c_ref[...]
    s = c * jax.nn.sigmoid(c)
    o_ref[0] = _dot(s.astype(BF16), w_ref[0].astype(BF16)) + b_ref[0]


def _modulation(cond8, w_ada, b_ada):
    L = w_ada.shape[0]
    tn = 1024
    return pl.pallas_call(
        _mod_kernel,
        grid=(L, 6 * D // tn),
        in_specs=[
            pl.BlockSpec((8, D), lambda l, n: (0, 0)),
            pl.BlockSpec((1, D, tn), lambda l, n: (l, 0, n)),
            pl.BlockSpec((1, 1, tn), lambda l, n: (l, 0, n)),
        ],
        out_specs=pl.BlockSpec((1, 8, tn), lambda l, n: (l, 0, n)),
        out_shape=jax.ShapeDtypeStruct((L, 8, 6 * D), F32),
        compiler_params=_cparams(("arbitrary", "arbitrary")),
        name="modulation",
    )(cond8, w_ada, b_ada.reshape(L, 1, 6 * D))


def _tile_of(b, j, j0):
    return b * 9 + j0 + j


def _otile(b, j, j0):
    return b * (9 - j0) + j


def _mod_row(j, b, j0):
    return jnp.where(j0 + j == 0, 4, b)


def _mod_spec(k, j0):
    return pl.BlockSpec((1, 1, D), lambda b, j: (_mod_row(j, b, j0) * 6 + k, 0, 0))


def _stream_specs(xs, j0):
    if isinstance(xs, tuple):
        return [pl.BlockSpec((TM, D), lambda b, j, *_: (b, 0)),
                pl.BlockSpec((TM, D), lambda b, j, *_: (b * 8 + jnp.maximum(j0 + j - 1, 0), 0))], list(xs)
    return [pl.BlockSpec((TM, D), lambda b, j, *_: (_tile_of(b, j, j0), 0))], [xs]


def _stream_tile(refs, j0):
    if len(refs) == 1:
        return refs[0][...]
    return jnp.where(pl.program_id(1) + j0 == 0, refs[0][...], refs[1][...])


def _moe_residual(pos_ref, y_hbm, ybuf, sem, x, gate, w, step, n_steps):
    buf = lax.rem(step, 2)

    def start(tile, b, unrolled):
        base = tile * TM

        def issue(r):
            for k in range(2):
                pltpu.make_async_copy(y_hbm.at[pl.ds(pos_ref[2 * (base + r) + k], 1)],
                                      ybuf.at[b, k, pl.ds(r, 1)], sem.at[b]).start()

        if unrolled:
            for r in range(TM):
                issue(r)
        else:
            lax.fori_loop(0, TM, lambda r, c: (issue(r), c)[1], 0)

    def wait(b):
        for k in range(2):
            pltpu.make_async_copy(y_hbm.at[pl.ds(0, TM)], ybuf.at[b, k], sem.at[b]).wait()

    @pl.when(step == 0)
    def _():
        start(0, 0, False)

    for b in range(2):
        @pl.when(buf == 1 - b)
        def _():
            start(jnp.minimum(step + 1, n_steps - 1), b, True)

    wait(buf)
    out = x + gate * (w[:, 0:1] * ybuf[buf, 0] + w[:, 1:2] * ybuf[buf, 1])

    @pl.when(step == n_steps - 1)
    def _():
        wait(1 - buf)

    return out


_MOE_GATHER_SCRATCH = [pltpu.VMEM((2, 2, TM, D), F32), pltpu.SemaphoreType.DMA((2,))]


def _rope(r, cos_t, sin_t):
    lane = lax.broadcasted_iota(jnp.int32, r.shape, 1)
    partner = jnp.where(lane % 2 == 0, pltpu.roll(r, 127, 1), pltpu.roll(r, 1, 1))
    return r * cos_t + partner * sin_t


def _inproj_body(x, sh_ref, sc_ref, g_ref, w_ref, gq_ref, gkv_ref, wq_ref, wkv_ref, cos_ref, sin_ref,
                 zf_ref, naq_ref, nak_ref, navt_ref, q_ref, k_ref, vt_ref):
    h = _rms(x, g_ref[...])
    h = h * (1.0 + sc_ref[0]) + sh_ref[0]
    z = _dot_nt(h.astype(BF16), w_ref[0])
    c_zf =Q_LORA + KV_LORA + QK_ROPE
    zf_ref[...] = z[:, c_zf:c_zf + FNET_WIDTH].astype(BF16)
    c_na = c_zf + FNET_WIDTH
    naq_ref[...] = (z[:, c_na:c_na + NA_WIDTH] * NA_QSCALE).astype(BF16)
    nak_ref[...] = z[:, c_na + NA_WIDTH:c_na + 2 * NA_WIDTH].astype(BF16)
    for hd in range(NA_HEADS):
        c = c_na + 2 * NA_WIDTH + hd * NA_HEAD_DIM
        navt_ref[hd * NA_HEAD_DIM:(hd + 1) * NA_HEAD_DIM, :] = z[:, c:c + NA_HEAD_DIM].T.astype(BF16)
    cos_t = cos_ref[...]
    sin_t = sin_ref[...]
    q = _dot(_rms(z[:, 0:Q_LORA], gq_ref[...]).astype(BF16), wq_ref[...]) * MLA_QSCALE
    kv = _dot(_rms(z[:, Q_LORA:Q_LORA + KV_LORA], gkv_ref[...]).astype(BF16), wkv_ref[...])
    k_rope = _rope(z[:, Q_LORA + KV_LORA:Q_LORA + KV_LORA + 128], cos_t, sin_t).astype(BF16)
    for hd in range(MLA_HEADS):
        c = hd * 256
        q_ref[:, c:c + 128] = q[:, c:c + 128].astype(BF16)
        q_ref[:, c + 128:c + 256] = _rope(q[:, c + 128:c + 256], cos_t, sin_t).astype(BF16)
        k_ref[:, c:c + 128] = kv[:, c:c + 128].astype(BF16)
        k_ref[:, c + 128:c + 256] = k_rope
        vt_ref[hd * 128:(hd + 1) * 128, :] = kv[:, c + 128:c + 256].T.astype(BF16)


def _inproj_kernel(*refs, n_stream):
    _inproj_body(_stream_tile(refs[:n_stream], 0), *refs[n_stream:])


def _inproj_moe_kernel(pos_ref, x_ref, gf_ref, wts_ref, y_hbm, *refs, n_steps):
    *refs, xs_ref, ybuf, sem = refs
    step = pl.program_id(0) * 9 + pl.program_id(1)
    x = _moe_residual(pos_ref, y_hbm, ybuf, sem, x_ref[...], gf_ref[0], wts_ref[...], step, n_steps)
    xs_ref[...] = x
    _inproj_body(x, *refs)


def _in_projection(xs, mods, g_attn, w_in_ext, layer, mla, B, moe=None):
    T = B * 9 * TM
    row = lambda b, j, *_: (_tile_of(b, j, 0), 0)
    col = lambda b, j, *_: (0, _tile_of(b, j, 0))
    const = lambda b, j, *_: (0, 0)
    mod = lambda k: pl.BlockSpec((1, 1, D), lambda b, j, *_: (_mod_row(j, b, 0) * 6 + k, 0, 0))
    once = dict(pipeline_mode=pl.Buffered(1))
    g_q, g_kv, w_uq_ext, w_ukv, cos_t, sin_t = mla
    in_specs = [
        pl.BlockSpec((TM, D), row),
        mod(0),
        mod(1),
        pl.BlockSpec((1, D), const),
        pl.BlockSpec((1, IN_COLS, D), lambda b, j, *_: (layer, 0, 0), **once),
        pl.BlockSpec((1, Q_LORA), const),
        pl.BlockSpec((1, KV_LORA), const),
        pl.BlockSpec((Q_LORA, MLA_HEADS * 256), const, **once),
        pl.BlockSpec((KV_LORA, MLA_HEADS * 256), const, **once),
        pl.BlockSpec((TM, 128), lambda b, j, *_: (j, 0)),
        pl.BlockSpec((TM, 128), lambda b, j, *_: (j, 0)),
    ]
    out_specs = [
        pl.BlockSpec((TM, FNET_WIDTH), row),
        pl.BlockSpec((TM, NA_WIDTH), row),
        pl.BlockSpec((TM, NA_WIDTH), row),
        pl.BlockSpec((NA_WIDTH, TM), col),
        pl.BlockSpec((TM, MLA_HEADS * 256), row),
        pl.BlockSpec((TM, MLA_HEADS * 256), row),
        pl.BlockSpec((MLA_WIDTH, TM), col),
    ]
    out_shape = [
        jax.ShapeDtypeStruct((T, FNET_WIDTH), BF16),
        jax.ShapeDtypeStruct((T, NA_WIDTH), BF16),
        jax.ShapeDtypeStruct((T, NA_WIDTH), BF16),
        jax.ShapeDtypeStruct((NA_WIDTH, T), BF16),
        jax.ShapeDtypeStruct((T, MLA_HEADS * 256), BF16),
        jax.ShapeDtypeStruct((T, MLA_HEADS * 256), BF16),
        jax.ShapeDtypeStruct((MLA_WIDTH, T), BF16),
    ]
    args = [xs, mods, mods, g_attn.reshape(1, D), w_in_ext, g_q.reshape(1, -1), g_kv.reshape(1, -1),
            w_uq_ext, w_ukv, cos_t, sin_t]
    if moe is None:
        stream_specs, stream_args = _stream_specs(xs, 0)
        return pl.pallas_call(
            functools.partial(_inproj_kernel, n_stream=len(stream_args)),
            grid=(B, 9),
            in_specs=stream_specs + in_specs[1:],
            out_specs=out_specs,
            out_shape=out_shape,
            compiler_params=_cparams(("arbitrary", "arbitrary")),
            name="in_projection",
        )(*stream_args, *args[1:])
    pos, wts, y, mods_prev = moe
    in_specs = [in_specs[0], mod(5), pl.BlockSpec((TM, 2), row), pl.BlockSpec(memory_space=pl.ANY)] \
        + in_specs[1:]
    grid_spec = pltpu.PrefetchScalarGridSpec(
        num_scalar_prefetch=1,
        grid=(B, 9),
        in_specs=in_specs,
        out_specs=out_specs + [pl.BlockSpec((TM, D), row)],
        scratch_shapes=_MOE_GATHER_SCRATCH,
    )
    return pl.pallas_call(
        functools.partial(_inproj_moe_kernel, n_steps=B * 9),
        grid_spec=grid_spec,
        out_shape=out_shape + [jax.ShapeDtypeStruct((T, D), F32)],
        compiler_params=_cparams(("arbitrary", "arbitrary")),
        name="in_projection_moe",
    )(pos, xs, mods_prev, wts, y, *args[1:])


def _softmax_numerator(st, m):
    return jnp.exp2(st - m).astype(BF16)


def _pv_with_sum(vt, p):
    ones = jnp.ones((2 * 8, vt.shape[1]), BF16)
    o = _dot(jnp.concatenate([vt, ones], axis=0), p)
    return o[0:vt.shape[0]], o[vt.shape[0]:vt.shape[0] + 1]


def _mla_attn_kernel(q_ref, k_ref, vt_ref, o_ref, *, j0):
    def attend(q0, nq, nk, o0):
        st = _dot_nt(k_ref[0:nk, :], q_ref[q0:q0 + nq, :])
        m = jnp.max(st, axis=0, keepdims=True)
        ot, l = _pv_with_sum(vt_ref[:, 0:nk], _softmax_numerator(st, m))
        o_ref[o0:o0 + nq, :] = (ot / l).T

    if j0 == 0:
        attend(0, CTX_LEN, CTX_LEN, 0)
    for c in range(8 // MLA_CHUNK_TILES):
        attend(CTX_LEN + MLA_CHUNK_TILES * c * TM, MLA_CHUNK_TILES * TM, 9 * TM,
               (1 - j0 + MLA_CHUNK_TILES * c) * TM)


def _mla_attention(q, k, vt, B, j0):
    rows = (9 - j0) * TM
    return pl.pallas_call(
        functools.partial(_mla_attn_kernel, j0=j0),
        grid=(B, MLA_HEADS),
        in_specs=[
            pl.BlockSpec((9 * TM, 256), lambda b, h: (b, h)),
            pl.BlockSpec((9 * TM, 256), lambda b, h: (b, h)),
            pl.BlockSpec((V_DIM, 9 * TM), lambda b, h: (h, b)),
        ],
        out_specs=pl.BlockSpec((rows, V_DIM), lambda b, h: (b, h)),
        out_shape=jax.ShapeDtypeStruct((B * rows, MLA_WIDTH), F32),
        compiler_params=_cparams(("arbitrary", "arbitrary")),
        name="mla_attention",
    )(q, k, vt)


def _na_chunk(g):
    start_row = min(max(4 * g - 4, 0), 8 * TM // GRID_W - NA_WIN_ROWS)
    pattern = 0 if g == 0 else (2 if g == 7 else 1)
    return start_row, pattern


def _na_kernel(q_ref, k_ref, vt_ref, bias_ref, o_ref, *, j0):
    def finish(parts, o0):
        m = None
        for st, _ in parts:
            pm = jnp.max(st, axis=0, keepdims=True)
            m = pm if m is None else jnp.maximum(m, pm)
        l = None
        ot = None
        for st, vt in parts:
            po, pl_sum = _pv_with_sum(vt, _softmax_numerator(st, m))
            l = pl_sum if l is None else l + pl_sum
            ot = po if ot is None else ot + po
        o_ref[o0:o0 + TM, :] = (ot / l).T

    if j0 == 0:
        st = _dot_nt(k_ref[0:CTX_LEN, :], q_ref[0:CTX_LEN, :])
        finish([(st, vt_ref[:, 0:CTX_LEN])], 0)
    for g in range(8):
        start_row, pattern = _na_chunk(g)
        k0 = CTX_LEN + start_row * GRID_W
        nk = NA_WIN_ROWS * GRID_W
        q = q_ref[CTX_LEN + g * TM:CTX_LEN + (g + 1) * TM, :]
        st_loc = _dot_nt(k_ref[k0:k0 + nk, :], q) + bias_ref[0, pattern]
        st_ctx = _dot_nt(k_ref[0:CTX_LEN, :], q)
        finish([(st_loc, vt_ref[:, k0:k0 + nk]), (st_ctx, vt_ref[:, 0:CTX_LEN])], (1 - j0 + g) * TM)


def _na_attention(naq, nak, navt, bias, layer, B, j0):
    rows = (9 - j0) * TM
    bias = bias.reshape((-1,) + bias.shape[2:])
    return pl.pallas_call(
        functools.partial(_na_kernel, j0=j0),
        grid=(NA_HEADS, B),
        in_specs=[
            pl.BlockSpec((9 * TM, NA_HEAD_DIM), lambda h, b: (b, h)),
            pl.BlockSpec((9 * TM, NA_HEAD_DIM), lambda h, b: (b, h)),
            pl.BlockSpec((NA_HEAD_DIM, 9 * TM), lambda h, b: (h, b)),
            pl.BlockSpec((1, 3, NA_WIN_ROWS * GRID_W, TM), lambda h, b: (layer * NA_HEADS + h, 0, 0, 0)),
        ],
        out_specs=pl.BlockSpec((rows, NA_HEAD_DIM), lambda h, b: (b, h)),
        out_shape=jax.ShapeDtypeStruct((B * rows, NA_WIDTH), F32),
        compiler_params=_cparams(("arbitrary", "arbitrary")),
        name="na_attention",
    )(naq, nak, navt, bias)


def _na_bias(rpb):
    kh, rows, nq = NA_KH_MAX, 8 * TM // GRID_W, TM // GRID_W
    cq = np.arange(GRID_W)
    ck = np.arange(GRID_W)
    col_start = np.clip(cq - NA_KW // 2, 0, GRID_W - NA_KW)
    col_ok = (ck[:, None] >= col_start[None, :]) & (ck[:, None] < col_start[None, :] + NA_KW)
    dcol = np.clip(ck[:, None] - cq[None, :] + (NA_KW - 1), 0, 2 * NA_KW - 2)
    select = np.zeros((2 * NA_KW - 1, GRID_W * GRID_W), np.float32)
    select[dcol.reshape(-1), np.arange(GRID_W * GRID_W)] = 1.0
    blocks = jnp.einsum("lhdm,mn->lhdn", rpb.astype(F32) * float(np.log2(np.e)), jnp.asarray(select),
                        precision=lax.Precision.HIGHEST)
    blocks = blocks.reshape(rpb.shape[:3] + (GRID_W, GRID_W))
    blocks = jnp.where(jnp.asarray(col_ok), blocks, NEG_INF)
    masked = jnp.full(rpb.shape[:2] + (GRID_W, GRID_W), NEG_INF, F32)
    patterns = []
    for g in (0, 1, 7):
        start_row, _ = _na_chunk(g)
        key_rows = []
        for kr in range(NA_WIN_ROWS):
            key_row = start_row + kr
            row = []
            for qr in range(nq):
                r = 4 * g + qr
                r_start = min(max(r - kh // 2, 0), rows - kh)
                in_rows = r_start <= key_row < r_start + kh
                row.append(blocks[:, :, key_row - r + (kh - 1)] if in_rows else masked)
            key_rows.append(jnp.concatenate(row, axis=-1))
        patterns.append(jnp.concatenate(key_rows, axis=-2))
    return jnp.stack(patterns, axis=2)


def _fnet_kernel(z_ref, csl_ref, csc_ref, cd_ref, w_ref, b_ref, o_ref, ab_ref, *, j0, seq):
    j = pl.program_id(1) + j0

    def small_side(row0, length):
        for g in range(FNET_GROUPS):
            c = g * FNET_GROUP_DIM
            ab = _dot(z_ref[row0:row0 + length, c:c + FNET_GROUP_DIM], cd_ref[...])
            ab_ref[0:length, c:c + FNET_GROUP_DIM] = ab[:, 0:FNET_GROUP_DIM].astype(BF16)
            ab_ref[length:2 * length, c:c + FNET_GROUP_DIM] = ab[:, FNET_GROUP_DIM:].astype(BF16)

    def long_side(cs, length):
        f = _dot(cs, ab_ref[0:2 * length, :]) * (length * FNET_GROUP_DIM) ** -0.5
        o_ref[...] = _dot(f.astype(BF16), w_ref[...]) + b_ref[...]

    if j0 == 0:
        @pl.when(j == 0)
        def _():
            small_side(0, CTX_LEN)
            long_side(csc_ref[...], CTX_LEN)

    @pl.when(j == 1)
    def _():
        small_side(CTX_LEN, seq)

    @pl.when(j >= 1)
    def _():
        long_side(csl_ref[...], seq)


def _fnet(zf, cs_lat, cs_ctx, cd, w_fnet, b_fnet, B, j0):
    seq = 8 * TM
    return pl.pallas_call(
        functools.partial(_fnet_kernel, j0=j0, seq=seq),
        grid=(B, 9 - j0),
        in_specs=[
            pl.BlockSpec((9 * TM, FNET_WIDTH), lambda b, j: (b, 0)),
            pl.BlockSpec((TM, 2 * seq), lambda b, j: (jnp.maximum(j0 + j - 1, 0), 0)),
            pl.BlockSpec((CTX_LEN, 2 * CTX_LEN), lambda b, j: (0, 0)),
            pl.BlockSpec((FNET_GROUP_DIM, 2 * FNET_GROUP_DIM), lambda b, j: (0, 0)),
            pl.BlockSpec((FNET_WIDTH, FNET_WIDTH), lambda b, j: (0, 0)),
            pl.BlockSpec((1, FNET_WIDTH), lambda b, j: (0, 0)),
        ],
        out_specs=pl.BlockSpec((TM, FNET_WIDTH), lambda b, j: (_otile(b, j, j0), 0)),
        out_shape=jax.ShapeDtypeStruct((B * (9 - j0) * TM, FNET_WIDTH), F32),
        scratch_shapes=[pltpu.VMEM((2 * seq, FNET_WIDTH), BF16)],
        compiler_params=_cparams(("arbitrary", "arbitrary")),
        name="fnet",
    )(zf, cs_lat, cs_ctx, cd, w_fnet, b_fnet.reshape(1, -1))


def _dft_cos_sin(n):
    j = jnp.arange(n, dtype=jnp.int32)[:, None]
    if n <= 64:
        ang = ((j * j.T) % n).astype(F32) * (2.0 * np.pi / n)
        return jnp.cos(ang), jnp.sin(ang)
    k1 = jnp.arange(n // 64, dtype=jnp.int32)[None, :]
    k0 = jnp.arange(64, dtype=jnp.int32)[None, :]
    a = ((j * k1 * 64) % n).astype(F32) * (2.0 * np.pi / n)
    b = ((j * k0) % n).astype(F32) * (2.0 * np.pi / n)
    ca, sa, cb, sb = jnp.cos(a), jnp.sin(a), jnp.cos(b), jnp.sin(b)
    c = ca[:, :, None] * cb[:, None, :] - sa[:, :, None] * sb[:, None, :]
    s = sa[:, :, None] * cb[:, None, :] + ca[:, :, None] * sb[:, None, :]
    return c.reshape(n, n), s.reshape(n, n)


def _dft_tables(n):
    c, s = _dft_cos_sin(n)
    return jnp.concatenate([c, s], axis=1).astype(BF16)


def _merge_kernel(*refs, n_stream, j0):
    x = _stream_tile(refs[:n_stream], j0)
    (om_ref, of_ref, on_ref, ga_ref, shf_ref, scf_ref, gout_ref, gffn_ref, wout_ref, wr_ref, br_ref,
     xn_ref, hf_ref, rw_ref, ri_ref, cnt_ref, run_ref) = refs[n_stream:]
    ym = _rms(om_ref[...], gout_ref[:, 0:MLA_WIDTH]).astype(BF16)
    yf = _rms(of_ref[...], gout_ref[:, MLA_WIDTH:MLA_WIDTH + FNET_WIDTH]).astype(BF16)
    yn = _rms(on_ref[...], gout_ref[:, MLA_WIDTH + FNET_WIDTH:]).astype(BF16)
    acc = _dot(ym, wout_ref[0, 0:MLA_WIDTH, :])
    acc = acc + _dot(yf, wout_ref[0, MLA_WIDTH:MLA_WIDTH + FNET_WIDTH, :])
    acc = acc + _dot(yn, wout_ref[0, MLA_WIDTH + FNET_WIDTH:, :])
    xn = x + ga_ref[0] * acc
    xn_ref[...] = xn
    hf = _rms(xn, gffn_ref[...]) * (1.0 + scf_ref[0]) + shf_ref[0]
    hf_ref[...] = hf
    hi = hf.astype(BF16)
    lo = (hf - hi.astype(F32)).astype(BF16)
    a = _dot(hi, wr_ref[0])
    b = _dot(lo, wr_ref[0])
    small = a[:, ROUTER_COLS:] + (b[:, :ROUTER_COLS] + b[:, ROUTER_COLS:])
    _route_tile(a[:, :ROUTER_COLS] + small + br_ref[0], rw_ref, ri_ref, cnt_ref, run_ref)


def _route_tile(lg, rw_ref, ri_ref, cnt_ref, run_ref):
    first = jnp.logical_and(pl.program_id(0) == 0, pl.program_id(1) == 0)

    @pl.when(first)
    def _():
        run_ref[...] = jnp.zeros_like(run_ref)

    lane = lax.broadcasted_iota(jnp.int32, lg.shape, 1)
    neg = jnp.float32(-jnp.inf)

    def top(v):
        vmax = jnp.max(v, axis=1, keepdims=True)
        idx = jnp.min(jnp.where(v == vmax, lane, ROUTER_COLS), axis=1, keepdims=True)
        return vmax, idx

    in_groups = lane < N_GROUPS
    gl = jnp.where(in_groups, lg, neg)
    g_max, g_sel = top(gl)
    g_w = 1.0 / jnp.sum(jnp.where(in_groups, jnp.exp(gl - g_max), 0.0), axis=1, keepdims=True)
    e_lo = N_GROUPS + g_sel * EXPERTS_PER_GROUP
    el = jnp.where(jnp.logical_and(lane >= e_lo, lane < e_lo + EXPERTS_PER_GROUP), lg, neg)
    e1_max, i1 = top(el)
    e2_max, i2 = top(jnp.where(lane == i1, neg, el))
    t = jnp.exp(e2_max - e1_max)
    w0 = g_w / (1.0 + t)
    w1 = w0 * t
    rw_ref[...] = jnp.where(lane == 0, w0, jnp.where(lane == 1, w1, 0.0))

    row = lax.broadcasted_iota(jnp.int32, (TM, TM), 0)
    col = lax.broadcasted_iota(jnp.int32, (TM, TM), 1)
    tri = jnp.where(row >= col, 1.0, 0.0).astype(BF16)
    hot0 = lane == i1
    hot1 = lane == i2
    c0 = _dot(tri, jnp.where(hot0, 1.0, 0.0).astype(BF16))
    c1 = _dot(tri, jnp.where(hot1, 1.0, 0.0).astype(BF16))
    run = run_ref[...]
    tot0 = c0[TM - 1:TM, :]
    rank0 = jnp.sum(jnp.where(hot0, run + c0 - 1.0, 0.0), axis=1, keepdims=True)
    rank1 = jnp.sum(jnp.where(hot1, run + tot0 + c1 - 1.0, 0.0), axis=1, keepdims=True)
    run = run + tot0 + c1[TM - 1:TM, :]
    run_ref[...] = run
    cnt_ref[...] = jnp.broadcast_to(run, cnt_ref.shape).astype(jnp.int32)
    ri_ref[...] = jnp.where(lane == 0, i1 - N_GROUPS, jnp.where(lane == 1, i2 - N_GROUPS, jnp.where(
        lane == 2, rank0.astype(jnp.int32), jnp.where(lane == 3, rank1.astype(jnp.int32), 0))))


def _merge(o_mla, o_f, o_na, xs, mods, g_out, g_ffn, w_out, w_router, b_router, layer, B, j0):
    T = B * (9 - j0) * TM
    row = lambda b, j: (_otile(b, j, j0), 0)
    const = lambda b, j: (0, 0)
    slab = lambda b, j: (layer, 0, 0)
    stream_specs, stream_args = _stream_specs(xs, j0)
    return pl.pallas_call(
        functools.partial(_merge_kernel, n_stream=len(stream_args), j0=j0),
        grid=(B, 9 - j0),
        in_specs=stream_specs + [
            pl.BlockSpec((TM, MLA_WIDTH), row),
            pl.BlockSpec((TM, FNET_WIDTH), row),
            pl.BlockSpec((TM, NA_WIDTH), row),
            _mod_spec(2, j0),
            _mod_spec(3, j0),
            _mod_spec(4, j0),
            pl.BlockSpec((1, D), const),
            pl.BlockSpec((1, D), const),
            pl.BlockSpec((1, D, D), slab),
            pl.BlockSpec((1, D, 2 * ROUTER_COLS), slab),
            pl.BlockSpec((1, 1, ROUTER_COLS), slab),
        ],
        out_specs=[
            pl.BlockSpec((TM, D), row),
            pl.BlockSpec((TM, D), row),
            pl.BlockSpec((TM, ROUTER_COLS), row),
            pl.BlockSpec((TM, ROUTER_COLS), row),
            pl.BlockSpec((8, ROUTER_COLS), const),
        ],
        out_shape=[
            jax.ShapeDtypeStruct((T, D), F32),
            jax.ShapeDtypeStruct((T, D), F32),
            jax.ShapeDtypeStruct((T, ROUTER_COLS), F32),
            jax.ShapeDtypeStruct((T, ROUTER_COLS), jnp.int32),
            jax.ShapeDtypeStruct((8, ROUTER_COLS), jnp.int32),
        ],
        scratch_shapes=[pltpu.VMEM((1, ROUTER_COLS), F32)],
        compiler_params=_cparams(("arbitrary", "arbitrary")),
        name="merge",
    )(*stream_args, o_mla, o_f, o_na, mods, mods, mods, g_out.reshape(1, D), g_ffn.reshape(1, D),
      w_out, w_router, b_router)


def _plan_kernel(cnt_ref, slot_ref, src_ref, tile_ref, exp_ref, lo_ref, hi_ref, flag_ref, nxt_ref, ni_ref,
                 gs_ref, *, n_pairs, max_items):
    def starts(e, acc):
        gs_ref[e] = acc
        return acc + cnt_ref[e]

    lax.fori_loop(0, N_EXPERTS, starts, 0)

    def place(p, c):
        src_ref[slot_ref[p]] = lax.shift_right_logical(p, 1)
        return c

    lax.fori_loop(0, n_pairs, place, 0, unroll=16)

    last = N_EXPERTS - 1

    def group_end(e):
        return gs_ref[e] + cnt_ref[e]

    def next_nonempty(e):
        return lax.while_loop(lambda x: jnp.logical_and(x < last, cnt_ref[jnp.minimum(x, last)] == 0),
                              lambda x: x + 1, e)

    def tile_items(t, carry):
        i, e, prev = carry
        row0 = t * TM
        e = lax.while_loop(lambda x: group_end(x) <= row0, lambda x: x + 1, e)

        def emit(state):
            i, e, prev, first, _ = state
            tile_ref[i] = t
            exp_ref[i] = e
            lo_ref[i] = jnp.clip(gs_ref[e] - row0, 0, TM)
            hi_ref[i] = jnp.clip(group_end(e) - row0, 0, TM)
            flag_ref[i] = first + 2 * (e != prev).astype(jnp.int32)
            done = group_end(e) >= row0 + TM
            e_next = jnp.where(done, e, next_nonempty(e + 1))
            return i + 1, e_next, e, jnp.int32(0), done

        i, e, prev, _, _ = lax.while_loop(lambda s: jnp.logical_not(s[4]), emit,
                                          (i, e, prev, jnp.int32(1), jnp.bool_(False)))
        return i, e, prev

    n_items, _, _ = lax.fori_loop(0, n_pairs // TM, tile_items,
                                  (jnp.int32(0), jnp.int32(0), jnp.int32(-1)))
    ni_ref[0] = n_items

    def pad(i, c):
        tile_ref[i] = tile_ref[n_items - 1]
        exp_ref[i] = exp_ref[n_items - 1]
        lo_ref[i] = 0
        hi_ref[i] = 0
        flag_ref[i] = 0
        nxt_ref[i] = -1
        return c

    lax.fori_loop(n_items, max_items, pad, 0)

    def parity(i, par):
        par = jnp.where((flag_ref[i] & 2) != 0, 1 - par, par)
        flag_ref[i] = flag_ref[i] + 4 * par
        return par

    lax.fori_loop(0, n_items, parity, jnp.int32(1))

    def lookahead(k, following):
        i = n_items - 1 - k
        nxt_ref[i] = following
        return jnp.where((flag_ref[i] & 2) != 0, exp_ref[i], following)

    lax.fori_loop(0, n_items, lookahead, jnp.int32(-1))


def _plan(counts, eid, rank):
    n_pairs = eid.shape[0]
    max_items = n_pairs // TM + N_EXPERTS - 1
    g_start = jnp.cumsum(counts) - counts
    experts = jnp.arange(N_EXPERTS, dtype=jnp.int32)
    slot = jnp.sum(jnp.where(eid[:, None] == experts[None, :], g_start[None, :], 0), axis=1) + rank
    smem = pl.BlockSpec(memory_space=pltpu.SMEM)
    i32 = lambda n: jax.ShapeDtypeStruct((n,), jnp.int32)
    src, it_tile, it_exp, it_lo, it_hi, flags, nxt, n_items = pl.pallas_call(
        functools.partial(_plan_kernel, n_pairs=n_pairs, max_items=max_items),
        in_specs=[smem] * 2,
        out_specs=[smem] * 8,
        out_shape=[i32(n_pairs)] + [i32(max_items)] * 6 + [i32(1)],
        scratch_shapes=[pltpu.SMEM((N_EXPERTS,), jnp.int32)],
        name="moe_plan",
    )(counts, slot)
    return (it_tile, it_exp, it_lo, it_hi, flags, nxt, n_items, src), slot


def _moe_kernel(tile_ref, exp_ref, lo_ref, hi_ref, flag_ref, nxt_ref, ni_ref, src_ref,
                hf_hbm, wg_hbm, wu_hbm, wd_hbm, y_ref, xbuf, wg_buf, wu_buf, wd_buf, wgb, wub, wdb,
                sem, wsem, *, n_tiles, e0):
    i = pl.program_id(0)
    t = tile_ref[i]
    slot = lax.rem(t, 2)

    def gather_start(tile, buf, unrolled):
        base = tile * TM

        def issue(r):
            tok = src_ref[base + r]
            pltpu.make_async_copy(hf_hbm.at[pl.ds(tok, 1)], xbuf.at[buf, pl.ds(r, 1)], sem.at[buf]).start()

        if unrolled:
            for r in range(TM):
                issue(r)
        else:
            lax.fori_loop(0, TM, lambda r, c: (issue(r), c)[1], 0)

    def gather_wait(buf):
        pltpu.make_async_copy(hf_hbm.at[pl.ds(0, TM)], xbuf.at[buf], sem.at[buf]).wait()

    def weight_copies(expert, b):
        e = e0 + expert
        return (pltpu.make_async_copy(wg_hbm.at[e], wg_buf.at[b], wsem.at[b]),
                pltpu.make_async_copy(wu_hbm.at[e], wu_buf.at[b], wsem.at[b]),
                pltpu.make_async_copy(wd_hbm.at[e], wd_buf.at[b], wsem.at[b]))

    def weights_start(expert, b):
        for cp in weight_copies(expert, b):
            cp.start()

    def weights_wait(b):
        for cp in weight_copies(0, b):
            cp.wait()

    @pl.when(i < ni_ref[0])
    def _():
        first_visit = (flag_ref[i] & 1) != 0
        new_expert = (flag_ref[i] & 2) != 0
        wslot = lax.shift_right_logical(flag_ref[i], 2) & 1

        @pl.when(i == 0)
        def _():
            gather_start(0, 0, False)

        @pl.when(first_visit)
        def _():
            gather_wait(slot)

        for b in range(2):
            @pl.when(jnp.logical_and(jnp.logical_and(first_visit, t + 1 < n_tiles), slot == 1 - b))
            def _():
                gather_start(t + 1, b, True)

        @pl.when(i == 0)
        def _():
            weights_start(exp_ref[0], 0)

        @pl.when(new_expert)
        def _():
            @pl.when(nxt_ref[i] >= 0)
            def _():
                weights_start(nxt_ref[i], 1 - wslot)

            weights_wait(wslot)
            wgb[...] = wg_buf[wslot].astype(BF16)
            wub[...] = wu_buf[wslot].astype(BF16)
            wdb[...] = wd_buf[wslot].astype(BF16)

        x = xbuf[slot].astype(BF16)
        a = _dot(x, wgb[...])
        u = _dot(x, wub[...])
        row = lax.broadcasted_iota(jnp.int32, (TM, 1), 0)
        mine = jnp.logical_and(row >= lo_ref[i], row < hi_ref[i])
        h = jnp.where(mine, (a * jax.nn.sigmoid(a)) * u, 0.0)
        yv = _dot(h.astype(BF16), wdb[...])

        @pl.when(first_visit)
        def _():
            y_ref[...] = yv

        @pl.when(jnp.logical_not(first_visit))
        def _():
            y_ref[...] += yv


def _moe(hf, meta, w_gate, w_up, w_down, layer):
    n_rows = meta[-1].shape[0]
    n_tiles = n_rows // TM
    max_items = meta[0].shape[0]
    grid_spec = pltpu.PrefetchScalarGridSpec(
        num_scalar_prefetch=len(meta),
        grid=(max_items,),
        in_specs=[pl.BlockSpec(memory_space=pl.ANY)] * 4,
        out_specs=pl.BlockSpec((TM, D), lambda i, tile, *_: (tile[i], 0)),
        scratch_shapes=[
            pltpu.VMEM((2, TM, D), F32),
            pltpu.VMEM((2, D, D_EXPERT), F32),
            pltpu.VMEM((2, D, D_EXPERT), F32),
            pltpu.VMEM((2, D_EXPERT, D), F32),
            pltpu.VMEM((D, D_EXPERT), BF16),
            pltpu.VMEM((D, D_EXPERT), BF16),
            pltpu.VMEM((D_EXPERT, D), BF16),
            pltpu.SemaphoreType.DMA((2,)),
            pltpu.SemaphoreType.DMA((2,)),
        ],
    )
    return pl.pallas_call(
        functools.partial(_moe_kernel, n_tiles=n_tiles, e0=layer * N_EXPERTS),
        grid_spec=grid_spec,
        out_shape=jax.ShapeDtypeStruct((n_rows, D), F32),
        compiler_params=_cparams(("arbitrary",)),
        name="moe_experts",
    )(*meta, hf, w_gate.reshape(-1, D, D_EXPERT), w_up.reshape(-1, D, D_EXPERT),
      w_down.reshape(-1, D_EXPERT, D))


def _final_kernel(pos_ref, x_ref, gf_ref, w_ref, g_ref, y_hbm, o_ref, ybuf, sem, *, n_steps):
    step = pl.program_id(0) * 8 + pl.program_id(1)
    out = _moe_residual(pos_ref, y_hbm, ybuf, sem, x_ref[...], gf_ref[0], w_ref[...], step, n_steps)
    o_ref[0] = _rms(out, g_ref[...])


def _final(pos, wts, xn, mods, y, g_final, B):
    tile = lambda b, j, p: (b * 8 + j, 0)
    grid_spec = pltpu.PrefetchScalarGridSpec(
        num_scalar_prefetch=1,
        grid=(B, 8),
        in_specs=[
            pl.BlockSpec((TM, D), tile),
            pl.BlockSpec((1, 1, D), lambda b, j, p: (b * 6 + 5, 0, 0)),
            pl.BlockSpec((TM, 2), tile),
            pl.BlockSpec((1, D), lambda b, j, p: (0, 0)),
            pl.BlockSpec(memory_space=pl.ANY),
        ],
        out_specs=pl.BlockSpec((1, TM, D), lambda b, j, p: (b, j, 0)),
        scratch_shapes=_MOE_GATHER_SCRATCH,
    )
    return pl.pallas_call(
        functools.partial(_final_kernel, n_steps=B * 8),
        grid_spec=grid_spec,
        out_shape=jax.ShapeDtypeStruct((B, 8 * TM, D), F32),
        compiler_params=_cparams(("arbitrary", "arbitrary")),
        name="final",
    )(pos, xn, mods, wts, g_final.reshape(1, D), y)


def _prep_w_in(w_in):
    return jnp.swapaxes(w_in, 1, 2).astype(BF16)


def _prep_w_uq(w_uq):
    w = w_uq.reshape(Q_LORA, MLA_HEADS, QK_NOPE + QK_ROPE)
    w = jnp.pad(w, ((0, 0), (0, 0), (0, 256 - QK_NOPE - QK_ROPE)))
    return w.reshape(Q_LORA, MLA_HEADS * 256).astype(BF16)


def _rope_tables(seq):
    half = QK_ROPE // 2
    inv_freq = ROPE_THETA ** (-jnp.arange(0, half, 2, dtype=F32) / half)
    t = jnp.arange(seq, dtype=jnp.int32)
    row = (t // GRID_W).astype(F32)
    col = (t % GRID_W).astype(F32)
    ang = jnp.concatenate([row[:, None] * inv_freq, col[:, None] * inv_freq], axis=-1)
    cos, sin = jnp.cos(ang), jnp.sin(ang)
    zeros = jnp.zeros((seq, 64), F32)
    cos_l = jnp.concatenate([jnp.repeat(cos, 2, axis=1), zeros], axis=1)
    sin_l = jnp.concatenate([jnp.stack([-sin, sin], axis=-1).reshape(seq, 64), zeros], axis=1)
    cos_c = jnp.concatenate([jnp.ones((CTX_LEN, 64), F32), jnp.zeros((CTX_LEN, 64), F32)], axis=1)
    sin_c = jnp.zeros((CTX_LEN, 128), F32)
    return jnp.concatenate([cos_c, cos_l], axis=0), jnp.concatenate([sin_c, sin_l], axis=0)


def kernel(x, c, ctx, c_ctx, w_ada, b_ada, g_attn, g_ffn, w_in, g_q, w_uq, g_kv, w_ukv, w_fnet, b_fnet,
           na_rpb, g_out, w_out, w_rg, b_rg, w_re, b_re, w_gate, w_up, w_down, g_final):
    B, S, _ = x.shape
    L = w_ada.shape[0]
    assert ctx.shape[1] == CTX_LEN == TM and S == 8 * TM and B <= 4
    T = B * 9 * TM

    cond8 = jnp.concatenate([c, jnp.zeros((4 - B, D), F32), c_ctx[None], jnp.zeros((3, D), F32)], axis=0)
    mods_all = _modulation(cond8, w_ada, b_ada)
    w_in_ext = _prep_w_in(w_in)
    pad = ROUTER_COLS - N_GROUPS - N_EXPERTS
    w_router = jnp.concatenate([w_rg, w_re, jnp.zeros((L, D, pad), F32)], axis=2)
    w_scaled = w_router * 65537.0
    w_router_hi = w_scaled - (w_scaled - w_router)
    w_router_all = jnp.concatenate([w_router_hi, w_router - w_router_hi], axis=2).astype(BF16)
    b_router_all = jnp.concatenate([b_rg, b_re, jnp.zeros((L, pad), F32)], axis=1).reshape(L, 1, ROUTER_COLS)
    w_out_all = w_out.astype(BF16)
    na_bias = _na_bias(na_rpb)
    cos_t, sin_t = _rope_tables(S)
    cs_lat = _dft_tables(S)
    cs_ctx = _dft_tables(CTX_LEN)
    cd_c, cd_s = _dft_cos_sin(FNET_GROUP_DIM)
    cd = jnp.concatenate([cd_c, -cd_s], axis=1).astype(BF16)

    xs = (ctx.reshape(B * CTX_LEN, D), x.reshape(B * S, D))
    pending = None
    for l in range(L):
        last = l == L - 1
        j0 = 1 if last else 0
        mods = mods_all[l].reshape(48, 1, D)
        w_uq_ext = _prep_w_uq(w_uq[l])
        mla = (g_q[l], g_kv[l], w_uq_ext, w_ukv[l].astype(BF16), cos_t, sin_t)
        if pending is None:
            zf, naq, nak, navt, q, k, vt = _in_projection(xs, mods, g_attn[l], w_in_ext, l, mla, B)
        else:
            zf, naq, nak, navt, q, k, vt, xs = _in_projection(xs, mods, g_attn[l], w_in_ext, l, mla, B,
                                                              moe=pending)
        o_mla = _mla_attention(q, k, vt, B, j0)
        o_na = _na_attention(naq, nak, navt, na_bias, l, B, j0)
        w_f = w_fnet[l].astype(BF16)
        o_f = _fnet(zf, cs_lat, cs_ctx, cd, w_f, b_fnet[l], B, j0)
        xn, hf, route_w, route_i, counts = _merge(o_mla, o_f, o_na, xs, mods, g_out[l], g_ffn[l],
                                                  w_out_all, w_router_all, b_router_all, l, B, j0)
        wts = route_w[:, 0:2]
        meta, slot = _plan(counts[0, N_GROUPS:N_GROUPS + N_EXPERTS], route_i[:, 0:2].reshape(-1),
                           route_i[:, 2:4].reshape(-1))
        y = _moe(hf, meta, w_gate, w_up, w_down, l)
        if last:
            return _final(slot, wts, xn, mods, y, g_final, B)
        xs, pending = xn, (slot, wts, y, mods)
```

```python
import functools

import numpy as np
import jax
import jax.numpy as jnp
from jax import lax
from jax.experimental import pallas as pl
from jax.experimental.pallas import tpu as pltpu

F32 = jnp.float32
BF16 = jnp.bfloat16

D = 2048
GRID_W = 64
CTX_LEN = 256
EPS = 1e-6
NEG_INF = -1e30
ROPE_THETA = 10000.0

V_DIM = 128
MLA_WIDTH = D // 2
MLA_HEADS = MLA_WIDTH // V_DIM
QK_NOPE = 128
QK_ROPE = 64
Q_LORA = D // 4
KV_LORA = D // 8
FNET_WIDTH = D // 4
FNET_GROUP_DIM = 128
FNET_GROUPS = FNET_WIDTH // FNET_GROUP_DIM
NA_WIDTH = D // 4
NA_HEAD_DIM = 128
NA_HEADS = NA_WIDTH // NA_HEAD_DIM
NA_KH_MAX = 8
NA_KW = 16
N_GROUPS = 4
EXPERTS_PER_GROUP = 8
N_EXPERTS = N_GROUPS * EXPERTS_PER_GROUP
D_EXPERT = D // 4

MLA_QSCALE = (QK_NOPE + QK_ROPE) ** -0.5 * float(np.log2(np.e))
MLA_CHUNK_TILES = 8
NA_QSCALE = NA_HEAD_DIM ** -0.5 * float(np.log2(np.e))
NA_WIN_ROWS = 12
TM = 256
IN_COLS = Q_LORA + KV_LORA + QK_ROPE + FNET_WIDTH + 3 * NA_WIDTH
ROUTER_COLS = 128
VMEM_LIMIT = 56 * 1024 * 1024


def _cparams(sem):
    return pltpu.CompilerParams(dimension_semantics=sem, vmem_limit_bytes=VMEM_LIMIT)


def _rms(v, g):
    return v * lax.rsqrt(jnp.mean(v * v, axis=-1, keepdims=True) + EPS) * g


def _dot(a, b):
    return jnp.dot(a, b, preferred_element_type=F32)


def _dot_nt(a, b):
    return lax.dot_general(a, b, (((1,), (1,)), ((), ())), preferred_element_type=F32)


def _mod_kernel(c_ref, w_ref, b_ref, o_ref):
    c = c_ref[...]
    s = c * jax.nn.sigmoid(c)
    o_ref[0] = _dot(s.astype(BF16), w_ref[0].astype(BF16)) + b_ref[0]


def _modulation(cond8, w_ada, b_ada):
    L = w_ada.shape[0]
    tn = 1024
    return pl.pallas_call(
        _mod_kernel,
        grid=(L, 6 * D // tn),
        in_specs=[
            pl.BlockSpec((8, D), lambda l, n: (0, 0)),
            pl.BlockSpec((1, D, tn), lambda l, n: (l, 0, n)),
            pl.BlockSpec((1, 1, tn), lambda l, n: (l, 0, n)),
        ],
        out_specs=pl.BlockSpec((1, 8, tn), lambda l, n: (l, 0, n)),
        out_shape=jax.ShapeDtypeStruct((L, 8, 6 * D), F32),
        compiler_params=_cparams(("arbitrary", "arbitrary")),
        name="modulation",
    )(cond8, w_ada, b_ada.reshape(L, 1, 6 * D))


def _tile_of(b, j, j0):
    return b * 9 + j0 + j


def _otile(b, j, j0):
    return b * (9 - j0) + j


def _mod_row(j, b, j0):
    return jnp.where(j0 + j == 0, 4, b)


def _mod_spec(k, j0):
    return pl.BlockSpec((1, 1, D), lambda b, j: (_mod_row(j, b, j0) * 6 + k, 0, 0))


def _stream_specs(xs, j0):
    if isinstance(xs, tuple):
        return [pl.BlockSpec((TM, D), lambda b, j, *_: (b, 0)),
                pl.BlockSpec((TM, D), lambda b, j, *_: (b * 8 + jnp.maximum(j0 + j - 1, 0), 0))], list(xs)
    return [pl.BlockSpec((TM, D), lambda b, j, *_: (_tile_of(b, j, j0), 0))], [xs]


def _stream_tile(refs, j0):
    if len(refs) == 1:
        return refs[0][...]
    return jnp.where(pl.program_id(1) + j0 == 0, refs[0][...], refs[1][...])


def _moe_residual(pos_ref, y_hbm, ybuf, sem, x, gate, w, step, n_steps):
    buf = lax.rem(step, 2)

    def start(tile, b, unrolled):
        base = tile * TM

        def issue(r):
            for k in range(2):
                pltpu.make_async_copy(y_hbm.at[pl.ds(pos_ref[2 * (base + r) + k], 1)],
                                      ybuf.at[b, k, pl.ds(r, 1)], sem.at[b]).start(priority=k)

        if unrolled:
            for r in range(TM):
                issue(r)
        else:
            lax.fori_loop(0, TM, lambda r, c: (issue(r), c)[1], 0)

    def wait(b):
        for k in range(2):
            pltpu.make_async_copy(y_hbm.at[pl.ds(0, TM)], ybuf.at[b, k], sem.at[b]).wait()

    @pl.when(step == 0)
    def _():
        start(0, 0, False)

    for b in range(2):
        @pl.when(buf == 1 - b)
        def _():
            start(jnp.minimum(step + 1, n_steps - 1), b, True)

    wait(buf)
    out = x + gate * (w[:, 0:1] * ybuf[buf, 0] + w[:, 1:2] * ybuf[buf, 1])

    @pl.when(step == n_steps - 1)
    def _():
        wait(1 - buf)

    return out


_MOE_GATHER_SCRATCH = [pltpu.VMEM((2, 2, TM, D), F32), pltpu.SemaphoreType.DMA((2,))]


def _rope(r, cos_t, sin_t):
    lane = lax.broadcasted_iota(jnp.int32, r.shape, 1)
    partner = jnp.where(lane % 2 == 0, pltpu.roll(r, 127, 1), pltpu.roll(r, 1, 1))
    return r * cos_t + partner * sin_t


def _inproj_body(x, sh_ref, sc_ref, g_ref, w_ref, gq_ref, gkv_ref, wq_ref, wkv_ref, cos_ref, sin_ref,
                 zf_ref, naq_ref, nak_ref, navt_ref, q_ref, k_ref, vt_ref):
    h = _rms(x, g_ref[...])
    h = h * (1.0 + sc_ref[0]) + sh_ref[0]
    z = _dot_nt(h.astype(BF16), w_ref[0])
    c_zf =Q_LORA + KV_LORA + QK_ROPE
    zf_ref[...] = z[:, c_zf:c_zf + FNET_WIDTH].astype(BF16)
    c_na = c_zf + FNET_WIDTH
    naq_ref[...] = (z[:, c_na:c_na + NA_WIDTH] * NA_QSCALE).astype(BF16)
    nak_ref[...] = z[:, c_na + NA_WIDTH:c_na + 2 * NA_WIDTH].astype(BF16)
    for hd in range(NA_HEADS):
        c = c_na + 2 * NA_WIDTH + hd * NA_HEAD_DIM
        navt_ref[hd * NA_HEAD_DIM:(hd + 1) * NA_HEAD_DIM, :] = z[:, c:c + NA_HEAD_DIM].T.astype(BF16)
    cos_t = cos_ref[...]
    sin_t = sin_ref[...]
    q = _dot(_rms(z[:, 0:Q_LORA], gq_ref[...]).astype(BF16), wq_ref[...]) * MLA_QSCALE
    kv = _dot(_rms(z[:, Q_LORA:Q_LORA + KV_LORA], gkv_ref[...]).astype(BF16), wkv_ref[...])
    k_rope = _rope(z[:, Q_LORA + KV_LORA:Q_LORA + KV_LORA + 128], cos_t, sin_t).astype(BF16)
    for hd in range(MLA_HEADS):
        c = hd * 256
        q_ref[:, c:c + 128] = q[:, c:c + 128].astype(BF16)
        q_ref[:, c + 128:c + 256] = _rope(q[:, c + 128:c + 256], cos_t, sin_t).astype(BF16)
        k_ref[:, c:c + 128] = kv[:, c:c + 128].astype(BF16)
        k_ref[:, c + 128:c + 256] = k_rope
        vt_ref[hd * 128:(hd + 1) * 128, :] = kv[:, c + 128:c + 256].T.astype(BF16)


def _inproj_kernel(*refs, n_stream):
    _inproj_body(_stream_tile(refs[:n_stream], 0), *refs[n_stream:])


def _inproj_moe_kernel(pos_ref, x_ref, gf_ref, wts_ref, y_hbm, *refs, n_steps):
    *refs, xs_ref, ybuf, sem = refs
    step = pl.program_id(0) * 9 + pl.program_id(1)
    x = _moe_residual(pos_ref, y_hbm, ybuf, sem, x_ref[...], gf_ref[0], wts_ref[...], step, n_steps)
    xs_ref[...] = x
    _inproj_body(x, *refs)


def _in_projection(xs, mods, g_attn, w_in_ext, layer, mla, B, moe=None):
    T = B * 9 * TM
    row = lambda b, j, *_: (_tile_of(b, j, 0), 0)
    col = lambda b, j, *_: (0, _tile_of(b, j, 0))
    const = lambda b, j, *_: (0, 0)
    mod = lambda k: pl.BlockSpec((1, 1, D), lambda b, j, *_: (_mod_row(j, b, 0) * 6 + k, 0, 0))
    once = dict(pipeline_mode=pl.Buffered(1))
    g_q, g_kv, w_uq_ext, w_ukv, cos_t, sin_t = mla
    in_specs = [
        pl.BlockSpec((TM, D), row),
        mod(0),
        mod(1),
        pl.BlockSpec((1, D), const),
        pl.BlockSpec((1, IN_COLS, D), lambda b, j, *_: (layer, 0, 0), **once),
        pl.BlockSpec((1, Q_LORA), const),
        pl.BlockSpec((1, KV_LORA), const),
        pl.BlockSpec((Q_LORA, MLA_HEADS * 256), const, **once),
        pl.BlockSpec((KV_LORA, MLA_HEADS * 256), const, **once),
        pl.BlockSpec((TM, 128), lambda b, j, *_: (j, 0)),
        pl.BlockSpec((TM, 128), lambda b, j, *_: (j, 0)),
    ]
    out_specs = [
        pl.BlockSpec((TM, FNET_WIDTH), row),
        pl.BlockSpec((TM, NA_WIDTH), row),
        pl.BlockSpec((TM, NA_WIDTH), row),
        pl.BlockSpec((NA_WIDTH, TM), col),
        pl.BlockSpec((TM, MLA_HEADS * 256), row),
        pl.BlockSpec((TM, MLA_HEADS * 256), row),
        pl.BlockSpec((MLA_WIDTH, TM), col),
    ]
    out_shape = [
        jax.ShapeDtypeStruct((T, FNET_WIDTH), BF16),
        jax.ShapeDtypeStruct((T, NA_WIDTH), BF16),
        jax.ShapeDtypeStruct((T, NA_WIDTH), BF16),
        jax.ShapeDtypeStruct((NA_WIDTH, T), BF16),
        jax.ShapeDtypeStruct((T, MLA_HEADS * 256), BF16),
        jax.ShapeDtypeStruct((T, MLA_HEADS * 256), BF16),
        jax.ShapeDtypeStruct((MLA_WIDTH, T), BF16),
    ]
    args = [xs, mods, mods, g_attn.reshape(1, D), w_in_ext, g_q.reshape(1, -1), g_kv.reshape(1, -1),
            w_uq_ext, w_ukv, cos_t, sin_t]
    if moe is None:
        stream_specs, stream_args = _stream_specs(xs, 0)
        return pl.pallas_call(
            functools.partial(_inproj_kernel, n_stream=len(stream_args)),
            grid=(B, 9),
            in_specs=stream_specs + in_specs[1:],
            out_specs=out_specs,
            out_shape=out_shape,
            compiler_params=_cparams(("arbitrary", "arbitrary")),
            name="in_projection",
        )(*stream_args, *args[1:])
    pos, wts, y, mods_prev = moe
    in_specs = [in_specs[0], mod(5), pl.BlockSpec((TM, 2), row), pl.BlockSpec(memory_space=pl.ANY)] \
        + in_specs[1:]
    grid_spec = pltpu.PrefetchScalarGridSpec(
        num_scalar_prefetch=1,
        grid=(B, 9),
        in_specs=in_specs,
        out_specs=out_specs + [pl.BlockSpec((TM, D), row)],
        scratch_shapes=_MOE_GATHER_SCRATCH,
    )
    return pl.pallas_call(
        functools.partial(_inproj_moe_kernel, n_steps=B * 9),
        grid_spec=grid_spec,
        out_shape=out_shape + [jax.ShapeDtypeStruct((T, D), F32)],
        compiler_params=_cparams(("arbitrary", "arbitrary")),
        name="in_projection_moe",
    )(pos, xs, mods_prev, wts, y, *args[1:])


def _softmax_numerator(st, m):
    return jnp.exp2(st - m).astype(BF16)


def _pv_with_sum(vt, p):
    ones = jnp.ones((2 * 8, vt.shape[1]), BF16)
    o = _dot(jnp.concatenate([vt, ones], axis=0), p)
    return o[0:vt.shape[0]], o[vt.shape[0]:vt.shape[0] + 1]


def _mla_attn_kernel(q_ref, k_ref, vt_ref, o_ref, *, j0):
    def attend(q0, nq, nk, o0):
        st = _dot_nt(k_ref[0:nk, :], q_ref[q0:q0 + nq, :])
        m = jnp.max(st, axis=0, keepdims=True)
        ot, l = _pv_with_sum(vt_ref[:, 0:nk], _softmax_numerator(st, m))
        o_ref[o0:o0 + nq, :] = (ot / l).T

    if j0 == 0:
        attend(0, CTX_LEN, CTX_LEN, 0)
    for c in range(8 // MLA_CHUNK_TILES):
        attend(CTX_LEN + MLA_CHUNK_TILES * c * TM, MLA_CHUNK_TILES * TM, 9 * TM,
               (1 - j0 + MLA_CHUNK_TILES * c) * TM)


def _mla_attention(q, k, vt, B, j0):
    rows = (9 - j0) * TM
    return pl.pallas_call(
        functools.partial(_mla_attn_kernel, j0=j0),
        grid=(B, MLA_HEADS),
        in_specs=[
            pl.BlockSpec((9 * TM, 256), lambda b, h: (b, h)),
            pl.BlockSpec((9 * TM, 256), lambda b, h: (b, h)),
            pl.BlockSpec((V_DIM, 9 * TM), lambda b, h: (h, b)),
        ],
        out_specs=pl.BlockSpec((rows, V_DIM), lambda b, h: (b, h)),
        out_shape=jax.ShapeDtypeStruct((B * rows, MLA_WIDTH), F32),
        compiler_params=_cparams(("arbitrary", "arbitrary")),
        name="mla_attention",
    )(q, k, vt)


def _na_chunk(g):
    start_row = min(max(4 * g - 4, 0), 8 * TM // GRID_W - NA_WIN_ROWS)
    pattern = 0 if g == 0 else (2 if g == 7 else 1)
    return start_row, pattern


def _na_kernel(q_ref, k_ref, vt_ref, bias_ref, o_ref, *, j0):
    def finish(parts, o0):
        m = None
        for st, _ in parts:
            pm = jnp.max(st, axis=0, keepdims=True)
            m = pm if m is None else jnp.maximum(m, pm)
        l = None
        ot = None
        for st, vt in parts:
            po, pl_sum = _pv_with_sum(vt, _softmax_numerator(st, m))
            l = pl_sum if l is None else l + pl_sum
            ot = po if ot is None else ot + po
        o_ref[o0:o0 + TM, :] = (ot / l).T

    if j0 == 0:
        st = _dot_nt(k_ref[0:CTX_LEN, :], q_ref[0:CTX_LEN, :])
        finish([(st, vt_ref[:, 0:CTX_LEN])], 0)
    for g in range(8):
        start_row, pattern = _na_chunk(g)
        k0 = CTX_LEN + start_row * GRID_W
        nk = NA_WIN_ROWS * GRID_W
        q = q_ref[CTX_LEN + g * TM:CTX_LEN + (g + 1) * TM, :]
        st_loc = _dot_nt(k_ref[k0:k0 + nk, :], q) + bias_ref[0, pattern]
        st_ctx = _dot_nt(k_ref[0:CTX_LEN, :], q)
        finish([(st_loc, vt_ref[:, k0:k0 + nk]), (st_ctx, vt_ref[:, 0:CTX_LEN])], (1 - j0 + g) * TM)


def _na_attention(naq, nak, navt, bias, layer, B, j0):
    rows = (9 - j0) * TM
    bias = bias.reshape((-1,) + bias.shape[2:])
    return pl.pallas_call(
        functools.partial(_na_kernel, j0=j0),
        grid=(NA_HEADS, B),
        in_specs=[
            pl.BlockSpec((9 * TM, NA_HEAD_DIM), lambda h, b: (b, h)),
            pl.BlockSpec((9 * TM, NA_HEAD_DIM), lambda h, b: (b, h)),
            pl.BlockSpec((NA_HEAD_DIM, 9 * TM), lambda h, b: (h, b)),
            pl.BlockSpec((1, 3, NA_WIN_ROWS * GRID_W, TM), lambda h, b: (layer * NA_HEADS + h, 0, 0, 0)),
        ],
        out_specs=pl.BlockSpec((rows, NA_HEAD_DIM), lambda h, b: (b, h)),
        out_shape=jax.ShapeDtypeStruct((B * rows, NA_WIDTH), F32),
        compiler_params=_cparams(("arbitrary", "arbitrary")),
        name="na_attention",
    )(naq, nak, navt, bias)


def _na_bias(rpb):
    kh, rows, nq = NA_KH_MAX, 8 * TM // GRID_W, TM // GRID_W
    cq = np.arange(GRID_W)
    ck = np.arange(GRID_W)
    col_start = np.clip(cq - NA_KW // 2, 0, GRID_W - NA_KW)
    col_ok = (ck[:, None] >= col_start[None, :]) & (ck[:, None] < col_start[None, :] + NA_KW)
    dcol = np.clip(ck[:, None] - cq[None, :] + (NA_KW - 1), 0, 2 * NA_KW - 2)
    select = np.zeros((2 * NA_KW - 1, GRID_W * GRID_W), np.float32)
    select[dcol.reshape(-1), np.arange(GRID_W * GRID_W)] = 1.0
    blocks = jnp.einsum("lhdm,mn->lhdn", rpb.astype(F32) * float(np.log2(np.e)), jnp.asarray(select),
                        precision=lax.Precision.HIGHEST)
    blocks = blocks.reshape(rpb.shape[:3] + (GRID_W, GRID_W))
    blocks = jnp.where(jnp.asarray(col_ok), blocks, NEG_INF)
    masked = jnp.full(rpb.shape[:2] + (GRID_W, GRID_W), NEG_INF, F32)
    patterns = []
    for g in (0, 1, 7):
        start_row, _ = _na_chunk(g)
        key_rows = []
        for kr in range(NA_WIN_ROWS):
            key_row = start_row + kr
            row = []
            for qr in range(nq):
                r = 4 * g + qr
                r_start = min(max(r - kh // 2, 0), rows - kh)
                in_rows = r_start <= key_row < r_start + kh
                row.append(blocks[:, :, key_row - r + (kh - 1)] if in_rows else masked)
            key_rows.append(jnp.concatenate(row, axis=-1))
        patterns.append(jnp.concatenate(key_rows, axis=-2))
    return jnp.stack(patterns, axis=2)


def _fnet_kernel(z_ref, csl_ref, csc_ref, cd_ref, w_ref, b_ref, o_ref, ab_ref, *, j0, seq):
    j = pl.program_id(1) + j0

    def small_side(row0, length):
        for g in range(FNET_GROUPS):
            c = g * FNET_GROUP_DIM
            ab = _dot(z_ref[row0:row0 + length, c:c + FNET_GROUP_DIM], cd_ref[...])
            ab_ref[0:length, c:c + FNET_GROUP_DIM] = ab[:, 0:FNET_GROUP_DIM].astype(BF16)
            ab_ref[length:2 * length, c:c + FNET_GROUP_DIM] = ab[:, FNET_GROUP_DIM:].astype(BF16)

    def long_side(cs, length):
        f = _dot(cs, ab_ref[0:2 * length, :]) * (length * FNET_GROUP_DIM) ** -0.5
        o_ref[...] = _dot(f.astype(BF16), w_ref[...]) + b_ref[...]

    if j0 == 0:
        @pl.when(j == 0)
        def _():
            small_side(0, CTX_LEN)
            long_side(csc_ref[...], CTX_LEN)

    @pl.when(j == 1)
    def _():
        small_side(CTX_LEN, seq)

    @pl.when(j >= 1)
    def _():
        long_side(csl_ref[...], seq)


def _fnet(zf, cs_lat, cs_ctx, cd, w_fnet, b_fnet, B, j0):
    seq = 8 * TM
    return pl.pallas_call(
        functools.partial(_fnet_kernel, j0=j0, seq=seq),
        grid=(B, 9 - j0),
        in_specs=[
            pl.BlockSpec((9 * TM, FNET_WIDTH), lambda b, j: (b, 0)),
            pl.BlockSpec((TM, 2 * seq), lambda b, j: (jnp.maximum(j0 + j - 1, 0), 0)),
            pl.BlockSpec((CTX_LEN, 2 * CTX_LEN), lambda b, j: (0, 0)),
            pl.BlockSpec((FNET_GROUP_DIM, 2 * FNET_GROUP_DIM), lambda b, j: (0, 0)),
            pl.BlockSpec((FNET_WIDTH, FNET_WIDTH), lambda b, j: (0, 0)),
            pl.BlockSpec((1, FNET_WIDTH), lambda b, j: (0, 0)),
        ],
        out_specs=pl.BlockSpec((TM, FNET_WIDTH), lambda b, j: (_otile(b, j, j0), 0)),
        out_shape=jax.ShapeDtypeStruct((B * (9 - j0) * TM, FNET_WIDTH), F32),
        scratch_shapes=[pltpu.VMEM((2 * seq, FNET_WIDTH), BF16)],
        compiler_params=_cparams(("arbitrary", "arbitrary")),
        name="fnet",
    )(zf, cs_lat, cs_ctx, cd, w_fnet, b_fnet.reshape(1, -1))


def _dft_cos_sin(n):
    j = jnp.arange(n, dtype=jnp.int32)[:, None]
    if n <= 64:
        ang = ((j * j.T) % n).astype(F32) * (2.0 * np.pi / n)
        return jnp.cos(ang), jnp.sin(ang)
    k1 = jnp.arange(n // 64, dtype=jnp.int32)[None, :]
    k0 = jnp.arange(64, dtype=jnp.int32)[None, :]
    a = ((j * k1 * 64) % n).astype(F32) * (2.0 * np.pi / n)
    b = ((j * k0) % n).astype(F32) * (2.0 * np.pi / n)
    ca, sa, cb, sb = jnp.cos(a), jnp.sin(a), jnp.cos(b), jnp.sin(b)
    c = ca[:, :, None] * cb[:, None, :] - sa[:, :, None] * sb[:, None, :]
    s = sa[:, :, None] * cb[:, None, :] + ca[:, :, None] * sb[:, None, :]
    return c.reshape(n, n), s.reshape(n, n)


def _dft_tables(n):
    c, s = _dft_cos_sin(n)
    return jnp.concatenate([c, s], axis=1).astype(BF16)


def _merge_kernel(*refs, n_stream, j0):
    x = _stream_tile(refs[:n_stream], j0)
    (om_ref, of_ref, on_ref, ga_ref, shf_ref, scf_ref, gout_ref, gffn_ref, wout_ref, wr_ref, br_ref,
     xn_ref, hf_ref, rw_ref, ri_ref, cnt_ref, run_ref) = refs[n_stream:]
    ym = _rms(om_ref[...], gout_ref[:, 0:MLA_WIDTH]).astype(BF16)
    yf = _rms(of_ref[...], gout_ref[:, MLA_WIDTH:MLA_WIDTH + FNET_WIDTH]).astype(BF16)
    yn = _rms(on_ref[...], gout_ref[:, MLA_WIDTH + FNET_WIDTH:]).astype(BF16)
    acc = _dot(ym, wout_ref[0, 0:MLA_WIDTH, :])
    acc = acc + _dot(yf, wout_ref[0, MLA_WIDTH:MLA_WIDTH + FNET_WIDTH, :])
    acc = acc + _dot(yn, wout_ref[0, MLA_WIDTH + FNET_WIDTH:, :])
    xn = x + ga_ref[0] * acc
    xn_ref[...] = xn
    hf = _rms(xn, gffn_ref[...]) * (1.0 + scf_ref[0]) + shf_ref[0]
    hf_ref[...] = hf
    hi = hf.astype(BF16)
    lo = (hf - hi.astype(F32)).astype(BF16)
    a = _dot(hi, wr_ref[0])
    b = _dot(lo, wr_ref[0])
    small = a[:, ROUTER_COLS:] + (b[:, :ROUTER_COLS] + b[:, ROUTER_COLS:])
    _route_tile(a[:, :ROUTER_COLS] + small + br_ref[0], rw_ref, ri_ref, cnt_ref, run_ref)


def _route_tile(lg, rw_ref, ri_ref, cnt_ref, run_ref):
    first = jnp.logical_and(pl.program_id(0) == 0, pl.program_id(1) == 0)

    @pl.when(first)
    def _():
        run_ref[...] = jnp.zeros_like(run_ref)

    lane = lax.broadcasted_iota(jnp.int32, lg.shape, 1)
    neg = jnp.float32(-jnp.inf)

    def top(v):
        vmax = jnp.max(v, axis=1, keepdims=True)
        idx = jnp.min(jnp.where(v == vmax, lane, ROUTER_COLS), axis=1, keepdims=True)
        return vmax, idx

    in_groups = lane < N_GROUPS
    gl = jnp.where(in_groups, lg, neg)
    g_max, g_sel = top(gl)
    g_w = 1.0 / jnp.sum(jnp.where(in_groups, jnp.exp(gl - g_max), 0.0), axis=1, keepdims=True)
    e_lo = N_GROUPS + g_sel * EXPERTS_PER_GROUP
    el = jnp.where(jnp.logical_and(lane >= e_lo, lane < e_lo + EXPERTS_PER_GROUP), lg, neg)
    e1_max, i1 = top(el)
    e2_max, i2 = top(jnp.where(lane == i1, neg, el))
    t = jnp.exp(e2_max - e1_max)
    w0 = g_w / (1.0 + t)
    w1 = w0 * t
    rw_ref[...] = jnp.where(lane == 0, w0, jnp.where(lane == 1, w1, 0.0))

    row = lax.broadcasted_iota(jnp.int32, (TM, TM), 0)
    col = lax.broadcasted_iota(jnp.int32, (TM, TM), 1)
    tri = jnp.where(row >= col, 1.0, 0.0).astype(BF16)
    hot0 = lane == i1
    hot1 = lane == i2
    c0 = _dot(tri, jnp.where(hot0, 1.0, 0.0).astype(BF16))
    c1 = _dot(tri, jnp.where(hot1, 1.0, 0.0).astype(BF16))
    run = run_ref[...]
    tot0 = c0[TM - 1:TM, :]
    rank0 = jnp.sum(jnp.where(hot0, run + c0 - 1.0, 0.0), axis=1, keepdims=True)
    rank1 = jnp.sum(jnp.where(hot1, run + tot0 + c1 - 1.0, 0.0), axis=1, keepdims=True)
    run = run + tot0 + c1[TM - 1:TM, :]
    run_ref[...] = run
    cnt_ref[...] = jnp.broadcast_to(run, cnt_ref.shape).astype(jnp.int32)
    ri_ref[...] = jnp.where(lane == 0, i1 - N_GROUPS, jnp.where(lane == 1, i2 - N_GROUPS, jnp.where(
        lane == 2, rank0.astype(jnp.int32), jnp.where(lane == 3, rank1.astype(jnp.int32), 0))))


def _merge(o_mla, o_f, o_na, xs, mods, g_out, g_ffn, w_out, w_router, b_router, layer, B, j0):
    T = B * (9 - j0) * TM
    row = lambda b, j: (_otile(b, j, j0), 0)
    const = lambda b, j: (0, 0)
    slab = lambda b, j: (layer, 0, 0)
    stream_specs, stream_args = _stream_specs(xs, j0)
    return pl.pallas_call(
        functools.partial(_merge_kernel, n_stream=len(stream_args), j0=j0),
        grid=(B, 9 - j0),
        in_specs=stream_specs + [
            pl.BlockSpec((TM, MLA_WIDTH), row),
            pl.BlockSpec((TM, FNET_WIDTH), row),
            pl.BlockSpec((TM, NA_WIDTH), row),
            _mod_spec(2, j0),
            _mod_spec(3, j0),
            _mod_spec(4, j0),
            pl.BlockSpec((1, D), const),
            pl.BlockSpec((1, D), const),
            pl.BlockSpec((1, D, D), slab),
            pl.BlockSpec((1, D, 2 * ROUTER_COLS), slab),
            pl.BlockSpec((1, 1, ROUTER_COLS), slab),
        ],
        out_specs=[
            pl.BlockSpec((TM, D), row),
            pl.BlockSpec((TM, D), row),
            pl.BlockSpec((TM, ROUTER_COLS), row),
            pl.BlockSpec((TM, ROUTER_COLS), row),
            pl.BlockSpec((8, ROUTER_COLS), const),
        ],
        out_shape=[
            jax.ShapeDtypeStruct((T, D), F32),
            jax.ShapeDtypeStruct((T, D), F32),
            jax.ShapeDtypeStruct((T, ROUTER_COLS), F32),
            jax.ShapeDtypeStruct((T, ROUTER_COLS), jnp.int32),
            jax.ShapeDtypeStruct((8, ROUTER_COLS), jnp.int32),
        ],
        scratch_shapes=[pltpu.VMEM((1, ROUTER_COLS), F32)],
        compiler_params=_cparams(("arbitrary", "arbitrary")),
        name="merge",
    )(*stream_args, o_mla, o_f, o_na, mods, mods, mods, g_out.reshape(1, D), g_ffn.reshape(1, D),
      w_out, w_router, b_router)


def _plan_kernel(cnt_ref, slot_ref, src_ref, tile_ref, exp_ref, lo_ref, hi_ref, flag_ref, nxt_ref, ni_ref,
                 gs_ref, *, n_pairs, max_items):
    def starts(e, acc):
        gs_ref[e] = acc
        return acc + cnt_ref[e]

    lax.fori_loop(0, N_EXPERTS, starts, 0)

    def place(p, c):
        src_ref[slot_ref[p]] = lax.shift_right_logical(p, 1)
        return c

    lax.fori_loop(0, n_pairs, place, 0, unroll=16)

    last = N_EXPERTS - 1

    def group_end(e):
        return gs_ref[e] + cnt_ref[e]

    def next_nonempty(e):
        return lax.while_loop(lambda x: jnp.logical_and(x < last, cnt_ref[jnp.minimum(x, last)] == 0),
                              lambda x: x + 1, e)

    def tile_items(t, carry):
        i, e, prev = carry
        row0 = t * TM
        e = lax.while_loop(lambda x: group_end(x) <= row0, lambda x: x + 1, e)

        def emit(state):
            i, e, prev, first, _ = state
            tile_ref[i] = t
            exp_ref[i] = e
            lo_ref[i] = jnp.clip(gs_ref[e] - row0, 0, TM)
            hi_ref[i] = jnp.clip(group_end(e) - row0, 0, TM)
            flag_ref[i] = first + 2 * (e != prev).astype(jnp.int32)
            done = group_end(e) >= row0 + TM
            e_next = jnp.where(done, e, next_nonempty(e + 1))
            return i + 1, e_next, e, jnp.int32(0), done

        i, e, prev, _, _ = lax.while_loop(lambda s: jnp.logical_not(s[4]), emit,
                                          (i, e, prev, jnp.int32(1), jnp.bool_(False)))
        return i, e, prev

    n_items, _, _ = lax.fori_loop(0, n_pairs // TM, tile_items,
                                  (jnp.int32(0), jnp.int32(0), jnp.int32(-1)))
    ni_ref[0] = n_items

    def pad(i, c):
        tile_ref[i] = tile_ref[n_items - 1]
        exp_ref[i] = exp_ref[n_items - 1]
        lo_ref[i] = 0
        hi_ref[i] = 0
        flag_ref[i] = 0
        nxt_ref[i] = -1
        return c

    lax.fori_loop(n_items, max_items, pad, 0)

    def parity(i, par):
        par = jnp.where((flag_ref[i] & 2) != 0, 1 - par, par)
        flag_ref[i] = flag_ref[i] + 4 * par
        return par

    lax.fori_loop(0, n_items, parity, jnp.int32(1))

    def lookahead(k, following):
        i = n_items - 1 - k
        nxt_ref[i] = following
        return jnp.where((flag_ref[i] & 2) != 0, exp_ref[i], following)

    lax.fori_loop(0, n_items, lookahead, jnp.int32(-1))


def _plan(counts, eid, rank):
    n_pairs = eid.shape[0]
    max_items = n_pairs // TM + N_EXPERTS - 1
    g_start = jnp.cumsum(counts) - counts
    experts = jnp.arange(N_EXPERTS, dtype=jnp.int32)
    slot = jnp.sum(jnp.where(eid[:, None] == experts[None, :], g_start[None, :], 0), axis=1) + rank
    smem = pl.BlockSpec(memory_space=pltpu.SMEM)
    i32 = lambda n: jax.ShapeDtypeStruct((n,), jnp.int32)
    src, it_tile, it_exp, it_lo, it_hi, flags, nxt, n_items = pl.pallas_call(
        functools.partial(_plan_kernel, n_pairs=n_pairs, max_items=max_items),
        in_specs=[smem] * 2,
        out_specs=[smem] * 8,
        out_shape=[i32(n_pairs)] + [i32(max_items)] * 6 + [i32(1)],
        scratch_shapes=[pltpu.SMEM((N_EXPERTS,), jnp.int32)],
        name="moe_plan",
    )(counts, slot)
    return (it_tile, it_exp, it_lo, it_hi, flags, nxt, n_items, src), slot


def _moe_kernel(tile_ref, exp_ref, lo_ref, hi_ref, flag_ref, nxt_ref, ni_ref, src_ref,
                hf_hbm, wg_hbm, wu_hbm, wd_hbm, y_ref, xbuf, wg_buf, wu_buf, wd_buf, wgb, wub, wdb,
                sem, wsem, *, n_tiles, e0):
    i = pl.program_id(0)
    t = tile_ref[i]
    slot = lax.rem(t, 2)

    def gather_start(tile, buf, unrolled):
        base = tile * TM

        def issue(r):
            tok = src_ref[base + r]
            pltpu.make_async_copy(hf_hbm.at[pl.ds(tok, 1)], xbuf.at[buf, pl.ds(r, 1)], sem.at[buf]).start()

        if unrolled:
            for r in range(TM):
                issue(r)
        else:
            lax.fori_loop(0, TM, lambda r, c: (issue(r), c)[1], 0)

    def gather_wait(buf):
        pltpu.make_async_copy(hf_hbm.at[pl.ds(0, TM)], xbuf.at[buf], sem.at[buf]).wait()

    def weight_copies(expert, b):
        e = e0 + expert
        return (pltpu.make_async_copy(wg_hbm.at[e], wg_buf.at[b], wsem.at[b]),
                pltpu.make_async_copy(wu_hbm.at[e], wu_buf.at[b], wsem.at[b]),
                pltpu.make_async_copy(wd_hbm.at[e], wd_buf.at[b], wsem.at[b]))

    def weights_start(expert, b):
        for cp in weight_copies(expert, b):
            cp.start(priority=1)

    def weights_wait(b):
        for cp in weight_copies(0, b):
            cp.wait()

    @pl.when(i < ni_ref[0])
    def _():
        first_visit = (flag_ref[i] & 1) != 0
        new_expert = (flag_ref[i] & 2) != 0
        wslot = lax.shift_right_logical(flag_ref[i], 2) & 1

        @pl.when(i == 0)
        def _():
            gather_start(0, 0, False)

        @pl.when(first_visit)
        def _():
            gather_wait(slot)

        for b in range(2):
            @pl.when(jnp.logical_and(jnp.logical_and(first_visit, t + 1 < n_tiles), slot == 1 - b))
            def _():
                gather_start(t + 1, b, True)

        @pl.when(i == 0)
        def _():
            weights_start(exp_ref[0], 0)

        @pl.when(new_expert)
        def _():
            @pl.when(nxt_ref[i] >= 0)
            def _():
                weights_start(nxt_ref[i], 1 - wslot)

            weights_wait(wslot)
            wgb[...] = wg_buf[wslot].astype(BF16)
            wub[...] = wu_buf[wslot].astype(BF16)
            wdb[...] = wd_buf[wslot].astype(BF16)

        x = xbuf[slot].astype(BF16)
        a = _dot(x, wgb[...])
        u = _dot(x, wub[...])
        row = lax.broadcasted_iota(jnp.int32, (TM, 1), 0)
        mine = jnp.logical_and(row >= lo_ref[i], row < hi_ref[i])
        h = jnp.where(mine, (a * jax.nn.sigmoid(a)) * u, 0.0)
        yv = _dot(h.astype(BF16), wdb[...])

        @pl.when(first_visit)
        def _():
            y_ref[...] = yv

        @pl.when(jnp.logical_not(first_visit))
        def _():
            y_ref[...] += yv


def _moe(hf, meta, w_gate, w_up, w_down, layer):
    n_rows = meta[-1].shape[0]
    n_tiles = n_rows // TM
    max_items = meta[0].shape[0]
    grid_spec = pltpu.PrefetchScalarGridSpec(
        num_scalar_prefetch=len(meta),
        grid=(max_items,),
        in_specs=[pl.BlockSpec(memory_space=pl.ANY)] * 4,
        out_specs=pl.BlockSpec((TM, D), lambda i, tile, *_: (tile[i], 0)),
        scratch_shapes=[
            pltpu.VMEM((2, TM, D), F32),
            pltpu.VMEM((2, D, D_EXPERT), F32),
            pltpu.VMEM((2, D, D_EXPERT), F32),
            pltpu.VMEM((2, D_EXPERT, D), F32),
            pltpu.VMEM((D, D_EXPERT), BF16),
            pltpu.VMEM((D, D_EXPERT), BF16),
            pltpu.VMEM((D_EXPERT, D), BF16),
            pltpu.SemaphoreType.DMA((2,)),
            pltpu.SemaphoreType.DMA((2,)),
        ],
    )
    return pl.pallas_call(
        functools.partial(_moe_kernel, n_tiles=n_tiles, e0=layer * N_EXPERTS),
        grid_spec=grid_spec,
        out_shape=jax.ShapeDtypeStruct((n_rows, D), F32),
        compiler_params=_cparams(("arbitrary",)),
        name="moe_experts",
    )(*meta, hf, w_gate.reshape(-1, D, D_EXPERT), w_up.reshape(-1, D, D_EXPERT),
      w_down.reshape(-1, D_EXPERT, D))


def _final_kernel(pos_ref, x_ref, gf_ref, w_ref, g_ref, y_hbm, o_ref, ybuf, sem, *, n_steps):
    step = pl.program_id(0) * 8 + pl.program_id(1)
    out = _moe_residual(pos_ref, y_hbm, ybuf, sem, x_ref[...], gf_ref[0], w_ref[...], step, n_steps)
    o_ref[0] = _rms(out, g_ref[...])


def _final(pos, wts, xn, mods, y, g_final, B):
    tile = lambda b, j, p: (b * 8 + j, 0)
    grid_spec = pltpu.PrefetchScalarGridSpec(
        num_scalar_prefetch=1,
        grid=(B, 8),
        in_specs=[
            pl.BlockSpec((TM, D), tile),
            pl.BlockSpec((1, 1, D), lambda b, j, p: (b * 6 + 5, 0, 0)),
            pl.BlockSpec((TM, 2), tile),
            pl.BlockSpec((1, D), lambda b, j, p: (0, 0)),
            pl.BlockSpec(memory_space=pl.ANY),
        ],
        out_specs=pl.BlockSpec((1, TM, D), lambda b, j, p: (b, j, 0)),
        scratch_shapes=_MOE_GATHER_SCRATCH,
    )
    return pl.pallas_call(
        functools.partial(_final_kernel, n_steps=B * 8),
        grid_spec=grid_spec,
        out_shape=jax.ShapeDtypeStruct((B, 8 * TM, D), F32),
        compiler_params=_cparams(("arbitrary", "arbitrary")),
        name="final",
    )(pos, xn, mods, wts, g_final.reshape(1, D), y)


def _prep_w_in(w_in):
    return jnp.swapaxes(w_in, 1, 2).astype(BF16)


def _prep_w_uq(w_uq):
    w = w_uq.reshape(Q_LORA, MLA_HEADS, QK_NOPE + QK_ROPE)
    w = jnp.pad(w, ((0, 0), (0, 0), (0, 256 - QK_NOPE - QK_ROPE)))
    return w.reshape(Q_LORA, MLA_HEADS * 256).astype(BF16)


def _rope_tables(seq):
    half = QK_ROPE // 2
    inv_freq = ROPE_THETA ** (-jnp.arange(0, half, 2, dtype=F32) / half)
    t = jnp.arange(seq, dtype=jnp.int32)
    row = (t // GRID_W).astype(F32)
    col = (t % GRID_W).astype(F32)
    ang = jnp.concatenate([row[:, None] * inv_freq, col[:, None] * inv_freq], axis=-1)
    cos, sin = jnp.cos(ang), jnp.sin(ang)
    zeros = jnp.zeros((seq, 64), F32)
    cos_l = jnp.concatenate([jnp.repeat(cos, 2, axis=1), zeros], axis=1)
    sin_l = jnp.concatenate([jnp.stack([-sin, sin], axis=-1).reshape(seq, 64), zeros], axis=1)
    cos_c = jnp.concatenate([jnp.ones((CTX_LEN, 64), F32), jnp.zeros((CTX_LEN, 64), F32)], axis=1)
    sin_c = jnp.zeros((CTX_LEN, 128), F32)
    return jnp.concatenate([cos_c, cos_l], axis=0), jnp.concatenate([sin_c, sin_l], axis=0)


def kernel(x, c, ctx, c_ctx, w_ada, b_ada, g_attn, g_ffn, w_in, g_q, w_uq, g_kv, w_ukv, w_fnet, b_fnet,
           na_rpb, g_out, w_out, w_rg, b_rg, w_re, b_re, w_gate, w_up, w_down, g_final):
    B, S, _ = x.shape
    L = w_ada.shape[0]
    assert ctx.shape[1] == CTX_LEN == TM and S == 8 * TM and B <= 4
    T = B * 9 * TM

    cond8 = jnp.concatenate([c, jnp.zeros((4 - B, D), F32), c_ctx[None], jnp.zeros((3, D), F32)], axis=0)
    mods_all = _modulation(cond8, w_ada, b_ada)
    w_in_ext = _prep_w_in(w_in)
    pad = ROUTER_COLS - N_GROUPS - N_EXPERTS
    w_router = jnp.concatenate([w_rg, w_re, jnp.zeros((L, D, pad), F32)], axis=2)
    w_scaled = w_router * 65537.0
    w_router_hi = w_scaled - (w_scaled - w_router)
    w_router_all = jnp.concatenate([w_router_hi, w_router - w_router_hi], axis=2).astype(BF16)
    b_router_all = jnp.concatenate([b_rg, b_re, jnp.zeros((L, pad), F32)], axis=1).reshape(L, 1, ROUTER_COLS)
    w_out_all = w_out.astype(BF16)
    na_bias = _na_bias(na_rpb)
    cos_t, sin_t = _rope_tables(S)
    cs_lat = _dft_tables(S)
    cs_ctx = _dft_tables(CTX_LEN)
    cd_c, cd_s = _dft_cos_sin(FNET_GROUP_DIM)
    cd = jnp.concatenate([cd_c, -cd_s], axis=1).astype(BF16)

    xs = (ctx.reshape(B * CTX_LEN, D), x.reshape(B * S, D))
    pending = None
    for l in range(L):
        last = l == L - 1
        j0 = 1 if last else 0
        mods = mods_all[l].reshape(48, 1, D)
        w_uq_ext = _prep_w_uq(w_uq[l])
        mla = (g_q[l], g_kv[l], w_uq_ext, w_ukv[l].astype(BF16), cos_t, sin_t)
        if pending is None:
            zf, naq, nak, navt, q, k, vt = _in_projection(xs, mods, g_attn[l], w_in_ext, l, mla, B)
        else:
            zf, naq, nak, navt, q, k, vt, xs = _in_projection(xs, mods, g_attn[l], w_in_ext, l, mla, B,
                                                              moe=pending)
        o_mla = _mla_attention(q, k, vt, B, j0)
        o_na = _na_attention(naq, nak, navt, na_bias, l, B, j0)
        w_f = w_fnet[l].astype(BF16)
        o_f = _fnet(zf, cs_lat, cs_ctx, cd, w_f, b_fnet[l], B, j0)
        xn, hf, route_w, route_i, counts = _merge(o_mla, o_f, o_na, xs, mods, g_out[l], g_ffn[l],
                                                  w_out_all, w_router_all, b_router_all, l, B, j0)
        wts = route_w[:, 0:2]
        meta, slot = _plan(counts[0, N_GROUPS:N_GROUPS + N_EXPERTS], route_i[:, 0:2].reshape(-1),
                           route_i[:, 2:4].reshape(-1))
        y = _moe(hf, meta, w_gate, w_up, w_down, l)
        if last:
            return _final(slot, wts, xn, mods, y, g_final, B)
        xs, pending = xn, (slot, wts, y, mods)
```

```python
import functools

import numpy as np
import jax
import jax.numpy as jnp
from jax import lax
from jax.experimental import pallas as pl
from jax.experimental.pallas import tpu as pltpu

F32 = jnp.float32
BF16 = jnp.bfloat16

D = 2048
GRID_W = 64
CTX_LEN = 256
EPS = 1e-6
NEG_INF = -1e30
ROPE_THETA = 10000.0

V_DIM = 128
MLA_WIDTH = D // 2
MLA_HEADS = MLA_WIDTH // V_DIM
QK_NOPE = 128
QK_ROPE = 64
Q_LORA = D // 4
KV_LORA = D // 8
FNET_WIDTH = D // 4
FNET_GROUP_DIM = 128
FNET_GROUPS = FNET_WIDTH // FNET_GROUP_DIM
NA_WIDTH = D // 4
NA_HEAD_DIM = 128
NA_HEADS = NA_WIDTH // NA_HEAD_DIM
NA_KH_MAX = 8
NA_KW = 16
N_GROUPS = 4
EXPERTS_PER_GROUP = 8
N_EXPERTS = N_GROUPS * EXPERTS_PER_GROUP
D_EXPERT = D // 4

MLA_QSCALE = (QK_NOPE + QK_ROPE) ** -0.5 * float(np.log2(np.e))
MLA_CHUNK_TILES = 4
NA_QSCALE = NA_HEAD_DIM ** -0.5 * float(np.log2(np.e))
NA_WIN_ROWS = 12
TM = 256
IN_COLS = Q_LORA + KV_LORA + QK_ROPE + FNET_WIDTH + 3 * NA_WIDTH
ROUTER_COLS = 128
VMEM_LIMIT = 56 * 1024 * 1024


def _cparams(sem):
    return pltpu.CompilerParams(dimension_semantics=sem, vmem_limit_bytes=VMEM_LIMIT)


def _rms(v, g):
    return v * lax.rsqrt(jnp.mean(v * v, axis=-1, keepdims=True) + EPS) * g


def _dot(a, b):
    return jnp.dot(a, b, preferred_element_type=F32)


def _dot_nt(a, b):
    return lax.dot_general(a, b, (((1,), (1,)), ((), ())), preferred_element_type=F32)


def _mod_kernel(c_ref, w_ref, b_ref, o_ref):
    c = c_ref[...]
    s = c * jax.nn.sigmoid(c)
    o_ref[0] = _dot(s.astype(BF16), w_ref[0].astype(BF16)) + b_ref[0]


def _modulation(cond8, w_ada, b_ada):
    L = w_ada.shape[0]
    tn = 1024
    return pl.pallas_call(
        _mod_kernel,
        grid=(L, 6 * D // tn),
        in_specs=[
            pl.BlockSpec((8, D), lambda l, n: (0, 0)),
            pl.BlockSpec((1, D, tn), lambda l, n: (l, 0, n)),
            pl.BlockSpec((1, 1, tn), lambda l, n: (l, 0, n)),
        ],
        out_specs=pl.BlockSpec((1, 8, tn), lambda l, n: (l, 0, n)),
        out_shape=jax.ShapeDtypeStruct((L, 8, 6 * D), F32),
        compiler_params=_cparams(("arbitrary", "arbitrary")),
        name="modulation",
    )(cond8, w_ada, b_ada.reshape(L, 1, 6 * D))


def _tile_of(b, j, j0):
    return b * 9 + j0 + j


def _otile(b, j, j0):
    return b * (9 - j0) + j


def _mod_row(j, b, j0):
    return jnp.where(j0 + j == 0, 4, b)


def _mod_spec(k, j0):
    return pl.BlockSpec((1, 1, D), lambda b, j: (_mod_row(j, b, j0) * 6 + k, 0, 0))


def _stream_specs(xs, j0):
    if isinstance(xs, tuple):
        return [pl.BlockSpec((TM, D), lambda b, j, *_: (b, 0)),
                pl.BlockSpec((TM, D), lambda b, j, *_: (b * 8 + jnp.maximum(j0 + j - 1, 0), 0))], list(xs)
    return [pl.BlockSpec((TM, D), lambda b, j, *_: (_tile_of(b, j, j0), 0))], [xs]


def _stream_tile(refs, j):
    if len(refs) == 1:
        return refs[0][...]
    return jnp.where(j == 0, refs[0][...], refs[1][...])


def _moe_residual(pos_ref, y_hbm, ybuf, sem, x, gate, w, step, n_steps):
    buf = lax.rem(step, 2)

    def start(tile, b, unrolled):
        base = tile * TM

        def issue(r):
            for k in range(2):
                pltpu.make_async_copy(y_hbm.at[pl.ds(pos_ref[2 * (base + r) + k], 1)],
                                      ybuf.at[b, k, pl.ds(r, 1)], sem.at[b]).start(priority=k)

        if unrolled:
            for r in range(TM):
                issue(r)
        else:
            lax.fori_loop(0, TM, lambda r, c: (issue(r), c)[1], 0)

    def wait(b):
        for k in range(2):
            pltpu.make_async_copy(y_hbm.at[pl.ds(0, TM)], ybuf.at[b, k], sem.at[b]).wait()

    @pl.when(step == 0)
    def _():
        start(0, 0, False)

    for b in range(2):
        @pl.when(buf == 1 - b)
        def _():
            start(jnp.minimum(step + 1, n_steps - 1), b, True)

    wait(buf)
    out = x + gate * (w[:, 0:1] * ybuf[buf, 0] + w[:, 1:2] * ybuf[buf, 1])

    @pl.when(step == n_steps - 1)
    def _():
        wait(1 - buf)

    return out


_MOE_GATHER_SCRATCH = [pltpu.VMEM((2, 2, TM, D), F32), pltpu.SemaphoreType.DMA((2,))]


def _rope(r, cos_t, sin_t):
    lane = lax.broadcasted_iota(jnp.int32, r.shape, 1)
    partner = jnp.where(lane % 2 == 0, pltpu.roll(r, 127, 1), pltpu.roll(r, 1, 1))
    return r * cos_t + partner * sin_t


def _inproj_body(x, sh_ref, sc_ref, g_ref, w_ref, gq_ref, gkv_ref, wq_ref, wkv_ref, cos_ref, sin_ref,
                 zf_ref, naq_ref, nak_ref, navt_ref, q_ref, k_ref, vt_ref):
    h = _rms(x, g_ref[...])
    h = h * (1.0 + sc_ref[0]) + sh_ref[0]
    z = _dot_nt(h.astype(BF16), w_ref[0])
    c_zf =Q_LORA + KV_LORA + QK_ROPE
    zf_ref[...] = z[:, c_zf:c_zf + FNET_WIDTH].astype(BF16)
    c_na = c_zf + FNET_WIDTH
    naq_ref[...] = (z[:, c_na:c_na + NA_WIDTH] * NA_QSCALE).astype(BF16)
    nak_ref[...] = z[:, c_na + NA_WIDTH:c_na + 2 * NA_WIDTH].astype(BF16)
    for hd in range(NA_HEADS):
        c = c_na + 2 * NA_WIDTH + hd * NA_HEAD_DIM
        navt_ref[hd * NA_HEAD_DIM:(hd + 1) * NA_HEAD_DIM, :] = z[:, c:c + NA_HEAD_DIM].T.astype(BF16)
    cos_t = cos_ref[...]
    sin_t = sin_ref[...]
    q = _dot(_rms(z[:, 0:Q_LORA], gq_ref[...]).astype(BF16), wq_ref[...]) * MLA_QSCALE
    kv = _dot(_rms(z[:, Q_LORA:Q_LORA + KV_LORA], gkv_ref[...]).astype(BF16), wkv_ref[...])
    k_rope = _rope(z[:, Q_LORA + KV_LORA:Q_LORA + KV_LORA + 128], cos_t, sin_t).astype(BF16)
    for hd in range(MLA_HEADS):
        c = hd * 256
        q_ref[:, c:c + 128] = q[:, c:c + 128].astype(BF16)
        q_ref[:, c + 128:c + 256] = _rope(q[:, c + 128:c + 256], cos_t, sin_t).astype(BF16)
        k_ref[:, c:c + 128] = kv[:, c:c + 128].astype(BF16)
        k_ref[:, c + 128:c + 256] = k_rope
        vt_ref[hd * 128:(hd + 1) * 128, :] = kv[:, c + 128:c + 256].T.astype(BF16)


def _inproj_kernel(*refs, n_stream):
    _inproj_body(_stream_tile(refs[:n_stream], pl.program_id(1)), *refs[n_stream:])


def _inproj_moe_kernel(pos_ref, x_ref, gf_ref, wts_ref, y_hbm, *refs, n_steps):
    *refs, xs_ref, ybuf, sem = refs
    step = pl.program_id(0) * 9 + pl.program_id(1)
    x = _moe_residual(pos_ref, y_hbm, ybuf, sem, x_ref[...], gf_ref[0], wts_ref[...], step, n_steps)
    xs_ref[...] = x
    _inproj_body(x, *refs)


def _in_projection(xs, mods, g_attn, w_in_ext, layer, mla, B, moe=None):
    T = B * 9 * TM
    row = lambda b, j, *_: (_tile_of(b, j, 0), 0)
    col = lambda b, j, *_: (0, _tile_of(b, j, 0))
    const = lambda b, j, *_: (0, 0)
    mod = lambda k: pl.BlockSpec((1, 1, D), lambda b, j, *_: (_mod_row(j, b, 0) * 6 + k, 0, 0))
    once = dict(pipeline_mode=pl.Buffered(1))
    g_q, g_kv, w_uq_ext, w_ukv, cos_t, sin_t = mla
    in_specs = [
        pl.BlockSpec((TM, D), row),
        mod(0),
        mod(1),
        pl.BlockSpec((1, D), const),
        pl.BlockSpec((1, IN_COLS, D), lambda b, j, *_: (layer, 0, 0), **once),
        pl.BlockSpec((1, Q_LORA), const),
        pl.BlockSpec((1, KV_LORA), const),
        pl.BlockSpec((Q_LORA, MLA_HEADS * 256), const, **once),
        pl.BlockSpec((KV_LORA, MLA_HEADS * 256), const, **once),
        pl.BlockSpec((TM, 128), lambda b, j, *_: (j, 0)),
        pl.BlockSpec((TM, 128), lambda b, j, *_: (j, 0)),
    ]
    out_specs = [
        pl.BlockSpec((TM, FNET_WIDTH), row),
        pl.BlockSpec((TM, NA_WIDTH), row),
        pl.BlockSpec((TM, NA_WIDTH), row),
        pl.BlockSpec((NA_WIDTH, TM), col),
        pl.BlockSpec((TM, MLA_HEADS * 256), row),
        pl.BlockSpec((TM, MLA_HEADS * 256), row),
        pl.BlockSpec((MLA_WIDTH, TM), col),
    ]
    out_shape = [
        jax.ShapeDtypeStruct((T, FNET_WIDTH), BF16),
        jax.ShapeDtypeStruct((T, NA_WIDTH), BF16),
        jax.ShapeDtypeStruct((T, NA_WIDTH), BF16),
        jax.ShapeDtypeStruct((NA_WIDTH, T), BF16),
        jax.ShapeDtypeStruct((T, MLA_HEADS * 256), BF16),
        jax.ShapeDtypeStruct((T, MLA_HEADS * 256), BF16),
        jax.ShapeDtypeStruct((MLA_WIDTH, T), BF16),
    ]
    args = [xs, mods, mods, g_attn.reshape(1, D), w_in_ext, g_q.reshape(1, -1), g_kv.reshape(1, -1),
            w_uq_ext, w_ukv, cos_t, sin_t]
    if moe is None:
        stream_specs, stream_args = _stream_specs(xs, 0)
        return pl.pallas_call(
            functools.partial(_inproj_kernel, n_stream=len(stream_args)),
            grid=(B, 9),
            in_specs=stream_specs + in_specs[1:],
            out_specs=out_specs,
            out_shape=out_shape,
            compiler_params=_cparams(("arbitrary", "arbitrary")),
            name="in_projection",
        )(*stream_args, *args[1:])
    pos, wts, y, mods_prev = moe
    in_specs = [in_specs[0], mod(5), pl.BlockSpec((TM, 2), row), pl.BlockSpec(memory_space=pl.ANY)] \
        + in_specs[1:]
    grid_spec = pltpu.PrefetchScalarGridSpec(
        num_scalar_prefetch=1,
        grid=(B, 9),
        in_specs=in_specs,
        out_specs=out_specs + [pl.BlockSpec((TM, D), row)],
        scratch_shapes=_MOE_GATHER_SCRATCH,
    )
    return pl.pallas_call(
        functools.partial(_inproj_moe_kernel, n_steps=B * 9),
        grid_spec=grid_spec,
        out_shape=out_shape + [jax.ShapeDtypeStruct((T, D), F32)],
        compiler_params=_cparams(("arbitrary", "arbitrary")),
        name="in_projection_moe",
    )(pos, xs, mods_prev, wts, y, *args[1:])


def _softmax_numerator(st, m):
    return jnp.exp2(st - m).astype(BF16)


def _pv_with_sum(vt, p):
    ones = jnp.ones((2 * 8, vt.shape[1]), BF16)
    o = _dot(jnp.concatenate([vt, ones], axis=0), p)
    return o[0:vt.shape[0]], o[vt.shape[0]:vt.shape[0] + 1]


def _mla_attn_kernel(q_ref, k_ref, vt_ref, o_ref, *, j0):
    def attend(q0, nq, nk, o0):
        st = _dot_nt(k_ref[0:nk, :], q_ref[q0:q0 + nq, :])
        m = jnp.max(st, axis=0, keepdims=True)
        ot, l = _pv_with_sum(vt_ref[:, 0:nk], _softmax_numerator(st, m))
        o_ref[o0:o0 + nq, :] = (ot / l).T

    if j0 == 0:
        attend(0, CTX_LEN, CTX_LEN, 0)
    for c in range(8 // MLA_CHUNK_TILES):
        attend(CTX_LEN + MLA_CHUNK_TILES * c * TM, MLA_CHUNK_TILES * TM, 9 * TM,
               (1 - j0 + MLA_CHUNK_TILES * c) * TM)


def _mla_attention(q, k, vt, B, j0):
    rows = (9 - j0) * TM
    return pl.pallas_call(
        functools.partial(_mla_attn_kernel, j0=j0),
        grid=(B, MLA_HEADS),
        in_specs=[
            pl.BlockSpec((9 * TM, 256), lambda b, h: (b, h)),
            pl.BlockSpec((9 * TM, 256), lambda b, h: (b, h)),
            pl.BlockSpec((V_DIM, 9 * TM), lambda b, h: (h, b)),
        ],
        out_specs=pl.BlockSpec((rows, V_DIM), lambda b, h: (b, h)),
        out_shape=jax.ShapeDtypeStruct((B * rows, MLA_WIDTH), F32),
        compiler_params=_cparams(("arbitrary", "arbitrary")),
        name="mla_attention",
    )(q, k, vt)


def _na_chunk(g):
    start_row = min(max(4 * g - 4, 0), 8 * TM // GRID_W - NA_WIN_ROWS)
    pattern = 0 if g == 0 else (2 if g == 7 else 1)
    return start_row, pattern


def _na_kernel(q_ref, k_ref, vt_ref, bias_ref, o_ref, *, j0):
    def finish(parts, o0):
        m = None
        for st, _ in parts:
            pm = jnp.max(st, axis=0, keepdims=True)
            m = pm if m is None else jnp.maximum(m, pm)
        l = None
        ot = None
        for st, vt in parts:
            po, pl_sum = _pv_with_sum(vt, _softmax_numerator(st, m))
            l = pl_sum if l is None else l + pl_sum
            ot = po if ot is None else ot + po
        o_ref[o0:o0 + TM, :] = (ot / l).T

    if j0 == 0:
        st = _dot_nt(k_ref[0:CTX_LEN, :], q_ref[0:CTX_LEN, :])
        finish([(st, vt_ref[:, 0:CTX_LEN])], 0)
    for g in range(8):
        start_row, pattern = _na_chunk(g)
        k0 = CTX_LEN + start_row * GRID_W
        nk = NA_WIN_ROWS * GRID_W
        q = q_ref[CTX_LEN + g * TM:CTX_LEN + (g + 1) * TM, :]
        st_loc = _dot_nt(k_ref[k0:k0 + nk, :], q) + bias_ref[0, pattern]
        st_ctx = _dot_nt(k_ref[0:CTX_LEN, :], q)
        finish([(st_loc, vt_ref[:, k0:k0 + nk]), (st_ctx, vt_ref[:, 0:CTX_LEN])], (1 - j0 + g) * TM)


def _na_attention(naq, nak, navt, bias, layer, B, j0):
    rows = (9 - j0) * TM
    bias = bias.reshape((-1,) + bias.shape[2:])
    return pl.pallas_call(
        functools.partial(_na_kernel, j0=j0),
        grid=(NA_HEADS, B),
        in_specs=[
            pl.BlockSpec((9 * TM, NA_HEAD_DIM), lambda h, b: (b, h)),
            pl.BlockSpec((9 * TM, NA_HEAD_DIM), lambda h, b: (b, h)),
            pl.BlockSpec((NA_HEAD_DIM, 9 * TM), lambda h, b: (h, b)),
            pl.BlockSpec((1, 3, NA_WIN_ROWS * GRID_W, TM), lambda h, b: (layer * NA_HEADS + h, 0, 0, 0)),
        ],
        out_specs=pl.BlockSpec((rows, NA_HEAD_DIM), lambda h, b: (b, h)),
        out_shape=jax.ShapeDtypeStruct((B * rows, NA_WIDTH), F32),
        compiler_params=_cparams(("arbitrary", "arbitrary")),
        name="na_attention",
    )(naq, nak, navt, bias)


def _na_bias(rpb):
    kh, rows, nq = NA_KH_MAX, 8 * TM // GRID_W, TM // GRID_W
    cq = np.arange(GRID_W)
    ck = np.arange(GRID_W)
    col_start = np.clip(cq - NA_KW // 2, 0, GRID_W - NA_KW)
    col_ok = (ck[:, None] >= col_start[None, :]) & (ck[:, None] < col_start[None, :] + NA_KW)
    dcol = np.clip(ck[:, None] - cq[None, :] + (NA_KW - 1), 0, 2 * NA_KW - 2)
    select = np.zeros((2 * NA_KW - 1, GRID_W * GRID_W), np.float32)
    select[dcol.reshape(-1), np.arange(GRID_W * GRID_W)] = 1.0
    blocks = jnp.einsum("lhdm,mn->lhdn", rpb.astype(F32) * float(np.log2(np.e)), jnp.asarray(select),
                        precision=lax.Precision.HIGHEST)
    blocks = blocks.reshape(rpb.shape[:3] + (GRID_W, GRID_W))
    blocks = jnp.where(jnp.asarray(col_ok), blocks, NEG_INF)
    masked = jnp.full(rpb.shape[:2] + (GRID_W, GRID_W), NEG_INF, F32)
    patterns = []
    for g in (0, 1, 7):
        start_row, _ = _na_chunk(g)
        key_rows = []
        for kr in range(NA_WIN_ROWS):
            key_row = start_row + kr
            row = []
            for qr in range(nq):
                r = 4 * g + qr
                r_start = min(max(r - kh // 2, 0), rows - kh)
                in_rows = r_start <= key_row < r_start + kh
                row.append(blocks[:, :, key_row - r + (kh - 1)] if in_rows else masked)
            key_rows.append(jnp.concatenate(row, axis=-1))
        patterns.append(jnp.concatenate(key_rows, axis=-2))
    return jnp.stack(patterns, axis=2)


def _fnet_kernel(z_ref, csl_ref, csc_ref, cd_ref, w_ref, b_ref, o_ref, ab_ref, *, j0, seq):
    j = pl.program_id(1) + j0

    def small_side(row0, length):
        for g in range(FNET_GROUPS):
            c = g * FNET_GROUP_DIM
            ab = _dot(z_ref[row0:row0 + length, c:c + FNET_GROUP_DIM], cd_ref[...])
            ab_ref[0:length, c:c + FNET_GROUP_DIM] = ab[:, 0:FNET_GROUP_DIM].astype(BF16)
            ab_ref[length:2 * length, c:c + FNET_GROUP_DIM] = ab[:, FNET_GROUP_DIM:].astype(BF16)

    def long_side(cs, length):
        f = _dot(cs, ab_ref[0:2 * length, :]) * (length * FNET_GROUP_DIM) ** -0.5
        o_ref[...] = _dot(f.astype(BF16), w_ref[...]) + b_ref[...]

    if j0 == 0:
        @pl.when(j == 0)
        def _():
            small_side(0, CTX_LEN)
            long_side(csc_ref[...], CTX_LEN)

    @pl.when(j == 1)
    def _():
        small_side(CTX_LEN, seq)

    @pl.when(j >= 1)
    def _():
        rows = pl.multiple_of((j - 1) * TM, TM)
        long_side(csl_ref[pl.ds(rows, TM), :], seq)


def _fnet(zf, cs_lat, cs_ctx, cd, w_fnet, b_fnet, B, j0):
    seq = 8 * TM
    return pl.pallas_call(
        functools.partial(_fnet_kernel, j0=j0, seq=seq),
        grid=(B, 9 - j0),
        in_specs=[
            pl.BlockSpec((9 * TM, FNET_WIDTH), lambda b, j: (b, 0)),
            pl.BlockSpec((seq, 2 * seq), lambda b, j: (0, 0), pipeline_mode=pl.Buffered(1)),
            pl.BlockSpec((CTX_LEN, 2 * CTX_LEN), lambda b, j: (0, 0)),
            pl.BlockSpec((FNET_GROUP_DIM, 2 * FNET_GROUP_DIM), lambda b, j: (0, 0)),
            pl.BlockSpec((FNET_WIDTH, FNET_WIDTH), lambda b, j: (0, 0)),
            pl.BlockSpec((1, FNET_WIDTH), lambda b, j: (0, 0)),
        ],
        out_specs=pl.BlockSpec((TM, FNET_WIDTH), lambda b, j: (_otile(b, j, j0), 0)),
        out_shape=jax.ShapeDtypeStruct((B * (9 - j0) * TM, FNET_WIDTH), F32),
        scratch_shapes=[pltpu.VMEM((2 * seq, FNET_WIDTH), BF16)],
        compiler_params=_cparams(("arbitrary", "arbitrary")),
        name="fnet",
    )(zf, cs_lat, cs_ctx, cd, w_fnet, b_fnet.reshape(1, -1))


def _dft_cos_sin(n):
    j = jnp.arange(n, dtype=jnp.int32)[:, None]
    if n <= 64:
        ang = ((j * j.T) % n).astype(F32) * (2.0 * np.pi / n)
        return jnp.cos(ang), jnp.sin(ang)
    k1 = jnp.arange(n // 64, dtype=jnp.int32)[None, :]
    k0 = jnp.arange(64, dtype=jnp.int32)[None, :]
    a = ((j * k1 * 64) % n).astype(F32) * (2.0 * np.pi / n)
    b = ((j * k0) % n).astype(F32) * (2.0 * np.pi / n)
    ca, sa, cb, sb = jnp.cos(a), jnp.sin(a), jnp.cos(b), jnp.sin(b)
    c = ca[:, :, None] * cb[:, None, :] - sa[:, :, None] * sb[:, None, :]
    s = sa[:, :, None] * cb[:, None, :] + ca[:, :, None] * sb[:, None, :]
    return c.reshape(n, n), s.reshape(n, n)


def _dft_tables(n):
    c, s = _dft_cos_sin(n)
    return jnp.concatenate([c, s], axis=1).astype(BF16)


def _merge_kernel(*refs, n_stream, j0):
    x = _stream_tile(refs[:n_stream], pl.program_id(1) + j0)
    (om_ref, of_ref, on_ref, ga_ref, shf_ref, scf_ref, gout_ref, gffn_ref, wout_ref, wr_ref, br_ref,
     xn_ref, hf_ref, rw_ref, ri_ref, cnt_ref, run_ref) = refs[n_stream:]
    ym = _rms(om_ref[...], gout_ref[:, 0:MLA_WIDTH]).astype(BF16)
    yf = _rms(of_ref[...], gout_ref[:, MLA_WIDTH:MLA_WIDTH + FNET_WIDTH]).astype(BF16)
    yn = _rms(on_ref[...], gout_ref[:, MLA_WIDTH + FNET_WIDTH:]).astype(BF16)
    acc = _dot(ym, wout_ref[0, 0:MLA_WIDTH, :])
    acc = acc + _dot(yf, wout_ref[0, MLA_WIDTH:MLA_WIDTH + FNET_WIDTH, :])
    acc = acc + _dot(yn, wout_ref[0, MLA_WIDTH + FNET_WIDTH:, :])
    xn = x + ga_ref[0] * acc
    xn_ref[...] = xn
    hf = _rms(xn, gffn_ref[...]) * (1.0 + scf_ref[0]) + shf_ref[0]
    hf_ref[...] = hf
    hi = hf.astype(BF16)
    lo = (hf - hi.astype(F32)).astype(BF16)
    a = _dot(hi, wr_ref[0])
    b = _dot(lo, wr_ref[0])
    small = a[:, ROUTER_COLS:] + (b[:, :ROUTER_COLS] + b[:, ROUTER_COLS:])
    _route_tile(a[:, :ROUTER_COLS] + small + br_ref[0], rw_ref, ri_ref, cnt_ref, run_ref)


def _route_tile(lg, rw_ref, ri_ref, cnt_ref, run_ref):
    first = jnp.logical_and(pl.program_id(0) == 0, pl.program_id(1) == 0)

    @pl.when(first)
    def _():
        run_ref[...] = jnp.zeros_like(run_ref)

    lane = lax.broadcasted_iota(jnp.int32, lg.shape, 1)
    neg = jnp.float32(-jnp.inf)

    def top(v):
        vmax = jnp.max(v, axis=1, keepdims=True)
        idx = jnp.min(jnp.where(v == vmax, lane, ROUTER_COLS), axis=1, keepdims=True)
        return vmax, idx

    in_groups = lane < N_GROUPS
    gl = jnp.where(in_groups, lg, neg)
    g_max, g_sel = top(gl)
    g_w = 1.0 / jnp.sum(jnp.where(in_groups, jnp.exp(gl - g_max), 0.0), axis=1, keepdims=True)
    e_lo = N_GROUPS + g_sel * EXPERTS_PER_GROUP
    el = jnp.where(jnp.logical_and(lane >= e_lo, lane < e_lo + EXPERTS_PER_GROUP), lg, neg)
    e1_max, i1 = top(el)
    e2_max, i2 = top(jnp.where(lane == i1, neg, el))
    t = jnp.exp(e2_max - e1_max)
    w0 = g_w / (1.0 + t)
    w1 = w0 * t
    rw_ref[...] = jnp.where(lane == 0, w0, jnp.where(lane == 1, w1, 0.0))

    row = lax.broadcasted_iota(jnp.int32, (TM, TM), 0)
    col = lax.broadcasted_iota(jnp.int32, (TM, TM), 1)
    tri = jnp.where(row >= col, 1.0, 0.0).astype(BF16)
    hot0 = lane == i1
    hot1 = lane == i2
    c0 = _dot(tri, jnp.where(hot0, 1.0, 0.0).astype(BF16))
    c1 = _dot(tri, jnp.where(hot1, 1.0, 0.0).astype(BF16))
    run = run_ref[...]
    tot0 = c0[TM - 1:TM, :]
    rank0 = jnp.sum(jnp.where(hot0, run + c0 - 1.0, 0.0), axis=1, keepdims=True)
    rank1 = jnp.sum(jnp.where(hot1, run + tot0 + c1 - 1.0, 0.0), axis=1, keepdims=True)
    run = run + tot0 + c1[TM - 1:TM, :]
    run_ref[...] = run
    cnt_ref[...] = jnp.broadcast_to(run, cnt_ref.shape).astype(jnp.int32)
    ri_ref[...] = jnp.where(lane == 0, i1 - N_GROUPS, jnp.where(lane == 1, i2 - N_GROUPS, jnp.where(
        lane == 2, rank0.astype(jnp.int32), jnp.where(lane == 3, rank1.astype(jnp.int32), 0))))


def _merge(o_mla, o_f, o_na, xs, mods, g_out, g_ffn, w_out, w_router, b_router, layer, B, j0):
    T = B * (9 - j0) * TM
    row = lambda b, j: (_otile(b, j, j0), 0)
    const = lambda b, j: (0, 0)
    slab = lambda b, j: (layer, 0, 0)
    stream_specs, stream_args = _stream_specs(xs, j0)
    return pl.pallas_call(
        functools.partial(_merge_kernel, n_stream=len(stream_args), j0=j0),
        grid=(B, 9 - j0),
        in_specs=stream_specs + [
            pl.BlockSpec((TM, MLA_WIDTH), row),
            pl.BlockSpec((TM, FNET_WIDTH), row),
            pl.BlockSpec((TM, NA_WIDTH), row),
            _mod_spec(2, j0),
            _mod_spec(3, j0),
            _mod_spec(4, j0),
            pl.BlockSpec((1, D), const),
            pl.BlockSpec((1, D), const),
            pl.BlockSpec((1, D, D), slab),
            pl.BlockSpec((1, D, 2 * ROUTER_COLS), slab),
            pl.BlockSpec((1, 1, ROUTER_COLS), slab),
        ],
        out_specs=[
            pl.BlockSpec((TM, D), row),
            pl.BlockSpec((TM, D), row),
            pl.BlockSpec((TM, ROUTER_COLS), row),
            pl.BlockSpec((TM, ROUTER_COLS), row),
            pl.BlockSpec((8, ROUTER_COLS), const),
        ],
        out_shape=[
            jax.ShapeDtypeStruct((T, D), F32),
            jax.ShapeDtypeStruct((T, D), F32),
            jax.ShapeDtypeStruct((T, ROUTER_COLS), F32),
            jax.ShapeDtypeStruct((T, ROUTER_COLS), jnp.int32),
            jax.ShapeDtypeStruct((8, ROUTER_COLS), jnp.int32),
        ],
        scratch_shapes=[pltpu.VMEM((1, ROUTER_COLS), F32)],
        compiler_params=_cparams(("arbitrary", "arbitrary")),
        name="merge",
    )(*stream_args, o_mla, o_f, o_na, mods, mods, mods, g_out.reshape(1, D), g_ffn.reshape(1, D),
      w_out, w_router, b_router)


def _plan_kernel(cnt_ref, slot_ref, src_ref, tile_ref, exp_ref, lo_ref, hi_ref, flag_ref, nxt_ref, ni_ref,
                 gs_ref, *, n_pairs, max_items):
    def starts(e, acc):
        gs_ref[e] = acc
        return acc + cnt_ref[e]

    lax.fori_loop(0, N_EXPERTS, starts, 0)

    def place(p, c):
        src_ref[slot_ref[p]] = lax.shift_right_logical(p, 1)
        return c

    lax.fori_loop(0, n_pairs, place, 0, unroll=16)

    last = N_EXPERTS - 1

    def group_end(e):
        return gs_ref[e] + cnt_ref[e]

    def next_nonempty(e):
        return lax.while_loop(lambda x: jnp.logical_and(x < last, cnt_ref[jnp.minimum(x, last)] == 0),
                              lambda x: x + 1, e)

    def tile_items(t, carry):
        i, e, prev = carry
        row0 = t * TM
        e = lax.while_loop(lambda x: group_end(x) <= row0, lambda x: x + 1, e)

        def emit(state):
            i, e, prev, first, _ = state
            tile_ref[i] = t
            exp_ref[i] = e
            lo_ref[i] = jnp.clip(gs_ref[e] - row0, 0, TM)
            hi_ref[i] = jnp.clip(group_end(e) - row0, 0, TM)
            flag_ref[i] = first + 2 * (e != prev).astype(jnp.int32)
            done = group_end(e) >= row0 + TM
            e_next = jnp.where(done, e, next_nonempty(e + 1))
            return i + 1, e_next, e, jnp.int32(0), done

        i, e, prev, _, _ = lax.while_loop(lambda s: jnp.logical_not(s[4]), emit,
                                          (i, e, prev, jnp.int32(1), jnp.bool_(False)))
        return i, e, prev

    n_items, _, _ = lax.fori_loop(0, n_pairs // TM, tile_items,
                                  (jnp.int32(0), jnp.int32(0), jnp.int32(-1)))
    ni_ref[0] = n_items

    def pad(i, c):
        tile_ref[i] = tile_ref[n_items - 1]
        exp_ref[i] = exp_ref[n_items - 1]
        lo_ref[i] = 0
        hi_ref[i] = 0
        flag_ref[i] = 0
        nxt_ref[i] = -1
        return c

    lax.fori_loop(n_items, max_items, pad, 0)

    def parity(i, par):
        par = jnp.where((flag_ref[i] & 2) != 0, 1 - par, par)
        flag_ref[i] = flag_ref[i] + 4 * par
        return par

    lax.fori_loop(0, n_items, parity, jnp.int32(1))

    def lookahead(k, following):
        i = n_items - 1 - k
        nxt_ref[i] = following
        return jnp.where((flag_ref[i] & 2) != 0, exp_ref[i], following)

    lax.fori_loop(0, n_items, lookahead, jnp.int32(-1))


def _plan(counts, eid, rank):
    n_pairs = eid.shape[0]
    max_items = n_pairs // TM + N_EXPERTS - 1
    g_start = jnp.cumsum(counts) - counts
    experts = jnp.arange(N_EXPERTS, dtype=jnp.int32)
    slot = jnp.sum(jnp.where(eid[:, None] == experts[None, :], g_start[None, :], 0), axis=1) + rank
    smem = pl.BlockSpec(memory_space=pltpu.SMEM)
    i32 = lambda n: jax.ShapeDtypeStruct((n,), jnp.int32)
    src, it_tile, it_exp, it_lo, it_hi, flags, nxt, n_items = pl.pallas_call(
        functools.partial(_plan_kernel, n_pairs=n_pairs, max_items=max_items),
        in_specs=[smem] * 2,
        out_specs=[smem] * 8,
        out_shape=[i32(n_pairs)] + [i32(max_items)] * 6 + [i32(1)],
        scratch_shapes=[pltpu.SMEM((N_EXPERTS,), jnp.int32)],
        name="moe_plan",
    )(counts, slot)
    return (it_tile, it_exp, it_lo, it_hi, flags, nxt, n_items, src), slot


def _moe_kernel(tile_ref, exp_ref, lo_ref, hi_ref, flag_ref, nxt_ref, ni_ref, src_ref,
                hf_hbm, wg_hbm, wu_hbm, wd_hbm, y_ref, xbuf, wg_buf, wu_buf, wd_buf, wgb, wub, wdb,
                sem, wsem, *, n_tiles, e0):
    i = pl.program_id(0)
    t = tile_ref[i]
    slot = lax.rem(t, 2)

    def gather_start(tile, buf, unrolled):
        base = tile * TM

        def issue(r):
            tok = src_ref[base + r]
            pltpu.make_async_copy(hf_hbm.at[pl.ds(tok, 1)], xbuf.at[buf, pl.ds(r, 1)], sem.at[buf]).start()

        if unrolled:
            for r in range(TM):
                issue(r)
        else:
            lax.fori_loop(0, TM, lambda r, c: (issue(r), c)[1], 0)

    def gather_wait(buf):
        pltpu.make_async_copy(hf_hbm.at[pl.ds(0, TM)], xbuf.at[buf], sem.at[buf]).wait()

    def weight_copies(expert, b):
        e = e0 + expert
        return (pltpu.make_async_copy(wg_hbm.at[e], wg_buf.at[b], wsem.at[b]),
                pltpu.make_async_copy(wu_hbm.at[e], wu_buf.at[b], wsem.at[b]),
                pltpu.make_async_copy(wd_hbm.at[e], wd_buf.at[b], wsem.at[b]))

    def weights_start(expert, b):
        for cp in weight_copies(expert, b):
            cp.start(priority=1)

    def weights_wait(b):
        for cp in weight_copies(0, b):
            cp.wait()

    @pl.when(i < ni_ref[0])
    def _():
        first_visit = (flag_ref[i] & 1) != 0
        new_expert = (flag_ref[i] & 2) != 0
        wslot = lax.shift_right_logical(flag_ref[i], 2) & 1

        @pl.when(i == 0)
        def _():
            gather_start(0, 0, False)

        @pl.when(first_visit)
        def _():
            gather_wait(slot)

        for b in range(2):
            @pl.when(jnp.logical_and(jnp.logical_and(first_visit, t + 1 < n_tiles), slot == 1 - b))
            def _():
                gather_start(t + 1, b, True)

        @pl.when(i == 0)
        def _():
            weights_start(exp_ref[0], 0)

        @pl.when(new_expert)
        def _():
            @pl.when(nxt_ref[i] >= 0)
            def _():
                weights_start(nxt_ref[i], 1 - wslot)

            weights_wait(wslot)
            wgb[...] = wg_buf[wslot].astype(BF16)
            wub[...] = wu_buf[wslot].astype(BF16)
            wdb[...] = wd_buf[wslot].astype(BF16)

        x = xbuf[slot].astype(BF16)
        a = _dot(x, wgb[...])
        u = _dot(x, wub[...])
        row = lax.broadcasted_iota(jnp.int32, (TM, 1), 0)
        mine = jnp.logical_and(row >= lo_ref[i], row < hi_ref[i])
        h = jnp.where(mine, (a * jax.nn.sigmoid(a)) * u, 0.0)
        yv = _dot(h.astype(BF16), wdb[...])

        @pl.when(first_visit)
        def _():
            y_ref[...] = yv

        @pl.when(jnp.logical_not(first_visit))
        def _():
            y_ref[...] += yv


def _moe(hf, meta, w_gate, w_up, w_down, layer):
    n_rows = meta[-1].shape[0]
    n_tiles = n_rows // TM
    max_items = meta[0].shape[0]
    grid_spec = pltpu.PrefetchScalarGridSpec(
        num_scalar_prefetch=len(meta),
        grid=(max_items,),
        in_specs=[pl.BlockSpec(memory_space=pl.ANY)] * 4,
        out_specs=pl.BlockSpec((TM, D), lambda i, tile, *_: (tile[i], 0)),
        scratch_shapes=[
            pltpu.VMEM((2, TM, D), F32),
            pltpu.VMEM((2, D, D_EXPERT), F32),
            pltpu.VMEM((2, D, D_EXPERT), F32),
            pltpu.VMEM((2, D_EXPERT, D), F32),
            pltpu.VMEM((D, D_EXPERT), BF16),
            pltpu.VMEM((D, D_EXPERT), BF16),
            pltpu.VMEM((D_EXPERT, D), BF16),
            pltpu.SemaphoreType.DMA((2,)),
            pltpu.SemaphoreType.DMA((2,)),
        ],
    )
    return pl.pallas_call(
        functools.partial(_moe_kernel, n_tiles=n_tiles, e0=layer * N_EXPERTS),
        grid_spec=grid_spec,
        out_shape=jax.ShapeDtypeStruct((n_rows, D), F32),
        compiler_params=_cparams(("arbitrary",)),
        name="moe_experts",
    )(*meta, hf, w_gate.reshape(-1, D, D_EXPERT), w_up.reshape(-1, D, D_EXPERT),
      w_down.reshape(-1, D_EXPERT, D))


def _final_kernel(pos_ref, x_ref, gf_ref, w_ref, g_ref, y_hbm, o_ref, ybuf, sem, *, n_steps):
    step = pl.program_id(0) * 8 + pl.program_id(1)
    out = _moe_residual(pos_ref, y_hbm, ybuf, sem, x_ref[...], gf_ref[0], w_ref[...], step, n_steps)
    o_ref[0] = _rms(out, g_ref[...])


def _final(pos, wts, xn, mods, y, g_final, B):
    tile = lambda b, j, p: (b * 8 + j, 0)
    grid_spec = pltpu.PrefetchScalarGridSpec(
        num_scalar_prefetch=1,
        grid=(B, 8),
        in_specs=[
            pl.BlockSpec((TM, D), tile),
            pl.BlockSpec((1, 1, D), lambda b, j, p: (b * 6 + 5, 0, 0)),
            pl.BlockSpec((TM, 2), tile),
            pl.BlockSpec((1, D), lambda b, j, p: (0, 0)),
            pl.BlockSpec(memory_space=pl.ANY),
        ],
        out_specs=pl.BlockSpec((1, TM, D), lambda b, j, p: (b, j, 0)),
        scratch_shapes=_MOE_GATHER_SCRATCH,
    )
    return pl.pallas_call(
        functools.partial(_final_kernel, n_steps=B * 8),
        grid_spec=grid_spec,
        out_shape=jax.ShapeDtypeStruct((B, 8 * TM, D), F32),
        compiler_params=_cparams(("arbitrary", "arbitrary")),
        name="final",
    )(pos, xn, mods, wts, g_final.reshape(1, D), y)


def _prep_w_in(w_in):
    return jnp.swapaxes(w_in, 1, 2).astype(BF16)


def _prep_w_uq(w_uq):
    w = w_uq.reshape(Q_LORA, MLA_HEADS, QK_NOPE + QK_ROPE)
    w = jnp.pad(w, ((0, 0), (0, 0), (0, 256 - QK_NOPE - QK_ROPE)))
    return w.reshape(Q_LORA, MLA_HEADS * 256).astype(BF16)


def _rope_tables(seq):
    half = QK_ROPE // 2
    inv_freq = ROPE_THETA ** (-jnp.arange(0, half, 2, dtype=F32) / half)
    t = jnp.arange(seq, dtype=jnp.int32)
    row = (t // GRID_W).astype(F32)
    col = (t % GRID_W).astype(F32)
    ang = jnp.concatenate([row[:, None] * inv_freq, col[:, None] * inv_freq], axis=-1)
    cos, sin = jnp.cos(ang), jnp.sin(ang)
    zeros = jnp.zeros((seq, 64), F32)
    cos_l = jnp.concatenate([jnp.repeat(cos, 2, axis=1), zeros], axis=1)
    sin_l = jnp.concatenate([jnp.stack([-sin, sin], axis=-1).reshape(seq, 64), zeros], axis=1)
    cos_c = jnp.concatenate([jnp.ones((CTX_LEN, 64), F32), jnp.zeros((CTX_LEN, 64), F32)], axis=1)
    sin_c = jnp.zeros((CTX_LEN, 128), F32)
    return jnp.concatenate([cos_c, cos_l], axis=0), jnp.concatenate([sin_c, sin_l], axis=0)


def kernel(x, c, ctx, c_ctx, w_ada, b_ada, g_attn, g_ffn, w_in, g_q, w_uq, g_kv, w_ukv, w_fnet, b_fnet,
           na_rpb, g_out, w_out, w_rg, b_rg, w_re, b_re, w_gate, w_up, w_down, g_final):
    B, S, _ = x.shape
    L = w_ada.shape[0]
    assert ctx.shape[1] == CTX_LEN == TM and S == 8 * TM and B <= 4
    T = B * 9 * TM

    cond8 = jnp.concatenate([c, jnp.zeros((4 - B, D), F32), c_ctx[None], jnp.zeros((3, D), F32)], axis=0)
    mods_all = _modulation(cond8, w_ada, b_ada)
    w_in_ext = _prep_w_in(w_in)
    pad = ROUTER_COLS - N_GROUPS - N_EXPERTS
    w_router = jnp.concatenate([w_rg, w_re, jnp.zeros((L, D, pad), F32)], axis=2)
    w_scaled = w_router * 65537.0
    w_router_hi = w_scaled - (w_scaled - w_router)
    w_router_all = jnp.concatenate([w_router_hi, w_router - w_router_hi], axis=2).astype(BF16)
    b_router_all = jnp.concatenate([b_rg, b_re, jnp.zeros((L, pad), F32)], axis=1).reshape(L, 1, ROUTER_COLS)
    w_out_all = w_out.astype(BF16)
    na_bias = _na_bias(na_rpb)
    cos_t, sin_t = _rope_tables(S)
    cs_lat = _dft_tables(S)
    cs_ctx = _dft_tables(CTX_LEN)
    cd_c, cd_s = _dft_cos_sin(FNET_GROUP_DIM)
    cd = jnp.concatenate([cd_c, -cd_s], axis=1).astype(BF16)

    xs = (ctx.reshape(B * CTX_LEN, D), x.reshape(B * S, D))
    pending = None
    for l in range(L):
        last = l == L - 1
        j0 = 1 if last else 0
        mods = mods_all[l].reshape(48, 1, D)
        w_uq_ext = _prep_w_uq(w_uq[l])
        mla = (g_q[l], g_kv[l], w_uq_ext, w_ukv[l].astype(BF16), cos_t, sin_t)
        if pending is None:
            zf, naq, nak, navt, q, k, vt = _in_projection(xs, mods, g_attn[l], w_in_ext, l, mla, B)
        else:
            zf, naq, nak, navt, q, k, vt, xs = _in_projection(xs, mods, g_attn[l], w_in_ext, l, mla, B,
                                                              moe=pending)
        o_mla = _mla_attention(q, k, vt, B, j0)
        o_na = _na_attention(naq, nak, navt, na_bias, l, B, j0)
        w_f = w_fnet[l].astype(BF16)
        o_f = _fnet(zf, cs_lat, cs_ctx, cd, w_f, b_fnet[l], B, j0)
        xn, hf, route_w, route_i, counts = _merge(o_mla, o_f, o_na, xs, mods, g_out[l], g_ffn[l],
                                                  w_out_all, w_router_all, b_router_all, l, B, j0)
        wts = route_w[:, 0:2]
        meta, slot = _plan(counts[0, N_GROUPS:N_GROUPS + N_EXPERTS], route_i[:, 0:2].reshape(-1),
                           route_i[:, 2:4].reshape(-1))
        y = _moe(hf, meta, w_gate, w_up, w_down, l)
        if last:
            return _final(slot, wts, xn, mods, y, g_final, B)
        xs, pending = xn, (slot, wts, y, mods)
```

```python
import functools

import numpy as np
import jax
import jax.numpy as jnp
from jax import lax
from jax.experimental import pallas as pl
from jax.experimental.pallas import tpu as pltpu

F32 = jnp.float32
BF16 = jnp.bfloat16

D = 2048
GRID_W = 64
CTX_LEN = 256
EPS = 1e-6
NEG_INF = -1e30
ROPE_THETA = 10000.0

V_DIM = 128
MLA_WIDTH = D // 2
MLA_HEADS = MLA_WIDTH // V_DIM
QK_NOPE = 128
QK_ROPE = 64
Q_LORA = D // 4
KV_LORA = D // 8
FNET_WIDTH = D // 4
FNET_GROUP_DIM = 128
FNET_GROUPS = FNET_WIDTH // FNET_GROUP_DIM
NA_WIDTH = D // 4
NA_HEAD_DIM = 128
NA_HEADS = NA_WIDTH // NA_HEAD_DIM
NA_KH_MAX = 8
NA_KW = 16
N_GROUPS = 4
EXPERTS_PER_GROUP = 8
N_EXPERTS = N_GROUPS * EXPERTS_PER_GROUP
D_EXPERT = D // 4

MLA_QSCALE = (QK_NOPE + QK_ROPE) ** -0.5 * float(np.log2(np.e))
MLA_SLAB = QK_NOPE + V_DIM
MLA_CHUNK_TILES = 8
NA_QSCALE = NA_HEAD_DIM ** -0.5 * float(np.log2(np.e))
NA_WIN_ROWS = 12
TM = 256
NT = 9
IN_COLS = Q_LORA + KV_LORA + QK_ROPE + FNET_WIDTH + 3 * NA_WIDTH
ROUTER_COLS = 128
VMEM_LIMIT = 56 * 1024 * 1024


def _cparams(sem):
    return pltpu.CompilerParams(dimension_semantics=sem, vmem_limit_bytes=VMEM_LIMIT)


def _rms(v, g):
    return v * lax.rsqrt(jnp.mean(v * v, axis=-1, keepdims=True) + EPS) * g


def _dot(a, b):
    return jnp.dot(a, b, preferred_element_type=F32)


def _dot_nt(a, b):
    return lax.dot_general(a, b, (((1,), (1,)), ((), ())), preferred_element_type=F32)


def _mod_kernel(c_ref, w_ref, b_ref, o_ref):
    c = c_ref[...]
    s = c * jax.nn.sigmoid(c)
    o_ref[0] = _dot(s.astype(BF16), w_ref[0].astype(BF16)) + b_ref[0]


def _modulation(cond8, w_ada, b_ada):
    L = w_ada.shape[0]
    tn = 1024
    return pl.pallas_call(
        _mod_kernel,
        grid=(L, 6 * D // tn),
        in_specs=[
            pl.BlockSpec((8, D), lambda l, n: (0, 0)),
            pl.BlockSpec((1, D, tn), lambda l, n: (l, 0, n)),
            pl.BlockSpec((1, 1, tn), lambda l, n: (l, 0, n)),
        ],
        out_specs=pl.BlockSpec((1, 8, tn), lambda l, n: (l, 0, n)),
        out_shape=jax.ShapeDtypeStruct((L, 8, 6 * D), F32),
        compiler_params=_cparams(("arbitrary", "arbitrary")),
        name="modulation",
    )(cond8, w_ada, b_ada.reshape(L, 1, 6 * D))


def _tile_of(b, j, j0):
    return b * NT + j0 + j


def _otile(b, j, j0):
    return b * (NT - j0) + j


def _mod_row(j, b, j0):
    return jnp.where(j0 + j == 0, 4, b)


def _mod_spec(k, j0):
    return pl.BlockSpec((1, 1, D), lambda b, j: (_mod_row(j, b, j0) * 6 + k, 0, 0))


def _stream_specs(xs, j0):
    if isinstance(xs, tuple):
        return [pl.BlockSpec((TM, D), lambda b, j, *_: (b, 0)),
                pl.BlockSpec((TM, D), lambda b, j, *_: (b * 8 + jnp.maximum(j0 + j - 1, 0), 0))], list(xs)
    return [pl.BlockSpec((TM, D), lambda b, j, *_: (_tile_of(b, j, j0), 0))], [xs]


def _stream_tile(refs, j0):
    if len(refs) == 1:
        return refs[0][...]
    return jnp.where(pl.program_id(1) + j0 == 0, refs[0][...], refs[1][...])


def _moe_residual(pos_ref, y_hbm, ybuf, sem, x, gate, w, step, n_steps):
    buf = lax.rem(step, 2)

    def start(tile, b, unrolled):
        base = tile * TM

        def issue(r):
            for k in range(2):
                pltpu.make_async_copy(y_hbm.at[pl.ds(pos_ref[2 * (base + r) + k], 1)],
                                      ybuf.at[b, k, pl.ds(r, 1)], sem.at[b]).start(priority=k)

        if unrolled:
            for r in range(TM):
                issue(r)
        else:
            lax.fori_loop(0, TM, lambda r, c: (issue(r), c)[1], 0)

    def wait(b):
        for k in range(2):
            pltpu.make_async_copy(y_hbm.at[pl.ds(0, TM)], ybuf.at[b, k], sem.at[b]).wait()

    @pl.when(step == 0)
    def _():
        start(0, 0, False)

    for b in range(2):
        @pl.when(buf == 1 - b)
        def _():
            start(jnp.minimum(step + 1, n_steps - 1), b, True)

    wait(buf)
    out = x + gate * (w[:, 0:1] * ybuf[buf, 0] + w[:, 1:2] * ybuf[buf, 1])

    @pl.when(step == n_steps - 1)
    def _():
        wait(1 - buf)

    return out


_MOE_GATHER_SCRATCH = [pltpu.VMEM((2, 2, TM, D), F32), pltpu.SemaphoreType.DMA((2,))]


def _rope(r, cos_t, sin_t):
    lane = lax.broadcasted_iota(jnp.int32, r.shape, 1)
    partner = jnp.where(lane % 2 == 0, pltpu.roll(r, 127, 1), pltpu.roll(r, 1, 1))
    return r * cos_t + partner * sin_t


def _inproj_body(x, sh_ref, sc_ref, g_ref, w_ref, gq_ref, gkv_ref, wq_ref, wkv_ref, cos_ref, sin_ref,
                 zf_ref, naq_ref, nak_ref, navt_ref, q_ref, k_ref, vt_ref):
    h = _rms(x, g_ref[...])
    h = h * (1.0 + sc_ref[0]) + sh_ref[0]
    z = _dot_nt(h.astype(BF16), w_ref[0])
    c_zf =Q_LORA + KV_LORA + QK_ROPE
    zf_ref[...] = z[:, c_zf:c_zf + FNET_WIDTH].astype(BF16)
    c_na = c_zf + FNET_WIDTH
    naq_ref[...] = (z[:, c_na:c_na + NA_WIDTH] * NA_QSCALE).astype(BF16)
    nak_ref[...] = z[:, c_na + NA_WIDTH:c_na + 2 * NA_WIDTH].astype(BF16)
    for hd in range(NA_HEADS):
        c = c_na + 2 * NA_WIDTH + hd * NA_HEAD_DIM
        navt_ref[hd * NA_HEAD_DIM:(hd + 1) * NA_HEAD_DIM, :] = z[:, c:c + NA_HEAD_DIM].T.astype(BF16)
    cos_t = cos_ref[...]
    sin_t = sin_ref[...]
    q = _dot(_rms(z[:, 0:Q_LORA], gq_ref[...]).astype(BF16), wq_ref[...]) * MLA_QSCALE
    kv = _dot(_rms(z[:, Q_LORA:Q_LORA + KV_LORA], gkv_ref[...]).astype(BF16), wkv_ref[...])
    k_rope = _rope(z[:, Q_LORA + KV_LORA:Q_LORA + KV_LORA + 128], cos_t, sin_t).astype(BF16)
    for hd in range(MLA_HEADS):
        c = hd * MLA_SLAB
        r = c + QK_NOPE
        q_ref[:, c:r] = q[:, c:r].astype(BF16)
        q_ref[:, r:c + MLA_SLAB] = _rope(q[:, r:c + MLA_SLAB], cos_t, sin_t).astype(BF16)
        k_ref[:, c:r] = kv[:, c:r].astype(BF16)
        k_ref[:, r:c + MLA_SLAB] = k_rope
        vt_ref[hd * V_DIM:(hd + 1) * V_DIM, :] = kv[:, r:c + MLA_SLAB].T.astype(BF16)


def _inproj_kernel(*refs, n_stream):
    _inproj_body(_stream_tile(refs[:n_stream], 0), *refs[n_stream:])


def _inproj_moe_kernel(pos_ref, x_ref, gf_ref, wts_ref, y_hbm, *refs, n_steps):
    *refs, xs_ref, ybuf, sem = refs
    step = pl.program_id(0) * NT + pl.program_id(1)
    x = _moe_residual(pos_ref, y_hbm, ybuf, sem, x_ref[...], gf_ref[0], wts_ref[...], step, n_steps)
    xs_ref[...] = x
    _inproj_body(x, *refs)


def _in_projection(xs, mods, g_attn, w_in_ext, layer, mla, B, moe=None):
    T = B * NT * TM
    row = lambda b, j, *_: (_tile_of(b, j, 0), 0)
    col = lambda b, j, *_: (0, _tile_of(b, j, 0))
    const = lambda b, j, *_: (0, 0)
    mod = lambda k: pl.BlockSpec((1, 1, D), lambda b, j, *_: (_mod_row(j, b, 0) * 6 + k, 0, 0))
    once = dict(pipeline_mode=pl.Buffered(1))
    g_q, g_kv, w_uq_ext, w_ukv, cos_t, sin_t = mla
    in_specs = [
        pl.BlockSpec((TM, D), row),
        mod(0),
        mod(1),
        pl.BlockSpec((1, D), const),
        pl.BlockSpec((1, IN_COLS, D), lambda b, j, *_: (layer, 0, 0), **once),
        pl.BlockSpec((1, Q_LORA), const),
        pl.BlockSpec((1, KV_LORA), const),
        pl.BlockSpec((Q_LORA, MLA_HEADS * MLA_SLAB), const, **once),
        pl.BlockSpec((KV_LORA, MLA_HEADS * MLA_SLAB), const, **once),
        pl.BlockSpec((TM, 128), lambda b, j, *_: (j, 0)),
        pl.BlockSpec((TM, 128), lambda b, j, *_: (j, 0)),
    ]
    out_specs = [
        pl.BlockSpec((TM, FNET_WIDTH), row),
        pl.BlockSpec((TM, NA_WIDTH), row),
        pl.BlockSpec((TM, NA_WIDTH), row),
        pl.BlockSpec((NA_WIDTH, TM), col),
        pl.BlockSpec((TM, MLA_HEADS * MLA_SLAB), row),
        pl.BlockSpec((TM, MLA_HEADS * MLA_SLAB), row),
        pl.BlockSpec((MLA_WIDTH, TM), col),
    ]
    out_shape = [
        jax.ShapeDtypeStruct((T, FNET_WIDTH), BF16),
        jax.ShapeDtypeStruct((T, NA_WIDTH), BF16),
        jax.ShapeDtypeStruct((T, NA_WIDTH), BF16),
        jax.ShapeDtypeStruct((NA_WIDTH, T), BF16),
        jax.ShapeDtypeStruct((T, MLA_HEADS * MLA_SLAB), BF16),
        jax.ShapeDtypeStruct((T, MLA_HEADS * MLA_SLAB), BF16),
        jax.ShapeDtypeStruct((MLA_WIDTH, T), BF16),
    ]
    args = [xs, mods, mods, g_attn.reshape(1, D), w_in_ext, g_q.reshape(1, -1), g_kv.reshape(1, -1),
            w_uq_ext, w_ukv, cos_t, sin_t]
    if moe is None:
        stream_specs, stream_args = _stream_specs(xs, 0)
        return pl.pallas_call(
            functools.partial(_inproj_kernel, n_stream=len(stream_args)),
            grid=(B, NT),
            in_specs=stream_specs + in_specs[1:],
            out_specs=out_specs,
            out_shape=out_shape,
            compiler_params=_cparams(("arbitrary", "arbitrary")),
            name="in_projection",
        )(*stream_args, *args[1:])
    pos, wts, y, mods_prev = moe
    in_specs = [in_specs[0], mod(5), pl.BlockSpec((TM, 2), row), pl.BlockSpec(memory_space=pl.ANY)] \
        + in_specs[1:]
    grid_spec = pltpu.PrefetchScalarGridSpec(
        num_scalar_prefetch=1,
        grid=(B, NT),
        in_specs=in_specs,
        out_specs=out_specs + [pl.BlockSpec((TM, D), row)],
        scratch_shapes=_MOE_GATHER_SCRATCH,
    )
    return pl.pallas_call(
        functools.partial(_inproj_moe_kernel, n_steps=B * NT),
        grid_spec=grid_spec,
        out_shape=out_shape + [jax.ShapeDtypeStruct((T, D), F32)],
        compiler_params=_cparams(("arbitrary", "arbitrary")),
        name="in_projection_moe",
    )(pos, xs, mods_prev, wts, y, *args[1:])


def _softmax_numerator(st, m):
    return jnp.exp2(st - m).astype(BF16)


def _pv_with_sum(vt, p):
    ones = jnp.ones((2 * 8, vt.shape[1]), BF16)
    o = _dot(jnp.concatenate([vt, ones], axis=0), p)
    return o[0:vt.shape[0]], o[vt.shape[0]:vt.shape[0] + 1]


def _mla_attn_kernel(q_ref, k_ref, vt_ref, o_ref, *, j0):
    def attend(q0, nq, nk, o0):
        st = _dot_nt(k_ref[0:nk, :], q_ref[q0:q0 + nq, :])
        m = jnp.max(st, axis=0, keepdims=True)
        ot, l = _pv_with_sum(vt_ref[:, 0:nk], _softmax_numerator(st, m))
        o_ref[o0:o0 + nq, :] = (ot / l).T

    if j0 == 0:
        attend(0, CTX_LEN, CTX_LEN, 0)
    for c in range(8 // MLA_CHUNK_TILES):
        attend(CTX_LEN + MLA_CHUNK_TILES * c * TM, MLA_CHUNK_TILES * TM, NT * TM,
               (1 - j0 + MLA_CHUNK_TILES * c) * TM)


def _mla_attention(q, k, vt, B, j0):
    rows = (NT - j0) * TM
    return pl.pallas_call(
        functools.partial(_mla_attn_kernel, j0=j0),
        grid=(B, MLA_HEADS),
        in_specs=[
            pl.BlockSpec((NT * TM, MLA_SLAB), lambda b, h: (b, h)),
            pl.BlockSpec((NT * TM, MLA_SLAB), lambda b, h: (b, h)),
            pl.BlockSpec((V_DIM, NT * TM), lambda b, h: (h, b)),
        ],
        out_specs=pl.BlockSpec((rows, V_DIM), lambda b, h: (b, h)),
        out_shape=jax.ShapeDtypeStruct((B * rows, MLA_WIDTH), F32),
        compiler_params=_cparams(("arbitrary", "arbitrary")),
        name="mla_attention",
    )(q, k, vt)


def _na_chunk(g):
    start_row = min(max(4 * g - 4, 0), 8 * TM // GRID_W - NA_WIN_ROWS)
    pattern = 0 if g == 0 else (2 if g == 7 else 1)
    return start_row, pattern


def _na_kernel(q_ref, k_ref, vt_ref, bias_ref, o_ref, *, j0):
    def finish(parts, o0):
        m = None
        for st, _ in parts:
            pm = jnp.max(st, axis=0, keepdims=True)
            m = pm if m is None else jnp.maximum(m, pm)
        l = None
        ot = None
        for st, vt in parts:
            po, pl_sum = _pv_with_sum(vt, _softmax_numerator(st, m))
            l = pl_sum if l is None else l + pl_sum
            ot = po if ot is None else ot + po
        o_ref[o0:o0 + TM, :] = (ot / l).T

    if j0 == 0:
        st = _dot_nt(k_ref[0:CTX_LEN, :], q_ref[0:CTX_LEN, :])
        finish([(st, vt_ref[:, 0:CTX_LEN])], 0)
    for g in range(8):
        start_row, pattern = _na_chunk(g)
        k0 = CTX_LEN + start_row * GRID_W
        nk = NA_WIN_ROWS * GRID_W
        q = q_ref[CTX_LEN + g * TM:CTX_LEN + (g + 1) * TM, :]
        st_loc = _dot_nt(k_ref[k0:k0 + nk, :], q) + bias_ref[0, pattern]
        st_ctx = _dot_nt(k_ref[0:CTX_LEN, :], q)
        finish([(st_loc, vt_ref[:, k0:k0 + nk]), (st_ctx, vt_ref[:, 0:CTX_LEN])], (1 - j0 + g) * TM)


def _na_attention(naq, nak, navt, bias, layer, B, j0):
    rows = (NT - j0) * TM
    bias = bias.reshape((-1,) + bias.shape[2:])
    return pl.pallas_call(
        functools.partial(_na_kernel, j0=j0),
        grid=(NA_HEADS, B),
        in_specs=[
            pl.BlockSpec((NT * TM, NA_HEAD_DIM), lambda h, b: (b, h)),
            pl.BlockSpec((NT * TM, NA_HEAD_DIM), lambda h, b: (b, h)),
            pl.BlockSpec((NA_HEAD_DIM, NT * TM), lambda h, b: (h, b)),
            pl.BlockSpec((1, 3, NA_WIN_ROWS * GRID_W, TM), lambda h, b: (layer * NA_HEADS + h, 0, 0, 0)),
        ],
        out_specs=pl.BlockSpec((rows, NA_HEAD_DIM), lambda h, b: (b, h)),
        out_shape=jax.ShapeDtypeStruct((B * rows, NA_WIDTH), F32),
        compiler_params=_cparams(("arbitrary", "arbitrary")),
        name="na_attention",
    )(naq, nak, navt, bias)


def _na_bias(rpb):
    kh, rows, nq = NA_KH_MAX, 8 * TM // GRID_W, TM // GRID_W
    cq = np.arange(GRID_W)
    ck = np.arange(GRID_W)
    col_start = np.clip(cq - NA_KW // 2, 0, GRID_W - NA_KW)
    col_ok = (ck[:, None] >= col_start[None, :]) & (ck[:, None] < col_start[None, :] + NA_KW)
    dcol = np.clip(ck[:, None] - cq[None, :] + (NA_KW - 1), 0, 2 * NA_KW - 2)
    select = np.zeros((2 * NA_KW - 1, GRID_W * GRID_W), np.float32)
    select[dcol.reshape(-1), np.arange(GRID_W * GRID_W)] = 1.0
    blocks = jnp.einsum("lhdm,mn->lhdn", rpb.astype(F32) * float(np.log2(np.e)), jnp.asarray(select),
                        precision=lax.Precision.HIGHEST)
    blocks = blocks.reshape(rpb.shape[:3] + (GRID_W, GRID_W))
    blocks = jnp.where(jnp.asarray(col_ok), blocks, NEG_INF)
    masked = jnp.full(rpb.shape[:2] + (GRID_W, GRID_W), NEG_INF, F32)
    patterns = []
    for g in (0, 1, 7):
        start_row, _ = _na_chunk(g)
        key_rows = []
        for kr in range(NA_WIN_ROWS):
            key_row = start_row + kr
            row = []
            for qr in range(nq):
                r = 4 * g + qr
                r_start = min(max(r - kh // 2, 0), rows - kh)
                in_rows = r_start <= key_row < r_start + kh
                row.append(blocks[:, :, key_row - r + (kh - 1)] if in_rows else masked)
            key_rows.append(jnp.concatenate(row, axis=-1))
        patterns.append(jnp.concatenate(key_rows, axis=-2))
    return jnp.stack(patterns, axis=2)


def _fnet_kernel(z_ref, cl_ref, sl_ref, cc_ref, sc_ref, cd_ref, w_ref, b_ref, o_ref, ab_ref, *, j0, seq):
    j = pl.program_id(1) + j0

    def small_side(row0, length):
        for g in range(FNET_GROUPS):
            c = g * FNET_GROUP_DIM
            ab = _dot(z_ref[row0:row0 + length, c:c + FNET_GROUP_DIM], cd_ref[...])
            ab_ref[0:length, c:c + FNET_GROUP_DIM] = ab[:, 0:FNET_GROUP_DIM].astype(BF16)
            ab_ref[length:2 * length, c:c + FNET_GROUP_DIM] = ab[:, FNET_GROUP_DIM:].astype(BF16)

    def long_side(c, s, length):
        f = _dot(c, ab_ref[0:length, :]) + _dot(s, ab_ref[length:2 * length, :])
        f = f * (length * FNET_GROUP_DIM) ** -0.5
        o_ref[...] = _dot(f.astype(BF16), w_ref[...]) + b_ref[...]

    if j0 == 0:
        @pl.when(j == 0)
        def _():
            small_side(0, CTX_LEN)
            long_side(cc_ref[...], sc_ref[...], CTX_LEN)

    @pl.when(j == 1)
    def _():
        small_side(CTX_LEN, seq)

    @pl.when(j >= 1)
    def _():
        long_side(cl_ref[...], sl_ref[...], seq)


def _fnet(zf, dft_lat, dft_ctx, cd, w_fnet, b_fnet, B, j0):
    seq = 8 * TM
    lat_rows = lambda b, j: (jnp.maximum(j0 + j - 1, 0), 0)
    return pl.pallas_call(
        functools.partial(_fnet_kernel, j0=j0, seq=seq),
        grid=(B, NT - j0),
        in_specs=[
            pl.BlockSpec((NT * TM, FNET_WIDTH), lambda b, j: (b, 0)),
            pl.BlockSpec((TM, seq), lat_rows),
            pl.BlockSpec((TM, seq), lat_rows),
            pl.BlockSpec((CTX_LEN, CTX_LEN), lambda b, j: (0, 0)),
            pl.BlockSpec((CTX_LEN, CTX_LEN), lambda b, j: (0, 0)),
            pl.BlockSpec((FNET_GROUP_DIM, 2 * FNET_GROUP_DIM), lambda b, j: (0, 0)),
            pl.BlockSpec((FNET_WIDTH, FNET_WIDTH), lambda b, j: (0, 0)),
            pl.BlockSpec((1, FNET_WIDTH), lambda b, j: (0, 0)),
        ],
        out_specs=pl.BlockSpec((TM, FNET_WIDTH), lambda b, j: (_otile(b, j, j0), 0)),
        out_shape=jax.ShapeDtypeStruct((B * (NT - j0) * TM, FNET_WIDTH), F32),
        scratch_shapes=[pltpu.VMEM((2 * seq, FNET_WIDTH), BF16)],
        compiler_params=_cparams(("arbitrary", "arbitrary")),
        name="fnet",
    )(zf, *dft_lat, *dft_ctx, cd, w_fnet, b_fnet.reshape(1, -1))


def _dft_cos_sin(n):
    j = jnp.arange(n, dtype=jnp.int32)[:, None]
    if n <= 64:
        ang = ((j * j.T) % n).astype(F32) * (2.0 * np.pi / n)
        return jnp.cos(ang), jnp.sin(ang)
    k1 = jnp.arange(n // 64, dtype=jnp.int32)[None, :]
    k0 = jnp.arange(64, dtype=jnp.int32)[None, :]
    a = ((j * k1 * 64) % n).astype(F32) * (2.0 * np.pi / n)
    b = ((j * k0) % n).astype(F32) * (2.0 * np.pi / n)
    ca, sa, cb, sb = jnp.cos(a), jnp.sin(a), jnp.cos(b), jnp.sin(b)
    c = ca[:, :, None] * cb[:, None, :] - sa[:, :, None] * sb[:, None, :]
    s = sa[:, :, None] * cb[:, None, :] + ca[:, :, None] * sb[:, None, :]
    return c.reshape(n, n), s.reshape(n, n)


def _dft_tables(n):
    c, s = _dft_cos_sin(n)
    return c.astype(BF16), s.astype(BF16)


def _merge_kernel(*refs, n_stream, j0):
    x = _stream_tile(refs[:n_stream], j0)
    (om_ref, of_ref, on_ref, ga_ref, shf_ref, scf_ref, gout_ref, gffn_ref, wout_ref, wr_ref, br_ref,
     xn_ref, hf_ref, rw_ref, ri_ref, cnt_ref, run_ref) = refs[n_stream:]
    ym = _rms(om_ref[...], gout_ref[:, 0:MLA_WIDTH]).astype(BF16)
    yf = _rms(of_ref[...], gout_ref[:, MLA_WIDTH:MLA_WIDTH + FNET_WIDTH]).astype(BF16)
    yn = _rms(on_ref[...], gout_ref[:, MLA_WIDTH + FNET_WIDTH:]).astype(BF16)
    acc = _dot(ym, wout_ref[0, 0:MLA_WIDTH, :])
    acc = acc + _dot(yf, wout_ref[0, MLA_WIDTH:MLA_WIDTH + FNET_WIDTH, :])
    acc = acc + _dot(yn, wout_ref[0, MLA_WIDTH + FNET_WIDTH:, :])
    xn = x + ga_ref[0] * acc
    xn_ref[...] = xn
    hf = _rms(xn, gffn_ref[...]) * (1.0 + scf_ref[0]) + shf_ref[0]
    hf_ref[...] = hf
    hi = hf.astype(BF16)
    lo = (hf - hi.astype(F32)).astype(BF16)
    a = _dot(hi, wr_ref[0])
    b = _dot(lo, wr_ref[0])
    small = a[:, ROUTER_COLS:] + (b[:, :ROUTER_COLS] + b[:, ROUTER_COLS:])
    _route_tile(a[:, :ROUTER_COLS] + small + br_ref[0], rw_ref, ri_ref, cnt_ref, run_ref)


def _route_tile(lg, rw_ref, ri_ref, cnt_ref, run_ref):
    first = jnp.logical_and(pl.program_id(0) == 0, pl.program_id(1) == 0)

    @pl.when(first)
    def _():
        run_ref[...] = jnp.zeros_like(run_ref)

    lane = lax.broadcasted_iota(jnp.int32, lg.shape, 1)
    neg = jnp.float32(-jnp.inf)

    def top(v):
        vmax = jnp.max(v, axis=1, keepdims=True)
        idx = jnp.min(jnp.where(v == vmax, lane, ROUTER_COLS), axis=1, keepdims=True)
        return vmax, idx

    in_groups = lane < N_GROUPS
    gl = jnp.where(in_groups, lg, neg)
    g_max, g_sel = top(gl)
    g_w = 1.0 / jnp.sum(jnp.where(in_groups, jnp.exp(gl - g_max), 0.0), axis=1, keepdims=True)
    e_lo = N_GROUPS + g_sel * EXPERTS_PER_GROUP
    el = jnp.where(jnp.logical_and(lane >= e_lo, lane < e_lo + EXPERTS_PER_GROUP), lg, neg)
    e1_max, i1 = top(el)
    e2_max, i2 = top(jnp.where(lane == i1, neg, el))
    t = jnp.exp(e2_max - e1_max)
    w0 = g_w / (1.0 + t)
    w1 = w0 * t
    rw_ref[...] = jnp.where(lane == 0, w0, jnp.where(lane == 1, w1, 0.0))

    row = lax.broadcasted_iota(jnp.int32, (TM, TM), 0)
    col = lax.broadcasted_iota(jnp.int32, (TM, TM), 1)
    tri = jnp.where(row >= col, 1.0, 0.0).astype(BF16)
    hot0 = lane == i1
    hot1 = lane == i2
    c0 = _dot(tri, jnp.where(hot0, 1.0, 0.0).astype(BF16))
    c1 = _dot(tri, jnp.where(hot1, 1.0, 0.0).astype(BF16))
    run = run_ref[...]
    tot0 = c0[TM - 1:TM, :]
    rank0 = jnp.sum(jnp.where(hot0, run + c0 - 1.0, 0.0), axis=1, keepdims=True)
    rank1 = jnp.sum(jnp.where(hot1, run + tot0 + c1 - 1.0, 0.0), axis=1, keepdims=True)
    run = run + tot0 + c1[TM - 1:TM, :]
    run_ref[...] = run
    cnt_ref[...] = jnp.broadcast_to(run, cnt_ref.shape).astype(jnp.int32)
    ri_ref[...] = jnp.where(lane == 0, i1 - N_GROUPS, jnp.where(lane == 1, i2 - N_GROUPS, jnp.where(
        lane == 2, rank0.astype(jnp.int32), jnp.where(lane == 3, rank1.astype(jnp.int32), 0))))


def _merge(o_mla, o_f, o_na, xs, mods, g_out, g_ffn, w_out, w_router, b_router, layer, B, j0):
    T = B * (NT - j0) * TM
    row = lambda b, j: (_otile(b, j, j0), 0)
    const = lambda b, j: (0, 0)
    slab = lambda b, j: (layer, 0, 0)
    stream_specs, stream_args = _stream_specs(xs, j0)
    return pl.pallas_call(
        functools.partial(_merge_kernel, n_stream=len(stream_args), j0=j0),
        grid=(B, NT - j0),
        in_specs=stream_specs + [
            pl.BlockSpec((TM, MLA_WIDTH), row),
            pl.BlockSpec((TM, FNET_WIDTH), row),
            pl.BlockSpec((TM, NA_WIDTH), row),
            _mod_spec(2, j0),
            _mod_spec(3, j0),
            _mod_spec(4, j0),
            pl.BlockSpec((1, D), const),
            pl.BlockSpec((1, D), const),
            pl.BlockSpec((1, D, D), slab),
            pl.BlockSpec((1, D, 2 * ROUTER_COLS), slab),
            pl.BlockSpec((1, 1, ROUTER_COLS), slab),
        ],
        out_specs=[
            pl.BlockSpec((TM, D), row),
            pl.BlockSpec((TM, D), row),
            pl.BlockSpec((TM, ROUTER_COLS), row),
            pl.BlockSpec((TM, ROUTER_COLS), row),
            pl.BlockSpec((8, ROUTER_COLS), const),
        ],
        out_shape=[
            jax.ShapeDtypeStruct((T, D), F32),
            jax.ShapeDtypeStruct((T, D), F32),
            jax.ShapeDtypeStruct((T, ROUTER_COLS), F32),
            jax.ShapeDtypeStruct((T, ROUTER_COLS), jnp.int32),
            jax.ShapeDtypeStruct((8, ROUTER_COLS), jnp.int32),
        ],
        scratch_shapes=[pltpu.VMEM((1, ROUTER_COLS), F32)],
        compiler_params=_cparams(("arbitrary", "arbitrary")),
        name="merge",
    )(*stream_args, o_mla, o_f, o_na, mods, mods, mods, g_out.reshape(1, D), g_ffn.reshape(1, D),
      w_out, w_router, b_router)


def _plan_kernel(cnt_ref, slot_ref, src_ref, tile_ref, exp_ref, lo_ref, hi_ref, flag_ref, nxt_ref, ni_ref,
                 gs_ref, *, n_pairs, max_items):
    def starts(e, acc):
        gs_ref[e] = acc
        return acc + cnt_ref[e]

    lax.fori_loop(0, N_EXPERTS, starts, 0)

    def place(p, c):
        src_ref[slot_ref[p]] = lax.shift_right_logical(p, 1)
        return c

    lax.fori_loop(0, n_pairs, place, 0, unroll=16)

    last = N_EXPERTS - 1

    def group_end(e):
        return gs_ref[e] + cnt_ref[e]

    def next_nonempty(e):
        return lax.while_loop(lambda x: jnp.logical_and(x < last, cnt_ref[jnp.minimum(x, last)] == 0),
                              lambda x: x + 1, e)

    def tile_items(t, carry):
        i, e, prev = carry
        row0 = t * TM
        e = lax.while_loop(lambda x: group_end(x) <= row0, lambda x: x + 1, e)

        def emit(state):
            i, e, prev, first, _ = state
            tile_ref[i] = t
            exp_ref[i] = e
            lo_ref[i] = jnp.clip(gs_ref[e] - row0, 0, TM)
            hi_ref[i] = jnp.clip(group_end(e) - row0, 0, TM)
            flag_ref[i] = first + 2 * (e != prev).astype(jnp.int32)
            done = group_end(e) >= row0 + TM
            e_next = jnp.where(done, e, next_nonempty(e + 1))
            return i + 1, e_next, e, jnp.int32(0), done

        i, e, prev, _, _ = lax.while_loop(lambda s: jnp.logical_not(s[4]), emit,
                                          (i, e, prev, jnp.int32(1), jnp.bool_(False)))
        return i, e, prev

    n_items, _, _ = lax.fori_loop(0, n_pairs // TM, tile_items,
                                  (jnp.int32(0), jnp.int32(0), jnp.int32(-1)))
    ni_ref[0] = n_items

    def pad(i, c):
        tile_ref[i] = tile_ref[n_items - 1]
        exp_ref[i] = exp_ref[n_items - 1]
        lo_ref[i] = 0
        hi_ref[i] = 0
        flag_ref[i] = 0
        nxt_ref[i] = -1
        return c

    lax.fori_loop(n_items, max_items, pad, 0)

    def parity(i, par):
        par = jnp.where((flag_ref[i] & 2) != 0, 1 - par, par)
        flag_ref[i] = flag_ref[i] + 4 * par
        return par

    lax.fori_loop(0, n_items, parity, jnp.int32(1))

    def lookahead(k, following):
        i = n_items - 1 - k
        nxt_ref[i] = following
        return jnp.where((flag_ref[i] & 2) != 0, exp_ref[i], following)

    lax.fori_loop(0, n_items, lookahead, jnp.int32(-1))


def _plan(counts, eid, rank):
    n_pairs = eid.shape[0]
    max_items = n_pairs // TM + N_EXPERTS - 1
    g_start = jnp.cumsum(counts) - counts
    experts = jnp.arange(N_EXPERTS, dtype=jnp.int32)
    slot = jnp.sum(jnp.where(eid[:, None] == experts[None, :], g_start[None, :], 0), axis=1) + rank
    smem = pl.BlockSpec(memory_space=pltpu.SMEM)
    i32 = lambda n: jax.ShapeDtypeStruct((n,), jnp.int32)
    src, it_tile, it_exp, it_lo, it_hi, flags, nxt, n_items = pl.pallas_call(
        functools.partial(_plan_kernel, n_pairs=n_pairs, max_items=max_items),
        in_specs=[smem] * 2,
        out_specs=[smem] * 8,
        out_shape=[i32(n_pairs)] + [i32(max_items)] * 6 + [i32(1)],
        scratch_shapes=[pltpu.SMEM((N_EXPERTS,), jnp.int32)],
        name="moe_plan",
    )(counts, slot)
    return (it_tile, it_exp, it_lo, it_hi, flags, nxt, n_items, src), slot


def _moe_kernel(tile_ref, exp_ref, lo_ref, hi_ref, flag_ref, nxt_ref, ni_ref, src_ref,
                hf_hbm, wg_hbm, wu_hbm, wd_hbm, y_ref, xbuf, wg_buf, wu_buf, wd_buf, wgb, wub, wdb,
                sem, wsem, *, n_tiles, e0):
    i = pl.program_id(0)
    t = tile_ref[i]
    slot = lax.rem(t, 2)

    def gather_start(tile, buf, unrolled):
        base = tile * TM

        def issue(r):
            tok = src_ref[base + r]
            pltpu.make_async_copy(hf_hbm.at[pl.ds(tok, 1)], xbuf.at[buf, pl.ds(r, 1)], sem.at[buf]).start()

        if unrolled:
            for r in range(TM):
                issue(r)
        else:
            lax.fori_loop(0, TM, lambda r, c: (issue(r), c)[1], 0)

    def gather_wait(buf):
        pltpu.make_async_copy(hf_hbm.at[pl.ds(0, TM)], xbuf.at[buf], sem.at[buf]).wait()

    def weight_copies(expert, b):
        e = e0 + expert
        return (pltpu.make_async_copy(wg_hbm.at[e], wg_buf.at[b], wsem.at[b]),
                pltpu.make_async_copy(wu_hbm.at[e], wu_buf.at[b], wsem.at[b]),
                pltpu.make_async_copy(wd_hbm.at[e], wd_buf.at[b], wsem.at[b]))

    def weights_start(expert, b):
        for cp in weight_copies(expert, b):
            cp.start(priority=1)

    def weights_wait(b):
        for cp in weight_copies(0, b):
            cp.wait()

    @pl.when(i < ni_ref[0])
    def _():
        first_visit = (flag_ref[i] & 1) != 0
        new_expert = (flag_ref[i] & 2) != 0
        wslot = lax.shift_right_logical(flag_ref[i], 2) & 1

        @pl.when(i == 0)
        def _():
            gather_start(0, 0, False)

        @pl.when(first_visit)
        def _():
            gather_wait(slot)

        for b in range(2):
            @pl.when(jnp.logical_and(jnp.logical_and(first_visit, t + 1 < n_tiles), slot == 1 - b))
            def _():
                gather_start(t + 1, b, True)

        @pl.when(i == 0)
        def _():
            weights_start(exp_ref[0], 0)

        @pl.when(new_expert)
        def _():
            @pl.when(nxt_ref[i] >= 0)
            def _():
                weights_start(nxt_ref[i], 1 - wslot)

            weights_wait(wslot)
            wgb[...] = wg_buf[wslot].astype(BF16)
            wub[...] = wu_buf[wslot].astype(BF16)
            wdb[...] = wd_buf[wslot].astype(BF16)

        x = xbuf[slot].astype(BF16)
        a = _dot(x, wgb[...])
        u = _dot(x, wub[...])
        row = lax.broadcasted_iota(jnp.int32, (TM, 1), 0)
        mine = jnp.logical_and(row >= lo_ref[i], row < hi_ref[i])
        h = jnp.where(mine, (a * jax.nn.sigmoid(a)) * u, 0.0)
        yv = _dot(h.astype(BF16), wdb[...])

        @pl.when(first_visit)
        def _():
            y_ref[...] = yv

        @pl.when(jnp.logical_not(first_visit))
        def _():
            y_ref[...] += yv


def _moe(hf, meta, w_gate, w_up, w_down, layer):
    n_rows = meta[-1].shape[0]
    n_tiles = n_rows // TM
    max_items = meta[0].shape[0]
    grid_spec = pltpu.PrefetchScalarGridSpec(
        num_scalar_prefetch=len(meta),
        grid=(max_items,),
        in_specs=[pl.BlockSpec(memory_space=pl.ANY)] * 4,
        out_specs=pl.BlockSpec((TM, D), lambda i, tile, *_: (tile[i], 0)),
        scratch_shapes=[
            pltpu.VMEM((2, TM, D), F32),
            pltpu.VMEM((2, D, D_EXPERT), F32),
            pltpu.VMEM((2, D, D_EXPERT), F32),
            pltpu.VMEM((2, D_EXPERT, D), F32),
            pltpu.VMEM((D, D_EXPERT), BF16),
            pltpu.VMEM((D, D_EXPERT), BF16),
            pltpu.VMEM((D_EXPERT, D), BF16),
            pltpu.SemaphoreType.DMA((2,)),
            pltpu.SemaphoreType.DMA((2,)),
        ],
    )
    return pl.pallas_call(
        functools.partial(_moe_kernel, n_tiles=n_tiles, e0=layer * N_EXPERTS),
        grid_spec=grid_spec,
        out_shape=jax.ShapeDtypeStruct((n_rows, D), F32),
        compiler_params=_cparams(("arbitrary",)),
        name="moe_experts",
    )(*meta, hf, w_gate.reshape(-1, D, D_EXPERT), w_up.reshape(-1, D, D_EXPERT),
      w_down.reshape(-1, D_EXPERT, D))


def _final_kernel(pos_ref, x_ref, gf_ref, w_ref, g_ref, y_hbm, o_ref, ybuf, sem, *, n_steps):
    step = pl.program_id(0) * 8 + pl.program_id(1)
    out = _moe_residual(pos_ref, y_hbm, ybuf, sem, x_ref[...], gf_ref[0], w_ref[...], step, n_steps)
    o_ref[0] = _rms(out, g_ref[...])


def _final(pos, wts, xn, mods, y, g_final, B):
    tile = lambda b, j, p: (b * 8 + j, 0)
    grid_spec = pltpu.PrefetchScalarGridSpec(
        num_scalar_prefetch=1,
        grid=(B, 8),
        in_specs=[
            pl.BlockSpec((TM, D), tile),
            pl.BlockSpec((1, 1, D), lambda b, j, p: (b * 6 + 5, 0, 0)),
            pl.BlockSpec((TM, 2), tile),
            pl.BlockSpec((1, D), lambda b, j, p: (0, 0)),
            pl.BlockSpec(memory_space=pl.ANY),
        ],
        out_specs=pl.BlockSpec((1, TM, D), lambda b, j, p: (b, j, 0)),
        scratch_shapes=_MOE_GATHER_SCRATCH,
    )
    return pl.pallas_call(
        functools.partial(_final_kernel, n_steps=B * 8),
        grid_spec=grid_spec,
        out_shape=jax.ShapeDtypeStruct((B, 8 * TM, D), F32),
        compiler_params=_cparams(("arbitrary", "arbitrary")),
        name="final",
    )(pos, xn, mods, wts, g_final.reshape(1, D), y)


def _prep_w_in(w_in):
    return jnp.swapaxes(w_in, 1, 2).astype(BF16)


def _prep_w_uq(w_uq):
    w = w_uq.reshape(Q_LORA, MLA_HEADS, QK_NOPE + QK_ROPE)
    w = jnp.pad(w, ((0, 0), (0, 0), (0, MLA_SLAB - QK_NOPE - QK_ROPE)))
    return w.reshape(Q_LORA, MLA_HEADS * MLA_SLAB).astype(BF16)


def _rope_tables(seq):
    half = QK_ROPE // 2
    inv_freq = ROPE_THETA ** (-jnp.arange(0, half, 2, dtype=F32) / half)
    t = jnp.arange(seq, dtype=jnp.int32)
    row = (t // GRID_W).astype(F32)
    col = (t % GRID_W).astype(F32)
    ang = jnp.concatenate([row[:, None] * inv_freq, col[:, None] * inv_freq], axis=-1)
    cos, sin = jnp.cos(ang), jnp.sin(ang)
    zeros = jnp.zeros((seq, 64), F32)
    cos_l = jnp.concatenate([jnp.repeat(cos, 2, axis=1), zeros], axis=1)
    sin_l = jnp.concatenate([jnp.stack([-sin, sin], axis=-1).reshape(seq, 64), zeros], axis=1)
    cos_c = jnp.concatenate([jnp.ones((CTX_LEN, 64), F32), jnp.zeros((CTX_LEN, 64), F32)], axis=1)
    sin_c = jnp.zeros((CTX_LEN, 128), F32)
    return jnp.concatenate([cos_c, cos_l], axis=0), jnp.concatenate([sin_c, sin_l], axis=0)


def kernel(x, c, ctx, c_ctx, w_ada, b_ada, g_attn, g_ffn, w_in, g_q, w_uq, g_kv, w_ukv, w_fnet, b_fnet,
           na_rpb, g_out, w_out, w_rg, b_rg, w_re, b_re, w_gate, w_up, w_down, g_final):
    B, S, _ = x.shape
    L = w_ada.shape[0]
    assert ctx.shape[1] == CTX_LEN == TM and S == 8 * TM and B <= 4
    T = B * NT * TM

    cond8 = jnp.concatenate([c, jnp.zeros((4 - B, D), F32), c_ctx[None], jnp.zeros((3, D), F32)], axis=0)
    mods_all = _modulation(cond8, w_ada, b_ada)
    w_in_ext = _prep_w_in(w_in)
    pad = ROUTER_COLS - N_GROUPS - N_EXPERTS
    w_router = jnp.concatenate([w_rg, w_re, jnp.zeros((L, D, pad), F32)], axis=2)
    w_scaled = w_router * 65537.0
    w_router_hi = w_scaled - (w_scaled - w_router)
    w_router_all = jnp.concatenate([w_router_hi, w_router - w_router_hi], axis=2).astype(BF16)
    b_router_all = jnp.concatenate([b_rg, b_re, jnp.zeros((L, pad), F32)], axis=1).reshape(L, 1, ROUTER_COLS)
    w_out_all = w_out.astype(BF16)
    na_bias = _na_bias(na_rpb)
    cos_t, sin_t = _rope_tables(S)
    dft_lat = _dft_tables(S)
    dft_ctx = _dft_tables(CTX_LEN)
    cd_c, cd_s = _dft_cos_sin(FNET_GROUP_DIM)
    cd = jnp.concatenate([cd_c, -cd_s], axis=1).astype(BF16)

    xs = (ctx.reshape(B * CTX_LEN, D), x.reshape(B * S, D))
    pending = None
    for l in range(L):
        last = l == L - 1
        j0 = 1 if last else 0
        mods = mods_all[l].reshape(48, 1, D)
        w_uq_ext = _prep_w_uq(w_uq[l])
        mla = (g_q[l], g_kv[l], w_uq_ext, w_ukv[l].astype(BF16), cos_t, sin_t)
        if pending is None:
            zf, naq, nak, navt, q, k, vt = _in_projection(xs, mods, g_attn[l], w_in_ext, l, mla, B)
        else:
            zf, naq, nak, navt, q, k, vt, xs = _in_projection(xs, mods, g_attn[l], w_in_ext, l, mla, B,
                                                              moe=pending)
        o_mla = _mla_attention(q, k, vt, B, j0)
        o_na = _na_attention(naq, nak, navt, na_bias, l, B, j0)
        w_f = w_fnet[l].astype(BF16)
        o_f = _fnet(zf, dft_lat, dft_ctx, cd, w_f, b_fnet[l], B, j0)
        xn, hf, route_w, route_i, counts = _merge(o_mla, o_f, o_na, xs, mods, g_out[l], g_ffn[l],
                                                  w_out_all, w_router_all, b_router_all, l, B, j0)
        wts = route_w[:, 0:2]
        meta, slot = _plan(counts[0, N_GROUPS:N_GROUPS + N_EXPERTS], route_i[:, 0:2].reshape(-1),
                           route_i[:, 2:4].reshape(-1))
        y = _moe(hf, meta, w_gate, w_up, w_down, l)
        if last:
            return _final(slot, wts, xn, mods, y, g_final, B)
        xs, pending = xn, (slot, wts, y, mods)
```

```python
import functools

import numpy as np
import jax
import jax.numpy as jnp
from jax import lax
from jax.experimental import pallas as pl
from jax.experimental.pallas import tpu as pltpu

F32 = jnp.float32
BF16 = jnp.bfloat16

D = 2048
GRID_W = 64
CTX_LEN = 256
EPS = 1e-6
NEG_INF = -1e30
ROPE_THETA = 10000.0

V_DIM = 128
MLA_WIDTH = D // 2
MLA_HEADS = MLA_WIDTH // V_DIM
QK_NOPE = 128
QK_ROPE = 64
Q_LORA = D // 4
KV_LORA = D // 8
FNET_WIDTH = D // 4
FNET_GROUP_DIM = 128
FNET_GROUPS = FNET_WIDTH // FNET_GROUP_DIM
NA_WIDTH = D // 4
NA_HEAD_DIM = 128
NA_HEADS = NA_WIDTH // NA_HEAD_DIM
NA_KH_MAX = 8
NA_KW = 16
N_GROUPS = 4
EXPERTS_PER_GROUP = 8
N_EXPERTS = N_GROUPS * EXPERTS_PER_GROUP
D_EXPERT = D // 4

MLA_QSCALE = (QK_NOPE + QK_ROPE) ** -0.5 * float(np.log2(np.e))
MLA_SLAB = QK_NOPE + V_DIM
MLA_CHUNK_TILES = 8
NA_QSCALE = NA_HEAD_DIM ** -0.5 * float(np.log2(np.e))
NA_WIN_ROWS = 12
TM = 256
NT = 9
IN_COLS = Q_LORA + KV_LORA + QK_ROPE + FNET_WIDTH + 3 * NA_WIDTH
ROUTER_COLS = 128
VMEM_LIMIT = 56 * 1024 * 1024


def _cparams(sem):
    return pltpu.CompilerParams(dimension_semantics=sem, vmem_limit_bytes=VMEM_LIMIT)


def _rms(v, g):
    return v * lax.rsqrt(jnp.mean(v * v, axis=-1, keepdims=True) + EPS) * g


def _dot(a, b):
    return jnp.dot(a, b, preferred_element_type=F32)


def _dot_nt(a, b):
    return lax.dot_general(a, b, (((1,), (1,)), ((), ())), preferred_element_type=F32)


def _mod_kernel(c_ref, w_ref, b_ref, o_ref):
    c = c_ref[...]
    s = c * jax.nn.sigmoid(c)
    o_ref[0] = _dot(s.astype(BF16), w_ref[0].astype(BF16)) + b_ref[0]


def _modulation(cond8, w_ada, b_ada):
    L = w_ada.shape[0]
    tn = 1024
    return pl.pallas_call(
        _mod_kernel,
        grid=(L, 6 * D // tn),
        in_specs=[
            pl.BlockSpec((8, D), lambda l, n: (0, 0)),
            pl.BlockSpec((1, D, tn), lambda l, n: (l, 0, n)),
            pl.BlockSpec((1, 1, tn), lambda l, n: (l, 0, n)),
        ],
        out_specs=pl.BlockSpec((1, 8, tn), lambda l, n: (l, 0, n)),
        out_shape=jax.ShapeDtypeStruct((L, 8, 6 * D), F32),
        compiler_params=_cparams(("arbitrary", "arbitrary")),
        name="modulation",
    )(cond8, w_ada, b_ada.reshape(L, 1, 6 * D))


def _tile_of(b, j, j0):
    return b * NT + j0 + j


def _otile(b, j, j0):
    return b * (NT - j0) + j


def _mod_row(j, b, j0):
    return jnp.where(j0 + j == 0, 4, b)


def _mod_spec(k, j0):
    return pl.BlockSpec((1, 1, D), lambda b, j: (_mod_row(j, b, j0) * 6 + k, 0, 0))


def _stream_specs(xs, j0):
    if isinstance(xs, tuple):
        return [pl.BlockSpec((TM, D), lambda b, j, *_: (b, 0)),
                pl.BlockSpec((TM, D), lambda b, j, *_: (b * 8 + jnp.maximum(j0 + j - 1, 0), 0))], list(xs)
    return [pl.BlockSpec((TM, D), lambda b, j, *_: (_tile_of(b, j, j0), 0))], [xs]


def _stream_tile(refs, j0):
    if len(refs) == 1:
        return refs[0][...]
    return jnp.where(pl.program_id(1) + j0 == 0, refs[0][...], refs[1][...])


def _moe_residual(pos_ref, y_hbm, ybuf, sem, x, gate, w, step, n_steps):
    buf = lax.rem(step, 2)

    def start(tile, b, unrolled):
        base = tile * TM

        def issue(r):
            for k in range(2):
                pltpu.make_async_copy(y_hbm.at[pl.ds(pos_ref[2 * (base + r) + k], 1)],
                                      ybuf.at[b, k, pl.ds(r, 1)], sem.at[b]).start(priority=k)

        if unrolled:
            for r in range(TM):
                issue(r)
        else:
            lax.fori_loop(0, TM, lambda r, c: (issue(r), c)[1], 0)

    def wait(b):
        for k in range(2):
            pltpu.make_async_copy(y_hbm.at[pl.ds(0, TM)], ybuf.at[b, k], sem.at[b]).wait()

    @pl.when(step == 0)
    def _():
        start(0, 0, False)

    for b in range(2):
        @pl.when(buf == 1 - b)
        def _():
            start(jnp.minimum(step + 1, n_steps - 1), b, True)

    wait(buf)
    out = x + gate * (w[:, 0:1] * ybuf[buf, 0] + w[:, 1:2] * ybuf[buf, 1])

    @pl.when(step == n_steps - 1)
    def _():
        wait(1 - buf)

    return out


_MOE_GATHER_SCRATCH = [pltpu.VMEM((2, 2, TM, D), F32), pltpu.SemaphoreType.DMA((2,))]


def _rope(r, cos_t, sin_t):
    lane = lax.broadcasted_iota(jnp.int32, r.shape, 1)
    partner = jnp.where(lane % 2 == 0, pltpu.roll(r, 127, 1), pltpu.roll(r, 1, 1))
    return r * cos_t + partner * sin_t


def _inproj_body(x, sh_ref, sc_ref, g_ref, w_ref, gq_ref, gkv_ref, wq_ref, wkv_ref, cos_ref, sin_ref,
                 zf_ref, naq_ref, nak_ref, navt_ref, q_ref, k_ref, vt_ref):
    h = _rms(x, g_ref[...])
    h = h * (1.0 + sc_ref[0]) + sh_ref[0]
    z = _dot_nt(h.astype(BF16), w_ref[0])
    c_zf =Q_LORA + KV_LORA + QK_ROPE
    zf_ref[...] = z[:, c_zf:c_zf + FNET_WIDTH].astype(BF16)
    c_na = c_zf + FNET_WIDTH
    naq_ref[...] = (z[:, c_na:c_na + NA_WIDTH] * NA_QSCALE).astype(BF16)
    nak_ref[...] = z[:, c_na + NA_WIDTH:c_na + 2 * NA_WIDTH].astype(BF16)
    for hd in range(NA_HEADS):
        c = c_na + 2 * NA_WIDTH + hd * NA_HEAD_DIM
        navt_ref[:, hd * NA_HEAD_DIM:(hd + 1) * NA_HEAD_DIM] = z[:, c:c + NA_HEAD_DIM].astype(BF16)
    cos_t = cos_ref[...]
    sin_t = sin_ref[...]
    q = _dot(_rms(z[:, 0:Q_LORA], gq_ref[...]).astype(BF16), wq_ref[...]) * MLA_QSCALE
    kv = _dot(_rms(z[:, Q_LORA:Q_LORA + KV_LORA], gkv_ref[...]).astype(BF16), wkv_ref[...])
    k_rope = _rope(z[:, Q_LORA + KV_LORA:Q_LORA + KV_LORA + 128], cos_t, sin_t).astype(BF16)
    for hd in range(MLA_HEADS):
        c = hd * MLA_SLAB
        r = c + QK_NOPE
        q_ref[:, c:r] = q[:, c:r].astype(BF16)
        q_ref[:, r:c + MLA_SLAB] = _rope(q[:, r:c + MLA_SLAB], cos_t, sin_t).astype(BF16)
        k_ref[:, c:r] = kv[:, c:r].astype(BF16)
        k_ref[:, r:c + MLA_SLAB] = k_rope
        vt_ref[:, hd * V_DIM:(hd + 1) * V_DIM] = kv[:, r:c + MLA_SLAB].astype(BF16)


def _inproj_kernel(*refs, n_stream):
    _inproj_body(_stream_tile(refs[:n_stream], 0), *refs[n_stream:])


def _inproj_moe_kernel(pos_ref, x_ref, gf_ref, wts_ref, y_hbm, *refs, n_steps):
    *refs, xs_ref, ybuf, sem = refs
    step = pl.program_id(0) * NT + pl.program_id(1)
    x = _moe_residual(pos_ref, y_hbm, ybuf, sem, x_ref[...], gf_ref[0], wts_ref[...], step, n_steps)
    xs_ref[...] = x
    _inproj_body(x, *refs)


def _in_projection(xs, mods, g_attn, w_in_ext, layer, mla, B, moe=None):
    T = B * NT * TM
    row = lambda b, j, *_: (_tile_of(b, j, 0), 0)
    col = lambda b, j, *_: (0, _tile_of(b, j, 0))
    const = lambda b, j, *_: (0, 0)
    mod = lambda k: pl.BlockSpec((1, 1, D), lambda b, j, *_: (_mod_row(j, b, 0) * 6 + k, 0, 0))
    once = dict(pipeline_mode=pl.Buffered(1))
    g_q, g_kv, w_uq_ext, w_ukv, cos_t, sin_t = mla
    in_specs = [
        pl.BlockSpec((TM, D), row),
        mod(0),
        mod(1),
        pl.BlockSpec((1, D), const),
        pl.BlockSpec((1, IN_COLS, D), lambda b, j, *_: (layer, 0, 0), **once),
        pl.BlockSpec((1, Q_LORA), const),
        pl.BlockSpec((1, KV_LORA), const),
        pl.BlockSpec((Q_LORA, MLA_HEADS * MLA_SLAB), const, **once),
        pl.BlockSpec((KV_LORA, MLA_HEADS * MLA_SLAB), const, **once),
        pl.BlockSpec((TM, 128), lambda b, j, *_: (j, 0)),
        pl.BlockSpec((TM, 128), lambda b, j, *_: (j, 0)),
    ]
    out_specs = [
        pl.BlockSpec((TM, FNET_WIDTH), row),
        pl.BlockSpec((TM, NA_WIDTH), row),
        pl.BlockSpec((TM, NA_WIDTH), row),
        pl.BlockSpec((TM, NA_WIDTH), row),
        pl.BlockSpec((TM, MLA_HEADS * MLA_SLAB), row),
        pl.BlockSpec((TM, MLA_HEADS * MLA_SLAB), row),
        pl.BlockSpec((TM, MLA_WIDTH), row),
    ]
    out_shape = [
        jax.ShapeDtypeStruct((T, FNET_WIDTH), BF16),
        jax.ShapeDtypeStruct((T, NA_WIDTH), BF16),
        jax.ShapeDtypeStruct((T, NA_WIDTH), BF16),
        jax.ShapeDtypeStruct((T, NA_WIDTH), BF16),
        jax.ShapeDtypeStruct((T, MLA_HEADS * MLA_SLAB), BF16),
        jax.ShapeDtypeStruct((T, MLA_HEADS * MLA_SLAB), BF16),
        jax.ShapeDtypeStruct((T, MLA_WIDTH), BF16),
    ]
    args = [xs, mods, mods, g_attn.reshape(1, D), w_in_ext, g_q.reshape(1, -1), g_kv.reshape(1, -1),
            w_uq_ext, w_ukv, cos_t, sin_t]
    if moe is None:
        stream_specs, stream_args = _stream_specs(xs, 0)
        return pl.pallas_call(
            functools.partial(_inproj_kernel, n_stream=len(stream_args)),
            grid=(B, NT),
            in_specs=stream_specs + in_specs[1:],
            out_specs=out_specs,
            out_shape=out_shape,
            compiler_params=_cparams(("arbitrary", "arbitrary")),
            name="in_projection",
        )(*stream_args, *args[1:])
    pos, wts, y, mods_prev = moe
    in_specs = [in_specs[0], mod(5), pl.BlockSpec((TM, 2), row), pl.BlockSpec(memory_space=pl.ANY)] \
        + in_specs[1:]
    grid_spec = pltpu.PrefetchScalarGridSpec(
        num_scalar_prefetch=1,
        grid=(B, NT),
        in_specs=in_specs,
        out_specs=out_specs + [pl.BlockSpec((TM, D), row)],
        scratch_shapes=_MOE_GATHER_SCRATCH,
    )
    return pl.pallas_call(
        functools.partial(_inproj_moe_kernel, n_steps=B * NT),
        grid_spec=grid_spec,
        out_shape=out_shape + [jax.ShapeDtypeStruct((T, D), F32)],
        compiler_params=_cparams(("arbitrary", "arbitrary")),
        name="in_projection_moe",
    )(pos, xs, mods_prev, wts, y, *args[1:])


def _softmax_numerator(st, m):
    return jnp.exp2(st - m).astype(BF16)


def _pv_with_sum(vt, p):
    ones = jnp.ones((2 * 8, vt.shape[1]), BF16)
    o = _dot(jnp.concatenate([vt, ones], axis=0), p)
    return o[0:vt.shape[0]], o[vt.shape[0]:vt.shape[0] + 1]


def _transposed(v_ref):
    return v_ref[...].astype(F32).T.astype(BF16)


def _mla_attn_kernel(q_ref, k_ref, v_ref, o_ref, *, j0):
    vt = _transposed(v_ref)

    def attend(q0, nq, nk, o0):
        st = _dot_nt(k_ref[0:nk, :], q_ref[q0:q0 + nq, :])
        m = jnp.max(st, axis=0, keepdims=True)
        ot, l = _pv_with_sum(vt[:, 0:nk], _softmax_numerator(st, m))
        o_ref[o0:o0 + nq, :] = (ot / l).T

    if j0 == 0:
        attend(0, CTX_LEN, CTX_LEN, 0)
    for c in range(8 // MLA_CHUNK_TILES):
        attend(CTX_LEN + MLA_CHUNK_TILES * c * TM, MLA_CHUNK_TILES * TM, NT * TM,
               (1 - j0 + MLA_CHUNK_TILES * c) * TM)


def _mla_attention(q, k, vt, B, j0):
    rows = (NT - j0) * TM
    return pl.pallas_call(
        functools.partial(_mla_attn_kernel, j0=j0),
        grid=(B, MLA_HEADS),
        in_specs=[
            pl.BlockSpec((NT * TM, MLA_SLAB), lambda b, h: (b, h)),
            pl.BlockSpec((NT * TM, MLA_SLAB), lambda b, h: (b, h)),
            pl.BlockSpec((NT * TM, V_DIM), lambda b, h: (b, h)),
        ],
        out_specs=pl.BlockSpec((rows, V_DIM), lambda b, h: (b, h)),
        out_shape=jax.ShapeDtypeStruct((B * rows, MLA_WIDTH), F32),
        compiler_params=_cparams(("arbitrary", "arbitrary")),
        name="mla_attention",
    )(q, k, vt)


def _na_chunk(g):
    start_row = min(max(4 * g - 4, 0), 8 * TM // GRID_W - NA_WIN_ROWS)
    pattern = 0 if g == 0 else (2 if g == 7 else 1)
    return start_row, pattern


def _na_kernel(q_ref, k_ref, v_ref, bias_ref, o_ref, *, j0):
    vt = _transposed(v_ref)

    def finish(parts, o0):
        m = None
        for st, _ in parts:
            pm = jnp.max(st, axis=0, keepdims=True)
            m = pm if m is None else jnp.maximum(m, pm)
        l = None
        ot = None
        for st, vt in parts:
            po, pl_sum = _pv_with_sum(vt, _softmax_numerator(st, m))
            l = pl_sum if l is None else l + pl_sum
            ot = po if ot is None else ot + po
        o_ref[o0:o0 + TM, :] = (ot / l).T

    if j0 == 0:
        st = _dot_nt(k_ref[0:CTX_LEN, :], q_ref[0:CTX_LEN, :])
        finish([(st, vt[:, 0:CTX_LEN])], 0)
    for g in range(8):
        start_row, pattern = _na_chunk(g)
        k0 = CTX_LEN + start_row * GRID_W
        nk = NA_WIN_ROWS * GRID_W
        q = q_ref[CTX_LEN + g * TM:CTX_LEN + (g + 1) * TM, :]
        st_loc = _dot_nt(k_ref[k0:k0 + nk, :], q) + bias_ref[0, pattern]
        st_ctx = _dot_nt(k_ref[0:CTX_LEN, :], q)
        finish([(st_loc, vt[:, k0:k0 + nk]), (st_ctx, vt[:, 0:CTX_LEN])], (1 - j0 + g) * TM)


def _na_attention(naq, nak, navt, bias, layer, B, j0):
    rows = (NT - j0) * TM
    bias = bias.reshape((-1,) + bias.shape[2:])
    return pl.pallas_call(
        functools.partial(_na_kernel, j0=j0),
        grid=(NA_HEADS, B),
        in_specs=[
            pl.BlockSpec((NT * TM, NA_HEAD_DIM), lambda h, b: (b, h)),
            pl.BlockSpec((NT * TM, NA_HEAD_DIM), lambda h, b: (b, h)),
            pl.BlockSpec((NT * TM, NA_HEAD_DIM), lambda h, b: (b, h)),
            pl.BlockSpec((1, 3, NA_WIN_ROWS * GRID_W, TM), lambda h, b: (layer * NA_HEADS + h, 0, 0, 0)),
        ],
        out_specs=pl.BlockSpec((rows, NA_HEAD_DIM), lambda h, b: (b, h)),
        out_shape=jax.ShapeDtypeStruct((B * rows, NA_WIDTH), F32),
        compiler_params=_cparams(("arbitrary", "arbitrary")),
        name="na_attention",
    )(naq, nak, navt, bias)


def _na_bias(rpb):
    kh, rows, nq = NA_KH_MAX, 8 * TM // GRID_W, TM // GRID_W
    cq = np.arange(GRID_W)
    ck = np.arange(GRID_W)
    col_start = np.clip(cq - NA_KW // 2, 0, GRID_W - NA_KW)
    col_ok = (ck[:, None] >= col_start[None, :]) & (ck[:, None] < col_start[None, :] + NA_KW)
    dcol = np.clip(ck[:, None] - cq[None, :] + (NA_KW - 1), 0, 2 * NA_KW - 2)
    select = np.zeros((2 * NA_KW - 1, GRID_W * GRID_W), np.float32)
    select[dcol.reshape(-1), np.arange(GRID_W * GRID_W)] = 1.0
    blocks = jnp.einsum("lhdm,mn->lhdn", rpb.astype(F32) * float(np.log2(np.e)), jnp.asarray(select),
                        precision=lax.Precision.HIGHEST)
    blocks = blocks.reshape(rpb.shape[:3] + (GRID_W, GRID_W))
    blocks = jnp.where(jnp.asarray(col_ok), blocks, NEG_INF)
    masked = jnp.full(rpb.shape[:2] + (GRID_W, GRID_W), NEG_INF, F32)
    patterns = []
    for g in (0, 1, 7):
        start_row, _ = _na_chunk(g)
        key_rows = []
        for kr in range(NA_WIN_ROWS):
            key_row = start_row + kr
            row = []
            for qr in range(nq):
                r = 4 * g + qr
                r_start = min(max(r - kh // 2, 0), rows - kh)
                in_rows = r_start <= key_row < r_start + kh
                row.append(blocks[:, :, key_row - r + (kh - 1)] if in_rows else masked)
            key_rows.append(jnp.concatenate(row, axis=-1))
        patterns.append(jnp.concatenate(key_rows, axis=-2))
    return jnp.stack(patterns, axis=2)


def _fnet_kernel(z_ref, cl_ref, sl_ref, cc_ref, sc_ref, cd_ref, w_ref, b_ref, o_ref, ab_ref, *, j0, seq):
    j = pl.program_id(1) + j0

    def small_side(row0, length):
        for g in range(FNET_GROUPS):
            c = g * FNET_GROUP_DIM
            ab = _dot(z_ref[row0:row0 + length, c:c + FNET_GROUP_DIM], cd_ref[...])
            ab_ref[0:length, c:c + FNET_GROUP_DIM] = ab[:, 0:FNET_GROUP_DIM].astype(BF16)
            ab_ref[length:2 * length, c:c + FNET_GROUP_DIM] = ab[:, FNET_GROUP_DIM:].astype(BF16)

    def long_side(c, s, length):
        f = _dot(c, ab_ref[0:length, :]) + _dot(s, ab_ref[length:2 * length, :])
        f = f * (length * FNET_GROUP_DIM) ** -0.5
        o_ref[...] = _dot(f.astype(BF16), w_ref[...]) + b_ref[...]

    if j0 == 0:
        @pl.when(j == 0)
        def _():
            small_side(0, CTX_LEN)
            long_side(cc_ref[...], sc_ref[...], CTX_LEN)

    @pl.when(j == 1)
    def _():
        small_side(CTX_LEN, seq)

    @pl.when(j >= 1)
    def _():
        long_side(cl_ref[...], sl_ref[...], seq)


def _fnet(zf, dft_lat, dft_ctx, cd, w_fnet, b_fnet, B, j0):
    seq = 8 * TM
    lat_rows = lambda b, j: (jnp.maximum(j0 + j - 1, 0), 0)
    return pl.pallas_call(
        functools.partial(_fnet_kernel, j0=j0, seq=seq),
        grid=(B, NT - j0),
        in_specs=[
            pl.BlockSpec((NT * TM, FNET_WIDTH), lambda b, j: (b, 0)),
            pl.BlockSpec((TM, seq), lat_rows),
            pl.BlockSpec((TM, seq), lat_rows),
            pl.BlockSpec((CTX_LEN, CTX_LEN), lambda b, j: (0, 0)),
            pl.BlockSpec((CTX_LEN, CTX_LEN), lambda b, j: (0, 0)),
            pl.BlockSpec((FNET_GROUP_DIM, 2 * FNET_GROUP_DIM), lambda b, j: (0, 0)),
            pl.BlockSpec((FNET_WIDTH, FNET_WIDTH), lambda b, j: (0, 0)),
            pl.BlockSpec((1, FNET_WIDTH), lambda b, j: (0, 0)),
        ],
        out_specs=pl.BlockSpec((TM, FNET_WIDTH), lambda b, j: (_otile(b, j, j0), 0)),
        out_shape=jax.ShapeDtypeStruct((B * (NT - j0) * TM, FNET_WIDTH), F32),
        scratch_shapes=[pltpu.VMEM((2 * seq, FNET_WIDTH), BF16)],
        compiler_params=_cparams(("arbitrary", "arbitrary")),
        name="fnet",
    )(zf, *dft_lat, *dft_ctx, cd, w_fnet, b_fnet.reshape(1, -1))


def _dft_cos_sin(n):
    j = jnp.arange(n, dtype=jnp.int32)[:, None]
    if n <= 64:
        ang = ((j * j.T) % n).astype(F32) * (2.0 * np.pi / n)
        return jnp.cos(ang), jnp.sin(ang)
    k1 = jnp.arange(n // 64, dtype=jnp.int32)[None, :]
    k0 = jnp.arange(64, dtype=jnp.int32)[None, :]
    a = ((j * k1 * 64) % n).astype(F32) * (2.0 * np.pi / n)
    b = ((j * k0) % n).astype(F32) * (2.0 * np.pi / n)
    ca, sa, cb, sb = jnp.cos(a), jnp.sin(a), jnp.cos(b), jnp.sin(b)
    c = ca[:, :, None] * cb[:, None, :] - sa[:, :, None] * sb[:, None, :]
    s = sa[:, :, None] * cb[:, None, :] + ca[:, :, None] * sb[:, None, :]
    return c.reshape(n, n), s.reshape(n, n)


def _dft_tables(n):
    c, s = _dft_cos_sin(n)
    return c.astype(BF16), s.astype(BF16)


def _merge_kernel(*refs, n_stream, j0):
    x = _stream_tile(refs[:n_stream], j0)
    (om_ref, of_ref, on_ref, ga_ref, shf_ref, scf_ref, gout_ref, gffn_ref, wout_ref, wr_ref, br_ref,
     xn_ref, hf_ref, rw_ref, ri_ref, cnt_ref, run_ref) = refs[n_stream:]
    ym = _rms(om_ref[...], gout_ref[:, 0:MLA_WIDTH]).astype(BF16)
    yf = _rms(of_ref[...], gout_ref[:, MLA_WIDTH:MLA_WIDTH + FNET_WIDTH]).astype(BF16)
    yn = _rms(on_ref[...], gout_ref[:, MLA_WIDTH + FNET_WIDTH:]).astype(BF16)
    acc = _dot(ym, wout_ref[0, 0:MLA_WIDTH, :])
    acc = acc + _dot(yf, wout_ref[0, MLA_WIDTH:MLA_WIDTH + FNET_WIDTH, :])
    acc = acc + _dot(yn, wout_ref[0, MLA_WIDTH + FNET_WIDTH:, :])
    xn = x + ga_ref[0] * acc
    xn_ref[...] = xn
    hf = _rms(xn, gffn_ref[...]) * (1.0 + scf_ref[0]) + shf_ref[0]
    hf_ref[...] = hf
    hi = hf.astype(BF16)
    lo = (hf - hi.astype(F32)).astype(BF16)
    a = _dot(hi, wr_ref[0])
    b = _dot(lo, wr_ref[0])
    small = a[:, ROUTER_COLS:] + (b[:, :ROUTER_COLS] + b[:, ROUTER_COLS:])
    _route_tile(a[:, :ROUTER_COLS] + small + br_ref[0], rw_ref, ri_ref, cnt_ref, run_ref)


def _route_tile(lg, rw_ref, ri_ref, cnt_ref, run_ref):
    first = jnp.logical_and(pl.program_id(0) == 0, pl.program_id(1) == 0)

    @pl.when(first)
    def _():
        run_ref[...] = jnp.zeros_like(run_ref)

    lane = lax.broadcasted_iota(jnp.int32, lg.shape, 1)
    neg = jnp.float32(-jnp.inf)

    def top(v):
        vmax = jnp.max(v, axis=1, keepdims=True)
        idx = jnp.min(jnp.where(v == vmax, lane, ROUTER_COLS), axis=1, keepdims=True)
        return vmax, idx

    in_groups = lane < N_GROUPS
    gl = jnp.where(in_groups, lg, neg)
    g_max, g_sel = top(gl)
    g_w = 1.0 / jnp.sum(jnp.where(in_groups, jnp.exp(gl - g_max), 0.0), axis=1, keepdims=True)
    e_lo = N_GROUPS + g_sel * EXPERTS_PER_GROUP
    el = jnp.where(jnp.logical_and(lane >= e_lo, lane < e_lo + EXPERTS_PER_GROUP), lg, neg)
    e1_max, i1 = top(el)
    e2_max, i2 = top(jnp.where(lane == i1, neg, el))
    t = jnp.exp(e2_max - e1_max)
    w0 = g_w / (1.0 + t)
    w1 = w0 * t
    rw_ref[...] = jnp.where(lane == 0, w0, jnp.where(lane == 1, w1, 0.0))

    row = lax.broadcasted_iota(jnp.int32, (TM, TM), 0)
    col = lax.broadcasted_iota(jnp.int32, (TM, TM), 1)
    tri = jnp.where(row >= col, 1.0, 0.0).astype(BF16)
    hot0 = lane == i1
    hot1 = lane == i2
    c0 = _dot(tri, jnp.where(hot0, 1.0, 0.0).astype(BF16))
    c1 = _dot(tri, jnp.where(hot1, 1.0, 0.0).astype(BF16))
    run = run_ref[...]
    tot0 = c0[TM - 1:TM, :]
    rank0 = jnp.sum(jnp.where(hot0, run + c0 - 1.0, 0.0), axis=1, keepdims=True)
    rank1 = jnp.sum(jnp.where(hot1, run + tot0 + c1 - 1.0, 0.0), axis=1, keepdims=True)
    run = run + tot0 + c1[TM - 1:TM, :]
    run_ref[...] = run
    cnt_ref[...] = jnp.broadcast_to(run, cnt_ref.shape).astype(jnp.int32)
    ri_ref[...] = jnp.where(lane == 0, i1 - N_GROUPS, jnp.where(lane == 1, i2 - N_GROUPS, jnp.where(
        lane == 2, rank0.astype(jnp.int32), jnp.where(lane == 3, rank1.astype(jnp.int32), 0))))


def _merge(o_mla, o_f, o_na, xs, mods, g_out, g_ffn, w_out, w_router, b_router, layer, B, j0):
    T = B * (NT - j0) * TM
    row = lambda b, j: (_otile(b, j, j0), 0)
    const = lambda b, j: (0, 0)
    slab = lambda b, j: (layer, 0, 0)
    stream_specs, stream_args = _stream_specs(xs, j0)
    return pl.pallas_call(
        functools.partial(_merge_kernel, n_stream=len(stream_args), j0=j0),
        grid=(B, NT - j0),
        in_specs=stream_specs + [
            pl.BlockSpec((TM, MLA_WIDTH), row),
            pl.BlockSpec((TM, FNET_WIDTH), row),
            pl.BlockSpec((TM, NA_WIDTH), row),
            _mod_spec(2, j0),
            _mod_spec(3, j0),
            _mod_spec(4, j0),
            pl.BlockSpec((1, D), const),
            pl.BlockSpec((1, D), const),
            pl.BlockSpec((1, D, D), slab),
            pl.BlockSpec((1, D, 2 * ROUTER_COLS), slab),
            pl.BlockSpec((1, 1, ROUTER_COLS), slab),
        ],
        out_specs=[
            pl.BlockSpec((TM, D), row),
            pl.BlockSpec((TM, D), row),
            pl.BlockSpec((TM, ROUTER_COLS), row),
            pl.BlockSpec((TM, ROUTER_COLS), row),
            pl.BlockSpec((8, ROUTER_COLS), const),
        ],
        out_shape=[
            jax.ShapeDtypeStruct((T, D), F32),
            jax.ShapeDtypeStruct((T, D), F32),
            jax.ShapeDtypeStruct((T, ROUTER_COLS), F32),
            jax.ShapeDtypeStruct((T, ROUTER_COLS), jnp.int32),
            jax.ShapeDtypeStruct((8, ROUTER_COLS), jnp.int32),
        ],
        scratch_shapes=[pltpu.VMEM((1, ROUTER_COLS), F32)],
        compiler_params=_cparams(("arbitrary", "arbitrary")),
        name="merge",
    )(*stream_args, o_mla, o_f, o_na, mods, mods, mods, g_out.reshape(1, D), g_ffn.reshape(1, D),
      w_out, w_router, b_router)


def _plan_kernel(cnt_ref, slot_ref, src_ref, tile_ref, exp_ref, lo_ref, hi_ref, flag_ref, nxt_ref, ni_ref,
                 gs_ref, *, n_pairs, max_items):
    def starts(e, acc):
        gs_ref[e] = acc
        return acc + cnt_ref[e]

    lax.fori_loop(0, N_EXPERTS, starts, 0)

    def place(p, c):
        src_ref[slot_ref[p]] = lax.shift_right_logical(p, 1)
        return c

    lax.fori_loop(0, n_pairs, place, 0, unroll=16)

    last = N_EXPERTS - 1

    def group_end(e):
        return gs_ref[e] + cnt_ref[e]

    def next_nonempty(e):
        return lax.while_loop(lambda x: jnp.logical_and(x < last, cnt_ref[jnp.minimum(x, last)] == 0),
                              lambda x: x + 1, e)

    def tile_items(t, carry):
        i, e, prev = carry
        row0 = t * TM
        e = lax.while_loop(lambda x: group_end(x) <= row0, lambda x: x + 1, e)

        def emit(state):
            i, e, prev, first, _ = state
            tile_ref[i] = t
            exp_ref[i] = e
            lo_ref[i] = jnp.clip(gs_ref[e] - row0, 0, TM)
            hi_ref[i] = jnp.clip(group_end(e) - row0, 0, TM)
            flag_ref[i] = first + 2 * (e != prev).astype(jnp.int32)
            done = group_end(e) >= row0 + TM
            e_next = jnp.where(done, e, next_nonempty(e + 1))
            return i + 1, e_next, e, jnp.int32(0), done

        i, e, prev, _, _ = lax.while_loop(lambda s: jnp.logical_not(s[4]), emit,
                                          (i, e, prev, jnp.int32(1), jnp.bool_(False)))
        return i, e, prev

    n_items, _, _ = lax.fori_loop(0, n_pairs // TM, tile_items,
                                  (jnp.int32(0), jnp.int32(0), jnp.int32(-1)))
    ni_ref[0] = n_items

    def pad(i, c):
        tile_ref[i] = tile_ref[n_items - 1]
        exp_ref[i] = exp_ref[n_items - 1]
        lo_ref[i] = 0
        hi_ref[i] = 0
        flag_ref[i] = 0
        nxt_ref[i] = -1
        return c

    lax.fori_loop(n_items, max_items, pad, 0)

    def parity(i, par):
        par = jnp.where((flag_ref[i] & 2) != 0, 1 - par, par)
        flag_ref[i] = flag_ref[i] + 4 * par
        return par

    lax.fori_loop(0, n_items, parity, jnp.int32(1))

    def lookahead(k, following):
        i = n_items - 1 - k
        nxt_ref[i] = following
        return jnp.where((flag_ref[i] & 2) != 0, exp_ref[i], following)

    lax.fori_loop(0, n_items, lookahead, jnp.int32(-1))


def _plan(counts, eid, rank):
    n_pairs = eid.shape[0]
    max_items = n_pairs // TM + N_EXPERTS - 1
    g_start = jnp.cumsum(counts) - counts
    experts = jnp.arange(N_EXPERTS, dtype=jnp.int32)
    slot = jnp.sum(jnp.where(eid[:, None] == experts[None, :], g_start[None, :], 0), axis=1) + rank
    smem = pl.BlockSpec(memory_space=pltpu.SMEM)
    i32 = lambda n: jax.ShapeDtypeStruct((n,), jnp.int32)
    src, it_tile, it_exp, it_lo, it_hi, flags, nxt, n_items = pl.pallas_call(
        functools.partial(_plan_kernel, n_pairs=n_pairs, max_items=max_items),
        in_specs=[smem] * 2,
        out_specs=[smem] * 8,
        out_shape=[i32(n_pairs)] + [i32(max_items)] * 6 + [i32(1)],
        scratch_shapes=[pltpu.SMEM((N_EXPERTS,), jnp.int32)],
        name="moe_plan",
    )(counts, slot)
    return (it_tile, it_exp, it_lo, it_hi, flags, nxt, n_items, src), slot


def _moe_kernel(tile_ref, exp_ref, lo_ref, hi_ref, flag_ref, nxt_ref, ni_ref, src_ref,
                hf_hbm, wg_hbm, wu_hbm, wd_hbm, y_ref, xbuf, wg_buf, wu_buf, wd_buf, wgb, wub, wdb,
                sem, wsem, *, n_tiles, e0):
    i = pl.program_id(0)
    t = tile_ref[i]
    slot = lax.rem(t, 2)

    def gather_start(tile, buf, unrolled):
        base = tile * TM

        def issue(r):
            tok = src_ref[base + r]
            pltpu.make_async_copy(hf_hbm.at[pl.ds(tok, 1)], xbuf.at[buf, pl.ds(r, 1)], sem.at[buf]).start()

        if unrolled:
            for r in range(TM):
                issue(r)
        else:
            lax.fori_loop(0, TM, lambda r, c: (issue(r), c)[1], 0)

    def gather_wait(buf):
        pltpu.make_async_copy(hf_hbm.at[pl.ds(0, TM)], xbuf.at[buf], sem.at[buf]).wait()

    def weight_copies(expert, b):
        e = e0 + expert
        return (pltpu.make_async_copy(wg_hbm.at[e], wg_buf.at[b], wsem.at[b]),
                pltpu.make_async_copy(wu_hbm.at[e], wu_buf.at[b], wsem.at[b]),
                pltpu.make_async_copy(wd_hbm.at[e], wd_buf.at[b], wsem.at[b]))

    def weights_start(expert, b):
        for cp in weight_copies(expert, b):
            cp.start(priority=1)

    def weights_wait(b):
        for cp in weight_copies(0, b):
            cp.wait()

    @pl.when(i < ni_ref[0])
    def _():
        first_visit = (flag_ref[i] & 1) != 0
        new_expert = (flag_ref[i] & 2) != 0
        wslot = lax.shift_right_logical(flag_ref[i], 2) & 1

        @pl.when(i == 0)
        def _():
            gather_start(0, 0, False)

        @pl.when(first_visit)
        def _():
            gather_wait(slot)

        for b in range(2):
            @pl.when(jnp.logical_and(jnp.logical_and(first_visit, t + 1 < n_tiles), slot == 1 - b))
            def _():
                gather_start(t + 1, b, True)

        @pl.when(i == 0)
        def _():
            weights_start(exp_ref[0], 0)

        @pl.when(new_expert)
        def _():
            @pl.when(nxt_ref[i] >= 0)
            def _():
                weights_start(nxt_ref[i], 1 - wslot)

            weights_wait(wslot)
            wgb[...] = wg_buf[wslot].astype(BF16)
            wub[...] = wu_buf[wslot].astype(BF16)
            wdb[...] = wd_buf[wslot].astype(BF16)

        x = xbuf[slot].astype(BF16)
        a = _dot(x, wgb[...])
        u = _dot(x, wub[...])
        row = lax.broadcasted_iota(jnp.int32, (TM, 1), 0)
        mine = jnp.logical_and(row >= lo_ref[i], row < hi_ref[i])
        h = jnp.where(mine, (a * jax.nn.sigmoid(a)) * u, 0.0)
        yv = _dot(h.astype(BF16), wdb[...])

        @pl.when(first_visit)
        def _():
            y_ref[...] = yv

        @pl.when(jnp.logical_not(first_visit))
        def _():
            y_ref[...] += yv


def _moe(hf, meta, w_gate, w_up, w_down, layer):
    n_rows = meta[-1].shape[0]
    n_tiles = n_rows // TM
    max_items = meta[0].shape[0]
    grid_spec = pltpu.PrefetchScalarGridSpec(
        num_scalar_prefetch=len(meta),
        grid=(max_items,),
        in_specs=[pl.BlockSpec(memory_space=pl.ANY)] * 4,
        out_specs=pl.BlockSpec((TM, D), lambda i, tile, *_: (tile[i], 0)),
        scratch_shapes=[
            pltpu.VMEM((2, TM, D), F32),
            pltpu.VMEM((2, D, D_EXPERT), F32),
            pltpu.VMEM((2, D, D_EXPERT), F32),
            pltpu.VMEM((2, D_EXPERT, D), F32),
            pltpu.VMEM((D, D_EXPERT), BF16),
            pltpu.VMEM((D, D_EXPERT), BF16),
            pltpu.VMEM((D_EXPERT, D), BF16),
            pltpu.SemaphoreType.DMA((2,)),
            pltpu.SemaphoreType.DMA((2,)),
        ],
    )
    return pl.pallas_call(
        functools.partial(_moe_kernel, n_tiles=n_tiles, e0=layer * N_EXPERTS),
        grid_spec=grid_spec,
        out_shape=jax.ShapeDtypeStruct((n_rows, D), F32),
        compiler_params=_cparams(("arbitrary",)),
        name="moe_experts",
    )(*meta, hf, w_gate.reshape(-1, D, D_EXPERT), w_up.reshape(-1, D, D_EXPERT),
      w_down.reshape(-1, D_EXPERT, D))


def _final_kernel(pos_ref, x_ref, gf_ref, w_ref, g_ref, y_hbm, o_ref, ybuf, sem, *, n_steps):
    step = pl.program_id(0) * 8 + pl.program_id(1)
    out = _moe_residual(pos_ref, y_hbm, ybuf, sem, x_ref[...], gf_ref[0], w_ref[...], step, n_steps)
    o_ref[0] = _rms(out, g_ref[...])


def _final(pos, wts, xn, mods, y, g_final, B):
    tile = lambda b, j, p: (b * 8 + j, 0)
    grid_spec = pltpu.PrefetchScalarGridSpec(
        num_scalar_prefetch=1,
        grid=(B, 8),
        in_specs=[
            pl.BlockSpec((TM, D), tile),
            pl.BlockSpec((1, 1, D), lambda b, j, p: (b * 6 + 5, 0, 0)),
            pl.BlockSpec((TM, 2), tile),
            pl.BlockSpec((1, D), lambda b, j, p: (0, 0)),
            pl.BlockSpec(memory_space=pl.ANY),
        ],
        out_specs=pl.BlockSpec((1, TM, D), lambda b, j, p: (b, j, 0)),
        scratch_shapes=_MOE_GATHER_SCRATCH,
    )
    return pl.pallas_call(
        functools.partial(_final_kernel, n_steps=B * 8),
        grid_spec=grid_spec,
        out_shape=jax.ShapeDtypeStruct((B, 8 * TM, D), F32),
        compiler_params=_cparams(("arbitrary", "arbitrary")),
        name="final",
    )(pos, xn, mods, wts, g_final.reshape(1, D), y)


def _prep_w_in(w_in):
    return jnp.swapaxes(w_in, 1, 2).astype(BF16)


def _prep_w_uq(w_uq):
    w = w_uq.reshape(Q_LORA, MLA_HEADS, QK_NOPE + QK_ROPE)
    w = jnp.pad(w, ((0, 0), (0, 0), (0, MLA_SLAB - QK_NOPE - QK_ROPE)))
    return w.reshape(Q_LORA, MLA_HEADS * MLA_SLAB).astype(BF16)


def _rope_tables(seq):
    half = QK_ROPE // 2
    inv_freq = ROPE_THETA ** (-jnp.arange(0, half, 2, dtype=F32) / half)
    t = jnp.arange(seq, dtype=jnp.int32)
    row = (t // GRID_W).astype(F32)
    col = (t % GRID_W).astype(F32)
    ang = jnp.concatenate([row[:, None] * inv_freq, col[:, None] * inv_freq], axis=-1)
    cos, sin = jnp.cos(ang), jnp.sin(ang)
    zeros = jnp.zeros((seq, 64), F32)
    cos_l = jnp.concatenate([jnp.repeat(cos, 2, axis=1), zeros], axis=1)
    sin_l = jnp.concatenate([jnp.stack([-sin, sin], axis=-1).reshape(seq, 64), zeros], axis=1)
    cos_c = jnp.concatenate([jnp.ones((CTX_LEN, 64), F32), jnp.zeros((CTX_LEN, 64), F32)], axis=1)
    sin_c = jnp.zeros((CTX_LEN, 128), F32)
    return jnp.concatenate([cos_c, cos_l], axis=0), jnp.concatenate([sin_c, sin_l], axis=0)


def kernel(x, c, ctx, c_ctx, w_ada, b_ada, g_attn, g_ffn, w_in, g_q, w_uq, g_kv, w_ukv, w_fnet, b_fnet,
           na_rpb, g_out, w_out, w_rg, b_rg, w_re, b_re, w_gate, w_up, w_down, g_final):
    B, S, _ = x.shape
    L = w_ada.shape[0]
    assert ctx.shape[1] == CTX_LEN == TM and S == 8 * TM and B <= 4
    T = B * NT * TM

    cond8 = jnp.concatenate([c, jnp.zeros((4 - B, D), F32), c_ctx[None], jnp.zeros((3, D), F32)], axis=0)
    mods_all = _modulation(cond8, w_ada, b_ada)
    w_in_ext = _prep_w_in(w_in)
    pad = ROUTER_COLS - N_GROUPS - N_EXPERTS
    w_router = jnp.concatenate([w_rg, w_re, jnp.zeros((L, D, pad), F32)], axis=2)
    w_scaled = w_router * 65537.0
    w_router_hi = w_scaled - (w_scaled - w_router)
    w_router_all = jnp.concatenate([w_router_hi, w_router - w_router_hi], axis=2).astype(BF16)
    b_router_all = jnp.concatenate([b_rg, b_re, jnp.zeros((L, pad), F32)], axis=1).reshape(L, 1, ROUTER_COLS)
    w_out_all = w_out.astype(BF16)
    na_bias = _na_bias(na_rpb)
    cos_t, sin_t = _rope_tables(S)
    dft_lat = _dft_tables(S)
    dft_ctx = _dft_tables(CTX_LEN)
    cd_c, cd_s = _dft_cos_sin(FNET_GROUP_DIM)
    cd = jnp.concatenate([cd_c, -cd_s], axis=1).astype(BF16)

    xs = (ctx.reshape(B * CTX_LEN, D), x.reshape(B * S, D))
    pending = None
    for l in range(L):
        last = l == L - 1
        j0 = 1 if last else 0
        mods = mods_all[l].reshape(48, 1, D)
        w_uq_ext = _prep_w_uq(w_uq[l])
        mla = (g_q[l], g_kv[l], w_uq_ext, w_ukv[l].astype(BF16), cos_t, sin_t)
        if pending is None:
            zf, naq, nak, navt, q, k, vt = _in_projection(xs, mods, g_attn[l], w_in_ext, l, mla, B)
        else:
            zf, naq, nak, navt, q, k, vt, xs = _in_projection(xs, mods, g_attn[l], w_in_ext, l, mla, B,
                                                              moe=pending)
        o_mla = _mla_attention(q, k, vt, B, j0)
        o_na = _na_attention(naq, nak, navt, na_bias, l, B, j0)
        w_f = w_fnet[l].astype(BF16)
        o_f = _fnet(zf, dft_lat, dft_ctx, cd, w_f, b_fnet[l], B, j0)
        xn, hf, route_w, route_i, counts = _merge(o_mla, o_f, o_na, xs, mods, g_out[l], g_ffn[l],
                                                  w_out_all, w_router_all, b_router_all, l, B, j0)
        wts = route_w[:, 0:2]
        meta, slot = _plan(counts[0, N_GROUPS:N_GROUPS + N_EXPERTS], route_i[:, 0:2].reshape(-1),
                           route_i[:, 2:4].reshape(-1))
        y = _moe(hf, meta, w_gate, w_up, w_down, l)
        if last:
            return _final(slot, wts, xn, mods, y, g_final, B)
        xs, pending = xn, (slot, wts, y, mods)
```

```python
import functools

import numpy as np
import jax
import jax.numpy as jnp
from jax import lax
from jax.experimental import pallas as pl
from jax.experimental.pallas import tpu as pltpu

F32 = jnp.float32
BF16 = jnp.bfloat16

D = 2048
GRID_W = 64
CTX_LEN = 256
EPS = 1e-6
NEG_INF = -1e30
ROPE_THETA = 10000.0

V_DIM = 128
MLA_WIDTH = D // 2
MLA_HEADS = MLA_WIDTH // V_DIM
QK_NOPE = 128
QK_ROPE = 64
Q_LORA = D // 4
KV_LORA = D // 8
FNET_WIDTH = D // 4
FNET_GROUP_DIM = 128
FNET_GROUPS = FNET_WIDTH // FNET_GROUP_DIM
NA_WIDTH = D // 4
NA_HEAD_DIM = 128
NA_HEADS = NA_WIDTH // NA_HEAD_DIM
NA_KH_MAX = 8
NA_KW = 16
N_GROUPS = 4
EXPERTS_PER_GROUP = 8
N_EXPERTS = N_GROUPS * EXPERTS_PER_GROUP
D_EXPERT = D // 4

MLA_QSCALE = (QK_NOPE + QK_ROPE) ** -0.5 * float(np.log2(np.e))
MLA_SLAB = QK_NOPE + V_DIM
MLA_CHUNK_TILES = 8
NA_QSCALE = NA_HEAD_DIM ** -0.5 * float(np.log2(np.e))
NA_WIN_ROWS = 12
TM = 256
NT = 9
IN_COLS = Q_LORA + KV_LORA + QK_ROPE + FNET_WIDTH + 3 * NA_WIDTH
ROUTER_COLS = 128
VMEM_LIMIT = 56 * 1024 * 1024


def _cparams(sem):
    return pltpu.CompilerParams(dimension_semantics=sem, vmem_limit_bytes=VMEM_LIMIT)


def _rms(v, g):
    return v * lax.rsqrt(jnp.mean(v * v, axis=-1, keepdims=True) + EPS) * g


def _dot(a, b):
    return jnp.dot(a, b, preferred_element_type=F32)


def _dot_nt(a, b):
    return lax.dot_general(a, b, (((1,), (1,)), ((), ())), preferred_element_type=F32)


def _mod_kernel(c_ref, w_ref, b_ref, o_ref):
    c = c_ref[...]
    s = c * jax.nn.sigmoid(c)
    o_ref[0] = _dot(s.astype(BF16), w_ref[0].astype(BF16)) + b_ref[0]


def _modulation(cond8, w_ada, b_ada):
    L = w_ada.shape[0]
    tn = 1024
    return pl.pallas_call(
        _mod_kernel,
        grid=(L, 6 * D // tn),
        in_specs=[
            pl.BlockSpec((8, D), lambda l, n: (0, 0)),
            pl.BlockSpec((1, D, tn), lambda l, n: (l, 0, n)),
            pl.BlockSpec((1, 1, tn), lambda l, n: (l, 0, n)),
        ],
        out_specs=pl.BlockSpec((1, 8, tn), lambda l, n: (l, 0, n)),
        out_shape=jax.ShapeDtypeStruct((L, 8, 6 * D), F32),
        compiler_params=_cparams(("arbitrary", "arbitrary")),
        name="modulation",
    )(cond8, w_ada, b_ada.reshape(L, 1, 6 * D))


def _tile_of(b, j, j0):
    return b * NT + j0 + j


def _otile(b, j, j0):
    return b * (NT - j0) + j


def _mod_row(j, b, j0):
    return jnp.where(j0 + j == 0, 4, b)


def _mod_spec(k, j0):
    return pl.BlockSpec((1, 1, D), lambda b, j: (_mod_row(j, b, j0) * 6 + k, 0, 0))


def _stream_specs(xs, j0):
    if isinstance(xs, tuple):
        return [pl.BlockSpec((TM, D), lambda b, j, *_: (b, 0)),
                pl.BlockSpec((TM, D), lambda b, j, *_: (b * 8 + jnp.maximum(j0 + j - 1, 0), 0))], list(xs)
    return [pl.BlockSpec((TM, D), lambda b, j, *_: (_tile_of(b, j, j0), 0))], [xs]


def _stream_tile(refs, j0):
    if len(refs) == 1:
        return refs[0][...]
    return jnp.where(pl.program_id(1) + j0 == 0, refs[0][...], refs[1][...])


def _moe_residual(pos_ref, y_hbm, ybuf, sem, x, gate, w, step, n_steps):
    buf = lax.rem(step, 2)

    def start(tile, b, unrolled):
        base = tile * TM

        def issue(r):
            for k in range(2):
                pltpu.make_async_copy(y_hbm.at[pl.ds(pos_ref[2 * (base + r) + k], 1)],
                                      ybuf.at[b, k, pl.ds(r, 1)], sem.at[b]).start(priority=k)

        if unrolled:
            for r in range(TM):
                issue(r)
        else:
            lax.fori_loop(0, TM, lambda r, c: (issue(r), c)[1], 0)

    def wait(b):
        for k in range(2):
            pltpu.make_async_copy(y_hbm.at[pl.ds(0, TM)], ybuf.at[b, k], sem.at[b]).wait()

    @pl.when(step == 0)
    def _():
        start(0, 0, False)

    for b in range(2):
        @pl.when(buf == 1 - b)
        def _():
            start(jnp.minimum(step + 1, n_steps - 1), b, True)

    wait(buf)
    out = x + gate * (w[:, 0:1] * ybuf[buf, 0] + w[:, 1:2] * ybuf[buf, 1])

    @pl.when(step == n_steps - 1)
    def _():
        wait(1 - buf)

    return out


_MOE_GATHER_SCRATCH = [pltpu.VMEM((2, 2, TM, D), F32), pltpu.SemaphoreType.DMA((2,))]


def _rope(r, cos_t, sin_t):
    lane = lax.broadcasted_iota(jnp.int32, r.shape, 1)
    partner = jnp.where(lane % 2 == 0, pltpu.roll(r, 127, 1), pltpu.roll(r, 1, 1))
    return r * cos_t + partner * sin_t


def _inproj_body(x, sh_ref, sc_ref, g_ref, w_ref, gq_ref, gkv_ref, wq_ref, wkv_ref, cos_ref, sin_ref,
                 zf_ref, naq_ref, nak_ref, nav_ref, q_ref, k_ref, v_ref):
    h = _rms(x, g_ref[...])
    h = h * (1.0 + sc_ref[0]) + sh_ref[0]
    z = _dot_nt(h.astype(BF16), w_ref[0])
    c_zf =Q_LORA + KV_LORA + QK_ROPE
    zf_ref[...] = z[:, c_zf:c_zf + FNET_WIDTH].astype(BF16)
    c_na = c_zf + FNET_WIDTH
    naq_ref[...] = (z[:, c_na:c_na + NA_WIDTH] * NA_QSCALE).astype(BF16)
    nak_ref[...] = z[:, c_na + NA_WIDTH:c_na + 2 * NA_WIDTH].astype(BF16)
    for hd in range(NA_HEADS):
        c = c_na + 2 * NA_WIDTH + hd * NA_HEAD_DIM
        nav_ref[:, hd * NA_HEAD_DIM:(hd + 1) * NA_HEAD_DIM] = z[:, c:c + NA_HEAD_DIM].astype(BF16)
    cos_t = cos_ref[...]
    sin_t = sin_ref[...]
    q = _dot(_rms(z[:, 0:Q_LORA], gq_ref[...]).astype(BF16), wq_ref[...]) * MLA_QSCALE
    kv = _dot(_rms(z[:, Q_LORA:Q_LORA + KV_LORA], gkv_ref[...]).astype(BF16), wkv_ref[...])
    k_rope = _rope(z[:, Q_LORA + KV_LORA:Q_LORA + KV_LORA + 128], cos_t, sin_t).astype(BF16)
    for hd in range(MLA_HEADS):
        c = hd * MLA_SLAB
        r = c + QK_NOPE
        q_ref[:, c:r] = q[:, c:r].astype(BF16)
        q_ref[:, r:c + MLA_SLAB] = _rope(q[:, r:c + MLA_SLAB], cos_t, sin_t).astype(BF16)
        k_ref[:, c:r] = kv[:, c:r].astype(BF16)
        k_ref[:, r:c + MLA_SLAB] = k_rope
        v_ref[:, hd * V_DIM:(hd + 1) * V_DIM] = kv[:, r:c + MLA_SLAB].astype(BF16)


def _inproj_kernel(*refs, n_stream):
    _inproj_body(_stream_tile(refs[:n_stream], 0), *refs[n_stream:])


def _inproj_moe_kernel(pos_ref, x_ref, gf_ref, wts_ref, y_hbm, *refs, n_steps):
    *refs, xs_ref, ybuf, sem = refs
    step = pl.program_id(0) * NT + pl.program_id(1)
    x = _moe_residual(pos_ref, y_hbm, ybuf, sem, x_ref[...], gf_ref[0], wts_ref[...], step, n_steps)
    xs_ref[...] = x
    _inproj_body(x, *refs)


def _in_projection(xs, mods, g_attn, w_in_ext, layer, mla, B, moe=None):
    T = B * NT * TM
    row = lambda b, j, *_: (_tile_of(b, j, 0), 0)
    col = lambda b, j, *_: (0, _tile_of(b, j, 0))
    const = lambda b, j, *_: (0, 0)
    mod = lambda k: pl.BlockSpec((1, 1, D), lambda b, j, *_: (_mod_row(j, b, 0) * 6 + k, 0, 0))
    once = dict(pipeline_mode=pl.Buffered(1))
    g_q, g_kv, w_uq_ext, w_ukv, cos_t, sin_t = mla
    in_specs = [
        pl.BlockSpec((TM, D), row),
        mod(0),
        mod(1),
        pl.BlockSpec((1, D), const),
        pl.BlockSpec((1, IN_COLS, D), lambda b, j, *_: (layer, 0, 0), **once),
        pl.BlockSpec((1, Q_LORA), const),
        pl.BlockSpec((1, KV_LORA), const),
        pl.BlockSpec((Q_LORA, MLA_HEADS * MLA_SLAB), const, **once),
        pl.BlockSpec((KV_LORA, MLA_HEADS * MLA_SLAB), const, **once),
        pl.BlockSpec((TM, 128), lambda b, j, *_: (j, 0)),
        pl.BlockSpec((TM, 128), lambda b, j, *_: (j, 0)),
    ]
    out_specs = [
        pl.BlockSpec((TM, FNET_WIDTH), row),
        pl.BlockSpec((TM, NA_WIDTH), row),
        pl.BlockSpec((TM, NA_WIDTH), row),
        pl.BlockSpec((TM, NA_WIDTH), row),
        pl.BlockSpec((TM, MLA_HEADS * MLA_SLAB), row),
        pl.BlockSpec((TM, MLA_HEADS * MLA_SLAB), row),
        pl.BlockSpec((TM, MLA_WIDTH), row),
    ]
    out_shape = [
        jax.ShapeDtypeStruct((T, FNET_WIDTH), BF16),
        jax.ShapeDtypeStruct((T, NA_WIDTH), BF16),
        jax.ShapeDtypeStruct((T, NA_WIDTH), BF16),
        jax.ShapeDtypeStruct((T, NA_WIDTH), BF16),
        jax.ShapeDtypeStruct((T, MLA_HEADS * MLA_SLAB), BF16),
        jax.ShapeDtypeStruct((T, MLA_HEADS * MLA_SLAB), BF16),
        jax.ShapeDtypeStruct((T, MLA_WIDTH), BF16),
    ]
    args = [xs, mods, mods, g_attn.reshape(1, D), w_in_ext, g_q.reshape(1, -1), g_kv.reshape(1, -1),
            w_uq_ext, w_ukv, cos_t, sin_t]
    if moe is None:
        stream_specs, stream_args = _stream_specs(xs, 0)
        return pl.pallas_call(
            functools.partial(_inproj_kernel, n_stream=len(stream_args)),
            grid=(B, NT),
            in_specs=stream_specs + in_specs[1:],
            out_specs=out_specs,
            out_shape=out_shape,
            compiler_params=_cparams(("arbitrary", "arbitrary")),
            name="in_projection",
        )(*stream_args, *args[1:])
    pos, wts, y, mods_prev = moe
    in_specs = [in_specs[0], mod(5), pl.BlockSpec((TM, 2), row), pl.BlockSpec(memory_space=pl.ANY)] \
        + in_specs[1:]
    grid_spec = pltpu.PrefetchScalarGridSpec(
        num_scalar_prefetch=1,
        grid=(B, NT),
        in_specs=in_specs,
        out_specs=out_specs + [pl.BlockSpec((TM, D), row)],
        scratch_shapes=_MOE_GATHER_SCRATCH,
    )
    return pl.pallas_call(
        functools.partial(_inproj_moe_kernel, n_steps=B * NT),
        grid_spec=grid_spec,
        out_shape=out_shape + [jax.ShapeDtypeStruct((T, D), F32)],
        compiler_params=_cparams(("arbitrary", "arbitrary")),
        name="in_projection_moe",
    )(pos, xs, mods_prev, wts, y, *args[1:])


def _softmax_numerator(st, m):
    return jnp.exp2(st - m).astype(BF16)


def _pv_with_sum(vt, p):
    ones = jnp.ones((2 * 8, vt.shape[1]), BF16)
    o = _dot(jnp.concatenate([vt, ones], axis=0), p)
    return o[0:vt.shape[0]], o[vt.shape[0]:vt.shape[0] + 1]


def _transposed(v_ref):
    return v_ref[...].astype(F32).T.astype(BF16)


def _mla_attn_kernel(q_ref, k_ref, v_ref, o_ref, *, j0):
    vt = _transposed(v_ref)

    def attend(q0, nq, nk, o0):
        st = _dot_nt(k_ref[0:nk, :], q_ref[q0:q0 + nq, :])
        m = jnp.max(st, axis=0, keepdims=True)
        ot, l = _pv_with_sum(vt[:, 0:nk], _softmax_numerator(st, m))
        o_ref[o0:o0 + nq, :] = (ot / l).T

    if j0 == 0:
        attend(0, CTX_LEN, CTX_LEN, 0)
    for c in range(8 // MLA_CHUNK_TILES):
        attend(CTX_LEN + MLA_CHUNK_TILES * c * TM, MLA_CHUNK_TILES * TM, NT * TM,
               (1 - j0 + MLA_CHUNK_TILES * c) * TM)


def _mla_attention(q, k, v, B, j0):
    rows = (NT - j0) * TM
    return pl.pallas_call(
        functools.partial(_mla_attn_kernel, j0=j0),
        grid=(B, MLA_HEADS),
        in_specs=[
            pl.BlockSpec((NT * TM, MLA_SLAB), lambda b, h: (b, h)),
            pl.BlockSpec((NT * TM, MLA_SLAB), lambda b, h: (b, h)),
            pl.BlockSpec((NT * TM, V_DIM), lambda b, h: (b, h)),
        ],
        out_specs=pl.BlockSpec((rows, V_DIM), lambda b, h: (b, h)),
        out_shape=jax.ShapeDtypeStruct((B * rows, MLA_WIDTH), F32),
        compiler_params=_cparams(("arbitrary", "arbitrary")),
        name="mla_attention",
    )(q, k, v)


def _na_chunk(g):
    start_row = min(max(4 * g - 4, 0), 8 * TM // GRID_W - NA_WIN_ROWS)
    pattern = 0 if g == 0 else (2 if g == 7 else 1)
    return start_row, pattern


def _na_kernel(q_ref, k_ref, v_ref, bias_ref, o_ref, *, j0):
    vt = _transposed(v_ref)

    def finish(parts, o0):
        m = None
        for st, _ in parts:
            pm = jnp.max(st, axis=0, keepdims=True)
            m = pm if m is None else jnp.maximum(m, pm)
        l = None
        ot = None
        for st, vt in parts:
            po, pl_sum = _pv_with_sum(vt, _softmax_numerator(st, m))
            l = pl_sum if l is None else l + pl_sum
            ot = po if ot is None else ot + po
        o_ref[o0:o0 + TM, :] = (ot / l).T

    if j0 == 0:
        st = _dot_nt(k_ref[0:CTX_LEN, :], q_ref[0:CTX_LEN, :])
        finish([(st, vt[:, 0:CTX_LEN])], 0)
    for g in range(8):
        start_row, pattern = _na_chunk(g)
        k0 = CTX_LEN + start_row * GRID_W
        nk = NA_WIN_ROWS * GRID_W
        q = q_ref[CTX_LEN + g * TM:CTX_LEN + (g + 1) * TM, :]
        st_loc = _dot_nt(k_ref[k0:k0 + nk, :], q) + bias_ref[0, pattern]
        st_ctx = _dot_nt(k_ref[0:CTX_LEN, :], q)
        finish([(st_loc, vt[:, k0:k0 + nk]), (st_ctx, vt[:, 0:CTX_LEN])], (1 - j0 + g) * TM)


def _na_attention(naq, nak, nav, bias, layer, B, j0):
    rows = (NT - j0) * TM
    bias = bias.reshape((-1,) + bias.shape[2:])
    return pl.pallas_call(
        functools.partial(_na_kernel, j0=j0),
        grid=(NA_HEADS, B),
        in_specs=[
            pl.BlockSpec((NT * TM, NA_HEAD_DIM), lambda h, b: (b, h)),
            pl.BlockSpec((NT * TM, NA_HEAD_DIM), lambda h, b: (b, h)),
            pl.BlockSpec((NT * TM, NA_HEAD_DIM), lambda h, b: (b, h)),
            pl.BlockSpec((1, 3, NA_WIN_ROWS * GRID_W, TM), lambda h, b: (layer * NA_HEADS + h, 0, 0, 0)),
        ],
        out_specs=pl.BlockSpec((rows, NA_HEAD_DIM), lambda h, b: (b, h)),
        out_shape=jax.ShapeDtypeStruct((B * rows, NA_WIDTH), F32),
        compiler_params=_cparams(("arbitrary", "arbitrary")),
        name="na_attention",
    )(naq, nak, nav, bias)


def _na_bias(rpb):
    kh, rows, nq = NA_KH_MAX, 8 * TM // GRID_W, TM // GRID_W
    cq = np.arange(GRID_W)
    ck = np.arange(GRID_W)
    col_start = np.clip(cq - NA_KW // 2, 0, GRID_W - NA_KW)
    col_ok = (ck[:, None] >= col_start[None, :]) & (ck[:, None] < col_start[None, :] + NA_KW)
    dcol = np.clip(ck[:, None] - cq[None, :] + (NA_KW - 1), 0, 2 * NA_KW - 2)
    select = np.zeros((2 * NA_KW - 1, GRID_W * GRID_W), np.float32)
    select[dcol.reshape(-1), np.arange(GRID_W * GRID_W)] = 1.0
    blocks = jnp.einsum("lhdm,mn->lhdn", rpb.astype(F32) * float(np.log2(np.e)), jnp.asarray(select),
                        precision=lax.Precision.HIGHEST)
    blocks = blocks.reshape(rpb.shape[:3] + (GRID_W, GRID_W))
    blocks = jnp.where(jnp.asarray(col_ok), blocks, NEG_INF)
    masked = jnp.full(rpb.shape[:2] + (GRID_W, GRID_W), NEG_INF, F32)
    patterns = []
    for g in (0, 1, 7):
        start_row, _ = _na_chunk(g)
        key_rows = []
        for kr in range(NA_WIN_ROWS):
            key_row = start_row + kr
            row = []
            for qr in range(nq):
                r = 4 * g + qr
                r_start = min(max(r - kh // 2, 0), rows - kh)
                in_rows = r_start <= key_row < r_start + kh
                row.append(blocks[:, :, key_row - r + (kh - 1)] if in_rows else masked)
            key_rows.append(jnp.concatenate(row, axis=-1))
        patterns.append(jnp.concatenate(key_rows, axis=-2))
    return jnp.stack(patterns, axis=2)


def _fnet_kernel(z_ref, cl_ref, sl_ref, cc_ref, sc_ref, cd_ref, w_ref, b_ref, o_ref, ab_ref, *, j0, seq):
    j = pl.program_id(1) + j0

    def small_side(row0, length):
        for g in range(FNET_GROUPS):
            c = g * FNET_GROUP_DIM
            ab = _dot(z_ref[row0:row0 + length, c:c + FNET_GROUP_DIM], cd_ref[...])
            ab_ref[0:length, c:c + FNET_GROUP_DIM] = ab[:, 0:FNET_GROUP_DIM].astype(BF16)
            ab_ref[length:2 * length, c:c + FNET_GROUP_DIM] = ab[:, FNET_GROUP_DIM:].astype(BF16)

    def long_side(c, s, length):
        f = _dot(c, ab_ref[0:length, :]) + _dot(s, ab_ref[length:2 * length, :])
        f = f * (length * FNET_GROUP_DIM) ** -0.5
        o_ref[...] = _dot(f.astype(BF16), w_ref[...]) + b_ref[...]

    if j0 == 0:
        @pl.when(j == 0)
        def _():
            small_side(0, CTX_LEN)
            long_side(cc_ref[...], sc_ref[...], CTX_LEN)

    @pl.when(j == 1)
    def _():
        small_side(CTX_LEN, seq)

    @pl.when(j >= 1)
    def _():
        long_side(cl_ref[...], sl_ref[...], seq)


def _fnet(zf, dft_lat, dft_ctx, cd, w_fnet, b_fnet, B, j0):
    seq = 8 * TM
    lat_rows = lambda b, j: (jnp.maximum(j0 + j - 1, 0), 0)
    return pl.pallas_call(
        functools.partial(_fnet_kernel, j0=j0, seq=seq),
        grid=(B, NT - j0),
        in_specs=[
            pl.BlockSpec((NT * TM, FNET_WIDTH), lambda b, j: (b, 0)),
            pl.BlockSpec((TM, seq), lat_rows),
            pl.BlockSpec((TM, seq), lat_rows),
            pl.BlockSpec((CTX_LEN, CTX_LEN), lambda b, j: (0, 0)),
            pl.BlockSpec((CTX_LEN, CTX_LEN), lambda b, j: (0, 0)),
            pl.BlockSpec((FNET_GROUP_DIM, 2 * FNET_GROUP_DIM), lambda b, j: (0, 0)),
            pl.BlockSpec((FNET_WIDTH, FNET_WIDTH), lambda b, j: (0, 0)),
            pl.BlockSpec((1, FNET_WIDTH), lambda b, j: (0, 0)),
        ],
        out_specs=pl.BlockSpec((TM, FNET_WIDTH), lambda b, j: (_otile(b, j, j0), 0)),
        out_shape=jax.ShapeDtypeStruct((B * (NT - j0) * TM, FNET_WIDTH), F32),
        scratch_shapes=[pltpu.VMEM((2 * seq, FNET_WIDTH), BF16)],
        compiler_params=_cparams(("arbitrary", "arbitrary")),
        name="fnet",
    )(zf, *dft_lat, *dft_ctx, cd, w_fnet, b_fnet.reshape(1, -1))


def _dft_cos_sin(n):
    j = jnp.arange(n, dtype=jnp.int32)[:, None]
    if n <= 64:
        ang = ((j * j.T) % n).astype(F32) * (2.0 * np.pi / n)
        return jnp.cos(ang), jnp.sin(ang)
    k1 = jnp.arange(n // 64, dtype=jnp.int32)[None, :]
    k0 = jnp.arange(64, dtype=jnp.int32)[None, :]
    a = ((j * k1 * 64) % n).astype(F32) * (2.0 * np.pi / n)
    b = ((j * k0) % n).astype(F32) * (2.0 * np.pi / n)
    ca, sa, cb, sb = jnp.cos(a), jnp.sin(a), jnp.cos(b), jnp.sin(b)
    c = ca[:, :, None] * cb[:, None, :] - sa[:, :, None] * sb[:, None, :]
    s = sa[:, :, None] * cb[:, None, :] + ca[:, :, None] * sb[:, None, :]
    return c.reshape(n, n), s.reshape(n, n)


def _dft_tables(n):
    c, s = _dft_cos_sin(n)
    return c.astype(BF16), s.astype(BF16)


def _merge_kernel(*refs, n_stream, j0):
    x = _stream_tile(refs[:n_stream], j0)
    (om_ref, of_ref, on_ref, ga_ref, shf_ref, scf_ref, gout_ref, gffn_ref, wout_ref, wr_ref, br_ref,
     xn_ref, hf_ref, rw_ref, ri_ref, cnt_ref, run_ref) = refs[n_stream:]
    ym = _rms(om_ref[...], gout_ref[:, 0:MLA_WIDTH]).astype(BF16)
    yf = _rms(of_ref[...], gout_ref[:, MLA_WIDTH:MLA_WIDTH + FNET_WIDTH]).astype(BF16)
    yn = _rms(on_ref[...], gout_ref[:, MLA_WIDTH + FNET_WIDTH:]).astype(BF16)
    acc = _dot(ym, wout_ref[0, 0:MLA_WIDTH, :])
    acc = acc + _dot(yf, wout_ref[0, MLA_WIDTH:MLA_WIDTH + FNET_WIDTH, :])
    acc = acc + _dot(yn, wout_ref[0, MLA_WIDTH + FNET_WIDTH:, :])
    xn = x + ga_ref[0] * acc
    xn_ref[...] = xn
    hf = _rms(xn, gffn_ref[...]) * (1.0 + scf_ref[0]) + shf_ref[0]
    hf_ref[...] = hf
    hi = hf.astype(BF16)
    lo = (hf - hi.astype(F32)).astype(BF16)
    a = _dot(hi, wr_ref[0])
    b = _dot(lo, wr_ref[0])
    small = a[:, ROUTER_COLS:] + (b[:, :ROUTER_COLS] + b[:, ROUTER_COLS:])
    _route_tile(a[:, :ROUTER_COLS] + small + br_ref[0], rw_ref, ri_ref, cnt_ref, run_ref)


def _route_tile(lg, rw_ref, ri_ref, cnt_ref, run_ref):
    first = jnp.logical_and(pl.program_id(0) == 0, pl.program_id(1) == 0)

    @pl.when(first)
    def _():
        run_ref[...] = jnp.zeros_like(run_ref)

    lane = lax.broadcasted_iota(jnp.int32, lg.shape, 1)
    neg = jnp.float32(-jnp.inf)

    def top(v):
        vmax = jnp.max(v, axis=1, keepdims=True)
        idx = jnp.min(jnp.where(v == vmax, lane, ROUTER_COLS), axis=1, keepdims=True)
        return vmax, idx

    in_groups = lane < N_GROUPS
    gl = jnp.where(in_groups, lg, neg)
    g_max, g_sel = top(gl)
    g_w = 1.0 / jnp.sum(jnp.where(in_groups, jnp.exp(gl - g_max), 0.0), axis=1, keepdims=True)
    e_lo = N_GROUPS + g_sel * EXPERTS_PER_GROUP
    el = jnp.where(jnp.logical_and(lane >= e_lo, lane < e_lo + EXPERTS_PER_GROUP), lg, neg)
    e1_max, i1 = top(el)
    e2_max, i2 = top(jnp.where(lane == i1, neg, el))
    t = jnp.exp(e2_max - e1_max)
    w0 = g_w / (1.0 + t)
    w1 = w0 * t
    rw_ref[...] = jnp.where(lane == 0, w0, jnp.where(lane == 1, w1, 0.0))

    row = lax.broadcasted_iota(jnp.int32, (TM, TM), 0)
    col = lax.broadcasted_iota(jnp.int32, (TM, TM), 1)
    tri = jnp.where(row >= col, 1.0, 0.0).astype(BF16)
    hot0 = lane == i1
    hot1 = lane == i2
    c0 = _dot(tri, jnp.where(hot0, 1.0, 0.0).astype(BF16))
    c1 = _dot(tri, jnp.where(hot1, 1.0, 0.0).astype(BF16))
    run = run_ref[...]
    tot0 = c0[TM - 1:TM, :]
    rank0 = jnp.sum(jnp.where(hot0, run + c0 - 1.0, 0.0), axis=1, keepdims=True)
    rank1 = jnp.sum(jnp.where(hot1, run + tot0 + c1 - 1.0, 0.0), axis=1, keepdims=True)
    run = run + tot0 + c1[TM - 1:TM, :]
    run_ref[...] = run
    cnt_ref[...] = jnp.broadcast_to(run, cnt_ref.shape).astype(jnp.int32)
    ri_ref[...] = jnp.where(lane == 0, i1 - N_GROUPS, jnp.where(lane == 1, i2 - N_GROUPS, jnp.where(
        lane == 2, rank0.astype(jnp.int32), jnp.where(lane == 3, rank1.astype(jnp.int32), 0))))


def _merge(o_mla, o_f, o_na, xs, mods, g_out, g_ffn, w_out, w_router, b_router, layer, B, j0):
    T = B * (NT - j0) * TM
    row = lambda b, j: (_otile(b, j, j0), 0)
    const = lambda b, j: (0, 0)
    slab = lambda b, j: (layer, 0, 0)
    stream_specs, stream_args = _stream_specs(xs, j0)
    return pl.pallas_call(
        functools.partial(_merge_kernel, n_stream=len(stream_args), j0=j0),
        grid=(B, NT - j0),
        in_specs=stream_specs + [
            pl.BlockSpec((TM, MLA_WIDTH), row),
            pl.BlockSpec((TM, FNET_WIDTH), row),
            pl.BlockSpec((TM, NA_WIDTH), row),
            _mod_spec(2, j0),
            _mod_spec(3, j0),
            _mod_spec(4, j0),
            pl.BlockSpec((1, D), const),
            pl.BlockSpec((1, D), const),
            pl.BlockSpec((1, D, D), slab),
            pl.BlockSpec((1, D, 2 * ROUTER_COLS), slab),
            pl.BlockSpec((1, 1, ROUTER_COLS), slab),
        ],
        out_specs=[
            pl.BlockSpec((TM, D), row),
            pl.BlockSpec((TM, D), row),
            pl.BlockSpec((TM, ROUTER_COLS), row),
            pl.BlockSpec((TM, ROUTER_COLS), row),
            pl.BlockSpec((8, ROUTER_COLS), const),
        ],
        out_shape=[
            jax.ShapeDtypeStruct((T, D), F32),
            jax.ShapeDtypeStruct((T, D), F32),
            jax.ShapeDtypeStruct((T, ROUTER_COLS), F32),
            jax.ShapeDtypeStruct((T, ROUTER_COLS), jnp.int32),
            jax.ShapeDtypeStruct((8, ROUTER_COLS), jnp.int32),
        ],
        scratch_shapes=[pltpu.VMEM((1, ROUTER_COLS), F32)],
        compiler_params=_cparams(("arbitrary", "arbitrary")),
        name="merge",
    )(*stream_args, o_mla, o_f, o_na, mods, mods, mods, g_out.reshape(1, D), g_ffn.reshape(1, D),
      w_out, w_router, b_router)


def _plan_kernel(cnt_ref, slot_ref, src_ref, tile_ref, exp_ref, lo_ref, hi_ref, flag_ref, nxt_ref, ni_ref,
                 gs_ref, *, n_pairs, max_items):
    def starts(e, acc):
        gs_ref[e] = acc
        return acc + cnt_ref[e]

    lax.fori_loop(0, N_EXPERTS, starts, 0)

    def place(p, c):
        src_ref[slot_ref[p]] = lax.shift_right_logical(p, 1)
        return c

    lax.fori_loop(0, n_pairs, place, 0, unroll=16)

    last = N_EXPERTS - 1

    def group_end(e):
        return gs_ref[e] + cnt_ref[e]

    def next_nonempty(e):
        return lax.while_loop(lambda x: jnp.logical_and(x < last, cnt_ref[jnp.minimum(x, last)] == 0),
                              lambda x: x + 1, e)

    def tile_items(t, carry):
        i, e, prev = carry
        row0 = t * TM
        e = lax.while_loop(lambda x: group_end(x) <= row0, lambda x: x + 1, e)

        def emit(state):
            i, e, prev, first, _ = state
            tile_ref[i] = t
            exp_ref[i] = e
            lo_ref[i] = jnp.clip(gs_ref[e] - row0, 0, TM)
            hi_ref[i] = jnp.clip(group_end(e) - row0, 0, TM)
            flag_ref[i] = first + 2 * (e != prev).astype(jnp.int32)
            done = group_end(e) >= row0 + TM
            e_next = jnp.where(done, e, next_nonempty(e + 1))
            return i + 1, e_next, e, jnp.int32(0), done

        i, e, prev, _, _ = lax.while_loop(lambda s: jnp.logical_not(s[4]), emit,
                                          (i, e, prev, jnp.int32(1), jnp.bool_(False)))
        return i, e, prev

    n_items, _, _ = lax.fori_loop(0, n_pairs // TM, tile_items,
                                  (jnp.int32(0), jnp.int32(0), jnp.int32(-1)))
    ni_ref[0] = n_items

    def pad(i, c):
        tile_ref[i] = tile_ref[n_items - 1]
        exp_ref[i] = exp_ref[n_items - 1]
        lo_ref[i] = 0
        hi_ref[i] = 0
        flag_ref[i] = 0
        nxt_ref[i] = -1
        return c

    lax.fori_loop(n_items, max_items, pad, 0)

    def parity(i, par):
        par = jnp.where((flag_ref[i] & 2) != 0, 1 - par, par)
        flag_ref[i] = flag_ref[i] + 4 * par
        return par

    lax.fori_loop(0, n_items, parity, jnp.int32(1))

    def lookahead(k, following):
        i = n_items - 1 - k
        nxt_ref[i] = following
        return jnp.where((flag_ref[i] & 2) != 0, exp_ref[i], following)

    lax.fori_loop(0, n_items, lookahead, jnp.int32(-1))


def _plan(counts, eid, rank):
    n_pairs = eid.shape[0]
    max_items = n_pairs // TM + N_EXPERTS - 1
    g_start = jnp.cumsum(counts) - counts
    experts = jnp.arange(N_EXPERTS, dtype=jnp.int32)
    slot = jnp.sum(jnp.where(eid[:, None] == experts[None, :], g_start[None, :], 0), axis=1) + rank
    smem = pl.BlockSpec(memory_space=pltpu.SMEM)
    i32 = lambda n: jax.ShapeDtypeStruct((n,), jnp.int32)
    src, it_tile, it_exp, it_lo, it_hi, flags, nxt, n_items = pl.pallas_call(
        functools.partial(_plan_kernel, n_pairs=n_pairs, max_items=max_items),
        in_specs=[smem] * 2,
        out_specs=[smem] * 8,
        out_shape=[i32(n_pairs)] + [i32(max_items)] * 6 + [i32(1)],
        scratch_shapes=[pltpu.SMEM((N_EXPERTS,), jnp.int32)],
        name="moe_plan",
    )(counts, slot)
    return (it_tile, it_exp, it_lo, it_hi, flags, nxt, n_items, src), slot


def _moe_kernel(tile_ref, exp_ref, lo_ref, hi_ref, flag_ref, nxt_ref, ni_ref, src_ref,
                hf_hbm, wg_hbm, wu_hbm, wd_hbm, y_ref, xbuf, wg_buf, wu_buf, wd_buf, wgb, wub, wdb,
                sem, wsem, *, n_tiles, e0):
    i = pl.program_id(0)
    t = tile_ref[i]
    slot = lax.rem(t, 2)

    def gather_start(tile, buf, unrolled):
        base = tile * TM

        def issue(r):
            tok = src_ref[base + r]
            pltpu.make_async_copy(hf_hbm.at[pl.ds(tok, 1)], xbuf.at[buf, pl.ds(r, 1)], sem.at[buf]).start()

        if unrolled:
            for r in range(TM):
                issue(r)
        else:
            lax.fori_loop(0, TM, lambda r, c: (issue(r), c)[1], 0)

    def gather_wait(buf):
        pltpu.make_async_copy(hf_hbm.at[pl.ds(0, TM)], xbuf.at[buf], sem.at[buf]).wait()

    def weight_copies(expert, b):
        e = e0 + expert
        return (pltpu.make_async_copy(wg_hbm.at[e], wg_buf.at[b], wsem.at[b]),
                pltpu.make_async_copy(wu_hbm.at[e], wu_buf.at[b], wsem.at[b]),
                pltpu.make_async_copy(wd_hbm.at[e], wd_buf.at[b], wsem.at[b]))

    def weights_start(expert, b):
        for cp in weight_copies(expert, b):
            cp.start(priority=1)

    def weights_wait(b):
        for cp in weight_copies(0, b):
            cp.wait()

    @pl.when(i < ni_ref[0])
    def _():
        first_visit = (flag_ref[i] & 1) != 0
        new_expert = (flag_ref[i] & 2) != 0
        wslot = lax.shift_right_logical(flag_ref[i], 2) & 1

        @pl.when(i == 0)
        def _():
            gather_start(0, 0, False)

        @pl.when(first_visit)
        def _():
            gather_wait(slot)

        for b in range(2):
            @pl.when(jnp.logical_and(jnp.logical_and(first_visit, t + 1 < n_tiles), slot == 1 - b))
            def _():
                gather_start(t + 1, b, True)

        @pl.when(i == 0)
        def _():
            weights_start(exp_ref[0], 0)

        @pl.when(new_expert)
        def _():
            @pl.when(nxt_ref[i] >= 0)
            def _():
                weights_start(nxt_ref[i], 1 - wslot)

            weights_wait(wslot)
            wgb[...] = wg_buf[wslot].astype(BF16)
            wub[...] = wu_buf[wslot].astype(BF16)
            wdb[...] = wd_buf[wslot].astype(BF16)

        x = xbuf[slot].astype(BF16)
        a = _dot(x, wgb[...])
        u = _dot(x, wub[...])
        row = lax.broadcasted_iota(jnp.int32, (TM, 1), 0)
        mine = jnp.logical_and(row >= lo_ref[i], row < hi_ref[i])
        h = jnp.where(mine, (a * jax.nn.sigmoid(a)) * u, 0.0)
        yv = _dot(h.astype(BF16), wdb[...])

        @pl.when(first_visit)
        def _():
            y_ref[...] = yv

        @pl.when(jnp.logical_not(first_visit))
        def _():
            y_ref[...] += yv


def _moe(hf, meta, w_gate, w_up, w_down, layer):
    n_rows = meta[-1].shape[0]
    n_tiles = n_rows // TM
    max_items = meta[0].shape[0]
    grid_spec = pltpu.PrefetchScalarGridSpec(
        num_scalar_prefetch=len(meta),
        grid=(max_items,),
        in_specs=[pl.BlockSpec(memory_space=pl.ANY)] * 4,
        out_specs=pl.BlockSpec((TM, D), lambda i, tile, *_: (tile[i], 0)),
        scratch_shapes=[
            pltpu.VMEM((2, TM, D), F32),
            pltpu.VMEM((2, D, D_EXPERT), F32),
            pltpu.VMEM((2, D, D_EXPERT), F32),
            pltpu.VMEM((2, D_EXPERT, D), F32),
            pltpu.VMEM((D, D_EXPERT), BF16),
            pltpu.VMEM((D, D_EXPERT), BF16),
            pltpu.VMEM((D_EXPERT, D), BF16),
            pltpu.SemaphoreType.DMA((2,)),
            pltpu.SemaphoreType.DMA((2,)),
        ],
    )
    return pl.pallas_call(
        functools.partial(_moe_kernel, n_tiles=n_tiles, e0=layer * N_EXPERTS),
        grid_spec=grid_spec,
        out_shape=jax.ShapeDtypeStruct((n_rows, D), F32),
        compiler_params=_cparams(("arbitrary",)),
        name="moe_experts",
    )(*meta, hf, w_gate.reshape(-1, D, D_EXPERT), w_up.reshape(-1, D, D_EXPERT),
      w_down.reshape(-1, D_EXPERT, D))


def _final_kernel(pos_ref, x_ref, gf_ref, w_ref, g_ref, y_hbm, o_ref, ybuf, sem, *, n_steps):
    step = pl.program_id(0) * 8 + pl.program_id(1)
    out = _moe_residual(pos_ref, y_hbm, ybuf, sem, x_ref[...], gf_ref[0], w_ref[...], step, n_steps)
    o_ref[0] = _rms(out, g_ref[...])


def _final(pos, wts, xn, mods, y, g_final, B):
    tile = lambda b, j, p: (b * 8 + j, 0)
    grid_spec = pltpu.PrefetchScalarGridSpec(
        num_scalar_prefetch=1,
        grid=(B, 8),
        in_specs=[
            pl.BlockSpec((TM, D), tile),
            pl.BlockSpec((1, 1, D), lambda b, j, p: (b * 6 + 5, 0, 0)),
            pl.BlockSpec((TM, 2), tile),
            pl.BlockSpec((1, D), lambda b, j, p: (0, 0)),
            pl.BlockSpec(memory_space=pl.ANY),
        ],
        out_specs=pl.BlockSpec((1, TM, D), lambda b, j, p: (b, j, 0)),
        scratch_shapes=_MOE_GATHER_SCRATCH,
    )
    return pl.pallas_call(
        functools.partial(_final_kernel, n_steps=B * 8),
        grid_spec=grid_spec,
        out_shape=jax.ShapeDtypeStruct((B, 8 * TM, D), F32),
        compiler_params=_cparams(("arbitrary", "arbitrary")),
        name="final",
    )(pos, xn, mods, wts, g_final.reshape(1, D), y)


def _prep_w_in(w_in):
    return jnp.swapaxes(w_in, 1, 2).astype(BF16)


def _prep_w_uq(w_uq):
    w = w_uq.reshape(Q_LORA, MLA_HEADS, QK_NOPE + QK_ROPE)
    w = jnp.pad(w, ((0, 0), (0, 0), (0, MLA_SLAB - QK_NOPE - QK_ROPE)))
    return w.reshape(Q_LORA, MLA_HEADS * MLA_SLAB).astype(BF16)


def _rope_tables(seq):
    half = QK_ROPE // 2
    inv_freq = ROPE_THETA ** (-jnp.arange(0, half, 2, dtype=F32) / half)
    t = jnp.arange(seq, dtype=jnp.int32)
    row = (t // GRID_W).astype(F32)
    col = (t % GRID_W).astype(F32)
    ang = jnp.concatenate([row[:, None] * inv_freq, col[:, None] * inv_freq], axis=-1)
    cos, sin = jnp.cos(ang), jnp.sin(ang)
    zeros = jnp.zeros((seq, 64), F32)
    cos_l = jnp.concatenate([jnp.repeat(cos, 2, axis=1), zeros], axis=1)
    sin_l = jnp.concatenate([jnp.stack([-sin, sin], axis=-1).reshape(seq, 64), zeros], axis=1)
    cos_c = jnp.concatenate([jnp.ones((CTX_LEN, 64), F32), jnp.zeros((CTX_LEN, 64), F32)], axis=1)
    sin_c = jnp.zeros((CTX_LEN, 128), F32)
    return jnp.concatenate([cos_c, cos_l], axis=0), jnp.concatenate([sin_c, sin_l], axis=0)


def kernel(x, c, ctx, c_ctx, w_ada, b_ada, g_attn, g_ffn, w_in, g_q, w_uq, g_kv, w_ukv, w_fnet, b_fnet,
           na_rpb, g_out, w_out, w_rg, b_rg, w_re, b_re, w_gate, w_up, w_down, g_final):
    B, S, _ = x.shape
    L = w_ada.shape[0]
    assert ctx.shape[1] == CTX_LEN == TM and S == 8 * TM and B <= 4
    T = B * NT * TM

    cond8 = jnp.concatenate([c, jnp.zeros((4 - B, D), F32), c_ctx[None], jnp.zeros((3, D), F32)], axis=0)
    mods_all = _modulation(cond8, w_ada, b_ada)
    w_in_ext = _prep_w_in(w_in)
    pad = ROUTER_COLS - N_GROUPS - N_EXPERTS
    w_router = jnp.concatenate([w_rg, w_re, jnp.zeros((L, D, pad), F32)], axis=2)
    w_scaled = w_router * 65537.0
    w_router_hi = w_scaled - (w_scaled - w_router)
    w_router_all = jnp.concatenate([w_router_hi, w_router - w_router_hi], axis=2).astype(BF16)
    b_router_all = jnp.concatenate([b_rg, b_re, jnp.zeros((L, pad), F32)], axis=1).reshape(L, 1, ROUTER_COLS)
    w_out_all = w_out.astype(BF16)
    na_bias = _na_bias(na_rpb)
    cos_t, sin_t = _rope_tables(S)
    dft_lat = _dft_tables(S)
    dft_ctx = _dft_tables(CTX_LEN)
    cd_c, cd_s = _dft_cos_sin(FNET_GROUP_DIM)
    cd = jnp.concatenate([cd_c, -cd_s], axis=1).astype(BF16)

    xs = (ctx.reshape(B * CTX_LEN, D), x.reshape(B * S, D))
    pending = None
    for l in range(L):
        last = l == L - 1
        j0 = 1 if last else 0
        mods = mods_all[l].reshape(48, 1, D)
        w_uq_ext = _prep_w_uq(w_uq[l])
        mla = (g_q[l], g_kv[l], w_uq_ext, w_ukv[l].astype(BF16), cos_t, sin_t)
        if pending is None:
            zf, naq, nak, nav, q, k, v = _in_projection(xs, mods, g_attn[l], w_in_ext, l, mla, B)
        else:
            zf, naq, nak, nav, q, k, v, xs = _in_projection(xs, mods, g_attn[l], w_in_ext, l, mla, B,
                                                              moe=pending)
        o_mla = _mla_attention(q, k, v, B, j0)
        o_na = _na_attention(naq, nak, nav, na_bias, l, B, j0)
        w_f = w_fnet[l].astype(BF16)
        o_f = _fnet(zf, dft_lat, dft_ctx, cd, w_f, b_fnet[l], B, j0)
        xn, hf, route_w, route_i, counts = _merge(o_mla, o_f, o_na, xs, mods, g_out[l], g_ffn[l],
                                                  w_out_all, w_router_all, b_router_all, l, B, j0)
        wts = route_w[:, 0:2]
        meta, slot = _plan(counts[0, N_GROUPS:N_GROUPS + N_EXPERTS], route_i[:, 0:2].reshape(-1),
                           route_i[:, 2:4].reshape(-1))
        y = _moe(hf, meta, w_gate, w_up, w_down, l)
        if last:
            return _final(slot, wts, xn, mods, y, g_final, B)
        xs, pending = xn, (slot, wts, y, mods)
```
